```python
import math
import jax, jax.numpy as jnp
from jax import lax
import numpy as np

D_MODEL = 1024
BATCH = 16
SEQ = 2048
DEPTH = 1

HEAD_DIM = 64
ATTN_CONFIGS = ((128, 1), (512, 4), (2048, 16))
HEADS_PER_GROUP = 4
N_ATTN_HEADS = HEADS_PER_GROUP * len(ATTN_CONFIGS)
ATTN_WIDTH = N_ATTN_HEADS * HEAD_DIM
ATTN_OUT_WIDTH = HEADS_PER_GROUP * HEAD_DIM
BLOCK = 128
N_REL_BUCKETS = 32
REL_MAX_DISTANCE = 2048
NEG_INF = -1e30

POOL_SIZES = (2, 4, 8, 16)
POOL_GROUP_DIM = 128
POOL_WIDTH = POOL_GROUP_DIM * len(POOL_SIZES)

Q_OFF = 0
K_OFF = Q_OFF + ATTN_WIDTH
V_OFF = K_OFF + ATTN_WIDTH
POOL_OFF = V_OFF + ATTN_WIDTH
GATE_A_OFF = POOL_OFF + POOL_WIDTH
GATE_B_OFF = GATE_A_OFF + D_MODEL
IN_WIDTH = GATE_B_OFF + D_MODEL

N_EXPERT_GROUPS = 4
EXPERTS_PER_GROUP = 8
N_EXPERTS = N_EXPERT_GROUPS * EXPERTS_PER_GROUP
EXPERT_TOP_K = 2
D_EXPERT = 256

LN_EPS = 1e-5
DEEPNORM_ALPHA = (2.0 * DEPTH) ** 0.25
DEEPNORM_BETA = (8.0 * DEPTH) ** -0.25

kernel_name = 'hybrid_dilated_pool_hiermoe_block'


def layer_norm(x, gamma, beta):
    xf = x.astype(jnp.float32)
    mu = jnp.mean(xf, axis=-1, keepdims=True)
    var = jnp.mean(jnp.square(xf - mu), axis=-1, keepdims=True)
    return ((xf - mu) * lax.rsqrt(var + LN_EPS) * gamma + beta).astype(x.dtype)


def t5_causal_bucket(dist):
    max_exact = N_REL_BUCKETS // 2
    is_small = dist < max_exact
    d = jnp.maximum(dist, 1).astype(jnp.float32)
    large = max_exact + (jnp.log(d / max_exact) / math.log(REL_MAX_DISTANCE / max_exact)
                         * (N_REL_BUCKETS - max_exact)).astype(jnp.int32)
    large = jnp.minimum(large, N_REL_BUCKETS - 1)
    return jnp.where(is_small, dist, large)


def dilated_window_attention(q, k, v, bias_table, dilation, span):
    B, S, H, Dh = q.shape
    Z = B * dilation
    L = S // dilation
    nb = -(-L // BLOCK)
    Lp = nb * BLOCK

    def to_blocks(t):
        t = t.reshape(B, L, dilation, H, Dh).transpose(0, 2, 1, 3, 4).reshape(Z, L, H, Dh)
        t = jnp.pad(t, ((0, 0), (0, Lp - L), (0, 0), (0, 0)))
        return t.reshape(Z, nb, BLOCK, H, Dh)

    def with_prev(t):
        prev = jnp.pad(t[:, :-1], ((0, 0), (1, 0), (0, 0), (0, 0), (0, 0)))
        return jnp.concatenate([prev, t], axis=2)

    qb = to_blocks(q)
    kc = with_prev(to_blocks(k))
    vc = with_prev(to_blocks(v))
    logits = jnp.einsum('znqhd,znkhd->znhqk', qb, kc).astype(jnp.float32) * (Dh ** -0.5)

    qi = jnp.arange(BLOCK)[:, None]
    ki = jnp.arange(2 * BLOCK)[None, :]
    step = qi + BLOCK - ki
    in_window = (step >= 0) & (step <= span)
    bucket = t5_causal_bucket(jnp.clip(step, 0, span) * dilation)
    bias = jnp.transpose(bias_table[bucket], (2, 0, 1)).astype(jnp.float32)
    after_start = (jnp.arange(nb)[:, None, None] > 0) | (ki >= BLOCK)[None]
    valid = in_window[None] & after_start
    logits = jnp.where(valid[None, :, None], logits + bias[None, None], NEG_INF)

    m = jnp.max(logits, axis=-1, keepdims=True)
    p = jnp.exp(logits - m)
    s = jnp.sum(p, axis=-1, keepdims=True)
    o = jnp.einsum('znhqk,znkhd->znqhd', p.astype(v.dtype), vc)
    o = o / jnp.transpose(s, (0, 1, 3, 2, 4))
    lse = jnp.transpose((m + jnp.log(s))[..., 0], (0, 1, 3, 2))

    def from_blocks(t):
        t = t.reshape((Z, Lp) + t.shape[3:])[:, :L]
        t = t.reshape((B, dilation, L) + t.shape[2:])
        t = jnp.moveaxis(t, 1, 2)
        return t.reshape((B, S) + t.shape[3:])

    return from_blocks(o).astype(q.dtype), from_blocks(lse)


def dilated_attention_mixer(q, k, v, rel_bias_table):
    B, S = q.shape[:2]
    outs, lses = [], []
    for g, (window, dilation) in enumerate(ATTN_CONFIGS):
        hs = slice(g * HEADS_PER_GROUP, (g + 1) * HEADS_PER_GROUP)
        o, lse = dilated_window_attention(q[:, :, hs], k[:, :, hs], v[:, :, hs],
                                          rel_bias_table[:, hs], dilation, window // dilation)
        outs.append(o)
        lses.append(lse)
    weights = jax.nn.softmax(jnp.stack(lses, axis=0), axis=0)
    o = jnp.sum(weights[..., None] * jnp.stack(outs, axis=0).astype(jnp.float32), axis=0)
    return o.reshape(B, S, ATTN_OUT_WIDTH).astype(q.dtype)


def multiscale_pool_mixer(u, w_pool, pool_scale):
    B, S, _ = u.shape
    ug = u.reshape(B, S, len(POOL_SIZES), POOL_GROUP_DIM)
    csum = jnp.pad(jnp.cumsum(ug.astype(jnp.float32), axis=1), ((0, 0), (1, 0), (0, 0), (0, 0)))
    t = jnp.arange(S)
    pooled = []
    for gi, w in enumerate(POOL_SIZES):
        start = jnp.maximum(t + 1 - w, 0)
        count = jnp.minimum(t + 1, w).astype(jnp.float32)
        pooled.append((csum[:, t + 1, gi] - csum[:, start, gi]) / count[None, :, None])
    diff = (jnp.stack(pooled, axis=2) - ug.astype(jnp.float32)).astype(u.dtype)
    y = jnp.einsum('bsgc,gcd->bsgd', diff, w_pool).reshape(B, S, POOL_WIDTH)
    return y * pool_scale


def hierarchical_moe(h, w_router_group, b_router_group, w_router_expert, b_router_expert,
                     w_expert_gate, w_expert_up, w_expert_down):
    B, S, D = h.shape
    hf = h.reshape(B * S, D)
    group_probs = jax.nn.softmax((hf @ w_router_group + b_router_group).astype(jnp.float32), axis=-1)
    g_prob, g_idx = lax.top_k(group_probs, 1)
    expert_logits = (hf @ w_router_expert + b_router_expert).astype(jnp.float32)
    expert_logits = expert_logits.reshape(-1, N_EXPERT_GROUPS, EXPERTS_PER_GROUP)
    in_group = jnp.take_along_axis(expert_logits, g_idx[:, :, None], axis=1)[:, 0]
    top_logits, top_idx = lax.top_k(in_group, EXPERT_TOP_K)
    top_w = jax.nn.softmax(top_logits, axis=-1) * g_prob
    expert_id = g_idx * EXPERTS_PER_GROUP + top_idx
    combine = jnp.sum(jax.nn.one_hot(expert_id, N_EXPERTS, dtype=jnp.float32) * top_w[..., None],
                      axis=1).astype(h.dtype)
    y = jnp.zeros_like(hf)
    for e in range(N_EXPERTS):
        hidden = jax.nn.silu(hf @ w_expert_gate[e]) * (hf @ w_expert_up[e])
        y = y + combine[:, e:e + 1] * (hidden @ w_expert_down[e])
    return y.reshape(B, S, D)


def setup_inputs(seed: int = 0) -> dict:
    key = jax.random.key(seed)
    ks = jax.random.split(key, 22)
    f32 = jnp.float32
    nrm = lambda k, shape, scale: jax.random.normal(k, shape, f32) * scale
    L = DEPTH
    col_scale = jnp.ones((IN_WIDTH,), f32).at[V_OFF:V_OFF + ATTN_WIDTH].set(DEEPNORM_BETA)
    return {
        'x': jax.random.normal(ks[0], (BATCH, SEQ, D_MODEL), f32),
        'w_in': nrm(ks[1], (L, D_MODEL, IN_WIDTH), D_MODEL ** -0.5) * col_scale,
        'b_in': nrm(ks[2], (L, IN_WIDTH), 0.02),
        'rel_bias_table': nrm(ks[3], (N_REL_BUCKETS, N_ATTN_HEADS), 0.5),
        'w_pool': nrm(ks[4], (L, len(POOL_SIZES), POOL_GROUP_DIM, POOL_GROUP_DIM), POOL_GROUP_DIM ** -0.5),
        'pool_scale': 1.0 + nrm(ks[5], (L, POOL_WIDTH), 0.1),
        'w_proj_attn': nrm(ks[6], (L, ATTN_OUT_WIDTH, D_MODEL), ATTN_OUT_WIDTH ** -0.5),
        'w_proj_pool': nrm(ks[7], (L, POOL_WIDTH, D_MODEL), POOL_WIDTH ** -0.5),
        'w_out': nrm(ks[8], (L, D_MODEL, D_MODEL), D_MODEL ** -0.5 * DEEPNORM_BETA),
        'ln1_gamma': 1.0 + nrm(ks[9], (L, D_MODEL), 0.05),
        'ln1_beta': nrm(ks[10], (L, D_MODEL), 0.02),
        'w_router_group': nrm(ks[11], (L, D_MODEL, N_EXPERT_GROUPS), D_MODEL ** -0.5),
        'b_router_group': nrm(ks[12], (L, N_EXPERT_GROUPS), 0.01),
        'w_router_expert': nrm(ks[13], (L, D_MODEL, N_EXPERTS), D_MODEL ** -0.5),
        'b_router_expert': nrm(ks[14], (L, N_EXPERTS), 0.01),
        'w_expert_gate': nrm(ks[15], (L, N_EXPERTS, D_MODEL, D_EXPERT), D_MODEL ** -0.5 * DEEPNORM_BETA),
        'w_expert_up': nrm(ks[16], (L, N_EXPERTS, D_MODEL, D_EXPERT), D_MODEL ** -0.5 * DEEPNORM_BETA),
        'w_expert_down': nrm(ks[17], (L, N_EXPERTS, D_EXPERT, D_MODEL), D_EXPERT ** -0.5 * DEEPNORM_BETA),
        'ln2_gamma': 1.0 + nrm(ks[18], (L, D_MODEL), 0.05),
        'ln2_beta': nrm(ks[19], (L, D_MODEL), 0.02),
    }


def reference(x, w_in, b_in, rel_bias_table, w_pool, pool_scale, w_proj_attn, w_proj_pool, w_out,
              ln1_gamma, ln1_beta, w_router_group, b_router_group, w_router_expert, b_router_expert,
              w_expert_gate, w_expert_up, w_expert_down, ln2_gamma, ln2_beta):
    B, S, D = x.shape
    for layer in range(DEPTH):
        proj = x @ w_in[layer] + b_in[layer]
        q = proj[..., Q_OFF:K_OFF].reshape(B, S, N_ATTN_HEADS, HEAD_DIM)
        k = proj[..., K_OFF:V_OFF].reshape(B, S, N_ATTN_HEADS, HEAD_DIM)
        v = proj[..., V_OFF:POOL_OFF].reshape(B, S, N_ATTN_HEADS, HEAD_DIM)
        u = proj[..., POOL_OFF:GATE_A_OFF]
        gate_a = jax.nn.sigmoid(proj[..., GATE_A_OFF:GATE_B_OFF])
        gate_b = jax.nn.sigmoid(proj[..., GATE_B_OFF:IN_WIDTH])

        y_attn = dilated_attention_mixer(q, k, v, rel_bias_table) @ w_proj_attn[layer]
        y_pool = multiscale_pool_mixer(u, w_pool[layer], pool_scale[layer]) @ w_proj_pool[layer]
        mixed = gate_a * y_attn + gate_b * y_pool
        x = layer_norm(DEEPNORM_ALPHA * x + mixed @ w_out[layer], ln1_gamma[layer], ln1_beta[layer])

        moe = hierarchical_moe(x, w_router_group[layer], b_router_group[layer],
                               w_router_expert[layer], b_router_expert[layer],
                               w_expert_gate[layer], w_expert_up[layer], w_expert_down[layer])
        x = layer_norm(DEEPNORM_ALPHA * x + moe, ln2_gamma[layer], ln2_beta[layer])
    return x
```

```python
import functools
import math

import jax
import jax.numpy as jnp
from jax import lax
from jax.experimental import pallas as pl
from jax.experimental.pallas import tpu as pltpu

F32 = jnp.float32
BF16 = jnp.bfloat16

HEAD_DIM = 64
ATTN_CONFIGS = ((128, 1), (512, 4), (2048, 16))
N_GROUPS = len(ATTN_CONFIGS)
HEADS_PER_GROUP = 4
GROUP_WIDTH = HEADS_PER_GROUP * HEAD_DIM
ATTN_WIDTH = N_GROUPS * GROUP_WIDTH
BLOCK = 128
N_REL_BUCKETS = 32
REL_MAX_DISTANCE = 2048
NEG_INF = -1e30

POOL_SIZES = (2, 4, 8, 16)
POOL_GROUP_DIM = 128
POOL_WIDTH = POOL_GROUP_DIM * len(POOL_SIZES)
POOL_HALO = 16

N_EXPERT_GROUPS = 4
EXPERTS_PER_GROUP = 8
N_EXPERTS = N_EXPERT_GROUPS * EXPERTS_PER_GROUP
D_EXPERT = 256
LN_EPS = 1e-5

VMEM_LIMIT_BYTES = 56 * 1024 * 1024
LANES = 128
HALVES = GROUP_WIDTH // LANES


def _layer_norm(z, gamma, beta):
    mu = jnp.mean(z, axis=-1, keepdims=True)
    zc = z - mu
    var = jnp.mean(zc * zc, axis=-1, keepdims=True)
    return zc * lax.rsqrt(var + LN_EPS) * gamma + beta


def _proj_kernel(x_ref, w_ref, b_ref, *refs, tm, d_model):
    qkv_refs = refs[:9]
    u_ref, ga_ref, gb_ref, xb_ref, acc_ref = refs[9:]
    xb_ref[...] = x_ref[...].astype(BF16)

    def chunk(c0, width):
        acc = jnp.dot(xb_ref[...], w_ref[:, c0:c0 + width], preferred_element_type=F32)
        return acc + b_ref[:, c0:c0 + width]

    for which in range(3):
        scale = HEAD_DIM ** -0.5 if which == 0 else 1.0
        for g, (_, dil) in enumerate(ATTN_CONFIGS):
            out = qkv_refs[which * 3 + g]
            acc = chunk(which * ATTN_WIDTH + g * GROUP_WIDTH, GROUP_WIDTH) * scale
            if dil == 1:
                out[0] = acc.astype(BF16)
            else:
                for half in range(HALVES):
                    acc_ref[half] = acc[:, half * LANES:(half + 1) * LANES]
                for r in range(dil):
                    for half in range(HALVES):
                        out[r, :, half * LANES:(half + 1) * LANES] = (
                            acc_ref[half, pl.ds(r, tm // dil, stride=dil), :].astype(BF16))
    pool_off = 3 * ATTN_WIDTH
    for c in range(POOL_WIDTH // 256):
        u_ref[:, c * 256:(c + 1) * 256] = chunk(pool_off + c * 256, 256)
    ga_off = pool_off + POOL_WIDTH
    for gate_ref, off in ((ga_ref, ga_off), (gb_ref, ga_off + d_model)):
        for c in range(d_model // 256):
            gate_ref[:, c * 256:(c + 1) * 256] = jax.nn.sigmoid(chunk(off + c * 256, 256)).astype(BF16)


def _proj_call(x, w_in, b_in, tm=512):
    B, S, D = x.shape
    in_width = w_in.shape[1]
    assert in_width == 3 * ATTN_WIDTH + POOL_WIDTH + 2 * D
    assert S % tm == 0
    grid = (B, S // tm)
    qkv_shapes, qkv_specs = [], []
    for _ in range(3):
        for (_, dil) in ATTN_CONFIGS:
            assert tm % (dil * 16) == 0
            qkv_shapes.append(jax.ShapeDtypeStruct((B, dil, S // dil, GROUP_WIDTH), BF16))
            qkv_specs.append(pl.BlockSpec((None, dil, tm // dil, GROUP_WIDTH), lambda b, i: (b, 0, i, 0)))
    row_spec = lambda width: pl.BlockSpec((None, tm, width), lambda b, i: (b, i, 0))
    out_shape = qkv_shapes + [jax.ShapeDtypeStruct((B, S, POOL_WIDTH), F32),
                              jax.ShapeDtypeStruct((B, S, D), BF16),
                              jax.ShapeDtypeStruct((B, S, D), BF16)]
    out_specs = qkv_specs + [row_spec(POOL_WIDTH), row_spec(D), row_spec(D)]
    outs = pl.pallas_call(
        functools.partial(_proj_kernel, tm=tm, d_model=D),
        grid=grid,
        in_specs=[row_spec(D),
                  pl.BlockSpec((D, in_width), lambda b, i: (0, 0)),
                  pl.BlockSpec((1, in_width), lambda b, i: (0, 0))],
        out_specs=out_specs,
        out_shape=out_shape,
        scratch_shapes=[pltpu.VMEM((tm, D), BF16), pltpu.VMEM((HALVES, tm, LANES), F32)],
        compiler_params=pltpu.CompilerParams(dimension_semantics=("parallel", "parallel"),
                                             vmem_limit_bytes=VMEM_LIMIT_BYTES),
        name="proj",
    )(x, w_in.astype(BF16), b_in.reshape(1, in_width))
    qkv = [o.reshape(B, S, GROUP_WIDTH) for o in outs[:9]]
    return qkv, outs[9], outs[10], outs[11]


def _t5_causal_bucket(dist):
    max_exact = N_REL_BUCKETS // 2
    is_small = dist < max_exact
    d = jnp.maximum(dist, 1).astype(F32)
    large = max_exact + (jnp.log(d / max_exact) / math.log(REL_MAX_DISTANCE / max_exact)
                         * (N_REL_BUCKETS - max_exact)).astype(jnp.int32)
    large = jnp.minimum(large, N_REL_BUCKETS - 1)
    return jnp.where(is_small, dist, large)


def _attn_bias(rel_bias_table):
    qi = jnp.arange(BLOCK)[:, None]
    ki = jnp.arange(2 * BLOCK)[None, :]
    step = qi + BLOCK - ki
    out = []
    for g, (window, dil) in enumerate(ATTN_CONFIGS):
        span = window // dil
        in_window = (step >= 0) & (step <= span)
        bucket = _t5_causal_bucket(jnp.clip(step, 0, span) * dil)
        table = rel_bias_table[:, g * HEADS_PER_GROUP:(g + 1) * HEADS_PER_GROUP].astype(F32)
        bias = jnp.transpose(table[bucket], (2, 0, 1))
        bias = jnp.where(in_window[None], bias, NEG_INF)
        out.append(bias.reshape(HEADS_PER_GROUP * BLOCK, 2 * BLOCK))
    return jnp.stack(out)


def _attn_kernel(*refs, seq):
    qkv = refs[:9]
    bias_ref, out_ref, o_scr, l_scr = refs[9:]
    rows = HEADS_PER_GROUP * BLOCK
    row_head = lax.broadcasted_iota(jnp.int32, (rows, GROUP_WIDTH), 0) // BLOCK
    lane_head_r = lax.broadcasted_iota(jnp.int32, (rows, GROUP_WIDTH), 1) // HEAD_DIM
    head_mask = row_head == lane_head_r
    lane_head = lax.broadcasted_iota(jnp.int32, (BLOCK, GROUP_WIDTH), 1) // HEAD_DIM

    def per_head_to_lanes(stacked):
        out = stacked[0:BLOCK]
        for h in range(1, HEADS_PER_GROUP):
            out = jnp.where(lane_head == h, stacked[h * BLOCK:(h + 1) * BLOCK], out)
        return out

    for g, (_, dil) in enumerate(ATTN_CONFIGS):
        q_ref, k_ref, v_ref = qkv[3 * g:3 * g + 3]
        sub_len = seq // dil
        n_blocks = sub_len // BLOCK

        def block(r, n, first, g=g, dil=dil, q_ref=q_ref, k_ref=k_ref, v_ref=v_ref, sub_len=sub_len):
            base = pl.multiple_of(r * sub_len + n * BLOCK, BLOCK)
            qb = q_ref[pl.ds(base, BLOCK), :]
            qs = jnp.where(head_mask, jnp.concatenate([qb] * HEADS_PER_GROUP, axis=0), jnp.zeros((), BF16))
            if first:
                kk = k_ref[pl.ds(base, BLOCK), :]
                vv = v_ref[pl.ds(base, BLOCK), :]
                bias = bias_ref[g, :, BLOCK:2 * BLOCK]
            else:
                kbase = pl.multiple_of(base - BLOCK, BLOCK)
                kk = k_ref[pl.ds(kbase, 2 * BLOCK), :]
                vv = v_ref[pl.ds(kbase, 2 * BLOCK), :]
                bias = bias_ref[g]
            logits = lax.dot_general(qs, kk, (((1,), (1,)), ((), ())), preferred_element_type=F32) + bias
            m = jnp.max(logits, axis=1, keepdims=True)
            p = jnp.exp(logits - m)
            s = jnp.sum(p, axis=1, keepdims=True)
            pv = jnp.dot(p.astype(BF16), vv, preferred_element_type=F32)
            o = per_head_to_lanes(pv * (1.0 / s))
            lse = per_head_to_lanes(jnp.broadcast_to(m + jnp.log(s), (rows, GROUP_WIDTH)))
            start = n * (BLOCK * dil) + r
            if dil == 1:
                dst = pl.ds(pl.multiple_of(start, BLOCK), BLOCK)
            else:
                dst = pl.ds(start, BLOCK, stride=dil)
            for half in range(HALVES):
                cols = slice(half * LANES, (half + 1) * LANES)
                o_scr[g * HALVES + half, dst, :] = o[:, cols]
                l_scr[g * HALVES + half, dst, :] = lse[:, cols]

        def per_subsequence(r, carry, block=block, n_blocks=n_blocks):
            block(r, 0, True)
            if n_blocks > 1:
                def per_block(n, c):
                    block(r, n, False)
                    return c
                lax.fori_loop(1, n_blocks, per_block, 0)
            return carry

        lax.fori_loop(0, dil, per_subsequence, 0)

    chunk = 256

    def merge(i, carry):
        sl = pl.ds(pl.multiple_of(i * chunk, chunk), chunk)
        for half in range(HALVES):
            ls = [l_scr[g * HALVES + half, sl, :] for g in range(N_GROUPS)]
            m = functools.reduce(jnp.maximum, ls)
            es = [jnp.exp(l - m) for l in ls]
            den = functools.reduce(lambda a, b: a + b, es)
            num = functools.reduce(lambda a, b: a + b,
                                   [e * o_scr[g * HALVES + half, sl, :] for g, e in enumerate(es)])
            out_ref[sl, half * LANES:(half + 1) * LANES] = (num / den).astype(BF16)
        return carry

    lax.fori_loop(0, seq // chunk, merge, 0)


def _attn_call(qkv, rel_bias_table):
    B, S, _ = qkv[0].shape
    for (_, dil) in ATTN_CONFIGS:
        assert S % (dil * BLOCK) == 0
    bias = _attn_bias(rel_bias_table)
    seq_spec = pl.BlockSpec((None, S, GROUP_WIDTH), lambda b: (b, 0, 0))
    ordered = []
    for g in range(N_GROUPS):
        ordered += [qkv[g], qkv[3 + g], qkv[6 + g]]
    return pl.pallas_call(
        functools.partial(_attn_kernel, seq=S),
        grid=(B,),
        in_specs=[seq_spec] * 9 + [pl.BlockSpec(bias.shape, lambda b: (0, 0, 0))],
        out_specs=seq_spec,
        out_shape=jax.ShapeDtypeStruct((B, S, GROUP_WIDTH), BF16),
        scratch_shapes=[pltpu.VMEM((N_GROUPS * HALVES, S, LANES), F32),
                        pltpu.VMEM((N_GROUPS * HALVES, S, LANES), F32)],
        compiler_params=pltpu.CompilerParams(dimension_semantics=("parallel",),
                                             vmem_limit_bytes=VMEM_LIMIT_BYTES),
        name="attn",
    )(*ordered, bias)


def _split_bf16(a):
    hi = a.astype(BF16)
    lo = (a - hi.astype(F32)).astype(BF16)
    return hi, lo


def _post_kernel(a_ref, u_ref, halo_ref, ga_ref, gb_ref, x_ref,
                 pa_ref, wpool_ref, pscale_ref, pb_ref, wout_ref, g1_ref, b1_ref,
                 wr_hi_ref, wr_lo_ref, br_ref,
                 h_ref, hb_ref, ids_ref, wts_ref, pool_scr, *, tm, alpha):
    i = pl.program_id(1)
    halo = halo_ref[...]
    pool_scr[0:POOL_HALO, :] = jnp.where(i > 0, halo, jnp.zeros_like(halo))
    pool_scr[POOL_HALO:, :] = u_ref[...]
    pos = i * tm + lax.broadcasted_iota(jnp.int32, (tm, POOL_GROUP_DIM), 0)
    mixed_in = []
    for gi, w in enumerate(POOL_SIZES):
        cols = slice(gi * POOL_GROUP_DIM, (gi + 1) * POOL_GROUP_DIM)
        ug = u_ref[:, cols]
        acc = ug
        for j in range(1, w):
            acc = acc + pool_scr[pl.ds(POOL_HALO - j, tm), cols]
        count = jnp.minimum(pos + 1, w).astype(F32)
        diff = (acc / count - ug).astype(BF16)
        yg = jnp.dot(diff, wpool_ref[gi], preferred_element_type=F32) * pscale_ref[:, cols]
        mixed_in.append(yg.astype(BF16))
    y_pool = jnp.dot(jnp.concatenate(mixed_in, axis=1), pb_ref[...], preferred_element_type=F32)
    y_attn = jnp.dot(a_ref[...], pa_ref[...], preferred_element_type=F32)
    mixed = ga_ref[...].astype(F32) * y_attn + gb_ref[...].astype(F32) * y_pool
    y = jnp.dot(mixed.astype(BF16), wout_ref[...], preferred_element_type=F32)
    h = _layer_norm(alpha * x_ref[...] + y, g1_ref[...], b1_ref[...])
    h_ref[...] = h
    hb_ref[...] = h.astype(BF16)

    h_hi, h_lo = _split_bf16(h)
    nt = (((1,), (1,)), ((), ()))
    logits = (lax.dot_general(h_hi, wr_hi_ref[...], nt, preferred_element_type=F32)
              + lax.dot_general(h_lo, wr_hi_ref[...], nt, preferred_element_type=F32)
              + lax.dot_general(h_hi, wr_lo_ref[...], nt, preferred_element_type=F32)) + br_ref[...]
    lane = lax.broadcasted_iota(jnp.int32, logits.shape, 1)
    big = jnp.int32(2 ** 30)
    is_group = lane < N_EXPERT_GROUPS
    gl = jnp.where(is_group, logits, -jnp.inf)
    gmax = jnp.max(gl, axis=1, keepdims=True)
    g_idx = jnp.min(jnp.where(gl == gmax, lane, big), axis=1, keepdims=True)
    g_prob = 1.0 / jnp.sum(jnp.exp(gl - gmax), axis=1, keepdims=True)
    expert = lane - N_EXPERT_GROUPS
    in_group = (expert >= g_idx * EXPERTS_PER_GROUP) & (expert < (g_idx + 1) * EXPERTS_PER_GROUP)
    el = jnp.where(in_group, logits, -jnp.inf)
    v1 = jnp.max(el, axis=1, keepdims=True)
    i1 = jnp.min(jnp.where(el == v1, expert, big), axis=1, keepdims=True)
    el2 = jnp.where(expert == i1, -jnp.inf, el)
    v2 = jnp.max(el2, axis=1, keepdims=True)
    i2 = jnp.min(jnp.where(el2 == v2, expert, big), axis=1, keepdims=True)
    e2 = jnp.exp(v2 - v1)
    den = 1.0 + e2
    ids_ref[...] = jnp.concatenate([i1, i2], axis=1)
    wts_ref[...] = jnp.concatenate([1.0 / den * g_prob, e2 / den * g_prob], axis=1)


def _post_call(a, u, ga, gb, x, w_proj_attn, w_pool, pool_scale, w_proj_pool, w_out, gamma, beta,
               w_router_group, b_router_group, w_router_expert, b_router_expert, alpha, tm=512):
    B, S, D = x.shape
    assert S % tm == 0 and tm % POOL_HALO == 0
    n_logits = N_EXPERT_GROUPS + N_EXPERTS
    wr = jnp.concatenate([w_router_group, w_router_expert], axis=1).T
    wr_hi = wr.astype(BF16)
    wr_lo = (wr - wr_hi.astype(F32)).astype(BF16)
    br = jnp.concatenate([b_router_group, b_router_expert]).reshape(1, n_logits)
    row_spec = lambda width: pl.BlockSpec((None, tm, width), lambda b, i: (b, i, 0))
    full = lambda arr: pl.BlockSpec(arr.shape, lambda b, i: (0,) * arr.ndim)
    halo_blocks = tm // POOL_HALO
    halo_spec = pl.BlockSpec((None, POOL_HALO, POOL_WIDTH),
                             lambda b, i: (b, jnp.maximum(i * halo_blocks - 1, 0), 0))
    weights = [w_proj_attn.astype(BF16), w_pool.astype(BF16), pool_scale.reshape(1, POOL_WIDTH),
               w_proj_pool.astype(BF16), w_out.astype(BF16), gamma.reshape(1, D), beta.reshape(1, D),
               wr_hi, wr_lo, br]
    return pl.pallas_call(
        functools.partial(_post_kernel, tm=tm, alpha=alpha),
        grid=(B, S // tm),
        in_specs=[row_spec(GROUP_WIDTH), row_spec(POOL_WIDTH), halo_spec, row_spec(D), row_spec(D),
                  row_spec(D)] + [full(w) for w in weights],
        out_specs=[row_spec(D), row_spec(D), row_spec(2), row_spec(2)],
        out_shape=[jax.ShapeDtypeStruct((B, S, D), F32), jax.ShapeDtypeStruct((B, S, D), BF16),
                   jax.ShapeDtypeStruct((B, S, 2), jnp.int32), jax.ShapeDtypeStruct((B, S, 2), F32)],
        scratch_shapes=[pltpu.VMEM((tm + POOL_HALO, POOL_WIDTH), F32)],
        compiler_params=pltpu.CompilerParams(dimension_semantics=("parallel", "parallel"),
                                             vmem_limit_bytes=VMEM_LIMIT_BYTES),
        name="post",
    )(a, u, u, ga, gb, x, *weights)


def _moe_kernel(h_ref, hb_ref, ids_ref, wts_ref, wg_ref, wu_ref, wd_ref, g2_ref, b2_ref,
                out_ref, acc_ref, *, alpha):
    e = pl.program_id(1)

    @pl.when(e == 0)
    def _():
        acc_ref[...] = jnp.zeros_like(acc_ref)

    hb = hb_ref[...]
    gate = jnp.dot(hb, wg_ref[...], preferred_element_type=F32)
    up = jnp.dot(hb, wu_ref[...], preferred_element_type=F32)
    hidden = (jax.nn.silu(gate) * up).astype(BF16)
    y = jnp.dot(hidden, wd_ref[...], preferred_element_type=F32)
    ids = ids_ref[...]
    wts = wts_ref[...]
    combine = jnp.sum(jnp.where(ids == e, wts, 0.0), axis=1, keepdims=True)
    acc_ref[...] += combine * y

    @pl.when(e == pl.num_programs(1) - 1)
    def _():
        out_ref[...] = _layer_norm(alpha * h_ref[...] + acc_ref[...], g2_ref[...], b2_ref[...])


def _moe_call(h, hb, ids, wts, w_gate, w_up, w_down, gamma, beta, alpha, tm=1024):
    N, D = h.shape
    assert N % tm == 0
    row_spec = lambda width: pl.BlockSpec((tm, width), lambda i, e: (i, 0))
    return pl.pallas_call(
        functools.partial(_moe_kernel, alpha=alpha),
        grid=(N // tm, N_EXPERTS),
        in_specs=[row_spec(D), row_spec(D), row_spec(2), row_spec(2),
                  pl.BlockSpec((None, D, D_EXPERT), lambda i, e: (e, 0, 0)),
                  pl.BlockSpec((None, D, D_EXPERT), lambda i, e: (e, 0, 0)),
                  pl.BlockSpec((None, D_EXPERT, D), lambda i, e: (e, 0, 0)),
                  pl.BlockSpec((1, D), lambda i, e: (0, 0)),
                  pl.BlockSpec((1, D), lambda i, e: (0, 0))],
        out_specs=row_spec(D),
        out_shape=jax.ShapeDtypeStruct((N, D), F32),
        scratch_shapes=[pltpu.VMEM((tm, D), F32)],
        compiler_params=pltpu.CompilerParams(dimension_semantics=("parallel", "arbitrary"),
                                             vmem_limit_bytes=VMEM_LIMIT_BYTES),
        name="moe",
    )(h, hb, ids, wts, w_gate.astype(BF16), w_up.astype(BF16), w_down.astype(BF16),
      gamma.reshape(1, D), beta.reshape(1, D))


@jax.jit
def kernel(x, w_in, b_in, rel_bias_table, w_pool, pool_scale, w_proj_attn, w_proj_pool, w_out, ln1_gamma, ln1_beta, w_router_group, b_router_group, w_router_expert, b_router_expert, w_expert_gate, w_expert_up, w_expert_down, ln2_gamma, ln2_beta):
    B, S, D = x.shape
    depth = w_in.shape[0]
    alpha = (2.0 * depth) ** 0.25
    for layer in range(depth):
        qkv, u, ga, gb = _proj_call(x, w_in[layer], b_in[layer])
        a = _attn_call(qkv, rel_bias_table)
        h, hb, ids, wts = _post_call(a, u, ga, gb, x, w_proj_attn[layer], w_pool[layer], pool_scale[layer],
                                     w_proj_pool[layer], w_out[layer], ln1_gamma[layer], ln1_beta[layer],
                                     w_router_group[layer], b_router_group[layer],
                                     w_router_expert[layer], b_router_expert[layer], alpha)
        out = _moe_call(h.reshape(B * S, D), hb.reshape(B * S, D), ids.reshape(B * S, 2),
                        wts.reshape(B * S, 2), w_expert_gate[layer], w_expert_up[layer],
                        w_expert_down[layer], ln2_gamma[layer], ln2_beta[layer], alpha)
        x = out.reshape(B, S, D)
    return x
```

```python
import functools
import math

import jax
import jax.numpy as jnp
from jax import lax
from jax.experimental import pallas as pl
from jax.experimental.pallas import tpu as pltpu
from jax.experimental.pallas import tpu_sc as plsc

F32 = jnp.float32
BF16 = jnp.bfloat16

HEAD_DIM = 64
ATTN_CONFIGS = ((128, 1), (512, 4), (2048, 16))
N_GROUPS = len(ATTN_CONFIGS)
HEADS_PER_GROUP = 4
GROUP_WIDTH = HEADS_PER_GROUP * HEAD_DIM
ATTN_WIDTH = N_GROUPS * GROUP_WIDTH
BLOCK = 128
N_REL_BUCKETS = 32
REL_MAX_DISTANCE = 2048
NEG_INF = -1e30

POOL_SIZES = (2, 4, 8, 16)
POOL_GROUP_DIM = 128
POOL_WIDTH = POOL_GROUP_DIM * len(POOL_SIZES)
POOL_HALO = 16

N_EXPERT_GROUPS = 4
EXPERTS_PER_GROUP = 8
N_EXPERTS = N_EXPERT_GROUPS * EXPERTS_PER_GROUP
D_EXPERT = 256
LN_EPS = 1e-5

VMEM_LIMIT_BYTES = 56 * 1024 * 1024
LANES = 128
HALVES = GROUP_WIDTH // LANES


def _layer_norm(z, gamma, beta):
    mu = jnp.mean(z, axis=-1, keepdims=True)
    zc = z - mu
    var = jnp.mean(zc * zc, axis=-1, keepdims=True)
    return zc * lax.rsqrt(var + LN_EPS) * gamma + beta


def _proj_kernel(x_ref, w_ref, b_ref, *refs, tm, d_model):
    qkv_refs = refs[:9]
    u_ref, ga_ref, gb_ref, xb_ref, acc_ref = refs[9:]
    xb_ref[...] = x_ref[...].astype(BF16)

    def chunk(c0, width):
        acc = jnp.dot(xb_ref[...], w_ref[:, c0:c0 + width], preferred_element_type=F32)
        return acc + b_ref[:, c0:c0 + width]

    for which in range(3):
        scale = HEAD_DIM ** -0.5 if which == 0 else 1.0
        for g, (_, dil) in enumerate(ATTN_CONFIGS):
            out = qkv_refs[which * 3 + g]
            acc = chunk(which * ATTN_WIDTH + g * GROUP_WIDTH, GROUP_WIDTH) * scale
            if dil == 1:
                out[0] = acc.astype(BF16)
            else:
                for half in range(HALVES):
                    acc_ref[half] = acc[:, half * LANES:(half + 1) * LANES]
                for r in range(dil):
                    for half in range(HALVES):
                        out[r, :, half * LANES:(half + 1) * LANES] = (
                            acc_ref[half, pl.ds(r, tm // dil, stride=dil), :].astype(BF16))
    pool_off = 3 * ATTN_WIDTH
    for c in range(POOL_WIDTH // 256):
        u_ref[:, c * 256:(c + 1) * 256] = chunk(pool_off + c * 256, 256)
    ga_off = pool_off + POOL_WIDTH
    for gate_ref, off in ((ga_ref, ga_off), (gb_ref, ga_off + d_model)):
        for c in range(d_model // 256):
            gate_ref[:, c * 256:(c + 1) * 256] = jax.nn.sigmoid(chunk(off + c * 256, 256)).astype(BF16)


def _proj_call(x, w_in, b_in, tm=512):
    B, S, D = x.shape
    in_width = w_in.shape[1]
    assert in_width == 3 * ATTN_WIDTH + POOL_WIDTH + 2 * D
    assert S % tm == 0
    grid = (B, S // tm)
    qkv_shapes, qkv_specs = [], []
    for _ in range(3):
        for (_, dil) in ATTN_CONFIGS:
            assert tm % (dil * 16) == 0
            qkv_shapes.append(jax.ShapeDtypeStruct((B, dil, S // dil, GROUP_WIDTH), BF16))
            qkv_specs.append(pl.BlockSpec((None, dil, tm // dil, GROUP_WIDTH), lambda b, i: (b, 0, i, 0)))
    row_spec = lambda width: pl.BlockSpec((None, tm, width), lambda b, i: (b, i, 0))
    out_shape = qkv_shapes + [jax.ShapeDtypeStruct((B, S, POOL_WIDTH), F32),
                              jax.ShapeDtypeStruct((B, S, D), BF16),
                              jax.ShapeDtypeStruct((B, S, D), BF16)]
    out_specs = qkv_specs + [row_spec(POOL_WIDTH), row_spec(D), row_spec(D)]
    outs = pl.pallas_call(
        functools.partial(_proj_kernel, tm=tm, d_model=D),
        grid=grid,
        in_specs=[row_spec(D),
                  pl.BlockSpec((D, in_width), lambda b, i: (0, 0)),
                  pl.BlockSpec((1, in_width), lambda b, i: (0, 0))],
        out_specs=out_specs,
        out_shape=out_shape,
        scratch_shapes=[pltpu.VMEM((tm, D), BF16), pltpu.VMEM((HALVES, tm, LANES), F32)],
        compiler_params=pltpu.CompilerParams(dimension_semantics=("parallel", "parallel"),
                                             vmem_limit_bytes=VMEM_LIMIT_BYTES),
        name="proj",
    )(x, w_in.astype(BF16), b_in.reshape(1, in_width))
    qkv = [o.reshape(B, S, GROUP_WIDTH) for o in outs[:9]]
    return qkv, outs[9], outs[10], outs[11]


def _t5_causal_bucket(dist):
    max_exact = N_REL_BUCKETS // 2
    is_small = dist < max_exact
    d = jnp.maximum(dist, 1).astype(F32)
    large = max_exact + (jnp.log(d / max_exact) / math.log(REL_MAX_DISTANCE / max_exact)
                         * (N_REL_BUCKETS - max_exact)).astype(jnp.int32)
    large = jnp.minimum(large, N_REL_BUCKETS - 1)
    return jnp.where(is_small, dist, large)


def _attn_bias(rel_bias_table):
    qi = jnp.arange(BLOCK)[:, None]
    ki = jnp.arange(2 * BLOCK)[None, :]
    step = qi + BLOCK - ki
    out = []
    for g, (window, dil) in enumerate(ATTN_CONFIGS):
        span = window // dil
        in_window = (step >= 0) & (step <= span)
        bucket = _t5_causal_bucket(jnp.clip(step, 0, span) * dil)
        table = rel_bias_table[:, g * HEADS_PER_GROUP:(g + 1) * HEADS_PER_GROUP].astype(F32)
        bias = jnp.einsum('qkb,bh->hqk', jax.nn.one_hot(bucket, N_REL_BUCKETS, dtype=F32), table,
                          precision=lax.Precision.HIGHEST)
        bias = jnp.where(in_window[None], bias, NEG_INF)
        out.append(bias.reshape(HEADS_PER_GROUP * BLOCK, 2 * BLOCK))
    return jnp.stack(out)


def _attn_kernel(*refs, seq):
    qkv = refs[:9]
    bias_ref, out_ref, o_scr, l_scr = refs[9:]
    rows = HEADS_PER_GROUP * BLOCK
    row_head = lax.broadcasted_iota(jnp.int32, (rows, GROUP_WIDTH), 0) // BLOCK
    lane_head_r = lax.broadcasted_iota(jnp.int32, (rows, GROUP_WIDTH), 1) // HEAD_DIM
    head_mask = row_head == lane_head_r
    lane_head = lax.broadcasted_iota(jnp.int32, (BLOCK, GROUP_WIDTH), 1) // HEAD_DIM

    def per_head_to_lanes(stacked):
        out = stacked[0:BLOCK]
        for h in range(1, HEADS_PER_GROUP):
            out = jnp.where(lane_head == h, stacked[h * BLOCK:(h + 1) * BLOCK], out)
        return out

    for g, (_, dil) in enumerate(ATTN_CONFIGS):
        q_ref, k_ref, v_ref = qkv[3 * g:3 * g + 3]
        sub_len = seq // dil
        n_blocks = sub_len // BLOCK

        def block(r, n, first, g=g, dil=dil, q_ref=q_ref, k_ref=k_ref, v_ref=v_ref, sub_len=sub_len):
            base = pl.multiple_of(r * sub_len + n * BLOCK, BLOCK)
            qb = q_ref[pl.ds(base, BLOCK), :]
            qs = jnp.where(head_mask, jnp.concatenate([qb] * HEADS_PER_GROUP, axis=0), jnp.zeros((), BF16))
            if first:
                kk = k_ref[pl.ds(base, BLOCK), :]
                vv = v_ref[pl.ds(base, BLOCK), :]
                bias = bias_ref[g, :, BLOCK:2 * BLOCK]
            else:
                kbase = pl.multiple_of(base - BLOCK, BLOCK)
                kk = k_ref[pl.ds(kbase, 2 * BLOCK), :]
                vv = v_ref[pl.ds(kbase, 2 * BLOCK), :]
                bias = bias_ref[g]
            logits = lax.dot_general(qs, kk, (((1,), (1,)), ((), ())), preferred_element_type=F32) + bias
            m = jnp.max(logits, axis=1, keepdims=True)
            p = jnp.exp(logits - m)
            s = jnp.sum(p, axis=1, keepdims=True)
            pv = jnp.dot(p.astype(BF16), vv, preferred_element_type=F32)
            o = per_head_to_lanes(pv * (1.0 / s))
            lse = per_head_to_lanes(jnp.broadcast_to(m + jnp.log(s), (rows, GROUP_WIDTH)))
            start = n * (BLOCK * dil) + r
            if dil == 1:
                dst = pl.ds(pl.multiple_of(start, BLOCK), BLOCK)
            else:
                dst = pl.ds(start, BLOCK, stride=dil)
            for half in range(HALVES):
                cols = slice(half * LANES, (half + 1) * LANES)
                o_scr[g * HALVES + half, dst, :] = o[:, cols]
                l_scr[g * HALVES + half, dst, :] = lse[:, cols]

        def per_subsequence(r, carry, block=block, n_blocks=n_blocks):
            block(r, 0, True)
            if n_blocks > 1:
                def per_block(n, c):
                    block(r, n, False)
                    return c
                lax.fori_loop(1, n_blocks, per_block, 0)
            return carry

        lax.fori_loop(0, dil, per_subsequence, 0)

    chunk = 256

    def merge(i, carry):
        sl = pl.ds(pl.multiple_of(i * chunk, chunk), chunk)
        for half in range(HALVES):
            ls = [l_scr[g * HALVES + half, sl, :] for g in range(N_GROUPS)]
            m = functools.reduce(jnp.maximum, ls)
            es = [jnp.exp(l - m) for l in ls]
            den = functools.reduce(lambda a, b: a + b, es)
            num = functools.reduce(lambda a, b: a + b,
                                   [e * o_scr[g * HALVES + half, sl, :] for g, e in enumerate(es)])
            out_ref[sl, half * LANES:(half + 1) * LANES] = (num / den).astype(BF16)
        return carry

    lax.fori_loop(0, seq // chunk, merge, 0)


def _attn_call(qkv, rel_bias_table):
    B, S, _ = qkv[0].shape
    for (_, dil) in ATTN_CONFIGS:
        assert S % (dil * BLOCK) == 0
    bias = _attn_bias(rel_bias_table)
    seq_spec = pl.BlockSpec((None, S, GROUP_WIDTH), lambda b: (b, 0, 0))
    ordered = []
    for g in range(N_GROUPS):
        ordered += [qkv[g], qkv[3 + g], qkv[6 + g]]
    return pl.pallas_call(
        functools.partial(_attn_kernel, seq=S),
        grid=(B,),
        in_specs=[seq_spec] * 9 + [pl.BlockSpec(bias.shape, lambda b: (0, 0, 0))],
        out_specs=seq_spec,
        out_shape=jax.ShapeDtypeStruct((B, S, GROUP_WIDTH), BF16),
        scratch_shapes=[pltpu.VMEM((N_GROUPS * HALVES, S, LANES), F32),
                        pltpu.VMEM((N_GROUPS * HALVES, S, LANES), F32)],
        compiler_params=pltpu.CompilerParams(dimension_semantics=("parallel",),
                                             vmem_limit_bytes=VMEM_LIMIT_BYTES),
        name="attn",
    )(*ordered, bias)


PACK_CHUNKS = 4


def _pack_bf16_pairs(v):
    w = v.shape[1] // 2
    hi = lax.bitcast_convert_type(v[:, :w].astype(BF16).astype(F32), jnp.int32)
    lo = lax.bitcast_convert_type(v[:, w:].astype(BF16).astype(F32), jnp.int32)
    return hi | lax.shift_right_logical(lo, jnp.full_like(lo, 16))


def _unpack_bf16_pairs(words):
    hi = lax.bitcast_convert_type(words & jnp.int32(-65536), F32).astype(BF16)
    lo = lax.bitcast_convert_type(lax.shift_left(words, jnp.full_like(words, 16)), F32).astype(BF16)
    return jnp.concatenate([hi, lo], axis=1)


def _split_bf16(a):
    hi = a.astype(BF16)
    lo = (a - hi.astype(F32)).astype(BF16)
    return hi, lo


def _post_kernel(a_ref, u_ref, halo_ref, ga_ref, gb_ref, x_ref,
                 pa_ref, wpool_ref, pscale_ref, pb_ref, wout_ref, g1_ref, b1_ref,
                 wr_hi_ref, wr_lo_ref, br_ref,
                 h_ref, hb_ref, ids_ref, wts_ref, pool_scr, *, tm, alpha):
    i = pl.program_id(1)
    halo = halo_ref[...]
    pool_scr[0:POOL_HALO, :] = jnp.where(i > 0, halo, jnp.zeros_like(halo))
    pool_scr[POOL_HALO:, :] = u_ref[...]
    pos = i * tm + lax.broadcasted_iota(jnp.int32, (tm, POOL_GROUP_DIM), 0)
    mixed_in = []
    for gi, w in enumerate(POOL_SIZES):
        cols = slice(gi * POOL_GROUP_DIM, (gi + 1) * POOL_GROUP_DIM)
        ug = u_ref[:, cols]
        acc = ug
        for j in range(1, w):
            acc = acc + pool_scr[pl.ds(POOL_HALO - j, tm), cols]
        count = jnp.minimum(pos + 1, w).astype(F32)
        diff = (acc / count - ug).astype(BF16)
        yg = jnp.dot(diff, wpool_ref[gi], preferred_element_type=F32) * pscale_ref[:, cols]
        mixed_in.append(yg.astype(BF16))
    y_pool = jnp.dot(jnp.concatenate(mixed_in, axis=1), pb_ref[...], preferred_element_type=F32)
    y_attn = jnp.dot(a_ref[...], pa_ref[...], preferred_element_type=F32)
    mixed = ga_ref[...].astype(F32) * y_attn + gb_ref[...].astype(F32) * y_pool
    y = jnp.dot(mixed.astype(BF16), wout_ref[...], preferred_element_type=F32)
    h = _layer_norm(alpha * x_ref[...] + y, g1_ref[...], b1_ref[...])
    h_ref[...] = h
    packed = _pack_bf16_pairs(h)
    for c in range(PACK_CHUNKS):
        hb_ref[c] = packed[:, c * LANES:(c + 1) * LANES]

    h_hi, h_lo = _split_bf16(h)
    nt = (((1,), (1,)), ((), ()))
    logits = (lax.dot_general(h_hi, wr_hi_ref[...], nt, preferred_element_type=F32)
              + lax.dot_general(h_lo, wr_hi_ref[...], nt, preferred_element_type=F32)
              + lax.dot_general(h_hi, wr_lo_ref[...], nt, preferred_element_type=F32)) + br_ref[...]
    lane = lax.broadcasted_iota(jnp.int32, logits.shape, 1)
    big = jnp.int32(2 ** 30)
    is_group = lane < N_EXPERT_GROUPS
    gl = jnp.where(is_group, logits, -jnp.inf)
    gmax = jnp.max(gl, axis=1, keepdims=True)
    g_idx = jnp.min(jnp.where(gl == gmax, lane, big), axis=1, keepdims=True)
    g_prob = 1.0 / jnp.sum(jnp.exp(gl - gmax), axis=1, keepdims=True)
    expert = lane - N_EXPERT_GROUPS
    in_group = (expert >= g_idx * EXPERTS_PER_GROUP) & (expert < (g_idx + 1) * EXPERTS_PER_GROUP)
    el = jnp.where(in_group, logits, -jnp.inf)
    v1 = jnp.max(el, axis=1, keepdims=True)
    i1 = jnp.min(jnp.where(el == v1, expert, big), axis=1, keepdims=True)
    el2 = jnp.where(expert == i1, -jnp.inf, el)
    v2 = jnp.max(el2, axis=1, keepdims=True)
    i2 = jnp.min(jnp.where(el2 == v2, expert, big), axis=1, keepdims=True)
    e2 = jnp.exp(v2 - v1)
    den = 1.0 + e2
    ids_ref[...] = jnp.concatenate([i1, i2], axis=1)
    wts_ref[...] = jnp.concatenate([1.0 / den * g_prob, e2 / den * g_prob], axis=1)


def _post_call(a, u, ga, gb, x, w_proj_attn, w_pool, pool_scale, w_proj_pool, w_out, gamma, beta,
               w_router_group, b_router_group, w_router_expert, b_router_expert, alpha, tm=512):
    B, S, D = x.shape
    assert S % tm == 0 and tm % POOL_HALO == 0
    n_logits = N_EXPERT_GROUPS + N_EXPERTS
    wr = jnp.concatenate([w_router_group, w_router_expert], axis=1).T
    wr_hi = wr.astype(BF16)
    wr_lo = (wr - wr_hi.astype(F32)).astype(BF16)
    br = jnp.concatenate([b_router_group, b_router_expert]).reshape(1, n_logits)
    row_spec = lambda width: pl.BlockSpec((None, tm, width), lambda b, i: (b, i, 0))
    full = lambda arr: pl.BlockSpec(arr.shape, lambda b, i: (0,) * arr.ndim)
    halo_blocks = tm // POOL_HALO
    halo_spec = pl.BlockSpec((None, POOL_HALO, POOL_WIDTH),
                             lambda b, i: (b, jnp.maximum(i * halo_blocks - 1, 0), 0))
    weights = [w_proj_attn.astype(BF16), w_pool.astype(BF16), pool_scale.reshape(1, POOL_WIDTH),
               w_proj_pool.astype(BF16), w_out.astype(BF16), gamma.reshape(1, D), beta.reshape(1, D),
               wr_hi, wr_lo, br]
    return pl.pallas_call(
        functools.partial(_post_kernel, tm=tm, alpha=alpha),
        grid=(B, S // tm),
        in_specs=[row_spec(GROUP_WIDTH), row_spec(POOL_WIDTH), halo_spec, row_spec(D), row_spec(D),
                  row_spec(D)] + [full(w) for w in weights],
        out_specs=[row_spec(D),
                   pl.BlockSpec((PACK_CHUNKS, tm, LANES), lambda b, i: (0, b * (S // tm) + i, 0)),
                   row_spec(2), row_spec(2)],
        out_shape=[jax.ShapeDtypeStruct((B, S, D), F32),
                   jax.ShapeDtypeStruct((PACK_CHUNKS, B * S, LANES), jnp.int32),
                   jax.ShapeDtypeStruct((B, S, 2), jnp.int32), jax.ShapeDtypeStruct((B, S, 2), F32)],
        scratch_shapes=[pltpu.VMEM((tm + POOL_HALO, POOL_WIDTH), F32)],
        compiler_params=pltpu.CompilerParams(dimension_semantics=("parallel", "parallel"),
                                             vmem_limit_bytes=VMEM_LIMIT_BYTES),
        name="post",
    )(a, u, u, ga, gb, x, *weights)


def _rank_kernel(ids_ref, pos_ref, ends_ref, run_ref, start_ref, *, tm, tile_rows):
    phase = pl.program_id(0)
    i = pl.program_id(1)
    ids = ids_ref[...]
    lane = lax.broadcasted_iota(jnp.int32, (tm, LANES), 1)
    oh0 = lane == ids[:, 0:1]
    oh1 = lane == ids[:, 1:2]
    onehot = jnp.where(oh0 | oh1, 1.0, 0.0)
    tile_count = jnp.sum(onehot, axis=0, keepdims=True)

    @pl.when((phase == 0) & (i == 0))
    def _():
        run_ref[...] = jnp.zeros_like(run_ref)

    @pl.when(phase == 0)
    def _():
        run_ref[...] += tile_count

    @pl.when((phase == 1) & (i == 0))
    def _():
        padded = jnp.ceil(run_ref[...] / tile_rows) * tile_rows
        lane1 = lax.broadcasted_iota(jnp.int32, padded.shape, 1)
        incl = padded
        shift = 1
        while shift < LANES:
            incl = incl + jnp.where(lane1 >= shift, pltpu.roll(incl, shift, axis=1), 0.0)
            shift *= 2
        start_ref[...] = incl - padded
        ends_ref[...] = incl
        run_ref[...] = jnp.zeros_like(run_ref)

    @pl.when(phase == 1)
    def _():
        row = lax.broadcasted_iota(jnp.int32, (tm, tm), 0)
        col = lax.broadcasted_iota(jnp.int32, (tm, tm), 1)
        earlier = jnp.where(row > col, 1.0, 0.0).astype(BF16)
        before = jnp.dot(earlier, onehot.astype(BF16), preferred_element_type=F32)
        slot = start_ref[0:1, :] + run_ref[0:1, :] + before
        p0 = jnp.sum(jnp.where(oh0, slot, 0.0), axis=1, keepdims=True)
        p1 = jnp.sum(jnp.where(oh1, slot, 0.0), axis=1, keepdims=True)
        pos_ref[...] = jnp.concatenate([p0, p1], axis=1).astype(jnp.int32)
        run_ref[...] += tile_count


def _rank_call(ids, tile_rows, tm=1024):
    N = ids.shape[0]
    assert N % tm == 0 and 2 * N + N_EXPERTS * tile_rows < 2 ** 24
    return pl.pallas_call(
        functools.partial(_rank_kernel, tm=tm, tile_rows=tile_rows),
        grid=(2, N // tm),
        in_specs=[pl.BlockSpec((tm, 2), lambda p, i: (i, 0))],
        out_specs=[pl.BlockSpec((tm, 2), lambda p, i: (i * p, 0)),
                   pl.BlockSpec((8, LANES), lambda p, i: (0, 0))],
        out_shape=[jax.ShapeDtypeStruct((N, 2), jnp.int32), jax.ShapeDtypeStruct((8, LANES), F32)],
        scratch_shapes=[pltpu.VMEM((8, LANES), F32), pltpu.VMEM((8, LANES), F32)],
        compiler_params=pltpu.CompilerParams(dimension_semantics=("arbitrary", "arbitrary"),
                                             vmem_limit_bytes=VMEM_LIMIT_BYTES),
        name="rank",
    )(ids)


SC_CORES = 2
SC_SUBCORES = 16
SC_WORKERS = SC_CORES * SC_SUBCORES
SC_CHUNK = 128


def _sc_mesh():
    return plsc.VectorSubcoreMesh(core_axis_name="c", subcore_axis_name="s",
                                  num_cores=SC_CORES, num_subcores=SC_SUBCORES)


def _sc_dispatch(packed, pos_t, n_rows):
    n_chunks, n_tok, width = packed.shape
    per_worker = n_tok // SC_WORKERS
    assert n_tok % (SC_WORKERS * SC_CHUNK) == 0

    @functools.partial(
        pl.kernel, mesh=_sc_mesh(),
        out_type=jax.ShapeDtypeStruct((n_chunks, n_rows, width), packed.dtype),
        scratch_types=[pltpu.VMEM((SC_CHUNK,), jnp.int32), pltpu.VMEM((SC_CHUNK,), jnp.int32),
                       pltpu.VMEM((SC_CHUNK, width), packed.dtype)],
        name="sc_dispatch")
    def run(packed_hbm, pos_hbm, out_hbm, idx0, idx1, rows):
        worker = lax.axis_index("s") * SC_CORES + lax.axis_index("c")

        @pl.loop(0, per_worker // SC_CHUNK)
        def _(j):
            base = worker * per_worker + j * SC_CHUNK
            pltpu.sync_copy(pos_hbm.at[0, pl.ds(base, SC_CHUNK)], idx0)
            pltpu.sync_copy(pos_hbm.at[1, pl.ds(base, SC_CHUNK)], idx1)
            for c in range(n_chunks):
                pltpu.sync_copy(packed_hbm.at[c, pl.ds(base, SC_CHUNK)], rows)
                pltpu.sync_copy(rows, out_hbm.at[c].at[idx0])
                pltpu.sync_copy(rows, out_hbm.at[c].at[idx1])

    return run(packed, pos_t)


def _sc_combine(sorted_rows, pos_t):
    n_chunks, _, width = sorted_rows.shape
    n_tok = pos_t.shape[1]
    per_worker = n_tok // SC_WORKERS
    assert n_tok % (SC_WORKERS * SC_CHUNK) == 0

    @functools.partial(
        pl.kernel, mesh=_sc_mesh(),
        out_type=jax.ShapeDtypeStruct((2, n_chunks, n_tok, width), sorted_rows.dtype),
        scratch_types=[pltpu.VMEM((SC_CHUNK,), jnp.int32),
                       pltpu.VMEM((SC_CHUNK, width), sorted_rows.dtype)],
        name="sc_combine")
    def run(rows_hbm, pos_hbm, out_hbm, idx, rows):
        worker = lax.axis_index("s") * SC_CORES + lax.axis_index("c")

        @pl.loop(0, per_worker // SC_CHUNK)
        def _(j):
            base = worker * per_worker + j * SC_CHUNK
            for k in range(2):
                pltpu.sync_copy(pos_hbm.at[k, pl.ds(base, SC_CHUNK)], idx)
                for c in range(n_chunks):
                    pltpu.sync_copy(rows_hbm.at[c].at[idx], rows)
                    pltpu.sync_copy(rows, out_hbm.at[k, c, pl.ds(base, SC_CHUNK)])

    return run(sorted_rows, pos_t)


def _expert_kernel(tile_expert_ref, n_used_ref, xs_ref, wg_ref, wu_ref, wd_ref, ys_ref):
    del tile_expert_ref

    @pl.when(pl.program_id(0) < n_used_ref[0])
    def _():
        x = _unpack_bf16_pairs(jnp.concatenate([xs_ref[c] for c in range(PACK_CHUNKS)], axis=1))
        gate = jnp.dot(x, wg_ref[...], preferred_element_type=F32)
        up = jnp.dot(x, wu_ref[...], preferred_element_type=F32)
        hidden = (jax.nn.silu(gate) * up).astype(BF16)
        y = _pack_bf16_pairs(jnp.dot(hidden, wd_ref[...], preferred_element_type=F32))
        for c in range(PACK_CHUNKS):
            ys_ref[c] = y[:, c * LANES:(c + 1) * LANES]


def _expert_call(xs, tile_expert, n_used, w_gate, w_up, w_down, tile_rows):
    _, n_rows, _ = xs.shape
    D = w_gate.shape[1]
    row_block = pl.BlockSpec((PACK_CHUNKS, tile_rows, LANES),
                             lambda i, te, nu: (0, jnp.minimum(i, nu[0] - 1), 0))
    return pl.pallas_call(
        _expert_kernel,
        grid_spec=pltpu.PrefetchScalarGridSpec(
            num_scalar_prefetch=2,
            grid=(n_rows // tile_rows,),
            in_specs=[row_block,
                      pl.BlockSpec((None, D, D_EXPERT), lambda i, te, nu: (te[i], 0, 0)),
                      pl.BlockSpec((None, D, D_EXPERT), lambda i, te, nu: (te[i], 0, 0)),
                      pl.BlockSpec((None, D_EXPERT, D), lambda i, te, nu: (te[i], 0, 0))],
            out_specs=row_block),
        out_shape=jax.ShapeDtypeStruct(xs.shape, xs.dtype),
        compiler_params=pltpu.CompilerParams(dimension_semantics=("arbitrary",),
                                             vmem_limit_bytes=VMEM_LIMIT_BYTES),
        name="experts",
    )(tile_expert, n_used, xs, w_gate.astype(BF16), w_up.astype(BF16), w_down.astype(BF16))


def _final_kernel(h_ref, y_ref, wts_ref, g2_ref, b2_ref, out_ref, *, alpha):
    wts = wts_ref[...]
    z = alpha * h_ref[...]
    for k in range(2):
        yk = _unpack_bf16_pairs(jnp.concatenate([y_ref[k, c] for c in range(PACK_CHUNKS)], axis=1))
        z = z + wts[:, k:k + 1] * yk.astype(F32)
    out_ref[...] = _layer_norm(z, g2_ref[...], b2_ref[...])


def _final_call(h, y, wts, gamma, beta, alpha, tm=512):
    N, D = h.shape
    assert N % tm == 0
    return pl.pallas_call(
        functools.partial(_final_kernel, alpha=alpha),
        grid=(N // tm,),
        in_specs=[pl.BlockSpec((tm, D), lambda i: (i, 0)),
                  pl.BlockSpec((2, PACK_CHUNKS, tm, LANES), lambda i: (0, 0, i, 0)),
                  pl.BlockSpec((tm, 2), lambda i: (i, 0)),
                  pl.BlockSpec((1, D), lambda i: (0, 0)),
                  pl.BlockSpec((1, D), lambda i: (0, 0))],
        out_specs=pl.BlockSpec((tm, D), lambda i: (i, 0)),
        out_shape=jax.ShapeDtypeStruct((N, D), F32),
        compiler_params=pltpu.CompilerParams(dimension_semantics=("parallel",),
                                             vmem_limit_bytes=VMEM_LIMIT_BYTES),
        name="final",
    )(h, y, wts, gamma.reshape(1, D), beta.reshape(1, D))


def _moe(h, packed, ids, wts, w_gate, w_up, w_down, gamma, beta, alpha, tile_rows=512):
    N = h.shape[0]
    pos, ends = _rank_call(ids, tile_rows)
    pos_t = pos.T
    n_tiles = 2 * N // tile_rows + N_EXPERTS
    seg_end = ends[0, :N_EXPERTS].astype(jnp.int32)
    tile_start = jnp.arange(n_tiles, dtype=jnp.int32) * tile_rows
    tile_expert = jnp.minimum(jnp.sum(seg_end[None, :] <= tile_start[:, None], axis=1),
                              N_EXPERTS - 1).astype(jnp.int32)
    n_used = (seg_end[N_EXPERTS - 1:] // tile_rows).astype(jnp.int32)
    xs = _sc_dispatch(packed, pos_t, n_tiles * tile_rows)
    ys = _expert_call(xs, tile_expert, n_used, w_gate, w_up, w_down, tile_rows)
    y = _sc_combine(ys, pos_t)
    return _final_call(h, y, wts, gamma, beta, alpha)


@jax.jit
def kernel(x, w_in, b_in, rel_bias_table, w_pool, pool_scale, w_proj_attn, w_proj_pool, w_out, ln1_gamma, ln1_beta, w_router_group, b_router_group, w_router_expert, b_router_expert, w_expert_gate, w_expert_up, w_expert_down, ln2_gamma, ln2_beta):
    B, S, D = x.shape
    depth = w_in.shape[0]
    alpha = (2.0 * depth) ** 0.25
    for layer in range(depth):
        qkv, u, ga, gb = _proj_call(x, w_in[layer], b_in[layer])
        a = _attn_call(qkv, rel_bias_table)
        h, hb, ids, wts = _post_call(a, u, ga, gb, x, w_proj_attn[layer], w_pool[layer], pool_scale[layer],
                                     w_proj_pool[layer], w_out[layer], ln1_gamma[layer], ln1_beta[layer],
                                     w_router_group[layer], b_router_group[layer],
                                     w_router_expert[layer], b_router_expert[layer], alpha)
        out = _moe(h.reshape(B * S, D), hb, ids.reshape(B * S, 2),
                   wts.reshape(B * S, 2), w_expert_gate[layer], w_expert_up[layer],
                   w_expert_down[layer], ln2_gamma[layer], ln2_beta[layer], alpha)
        x = out.reshape(B, S, D)
    return x
```

```python
import functools
import math

import jax
import jax.numpy as jnp
import numpy as np
from jax import lax
from jax.experimental import pallas as pl
from jax.experimental.pallas import tpu as pltpu
from jax.experimental.pallas import tpu_sc as plsc

F32 = jnp.float32
BF16 = jnp.bfloat16

HEAD_DIM = 64
ATTN_CONFIGS = ((128, 1), (512, 4), (2048, 16))
N_GROUPS = len(ATTN_CONFIGS)
HEADS_PER_GROUP = 4
GROUP_WIDTH = HEADS_PER_GROUP * HEAD_DIM
ATTN_WIDTH = N_GROUPS * GROUP_WIDTH
BLOCK = 128
N_REL_BUCKETS = 32
REL_MAX_DISTANCE = 2048
NEG_INF = -1e30

POOL_SIZES = (2, 4, 8, 16)
POOL_GROUP_DIM = 128
POOL_WIDTH = POOL_GROUP_DIM * len(POOL_SIZES)
POOL_HALO = 16

N_EXPERT_GROUPS = 4
EXPERTS_PER_GROUP = 8
N_EXPERTS = N_EXPERT_GROUPS * EXPERTS_PER_GROUP
D_EXPERT = 256
LN_EPS = 1e-5

VMEM_LIMIT_BYTES = 56 * 1024 * 1024
LANES = 128
HALVES = GROUP_WIDTH // LANES


def _layer_norm(z, gamma, beta):
    mu = jnp.mean(z, axis=-1, keepdims=True)
    zc = z - mu
    var = jnp.mean(zc * zc, axis=-1, keepdims=True)
    return zc * lax.rsqrt(var + LN_EPS) * gamma + beta


def _proj_kernel(x_ref, w_ref, b_ref, *refs, tm, d_model):
    qkv_refs = refs[:9]
    u_ref, ga_ref, gb_ref, xb_ref, acc_ref = refs[9:]
    xb_ref[...] = x_ref[...].astype(BF16)

    def chunk(c0, width):
        acc = jnp.dot(xb_ref[...], w_ref[:, c0:c0 + width], preferred_element_type=F32)
        return acc + b_ref[:, c0:c0 + width]

    for which in range(3):
        scale = HEAD_DIM ** -0.5 if which == 0 else 1.0
        for g, (_, dil) in enumerate(ATTN_CONFIGS):
            out = qkv_refs[which * 3 + g]
            acc = chunk(which * ATTN_WIDTH + g * GROUP_WIDTH, GROUP_WIDTH) * scale
            if dil == 1:
                out[0] = acc.astype(BF16)
            else:
                for half in range(HALVES):
                    acc_ref[half] = acc[:, half * LANES:(half + 1) * LANES]
                for r in range(dil):
                    for half in range(HALVES):
                        out[r, :, half * LANES:(half + 1) * LANES] = (
                            acc_ref[half, pl.ds(r, tm // dil, stride=dil), :].astype(BF16))
    pool_off = 3 * ATTN_WIDTH
    for c in range(POOL_WIDTH // 256):
        u_ref[:, c * 256:(c + 1) * 256] = chunk(pool_off + c * 256, 256)
    ga_off = pool_off + POOL_WIDTH
    for gate_ref, off in ((ga_ref, ga_off), (gb_ref, ga_off + d_model)):
        for c in range(d_model // 256):
            gate_ref[:, c * 256:(c + 1) * 256] = jax.nn.sigmoid(chunk(off + c * 256, 256)).astype(BF16)


def _proj_call(x, w_in, b_in, tm=512):
    B, S, D = x.shape
    in_width = w_in.shape[1]
    assert in_width == 3 * ATTN_WIDTH + POOL_WIDTH + 2 * D
    assert S % tm == 0
    grid = (B, S // tm)
    qkv_shapes, qkv_specs = [], []
    for _ in range(3):
        for (_, dil) in ATTN_CONFIGS:
            assert tm % (dil * 16) == 0
            qkv_shapes.append(jax.ShapeDtypeStruct((B, dil, S // dil, GROUP_WIDTH), BF16))
            qkv_specs.append(pl.BlockSpec((None, dil, tm // dil, GROUP_WIDTH), lambda b, i: (b, 0, i, 0)))
    row_spec = lambda width: pl.BlockSpec((None, tm, width), lambda b, i: (b, i, 0))
    out_shape = qkv_shapes + [jax.ShapeDtypeStruct((B, S, POOL_WIDTH), F32),
                              jax.ShapeDtypeStruct((B, S, D), BF16),
                              jax.ShapeDtypeStruct((B, S, D), BF16)]
    out_specs = qkv_specs + [row_spec(POOL_WIDTH), row_spec(D), row_spec(D)]
    outs = pl.pallas_call(
        functools.partial(_proj_kernel, tm=tm, d_model=D),
        grid=grid,
        in_specs=[row_spec(D),
                  pl.BlockSpec((D, in_width), lambda b, i: (0, 0)),
                  pl.BlockSpec((1, in_width), lambda b, i: (0, 0))],
        out_specs=out_specs,
        out_shape=out_shape,
        scratch_shapes=[pltpu.VMEM((tm, D), BF16), pltpu.VMEM((HALVES, tm, LANES), F32)],
        compiler_params=pltpu.CompilerParams(dimension_semantics=("parallel", "parallel"),
                                             vmem_limit_bytes=VMEM_LIMIT_BYTES),
        name="proj",
    )(x, w_in.astype(BF16), b_in.reshape(1, in_width))
    qkv = [o.reshape(B, S, GROUP_WIDTH) for o in outs[:9]]
    return qkv, outs[9], outs[10], outs[11]


def _t5_causal_bucket(dist):
    max_exact = N_REL_BUCKETS // 2
    is_small = dist < max_exact
    d = jnp.maximum(dist, 1).astype(F32)
    large = max_exact + (jnp.log(d / max_exact) / math.log(REL_MAX_DISTANCE / max_exact)
                         * (N_REL_BUCKETS - max_exact)).astype(jnp.int32)
    large = jnp.minimum(large, N_REL_BUCKETS - 1)
    return jnp.where(is_small, dist, large)


def _attn_bias(rel_bias_table):
    full, first = [], []
    for g, (window, dil) in enumerate(ATTN_CONFIGS):
        span = window // dil
        table = rel_bias_table[:, g * HEADS_PER_GROUP:(g + 1) * HEADS_PER_GROUP].astype(F32)
        lq = np.arange(BLOCK)
        for lk, dst in ((np.arange(-BLOCK, BLOCK), full), (lq, first)):
            step = jnp.asarray(lq[:, None] - lk[None, :], jnp.int32)
            in_window = (step >= 0) & (step <= span)
            bucket = _t5_causal_bucket(jnp.clip(step, 0, span) * dil)
            bias = jnp.einsum('qkb,bh->hqk', jax.nn.one_hot(bucket, N_REL_BUCKETS, dtype=F32), table,
                              precision=lax.Precision.HIGHEST)
            bias = jnp.where(in_window[None], bias, NEG_INF)
            dst.append(bias.reshape(HEADS_PER_GROUP * BLOCK, lk.shape[0]))
    return jnp.stack(full), jnp.stack(first)


def _attn_kernel(*refs, seq):
    qkv = refs[:9]
    bias_ref, bias_first_ref, out_ref, o_scr, l_scr = refs[9:]
    rows = HEADS_PER_GROUP * BLOCK
    row_head = lax.broadcasted_iota(jnp.int32, (rows, GROUP_WIDTH), 0) // BLOCK
    lane_head_r = lax.broadcasted_iota(jnp.int32, (rows, GROUP_WIDTH), 1) // HEAD_DIM
    head_mask = row_head == lane_head_r
    lane_head = lax.broadcasted_iota(jnp.int32, (BLOCK, GROUP_WIDTH), 1) // HEAD_DIM

    def per_head_to_lanes(stacked):
        out = stacked[0:BLOCK]
        for h in range(1, HEADS_PER_GROUP):
            out = jnp.where(lane_head == h, stacked[h * BLOCK:(h + 1) * BLOCK], out)
        return out

    for g, (_, dil) in enumerate(ATTN_CONFIGS):
        q_ref, k_ref, v_ref = qkv[3 * g:3 * g + 3]
        sub_len = seq // dil
        n_blocks = sub_len // BLOCK

        def block(r, n, first, g=g, dil=dil, q_ref=q_ref, k_ref=k_ref, v_ref=v_ref, sub_len=sub_len):
            base = pl.multiple_of(r * sub_len + n * BLOCK, BLOCK)
            qb = q_ref[pl.ds(base, BLOCK), :]
            if first:
                kk = k_ref[pl.ds(base, BLOCK), :]
                vv = v_ref[pl.ds(base, BLOCK), :]
                bias = bias_first_ref[g]
            else:
                kbase = pl.multiple_of(base - BLOCK, BLOCK)
                kk = k_ref[pl.ds(kbase, 2 * BLOCK), :]
                vv = v_ref[pl.ds(kbase, 2 * BLOCK), :]
                bias = bias_ref[g]
            qs = jnp.where(head_mask, jnp.concatenate([qb] * HEADS_PER_GROUP, axis=0), jnp.zeros((), BF16))
            logits = lax.dot_general(qs, kk, (((1,), (1,)), ((), ())), preferred_element_type=F32) + bias
            m = jnp.max(logits, axis=1, keepdims=True)
            p = jnp.exp(logits - m)
            s = jnp.sum(p, axis=1, keepdims=True)
            pv = jnp.dot(p.astype(BF16), vv, preferred_element_type=F32)
            o = per_head_to_lanes(pv * (1.0 / s))
            lse = per_head_to_lanes(jnp.broadcast_to(m + jnp.log(s), (rows, GROUP_WIDTH)))
            start = n * (BLOCK * dil) + r
            if dil == 1:
                dst = pl.ds(pl.multiple_of(start, BLOCK), BLOCK)
            else:
                dst = pl.ds(start, BLOCK, stride=dil)
            for half in range(HALVES):
                cols = slice(half * LANES, (half + 1) * LANES)
                o_scr[g * HALVES + half, dst, :] = o[:, cols]
                l_scr[g * HALVES + half, dst, :] = lse[:, cols]

        def per_subsequence(r, carry, block=block, n_blocks=n_blocks):
            block(r, 0, True)
            if n_blocks > 1:
                def per_block(n, c):
                    block(r, n, False)
                    return c
                lax.fori_loop(1, n_blocks, per_block, 0, unroll=3 if (n_blocks - 1) % 3 == 0 else 1)
            return carry

        lax.fori_loop(0, dil, per_subsequence, 0, unroll=2 if (n_blocks == 1 and dil % 2 == 0) else 1)

    chunk = 256

    def merge(i, carry):
        sl = pl.ds(pl.multiple_of(i * chunk, chunk), chunk)
        for half in range(HALVES):
            ls = [l_scr[g * HALVES + half, sl, :] for g in range(N_GROUPS)]
            m = functools.reduce(jnp.maximum, ls)
            es = [jnp.exp(l - m) for l in ls]
            den = functools.reduce(lambda a, b: a + b, es)
            num = functools.reduce(lambda a, b: a + b,
                                   [e * o_scr[g * HALVES + half, sl, :] for g, e in enumerate(es)])
            out_ref[sl, half * LANES:(half + 1) * LANES] = (num / den).astype(BF16)
        return carry

    lax.fori_loop(0, seq // chunk, merge, 0)


def _attn_call(qkv, rel_bias_table):
    B, S, _ = qkv[0].shape
    for (_, dil) in ATTN_CONFIGS:
        assert S % (dil * BLOCK) == 0
    bias, bias_first = _attn_bias(rel_bias_table)
    seq_spec = pl.BlockSpec((None, S, GROUP_WIDTH), lambda b: (b, 0, 0))
    ordered = []
    for g in range(N_GROUPS):
        ordered += [qkv[g], qkv[3 + g], qkv[6 + g]]
    return pl.pallas_call(
        functools.partial(_attn_kernel, seq=S),
        grid=(B,),
        in_specs=[seq_spec] * 9 + [pl.BlockSpec(bias.shape, lambda b: (0, 0, 0)),
                                   pl.BlockSpec(bias_first.shape, lambda b: (0, 0, 0))],
        out_specs=seq_spec,
        out_shape=jax.ShapeDtypeStruct((B, S, GROUP_WIDTH), BF16),
        scratch_shapes=[pltpu.VMEM((N_GROUPS * HALVES, S, LANES), F32),
                        pltpu.VMEM((N_GROUPS * HALVES, S, LANES), F32)],
        compiler_params=pltpu.CompilerParams(dimension_semantics=("parallel",),
                                             vmem_limit_bytes=VMEM_LIMIT_BYTES),
        name="attn",
    )(*ordered, bias, bias_first)


PACK_CHUNKS = 4
SUBLANES = 8
ROUTER_LANES = LANES // 2


def _pack_bf16_pairs(rounded):
    w = rounded.shape[1] // 2
    bits = lax.bitcast_convert_type(rounded, jnp.int32)
    return bits[:, :w] | lax.shift_right_logical(bits[:, w:], jnp.full((), 16, jnp.int32))


def _unpack_bf16_pairs(words):
    hi = lax.bitcast_convert_type(words & jnp.int32(-65536), F32).astype(BF16)
    lo = lax.bitcast_convert_type(lax.shift_left(words, jnp.full((), 16, jnp.int32)), F32).astype(BF16)
    return jnp.concatenate([hi, lo], axis=1)


def _post_kernel(a_ref, u_ref, halo_ref, ga_ref, gb_ref, x_ref,
                 pa_ref, wpool_ref, pscale_ref, pb_ref, wout_ref, g1_ref, b1_ref,
                 wr_cat_ref, br_ref,
                 h_ref, hb_ref, ids_ref, wts_ref, cnt_ref, pool_scr, *, tm, alpha):
    i = pl.program_id(1)
    halo = halo_ref[...]
    pool_scr[0:POOL_HALO, :] = jnp.where(i > 0, halo, jnp.zeros_like(halo))
    pool_scr[POOL_HALO:, :] = u_ref[...]
    head_pos = i * tm + lax.broadcasted_iota(jnp.int32, (POOL_HALO, POOL_GROUP_DIM), 0)
    mixed_in = []
    for gi, w in enumerate(POOL_SIZES):
        cols = slice(gi * POOL_GROUP_DIM, (gi + 1) * POOL_GROUP_DIM)
        ug = u_ref[:, cols]
        acc = ug
        for j in range(1, w):
            acc = acc + pool_scr[pl.ds(POOL_HALO - j, tm), cols]
        inv_count = jnp.concatenate(
            [1.0 / jnp.minimum(head_pos + 1, w).astype(F32),
             jnp.full((tm - POOL_HALO, POOL_GROUP_DIM), 1.0 / w, F32)], axis=0)
        diff = (acc * inv_count - ug).astype(BF16)
        yg = jnp.dot(diff, wpool_ref[gi], preferred_element_type=F32) * pscale_ref[:, cols]
        mixed_in.append(yg.astype(BF16))
    y_pool = jnp.dot(jnp.concatenate(mixed_in, axis=1), pb_ref[...], preferred_element_type=F32)
    y_attn = jnp.dot(a_ref[...], pa_ref[...], preferred_element_type=F32)
    mixed = ga_ref[...].astype(F32) * y_attn + gb_ref[...].astype(F32) * y_pool
    y = jnp.dot(mixed.astype(BF16), wout_ref[...], preferred_element_type=F32)
    h = _layer_norm(alpha * x_ref[...] + y, g1_ref[...], b1_ref[...])
    h_ref[...] = h
    h_hi = h.astype(BF16)
    h_rounded = h_hi.astype(F32)
    packed = _pack_bf16_pairs(h_rounded)
    for c in range(PACK_CHUNKS):
        hb_ref[c] = packed[:, c * LANES:(c + 1) * LANES]

    h_lo = (h - h_rounded).astype(BF16)
    nt = (((1,), (1,)), ((), ()))
    both = lax.dot_general(h_hi, wr_cat_ref[...], nt, preferred_element_type=F32)
    logits = (both + pltpu.roll(both, ROUTER_LANES, axis=1)
              + lax.dot_general(h_lo, wr_cat_ref[...], nt, preferred_element_type=F32)) + br_ref[...]
    lane = lax.broadcasted_iota(jnp.int32, logits.shape, 1)
    big = jnp.int32(2 ** 30)
    is_group = lane < N_EXPERT_GROUPS
    gl = jnp.where(is_group, logits, -jnp.inf)
    gmax = jnp.max(gl, axis=1, keepdims=True)
    g_idx = jnp.min(jnp.where(gl == gmax, lane, big), axis=1, keepdims=True)
    g_prob = 1.0 / jnp.sum(jnp.exp(gl - gmax), axis=1, keepdims=True)
    expert = lane - N_EXPERT_GROUPS
    in_group = (expert >= g_idx * EXPERTS_PER_GROUP) & (expert < (g_idx + 1) * EXPERTS_PER_GROUP)
    el = jnp.where(in_group, logits, -jnp.inf)
    v1 = jnp.max(el, axis=1, keepdims=True)
    i1 = jnp.min(jnp.where(el == v1, expert, big), axis=1, keepdims=True)
    el2 = jnp.where(expert == i1, -jnp.inf, el)
    v2 = jnp.max(el2, axis=1, keepdims=True)
    i2 = jnp.min(jnp.where(el2 == v2, expert, big), axis=1, keepdims=True)
    e2 = jnp.exp(v2 - v1)
    den = 1.0 + e2
    ids_ref[...] = jnp.concatenate([i1, i2], axis=1)
    wts_ref[...] = jnp.concatenate([1.0 / den * g_prob, e2 / den * g_prob], axis=1)
    chosen = jnp.where((lane == i1) | (lane == i2), 1.0, 0.0)
    cnt_ref[...] = jnp.broadcast_to(jnp.sum(chosen, axis=0, keepdims=True), cnt_ref.shape)


def _post_call(a, u, ga, gb, x, w_proj_attn, w_pool, pool_scale, w_proj_pool, w_out, gamma, beta,
               w_router_group, b_router_group, w_router_expert, b_router_expert, alpha, tm=512):
    B, S, D = x.shape
    assert S % tm == 0 and tm % POOL_HALO == 0
    n_logits = N_EXPERT_GROUPS + N_EXPERTS
    assert n_logits <= ROUTER_LANES
    wr = jnp.concatenate([w_router_group, w_router_expert], axis=1).T
    wr = jnp.pad(wr, ((0, ROUTER_LANES - n_logits), (0, 0)))
    wr_hi = wr.astype(BF16)
    wr_lo = (wr - wr_hi.astype(F32)).astype(BF16)
    wr_cat = jnp.concatenate([wr_hi, wr_lo], axis=0)
    br = jnp.pad(jnp.concatenate([b_router_group, b_router_expert]),
                 (0, 2 * ROUTER_LANES - n_logits)).reshape(1, 2 * ROUTER_LANES)
    n_tiles = B * (S // tm)
    row_spec = lambda width: pl.BlockSpec((None, tm, width), lambda b, i: (b, i, 0))
    full = lambda arr: pl.BlockSpec(arr.shape, lambda b, i: (0,) * arr.ndim)
    halo_blocks = tm // POOL_HALO
    halo_spec = pl.BlockSpec((None, POOL_HALO, POOL_WIDTH),
                             lambda b, i: (b, jnp.maximum(i * halo_blocks - 1, 0), 0))
    weights = [w_proj_attn.astype(BF16), w_pool.astype(BF16), pool_scale.reshape(1, POOL_WIDTH),
               w_proj_pool.astype(BF16), w_out.astype(BF16), gamma.reshape(1, D), beta.reshape(1, D),
               wr_cat, br]
    return pl.pallas_call(
        functools.partial(_post_kernel, tm=tm, alpha=alpha),
        grid=(B, S // tm),
        in_specs=[row_spec(GROUP_WIDTH), row_spec(POOL_WIDTH), halo_spec, row_spec(D), row_spec(D),
                  row_spec(D)] + [full(w) for w in weights],
        out_specs=[row_spec(D),
                   pl.BlockSpec((PACK_CHUNKS, tm, LANES), lambda b, i: (0, b * (S // tm) + i, 0)),
                   row_spec(2), row_spec(2),
                   pl.BlockSpec((SUBLANES, LANES), lambda b, i: (b * (S // tm) + i, 0))],
        out_shape=[jax.ShapeDtypeStruct((B, S, D), F32),
                   jax.ShapeDtypeStruct((PACK_CHUNKS, B * S, LANES), jnp.int32),
                   jax.ShapeDtypeStruct((B, S, 2), jnp.int32), jax.ShapeDtypeStruct((B, S, 2), F32),
                   jax.ShapeDtypeStruct((n_tiles * SUBLANES, LANES), F32)],
        scratch_shapes=[pltpu.VMEM((tm + POOL_HALO, POOL_WIDTH), F32)],
        compiler_params=pltpu.CompilerParams(dimension_semantics=("parallel", "parallel"),
                                             vmem_limit_bytes=VMEM_LIMIT_BYTES),
        name="post",
    )(a, u, u, ga, gb, x, *weights)


def _rank_kernel(ids_ref, cnt_ref, pos_ref, ends_ref, run_ref, start_ref, *, tm, tile_rows):
    i = pl.program_id(0)

    @pl.when(i == 0)
    def _():
        total = jnp.sum(cnt_ref[...], axis=0, keepdims=True) / SUBLANES
        padded = jnp.broadcast_to(jnp.ceil(total / tile_rows) * tile_rows, run_ref.shape)
        lane1 = lax.broadcasted_iota(jnp.int32, padded.shape, 1)
        incl = padded
        shift = 1
        while shift < LANES:
            incl = incl + jnp.where(lane1 >= shift, pltpu.roll(incl, shift, axis=1), 0.0)
            shift *= 2
        start_ref[...] = incl - padded
        ends_ref[...] = incl
        run_ref[...] = jnp.zeros_like(run_ref)

    ids = ids_ref[...]
    lane = lax.broadcasted_iota(jnp.int32, (tm, LANES), 1)
    oh0 = lane == ids[:, 0:1]
    oh1 = lane == ids[:, 1:2]
    onehot = jnp.where(oh0 | oh1, 1.0, 0.0)
    before = jnp.dot(_strict_lower_ones(tm), onehot.astype(BF16), preferred_element_type=F32)
    slot = start_ref[0:1, :] + run_ref[0:1, :] + before
    p0 = jnp.sum(jnp.where(oh0, slot, 0.0), axis=1, keepdims=True)
    p1 = jnp.sum(jnp.where(oh1, slot, 0.0), axis=1, keepdims=True)
    pos_ref[...] = jnp.concatenate([p0, p1], axis=1).astype(jnp.int32)
    run_ref[...] += jnp.sum(onehot, axis=0, keepdims=True)


def _strict_lower_ones(tm):
    row = lax.broadcasted_iota(jnp.int32, (tm, tm), 0)
    col = lax.broadcasted_iota(jnp.int32, (tm, tm), 1)
    return jnp.where(row > col, 1.0, 0.0).astype(BF16)


def _rank_call(ids, counts, tile_rows, tm=1024):
    N = ids.shape[0]
    assert N % tm == 0 and SUBLANES * (2 * N + N_EXPERTS * tile_rows) < 2 ** 24
    return pl.pallas_call(
        functools.partial(_rank_kernel, tm=tm, tile_rows=tile_rows),
        grid=(N // tm,),
        in_specs=[pl.BlockSpec((tm, 2), lambda i: (i, 0)),
                  pl.BlockSpec(counts.shape, lambda i: (0, 0))],
        out_specs=[pl.BlockSpec((tm, 2), lambda i: (i, 0)),
                   pl.BlockSpec((SUBLANES, LANES), lambda i: (0, 0))],
        out_shape=[jax.ShapeDtypeStruct((N, 2), jnp.int32), jax.ShapeDtypeStruct((SUBLANES, LANES), F32)],
        scratch_shapes=[pltpu.VMEM((SUBLANES, LANES), F32), pltpu.VMEM((SUBLANES, LANES), F32)],
        compiler_params=pltpu.CompilerParams(dimension_semantics=("arbitrary",),
                                             vmem_limit_bytes=VMEM_LIMIT_BYTES),
        name="rank",
    )(ids, counts)


SC_CORES = 2
SC_SUBCORES = 16
SC_WORKERS = SC_CORES * SC_SUBCORES
SC_CHUNK = 128


def _sc_mesh():
    return plsc.VectorSubcoreMesh(core_axis_name="c", subcore_axis_name="s",
                                  num_cores=SC_CORES, num_subcores=SC_SUBCORES)


def _sc_dispatch(packed, pos_t, n_rows):
    n_chunks, n_tok, width = packed.shape
    per_worker = n_tok // SC_WORKERS
    assert n_tok % (SC_WORKERS * SC_CHUNK) == 0

    @functools.partial(
        pl.kernel, mesh=_sc_mesh(),
        out_type=jax.ShapeDtypeStruct((n_chunks, n_rows, width), packed.dtype),
        scratch_types=[pltpu.VMEM((SC_CHUNK,), jnp.int32), pltpu.VMEM((SC_CHUNK,), jnp.int32),
                       pltpu.VMEM((SC_CHUNK, width), packed.dtype)],
        name="sc_dispatch")
    def run(packed_hbm, pos_hbm, out_hbm, idx0, idx1, rows):
        worker = lax.axis_index("s") * SC_CORES + lax.axis_index("c")

        @pl.loop(0, per_worker // SC_CHUNK)
        def _(j):
            base = worker * per_worker + j * SC_CHUNK
            pltpu.sync_copy(pos_hbm.at[0, pl.ds(base, SC_CHUNK)], idx0)
            pltpu.sync_copy(pos_hbm.at[1, pl.ds(base, SC_CHUNK)], idx1)
            for c in range(n_chunks):
                pltpu.sync_copy(packed_hbm.at[c, pl.ds(base, SC_CHUNK)], rows)
                pltpu.sync_copy(rows, out_hbm.at[c].at[idx0])
                pltpu.sync_copy(rows, out_hbm.at[c].at[idx1])

    return run(packed, pos_t)


def _sc_combine(sorted_rows, pos_t):
    n_chunks, _, width = sorted_rows.shape
    n_tok = pos_t.shape[1]
    per_worker = n_tok // SC_WORKERS
    assert n_tok % (SC_WORKERS * SC_CHUNK) == 0

    @functools.partial(
        pl.kernel, mesh=_sc_mesh(),
        out_type=jax.ShapeDtypeStruct((2, n_chunks, n_tok, width), sorted_rows.dtype),
        scratch_types=[pltpu.VMEM((SC_CHUNK,), jnp.int32),
                       pltpu.VMEM((n_chunks, SC_CHUNK, width), sorted_rows.dtype)]
                      + [pltpu.SemaphoreType.DMA] * (2 * n_chunks),
        name="sc_combine")
    def run(rows_hbm, pos_hbm, out_hbm, idx, bufs, *sems):
        gather_sems, write_sems = sems[:n_chunks], sems[n_chunks:]
        worker = lax.axis_index("s") * SC_CORES + lax.axis_index("c")

        @pl.loop(0, per_worker // SC_CHUNK)
        def _(j):
            base = worker * per_worker + j * SC_CHUNK
            for k in range(2):
                pltpu.sync_copy(pos_hbm.at[k, pl.ds(base, SC_CHUNK)], idx)
                gathers = [pltpu.async_copy(rows_hbm.at[c].at[idx], bufs.at[c], gather_sems[c])
                           for c in range(n_chunks)]
                writes = []
                for c in range(n_chunks):
                    gathers[c].wait()
                    writes.append(pltpu.async_copy(bufs.at[c], out_hbm.at[k, c, pl.ds(base, SC_CHUNK)],
                                                   write_sems[c]))
                for w in writes:
                    w.wait()

    return run(sorted_rows, pos_t)


def _expert_kernel(tile_expert_ref, n_used_ref, xs_ref, wg_ref, wu_ref, wd_ref, ys_ref):
    del tile_expert_ref

    @pl.when(pl.program_id(0) < n_used_ref[0])
    def _():
        x = _unpack_bf16_pairs(jnp.concatenate([xs_ref[c] for c in range(PACK_CHUNKS)], axis=1))
        gate = jnp.dot(x, wg_ref[...].astype(BF16), preferred_element_type=F32)
        up = jnp.dot(x, wu_ref[...].astype(BF16), preferred_element_type=F32)
        hidden = (jax.nn.silu(gate) * up).astype(BF16)
        y = jnp.dot(hidden, wd_ref[...].astype(BF16), preferred_element_type=F32)
        y = _pack_bf16_pairs(y.astype(BF16).astype(F32))
        for c in range(PACK_CHUNKS):
            ys_ref[c] = y[:, c * LANES:(c + 1) * LANES]


def _expert_call(xs, tile_expert, n_used, w_gate, w_up, w_down, tile_rows):
    _, n_rows, _ = xs.shape
    D = w_gate.shape[1]
    row_block = pl.BlockSpec((PACK_CHUNKS, tile_rows, LANES),
                             lambda i, te, nu: (0, jnp.minimum(i, nu[0] - 1), 0))
    return pl.pallas_call(
        _expert_kernel,
        grid_spec=pltpu.PrefetchScalarGridSpec(
            num_scalar_prefetch=2,
            grid=(n_rows // tile_rows,),
            in_specs=[row_block,
                      pl.BlockSpec((None, D, D_EXPERT), lambda i, te, nu: (te[i], 0, 0)),
                      pl.BlockSpec((None, D, D_EXPERT), lambda i, te, nu: (te[i], 0, 0)),
                      pl.BlockSpec((None, D_EXPERT, D), lambda i, te, nu: (te[i], 0, 0))],
            out_specs=row_block),
        out_shape=jax.ShapeDtypeStruct(xs.shape, xs.dtype),
        compiler_params=pltpu.CompilerParams(dimension_semantics=("arbitrary",),
                                             vmem_limit_bytes=VMEM_LIMIT_BYTES),
        name="experts",
    )(tile_expert, n_used, xs, w_gate, w_up, w_down)


def _final_kernel(h_ref, y_ref, wts_ref, g2_ref, b2_ref, out_ref, *, alpha):
    wts = wts_ref[...]
    z = alpha * h_ref[...]
    for k in range(2):
        yk = _unpack_bf16_pairs(jnp.concatenate([y_ref[k, c] for c in range(PACK_CHUNKS)], axis=1))
        z = z + wts[:, k:k + 1] * yk.astype(F32)
    out_ref[...] = _layer_norm(z, g2_ref[...], b2_ref[...])


def _final_call(h, y, wts, gamma, beta, alpha, tm=512):
    N, D = h.shape
    assert N % tm == 0
    return pl.pallas_call(
        functools.partial(_final_kernel, alpha=alpha),
        grid=(N // tm,),
        in_specs=[pl.BlockSpec((tm, D), lambda i: (i, 0)),
                  pl.BlockSpec((2, PACK_CHUNKS, tm, LANES), lambda i: (0, 0, i, 0)),
                  pl.BlockSpec((tm, 2), lambda i: (i, 0)),
                  pl.BlockSpec((1, D), lambda i: (0, 0)),
                  pl.BlockSpec((1, D), lambda i: (0, 0))],
        out_specs=pl.BlockSpec((tm, D), lambda i: (i, 0)),
        out_shape=jax.ShapeDtypeStruct((N, D), F32),
        compiler_params=pltpu.CompilerParams(dimension_semantics=("parallel",),
                                             vmem_limit_bytes=VMEM_LIMIT_BYTES),
        name="final",
    )(h, y, wts, gamma.reshape(1, D), beta.reshape(1, D))


def _moe(h, packed, ids, wts, counts, w_gate, w_up, w_down, gamma, beta, alpha, tile_rows=512):
    N = h.shape[0]
    pos, ends = _rank_call(ids, counts, tile_rows)
    pos_t = pos.T
    n_tiles = 2 * N // tile_rows + N_EXPERTS
    seg_end = ends[0, :N_EXPERTS].astype(jnp.int32)
    tile_start = jnp.arange(n_tiles, dtype=jnp.int32) * tile_rows
    tile_expert = jnp.minimum(jnp.sum(seg_end[None, :] <= tile_start[:, None], axis=1),
                              N_EXPERTS - 1).astype(jnp.int32)
    n_used = (seg_end[N_EXPERTS - 1:] // tile_rows).astype(jnp.int32)
    xs = _sc_dispatch(packed, pos_t, n_tiles * tile_rows)
    ys = _expert_call(xs, tile_expert, n_used, w_gate, w_up, w_down, tile_rows)
    y = _sc_combine(ys, pos_t)
    return _final_call(h, y, wts, gamma, beta, alpha)


@jax.jit
def kernel(x, w_in, b_in, rel_bias_table, w_pool, pool_scale, w_proj_attn, w_proj_pool, w_out, ln1_gamma, ln1_beta, w_router_group, b_router_group, w_router_expert, b_router_expert, w_expert_gate, w_expert_up, w_expert_down, ln2_gamma, ln2_beta):
    B, S, D = x.shape
    depth = w_in.shape[0]
    alpha = (2.0 * depth) ** 0.25
    for layer in range(depth):
        qkv, u, ga, gb = _proj_call(x, w_in[layer], b_in[layer])
        a = _attn_call(qkv, rel_bias_table)
        h, hb, ids, wts, counts = _post_call(a, u, ga, gb, x, w_proj_attn[layer], w_pool[layer], pool_scale[layer],
                                     w_proj_pool[layer], w_out[layer], ln1_gamma[layer], ln1_beta[layer],
                                     w_router_group[layer], b_router_group[layer],
                                     w_router_expert[layer], b_router_expert[layer], alpha)
        out = _moe(h.reshape(B * S, D), hb, ids.reshape(B * S, 2),
                   wts.reshape(B * S, 2), counts, w_expert_gate[layer], w_expert_up[layer],
                   w_expert_down[layer], ln2_gamma[layer], ln2_beta[layer], alpha)
        x = out.reshape(B, S, D)
    return x
```

```python
import functools
import math

import jax
import jax.numpy as jnp
import numpy as np
from jax import lax
from jax.experimental import pallas as pl
from jax.experimental.pallas import tpu as pltpu
from jax.experimental.pallas import tpu_sc as plsc

F32 = jnp.float32
BF16 = jnp.bfloat16

HEAD_DIM = 64
ATTN_CONFIGS = ((128, 1), (512, 4), (2048, 16))
N_GROUPS = len(ATTN_CONFIGS)
HEADS_PER_GROUP = 4
GROUP_WIDTH = HEADS_PER_GROUP * HEAD_DIM
ATTN_WIDTH = N_GROUPS * GROUP_WIDTH
BLOCK = 128
N_REL_BUCKETS = 32
REL_MAX_DISTANCE = 2048
NEG_INF = -1e30

POOL_SIZES = (2, 4, 8, 16)
POOL_GROUP_DIM = 128
POOL_WIDTH = POOL_GROUP_DIM * len(POOL_SIZES)
POOL_HALO = 16

N_EXPERT_GROUPS = 4
EXPERTS_PER_GROUP = 8
N_EXPERTS = N_EXPERT_GROUPS * EXPERTS_PER_GROUP
D_EXPERT = 256
LN_EPS = 1e-5

VMEM_LIMIT_BYTES = 56 * 1024 * 1024
LANES = 128
HALVES = GROUP_WIDTH // LANES


def _layer_norm(z, gamma, beta):
    mu = jnp.mean(z, axis=-1, keepdims=True)
    zc = z - mu
    var = jnp.mean(zc * zc, axis=-1, keepdims=True)
    return zc * lax.rsqrt(var + LN_EPS) * gamma + beta


def _proj_kernel(x_ref, w_ref, b_ref, *refs, tm, d_model):
    qkv_refs = refs[:9]
    u_ref, ga_ref, gb_ref, xb_ref, acc_ref = refs[9:]
    xb_ref[...] = x_ref[...].astype(BF16)

    def chunk(c0, width):
        acc = jnp.dot(xb_ref[...], w_ref[:, c0:c0 + width], preferred_element_type=F32)
        return acc + b_ref[:, c0:c0 + width]

    n_staged = 0
    for which in range(3):
        for g, (_, dil) in enumerate(ATTN_CONFIGS):
            out = qkv_refs[which * 3 + g]
            acc = chunk(which * ATTN_WIDTH + g * GROUP_WIDTH, GROUP_WIDTH)
            if which == 0:
                acc = acc * HEAD_DIM ** -0.5
            if dil == 1:
                out[0] = acc.astype(BF16)
            else:
                planes = [n_staged * HALVES + half for half in range(HALVES)]
                n_staged += 1
                for half, plane in enumerate(planes):
                    acc_ref[plane] = acc[:, half * LANES:(half + 1) * LANES]
                for r in range(dil):
                    for half, plane in enumerate(planes):
                        out[r, :, half * LANES:(half + 1) * LANES] = (
                            acc_ref[plane, pl.ds(r, tm // dil, stride=dil), :].astype(BF16))
    pool_off = 3 * ATTN_WIDTH
    for c in range(POOL_WIDTH // 256):
        u_ref[:, c * 256:(c + 1) * 256] = chunk(pool_off + c * 256, 256)
    ga_off = pool_off + POOL_WIDTH
    for gate_ref, off in ((ga_ref, ga_off), (gb_ref, ga_off + d_model)):
        for c in range(d_model // 256):
            gate_ref[:, c * 256:(c + 1) * 256] = jax.nn.sigmoid(chunk(off + c * 256, 256)).astype(BF16)


def _proj_call(x, w_in, b_in, b0, B, tm=512):
    _, S, D = x.shape
    in_width = w_in.shape[1]
    assert in_width == 3 * ATTN_WIDTH + POOL_WIDTH + 2 * D
    assert S % tm == 0
    grid = (B, S // tm)
    qkv_shapes, qkv_specs = [], []
    for _ in range(3):
        for (_, dil) in ATTN_CONFIGS:
            assert tm % (dil * 16) == 0
            qkv_shapes.append(jax.ShapeDtypeStruct((B, dil, S // dil, GROUP_WIDTH), BF16))
            qkv_specs.append(pl.BlockSpec((None, dil, tm // dil, GROUP_WIDTH), lambda b, i: (b, 0, i, 0)))
    row_spec = lambda width: pl.BlockSpec((None, tm, width), lambda b, i: (b, i, 0))
    out_shape = qkv_shapes + [jax.ShapeDtypeStruct((B, S, POOL_WIDTH), F32),
                              jax.ShapeDtypeStruct((B, S, D), BF16),
                              jax.ShapeDtypeStruct((B, S, D), BF16)]
    out_specs = qkv_specs + [row_spec(POOL_WIDTH), row_spec(D), row_spec(D)]
    outs = pl.pallas_call(
        functools.partial(_proj_kernel, tm=tm, d_model=D),
        grid=grid,
        in_specs=[pl.BlockSpec((None, tm, D), lambda b, i: (b + b0, i, 0)),
                  pl.BlockSpec((D, in_width), lambda b, i: (0, 0)),
                  pl.BlockSpec((1, in_width), lambda b, i: (0, 0))],
        out_specs=out_specs,
        out_shape=out_shape,
        scratch_shapes=[pltpu.VMEM((tm, D), BF16),
                        pltpu.VMEM((3 * sum(dil > 1 for _, dil in ATTN_CONFIGS) * HALVES, tm, LANES), F32)],
        compiler_params=pltpu.CompilerParams(dimension_semantics=("parallel", "parallel"),
                                             vmem_limit_bytes=VMEM_LIMIT_BYTES),
        name="proj",
    )(x, w_in.astype(BF16), b_in.reshape(1, in_width))
    qkv = [o.reshape(B, S, GROUP_WIDTH) for o in outs[:9]]
    return qkv, outs[9], outs[10], outs[11]


def _t5_causal_bucket(dist):
    max_exact = N_REL_BUCKETS // 2
    is_small = dist < max_exact
    d = jnp.maximum(dist, 1).astype(F32)
    large = max_exact + (jnp.log(d / max_exact) / math.log(REL_MAX_DISTANCE / max_exact)
                         * (N_REL_BUCKETS - max_exact)).astype(jnp.int32)
    large = jnp.minimum(large, N_REL_BUCKETS - 1)
    return jnp.where(is_small, dist, large)


def _attn_bias(rel_bias_table):
    full, first = [], []
    for g, (window, dil) in enumerate(ATTN_CONFIGS):
        span = window // dil
        table = rel_bias_table[:, g * HEADS_PER_GROUP:(g + 1) * HEADS_PER_GROUP].astype(F32)
        lq = np.arange(BLOCK)
        for lk, dst in ((np.arange(-BLOCK, BLOCK), full), (lq, first)):
            step = jnp.asarray(lq[:, None] - lk[None, :], jnp.int32)
            in_window = (step >= 0) & (step <= span)
            bucket = _t5_causal_bucket(jnp.clip(step, 0, span) * dil)
            bias = jnp.einsum('qkb,bh->hqk', jax.nn.one_hot(bucket, N_REL_BUCKETS, dtype=F32), table,
                              precision=lax.Precision.HIGHEST)
            bias = jnp.where(in_window[None], bias, NEG_INF)
            dst.append(bias.reshape(HEADS_PER_GROUP * BLOCK, lk.shape[0]))
    return jnp.stack(full), jnp.stack(first)


def _attn_kernel(*refs, seq):
    qkv = refs[:9]
    bias_ref, bias_first_ref, out_ref, o_scr, l_scr = refs[9:]
    rows = HEADS_PER_GROUP * BLOCK
    row_head = lax.broadcasted_iota(jnp.int32, (rows, GROUP_WIDTH), 0) // BLOCK
    lane_head_r = lax.broadcasted_iota(jnp.int32, (rows, GROUP_WIDTH), 1) // HEAD_DIM
    head_mask = row_head == lane_head_r
    lane_head = lax.broadcasted_iota(jnp.int32, (BLOCK, GROUP_WIDTH), 1) // HEAD_DIM

    def per_head_to_lanes(stacked):
        out = stacked[0:BLOCK]
        for h in range(1, HEADS_PER_GROUP):
            out = jnp.where(lane_head == h, stacked[h * BLOCK:(h + 1) * BLOCK], out)
        return out

    for g, (_, dil) in enumerate(ATTN_CONFIGS):
        q_ref, k_ref, v_ref = qkv[3 * g:3 * g + 3]
        sub_len = seq // dil
        n_blocks = sub_len // BLOCK

        def block(r, n, first, g=g, dil=dil, q_ref=q_ref, k_ref=k_ref, v_ref=v_ref, sub_len=sub_len):
            base = pl.multiple_of(r * sub_len + n * BLOCK, BLOCK)
            qb = q_ref[pl.ds(base, BLOCK), :]
            if first:
                kk = k_ref[pl.ds(base, BLOCK), :]
                vv = v_ref[pl.ds(base, BLOCK), :]
                bias = bias_first_ref[g]
            else:
                kbase = pl.multiple_of(base - BLOCK, BLOCK)
                kk = k_ref[pl.ds(kbase, 2 * BLOCK), :]
                vv = v_ref[pl.ds(kbase, 2 * BLOCK), :]
                bias = bias_ref[g]
            qs = jnp.where(head_mask, jnp.concatenate([qb] * HEADS_PER_GROUP, axis=0), jnp.zeros((), BF16))
            logits = lax.dot_general(qs, kk, (((1,), (1,)), ((), ())), preferred_element_type=F32) + bias
            m = jnp.max(logits, axis=1, keepdims=True)
            p = jnp.exp(logits - m)
            s = jnp.sum(p, axis=1, keepdims=True)
            pv = jnp.dot(p.astype(BF16), vv, preferred_element_type=F32)
            o = per_head_to_lanes(pv * (1.0 / s))
            lse = per_head_to_lanes(jnp.broadcast_to(m + jnp.log(s), (rows, GROUP_WIDTH)))
            start = n * (BLOCK * dil) + r
            if dil == 1:
                dst = pl.ds(pl.multiple_of(start, BLOCK), BLOCK)
            else:
                dst = pl.ds(start, BLOCK, stride=dil)
            for half in range(HALVES):
                cols = slice(half * LANES, (half + 1) * LANES)
                o_scr[g * HALVES + half, dst, :] = o[:, cols]
                l_scr[g * HALVES + half, dst, :] = lse[:, cols]

        def per_subsequence(r, carry, block=block, n_blocks=n_blocks):
            block(r, 0, True)
            if n_blocks > 1:
                def per_block(n, c):
                    block(r, n, False)
                    return c
                lax.fori_loop(1, n_blocks, per_block, 0, unroll=3 if (n_blocks - 1) % 3 == 0 else 1)
            return carry

        lax.fori_loop(0, dil, per_subsequence, 0, unroll=2 if (n_blocks == 1 and dil % 2 == 0) else 1)

    chunk = 256

    def merge(i, carry):
        sl = pl.ds(pl.multiple_of(i * chunk, chunk), chunk)
        for half in range(HALVES):
            ls = [l_scr[g * HALVES + half, sl, :] for g in range(N_GROUPS)]
            m = functools.reduce(jnp.maximum, ls)
            es = [jnp.exp(l - m) for l in ls]
            den = functools.reduce(lambda a, b: a + b, es)
            num = functools.reduce(lambda a, b: a + b,
                                   [e * o_scr[g * HALVES + half, sl, :] for g, e in enumerate(es)])
            out_ref[sl, half * LANES:(half + 1) * LANES] = (num / den).astype(BF16)
        return carry

    lax.fori_loop(0, seq // chunk, merge, 0)


def _attn_call(qkv, rel_bias_table):
    B, S, _ = qkv[0].shape
    for (_, dil) in ATTN_CONFIGS:
        assert S % (dil * BLOCK) == 0
    bias, bias_first = _attn_bias(rel_bias_table)
    seq_spec = pl.BlockSpec((None, S, GROUP_WIDTH), lambda b: (b, 0, 0))
    ordered = []
    for g in range(N_GROUPS):
        ordered += [qkv[g], qkv[3 + g], qkv[6 + g]]
    return pl.pallas_call(
        functools.partial(_attn_kernel, seq=S),
        grid=(B,),
        in_specs=[seq_spec] * 9 + [pl.BlockSpec(bias.shape, lambda b: (0, 0, 0)),
                                   pl.BlockSpec(bias_first.shape, lambda b: (0, 0, 0))],
        out_specs=seq_spec,
        out_shape=jax.ShapeDtypeStruct((B, S, GROUP_WIDTH), BF16),
        scratch_shapes=[pltpu.VMEM((N_GROUPS * HALVES, S, LANES), F32),
                        pltpu.VMEM((N_GROUPS * HALVES, S, LANES), F32)],
        compiler_params=pltpu.CompilerParams(dimension_semantics=("parallel",),
                                             vmem_limit_bytes=VMEM_LIMIT_BYTES),
        name="attn",
    )(*ordered, bias, bias_first)


PACK_CHUNKS = 4
SUBLANES = 8
ROUTER_LANES = LANES // 2


def _pack_bf16_pairs(rounded):
    w = rounded.shape[1] // 2
    bits = lax.bitcast_convert_type(rounded, jnp.int32)
    return bits[:, :w] | lax.shift_right_logical(bits[:, w:], jnp.full((), 16, jnp.int32))


def _unpack_bf16_pairs(words):
    hi = lax.bitcast_convert_type(words & jnp.int32(-65536), F32).astype(BF16)
    lo = lax.bitcast_convert_type(lax.shift_left(words, jnp.full((), 16, jnp.int32)), F32).astype(BF16)
    return jnp.concatenate([hi, lo], axis=1)


def _post_kernel(a_ref, u_ref, halo_ref, ga_ref, gb_ref, x_ref,
                 pa_ref, wpool_ref, pscale_ref, pb_ref, wout_ref, g1_ref, b1_ref,
                 wr_cat_ref, br_ref,
                 h_ref, hb_ref, ids_ref, wts_ref, cnt_ref, pool_scr, *, tm, sub, alpha):
    i = pl.program_id(1)
    halo = halo_ref[...]
    pool_scr[0:POOL_HALO, :] = jnp.where(i > 0, halo, jnp.zeros_like(halo))
    pool_scr[POOL_HALO:, :] = u_ref[...]
    head_pos = i * tm + lax.broadcasted_iota(jnp.int32, (POOL_HALO, POOL_GROUP_DIM), 0)

    def rows_chain(r0):
        rs = pl.ds(r0, sub)
        group_cols = [slice(gi * POOL_GROUP_DIM, (gi + 1) * POOL_GROUP_DIM) for gi in range(len(POOL_SIZES))]
        diffs = []
        for cols, w in zip(group_cols, POOL_SIZES):
            ug = u_ref[rs, cols]
            acc = ug
            for j in range(1, w):
                acc = acc + pool_scr[pl.ds(r0 + POOL_HALO - j, sub), cols]
            inv_count = jnp.full((sub, POOL_GROUP_DIM), 1.0 / w, F32)
            if r0 == 0:
                inv_count = jnp.concatenate([1.0 / jnp.minimum(head_pos + 1, w).astype(F32),
                                             inv_count[POOL_HALO:]], axis=0)
            diffs.append((acc * inv_count - ug).astype(BF16))
        yield
        mixed_in = [(jnp.dot(diff, wpool_ref[gi], preferred_element_type=F32) * pscale_ref[:, cols]).astype(BF16)
                    for gi, (diff, cols) in enumerate(zip(diffs, group_cols))]
        y_pool = jnp.dot(jnp.concatenate(mixed_in, axis=1), pb_ref[...], preferred_element_type=F32)
        y_attn = jnp.dot(a_ref[rs, :], pa_ref[...], preferred_element_type=F32)
        yield
        mixed = (ga_ref[rs, :].astype(F32) * y_attn + gb_ref[rs, :].astype(F32) * y_pool).astype(BF16)
        yield
        y = jnp.dot(mixed, wout_ref[...], preferred_element_type=F32)
        yield
        h = _layer_norm(alpha * x_ref[rs, :] + y, g1_ref[...], b1_ref[...])
        h_ref[rs, :] = h
        h_hi = h.astype(BF16)
        h_rounded = h_hi.astype(F32)
        packed = _pack_bf16_pairs(h_rounded)
        for c in range(PACK_CHUNKS):
            hb_ref[c, rs, :] = packed[:, c * LANES:(c + 1) * LANES]

        h_lo = (h - h_rounded).astype(BF16)
        yield
        nt = (((1,), (1,)), ((), ()))
        both = lax.dot_general(h_hi, wr_cat_ref[...], nt, preferred_element_type=F32)
        logits = (both + pltpu.roll(both, ROUTER_LANES, axis=1)
                  + lax.dot_general(h_lo, wr_cat_ref[...], nt, preferred_element_type=F32)) + br_ref[...]
        yield
        lane = lax.broadcasted_iota(jnp.int32, logits.shape, 1)
        big = jnp.int32(2 ** 30)
        is_group = lane < N_EXPERT_GROUPS
        gl = jnp.where(is_group, logits, -jnp.inf)
        gmax = jnp.max(gl, axis=1, keepdims=True)
        g_idx = jnp.min(jnp.where(gl == gmax, lane, big), axis=1, keepdims=True)
        g_prob = 1.0 / jnp.sum(jnp.exp(gl - gmax), axis=1, keepdims=True)
        expert = lane - N_EXPERT_GROUPS
        in_group = (expert >= g_idx * EXPERTS_PER_GROUP) & (expert < (g_idx + 1) * EXPERTS_PER_GROUP)
        el = jnp.where(in_group, logits, -jnp.inf)
        v1 = jnp.max(el, axis=1, keepdims=True)
        i1 = jnp.min(jnp.where(el == v1, expert, big), axis=1, keepdims=True)
        el2 = jnp.where(expert == i1, -jnp.inf, el)
        v2 = jnp.max(el2, axis=1, keepdims=True)
        i2 = jnp.min(jnp.where(el2 == v2, expert, big), axis=1, keepdims=True)
        e2 = jnp.exp(v2 - v1)
        den = 1.0 + e2
        ids_ref[rs, :] = jnp.concatenate([i1, i2], axis=1)
        wts_ref[rs, :] = jnp.concatenate([1.0 / den * g_prob, e2 / den * g_prob], axis=1)
        chosen = jnp.where((lane == i1) | (lane == i2), 1.0, 0.0)
        counts.append(jnp.sum(chosen, axis=0, keepdims=True))
        yield

    counts = []
    chains = [rows_chain(r0) for r0 in range(0, tm, sub)]
    n_stages = 7
    for step in range(n_stages + len(chains) - 1):
        for lag, chain in enumerate(chains):
            if 0 <= step - lag < n_stages:
                next(chain)
    cnt_ref[...] = jnp.broadcast_to(functools.reduce(lambda a, b: a + b, counts), cnt_ref.shape)


def _post_call(a, u, ga, gb, x, w_proj_attn, w_pool, pool_scale, w_proj_pool, w_out, gamma, beta,
               w_router_group, b_router_group, w_router_expert, b_router_expert, alpha, b0, tm=512):
    B = a.shape[0]
    _, S, D = x.shape
    assert S % tm == 0 and tm % POOL_HALO == 0
    n_logits = N_EXPERT_GROUPS + N_EXPERTS
    assert n_logits <= ROUTER_LANES
    wr = jnp.concatenate([w_router_group, w_router_expert], axis=1).T
    wr = jnp.pad(wr, ((0, ROUTER_LANES - n_logits), (0, 0)))
    wr_hi = wr.astype(BF16)
    wr_lo = (wr - wr_hi.astype(F32)).astype(BF16)
    wr_cat = jnp.concatenate([wr_hi, wr_lo], axis=0)
    br = jnp.pad(jnp.concatenate([b_router_group, b_router_expert]),
                 (0, 2 * ROUTER_LANES - n_logits)).reshape(1, 2 * ROUTER_LANES)
    n_tiles = B * (S // tm)
    row_spec = lambda width: pl.BlockSpec((None, tm, width), lambda b, i: (b, i, 0))
    full = lambda arr: pl.BlockSpec(arr.shape, lambda b, i: (0,) * arr.ndim)
    halo_blocks = tm // POOL_HALO
    halo_spec = pl.BlockSpec((None, POOL_HALO, POOL_WIDTH),
                             lambda b, i: (b, jnp.maximum(i * halo_blocks - 1, 0), 0))
    weights = [w_proj_attn.astype(BF16), w_pool.astype(BF16), pool_scale.reshape(1, POOL_WIDTH),
               w_proj_pool.astype(BF16), w_out.astype(BF16), gamma.reshape(1, D), beta.reshape(1, D),
               wr_cat, br]
    return pl.pallas_call(
        functools.partial(_post_kernel, tm=tm, sub=tm // 2, alpha=alpha),
        grid=(B, S // tm),
        in_specs=[row_spec(GROUP_WIDTH), row_spec(POOL_WIDTH), halo_spec, row_spec(D), row_spec(D),
                  pl.BlockSpec((None, tm, D), lambda b, i: (b + b0, i, 0))] + [full(w) for w in weights],
        out_specs=[row_spec(D),
                   pl.BlockSpec((PACK_CHUNKS, tm, LANES), lambda b, i: (0, b * (S // tm) + i, 0)),
                   row_spec(2), row_spec(2),
                   pl.BlockSpec((SUBLANES, LANES), lambda b, i: (b * (S // tm) + i, 0))],
        out_shape=[jax.ShapeDtypeStruct((B, S, D), F32),
                   jax.ShapeDtypeStruct((PACK_CHUNKS, B * S, LANES), jnp.int32),
                   jax.ShapeDtypeStruct((B, S, 2), jnp.int32), jax.ShapeDtypeStruct((B, S, 2), F32),
                   jax.ShapeDtypeStruct((n_tiles * SUBLANES, LANES), F32)],
        scratch_shapes=[pltpu.VMEM((tm + POOL_HALO, POOL_WIDTH), F32)],
        compiler_params=pltpu.CompilerParams(dimension_semantics=("parallel", "parallel"),
                                             vmem_limit_bytes=VMEM_LIMIT_BYTES),
        name="post",
    )(a, u, u, ga, gb, x, *weights)


def _rank_kernel(ids_ref, cnt_ref, pos_ref, ends_ref, run_ref, start_ref, *, tm, tile_rows):
    i = pl.program_id(0)

    @pl.when(i == 0)
    def _():
        total = jnp.sum(cnt_ref[...], axis=0, keepdims=True) / SUBLANES
        padded = jnp.broadcast_to(jnp.ceil(total / tile_rows) * tile_rows, run_ref.shape)
        lane1 = lax.broadcasted_iota(jnp.int32, padded.shape, 1)
        incl = padded
        shift = 1
        while shift < LANES:
            incl = incl + jnp.where(lane1 >= shift, pltpu.roll(incl, shift, axis=1), 0.0)
            shift *= 2
        start_ref[...] = incl - padded
        ends_ref[...] = incl
        run_ref[...] = jnp.zeros_like(run_ref)

    ids = ids_ref[...]
    lane = lax.broadcasted_iota(jnp.int32, (tm, LANES), 1)
    oh0 = lane == ids[:, 0:1]
    oh1 = lane == ids[:, 1:2]
    onehot = jnp.where(oh0 | oh1, 1.0, 0.0)
    before = jnp.dot(_strict_lower_ones(tm), onehot.astype(BF16), preferred_element_type=F32)
    slot = start_ref[0:1, :] + run_ref[0:1, :] + before
    p0 = jnp.sum(jnp.where(oh0, slot, 0.0), axis=1, keepdims=True)
    p1 = jnp.sum(jnp.where(oh1, slot, 0.0), axis=1, keepdims=True)
    pos_ref[...] = jnp.concatenate([p0, p1], axis=1).astype(jnp.int32)
    run_ref[...] += jnp.sum(onehot, axis=0, keepdims=True)


def _strict_lower_ones(tm):
    row = lax.broadcasted_iota(jnp.int32, (tm, tm), 0)
    col = lax.broadcasted_iota(jnp.int32, (tm, tm), 1)
    return jnp.where(row > col, 1.0, 0.0).astype(BF16)


def _rank_call(ids, counts, tile_rows, tm=1024):
    N = ids.shape[0]
    assert N % tm == 0 and SUBLANES * (2 * N + N_EXPERTS * tile_rows) < 2 ** 24
    return pl.pallas_call(
        functools.partial(_rank_kernel, tm=tm, tile_rows=tile_rows),
        grid=(N // tm,),
        in_specs=[pl.BlockSpec((tm, 2), lambda i: (i, 0)),
                  pl.BlockSpec(counts.shape, lambda i: (0, 0))],
        out_specs=[pl.BlockSpec((tm, 2), lambda i: (i, 0)),
                   pl.BlockSpec((SUBLANES, LANES), lambda i: (0, 0))],
        out_shape=[jax.ShapeDtypeStruct((N, 2), jnp.int32), jax.ShapeDtypeStruct((SUBLANES, LANES), F32)],
        scratch_shapes=[pltpu.VMEM((SUBLANES, LANES), F32), pltpu.VMEM((SUBLANES, LANES), F32)],
        compiler_params=pltpu.CompilerParams(dimension_semantics=("arbitrary",),
                                             vmem_limit_bytes=VMEM_LIMIT_BYTES),
        name="rank",
    )(ids, counts)


SC_CORES = 2
SC_SUBCORES = 16
SC_WORKERS = SC_CORES * SC_SUBCORES
SC_CHUNK = 128


def _sc_mesh():
    return plsc.VectorSubcoreMesh(core_axis_name="c", subcore_axis_name="s",
                                  num_cores=SC_CORES, num_subcores=SC_SUBCORES)


def _sc_dispatch(packed, pos_t, n_rows):
    n_chunks, n_tok, width = packed.shape
    per_worker = n_tok // SC_WORKERS
    assert n_tok % (SC_WORKERS * SC_CHUNK) == 0

    @functools.partial(
        pl.kernel, mesh=_sc_mesh(),
        out_type=jax.ShapeDtypeStruct((n_chunks, n_rows, width), packed.dtype),
        scratch_types=[pltpu.VMEM((SC_CHUNK,), jnp.int32), pltpu.VMEM((SC_CHUNK,), jnp.int32),
                       pltpu.VMEM((n_chunks, SC_CHUNK, width), packed.dtype)]
                      + [pltpu.SemaphoreType.DMA] * (3 * n_chunks),
        name="sc_dispatch")
    def run(packed_hbm, pos_hbm, out_hbm, idx0, idx1, bufs, *sems):
        load_sems, sems0, sems1 = sems[:n_chunks], sems[n_chunks:2 * n_chunks], sems[2 * n_chunks:]
        worker = lax.axis_index("s") * SC_CORES + lax.axis_index("c")

        @pl.loop(0, per_worker // SC_CHUNK)
        def _(j):
            base = worker * per_worker + j * SC_CHUNK
            loads = [pltpu.async_copy(packed_hbm.at[c, pl.ds(base, SC_CHUNK)], bufs.at[c], load_sems[c])
                     for c in range(n_chunks)]
            pltpu.sync_copy(pos_hbm.at[0, pl.ds(base, SC_CHUNK)], idx0)
            pltpu.sync_copy(pos_hbm.at[1, pl.ds(base, SC_CHUNK)], idx1)
            scatters = []
            for c in range(n_chunks):
                loads[c].wait()
                scatters.append(pltpu.async_copy(bufs.at[c], out_hbm.at[c].at[idx0], sems0[c]))
                scatters.append(pltpu.async_copy(bufs.at[c], out_hbm.at[c].at[idx1], sems1[c]))
            for s in scatters:
                s.wait()

    return run(packed, pos_t)


def _sc_combine(sorted_rows, pos_t):
    n_chunks, _, width = sorted_rows.shape
    n_tok = pos_t.shape[1]
    per_worker = n_tok // SC_WORKERS
    assert n_tok % (SC_WORKERS * SC_CHUNK) == 0

    @functools.partial(
        pl.kernel, mesh=_sc_mesh(),
        out_type=jax.ShapeDtypeStruct((2, n_chunks, n_tok, width), sorted_rows.dtype),
        scratch_types=[pltpu.VMEM((SC_CHUNK,), jnp.int32),
                       pltpu.VMEM((n_chunks, SC_CHUNK, width), sorted_rows.dtype)]
                      + [pltpu.SemaphoreType.DMA] * (2 * n_chunks),
        name="sc_combine")
    def run(rows_hbm, pos_hbm, out_hbm, idx, bufs, *sems):
        gather_sems, write_sems = sems[:n_chunks], sems[n_chunks:]
        worker = lax.axis_index("s") * SC_CORES + lax.axis_index("c")

        @pl.loop(0, per_worker // SC_CHUNK)
        def _(j):
            base = worker * per_worker + j * SC_CHUNK
            for k in range(2):
                pltpu.sync_copy(pos_hbm.at[k, pl.ds(base, SC_CHUNK)], idx)
                gathers = [pltpu.async_copy(rows_hbm.at[c].at[idx], bufs.at[c], gather_sems[c])
                           for c in range(n_chunks)]
                writes = []
                for c in range(n_chunks):
                    gathers[c].wait()
                    writes.append(pltpu.async_copy(bufs.at[c], out_hbm.at[k, c, pl.ds(base, SC_CHUNK)],
                                                   write_sems[c]))
                for w in writes:
                    w.wait()

    return run(sorted_rows, pos_t)


def _expert_kernel(tile_expert_ref, n_used_ref, xs_ref, wg_ref, wu_ref, wd_ref, ys_ref):
    del tile_expert_ref

    @pl.when(pl.program_id(0) < n_used_ref[0])
    def _():
        wg = wg_ref[...].astype(BF16)
        wu = wu_ref[...].astype(BF16)
        wd = wd_ref[...].astype(BF16)
        tile_rows = xs_ref.shape[1]
        sub = tile_rows // 2

        def rows_chain(r0):
            rs = pl.ds(r0, sub)
            x = _unpack_bf16_pairs(jnp.concatenate([xs_ref[c, rs, :] for c in range(PACK_CHUNKS)], axis=1))
            yield
            gate = jnp.dot(x, wg, preferred_element_type=F32)
            up = jnp.dot(x, wu, preferred_element_type=F32)
            yield
            hidden = (jax.nn.silu(gate) * up).astype(BF16)
            yield
            y = jnp.dot(hidden, wd, preferred_element_type=F32)
            yield
            y = _pack_bf16_pairs(y.astype(BF16).astype(F32))
            for c in range(PACK_CHUNKS):
                ys_ref[c, rs, :] = y[:, c * LANES:(c + 1) * LANES]
            yield

        chains = [rows_chain(r0) for r0 in range(0, tile_rows, sub)]
        n_stages = 5
        for step in range(n_stages + len(chains) - 1):
            for lag, chain in enumerate(chains):
                if 0 <= step - lag < n_stages:
                    next(chain)


def _expert_call(xs, tile_expert, n_used, w_gate, w_up, w_down, tile_rows):
    _, n_rows, _ = xs.shape
    D = w_gate.shape[1]
    row_block = pl.BlockSpec((PACK_CHUNKS, tile_rows, LANES),
                             lambda i, te, nu: (0, jnp.minimum(i, nu[0] - 1), 0))
    return pl.pallas_call(
        _expert_kernel,
        grid_spec=pltpu.PrefetchScalarGridSpec(
            num_scalar_prefetch=2,
            grid=(n_rows // tile_rows,),
            in_specs=[row_block,
                      pl.BlockSpec((None, D, D_EXPERT), lambda i, te, nu: (te[i], 0, 0)),
                      pl.BlockSpec((None, D, D_EXPERT), lambda i, te, nu: (te[i], 0, 0)),
                      pl.BlockSpec((None, D_EXPERT, D), lambda i, te, nu: (te[i], 0, 0))],
            out_specs=row_block),
        out_shape=jax.ShapeDtypeStruct(xs.shape, xs.dtype),
        compiler_params=pltpu.CompilerParams(dimension_semantics=("arbitrary",),
                                             vmem_limit_bytes=VMEM_LIMIT_BYTES),
        name="experts",
    )(tile_expert, n_used, xs, w_gate, w_up, w_down)


def _final_kernel(h_ref, y_ref, wts_ref, g2_ref, b2_ref, *rest, alpha):
    out_ref = rest[-1]
    wts = wts_ref[...]
    z = alpha * h_ref[...]
    for k in range(2):
        yk = _unpack_bf16_pairs(jnp.concatenate([y_ref[k, c] for c in range(PACK_CHUNKS)], axis=1))
        z = z + wts[:, k:k + 1] * yk.astype(F32)
    out_ref[...] = _layer_norm(z, g2_ref[...], b2_ref[...])


def _final_call(h, y, wts, gamma, beta, alpha, row0, n_total, earlier_out, tm=512):
    N, D = h.shape
    assert N % tm == 0 and row0 % tm == 0
    operands = [h, y, wts, gamma.reshape(1, D), beta.reshape(1, D)]
    in_specs = [pl.BlockSpec((tm, D), lambda i: (i, 0)),
                pl.BlockSpec((2, PACK_CHUNKS, tm, LANES), lambda i: (0, 0, i, 0)),
                pl.BlockSpec((tm, 2), lambda i: (i, 0)),
                pl.BlockSpec((1, D), lambda i: (0, 0)),
                pl.BlockSpec((1, D), lambda i: (0, 0))]
    aliases = {}
    if earlier_out is not None:
        aliases = {len(operands): 0}
        operands.append(earlier_out)
        in_specs.append(pl.BlockSpec(memory_space=pl.ANY))
    return pl.pallas_call(
        functools.partial(_final_kernel, alpha=alpha),
        grid=(N // tm,),
        in_specs=in_specs,
        out_specs=pl.BlockSpec((tm, D), lambda i: (i + row0 // tm, 0)),
        out_shape=jax.ShapeDtypeStruct((n_total, D), F32),
        input_output_aliases=aliases,
        compiler_params=pltpu.CompilerParams(dimension_semantics=("parallel",),
                                             vmem_limit_bytes=VMEM_LIMIT_BYTES),
        name="final",
    )(*operands)


def _moe(packed, ids, counts, w_gate, w_up, w_down, tile_rows=512):
    N = ids.shape[0]
    pos, ends = _rank_call(ids, counts, tile_rows)
    pos_t = pos.T
    n_tiles = 2 * N // tile_rows + N_EXPERTS
    seg_end = ends[0, :N_EXPERTS].astype(jnp.int32)
    tile_start = jnp.arange(n_tiles, dtype=jnp.int32) * tile_rows
    tile_expert = jnp.minimum(jnp.sum(seg_end[None, :] <= tile_start[:, None], axis=1),
                              N_EXPERTS - 1).astype(jnp.int32)
    n_used = (seg_end[N_EXPERTS - 1:] // tile_rows).astype(jnp.int32)
    xs = _sc_dispatch(packed, pos_t, n_tiles * tile_rows)
    ys = _expert_call(xs, tile_expert, n_used, w_gate, w_up, w_down, tile_rows)
    return _sc_combine(ys, pos_t)


@jax.jit
def kernel(x, w_in, b_in, rel_bias_table, w_pool, pool_scale, w_proj_attn, w_proj_pool, w_out, ln1_gamma, ln1_beta, w_router_group, b_router_group, w_router_expert, b_router_expert, w_expert_gate, w_expert_up, w_expert_down, ln2_gamma, ln2_beta):
    B, S, D = x.shape
    depth = w_in.shape[0]
    alpha = (2.0 * depth) ** 0.25
    n_parts = 2 if B % 2 == 0 else 1
    nb = B // n_parts
    for layer in range(depth):
        out = None
        for part in range(n_parts):
            b0 = part * nb
            qkv, u, ga, gb = _proj_call(x, w_in[layer], b_in[layer], b0, nb)
            a = _attn_call(qkv, rel_bias_table)
            h, packed, ids, wts, counts = _post_call(
                a, u, ga, gb, x, w_proj_attn[layer], w_pool[layer], pool_scale[layer],
                w_proj_pool[layer], w_out[layer], ln1_gamma[layer], ln1_beta[layer],
                w_router_group[layer], b_router_group[layer],
                w_router_expert[layer], b_router_expert[layer], alpha, b0)
            y = _moe(packed, ids.reshape(nb * S, 2), counts,
                     w_expert_gate[layer], w_expert_up[layer], w_expert_down[layer])
            out = _final_call(h.reshape(nb * S, D), y, wts.reshape(nb * S, 2), ln2_gamma[layer],
                              ln2_beta[layer], alpha, b0 * S, B * S, out)
        x = out.reshape(B, S, D)
    return x
```

```python
import functools
import math

import jax
import jax.numpy as jnp
import numpy as np
from jax import lax
from jax.experimental import pallas as pl
from jax.experimental.pallas import tpu as pltpu
from jax.experimental.pallas import tpu_sc as plsc

F32 = jnp.float32
BF16 = jnp.bfloat16

HEAD_DIM = 64
ATTN_CONFIGS = ((128, 1), (512, 4), (2048, 16))
N_GROUPS = len(ATTN_CONFIGS)
HEADS_PER_GROUP = 4
GROUP_WIDTH = HEADS_PER_GROUP * HEAD_DIM
ATTN_WIDTH = N_GROUPS * GROUP_WIDTH
BLOCK = 128
N_REL_BUCKETS = 32
REL_MAX_DISTANCE = 2048
NEG_INF = -1e30

POOL_SIZES = (2, 4, 8, 16)
POOL_GROUP_DIM = 128
POOL_WIDTH = POOL_GROUP_DIM * len(POOL_SIZES)
POOL_HALO = 16

N_EXPERT_GROUPS = 4
EXPERTS_PER_GROUP = 8
N_EXPERTS = N_EXPERT_GROUPS * EXPERTS_PER_GROUP
D_EXPERT = 256
LN_EPS = 1e-5

VMEM_LIMIT_BYTES = 56 * 1024 * 1024
LANES = 128
HALVES = GROUP_WIDTH // LANES


def _layer_norm(z, gamma, beta):
    mu = jnp.mean(z, axis=-1, keepdims=True)
    zc = z - mu
    var = jnp.mean(zc * zc, axis=-1, keepdims=True)
    return zc * lax.rsqrt(var + LN_EPS) * gamma + beta


def _proj_kernel(x_ref, w_ref, b_ref, *refs, tm, d_model):
    qkv_refs = refs[:9]
    u_ref, ga_ref, gb_ref, xb_ref, acc_ref = refs[9:]
    xb_ref[...] = x_ref[...].astype(BF16)

    def chunk(c0, width):
        acc = jnp.dot(xb_ref[...], w_ref[:, c0:c0 + width], preferred_element_type=F32)
        return acc + b_ref[:, c0:c0 + width]

    n_staged = 0
    for which in range(3):
        for g, (_, dil) in enumerate(ATTN_CONFIGS):
            out = qkv_refs[which * 3 + g]
            acc = chunk(which * ATTN_WIDTH + g * GROUP_WIDTH, GROUP_WIDTH)
            if which == 0:
                acc = acc * HEAD_DIM ** -0.5
            if dil == 1:
                out[0] = acc.astype(BF16)
            else:
                planes = [n_staged * HALVES + half for half in range(HALVES)]
                n_staged += 1
                for half, plane in enumerate(planes):
                    acc_ref[plane] = acc[:, half * LANES:(half + 1) * LANES]
                for r in range(dil):
                    for half, plane in enumerate(planes):
                        out[r, :, half * LANES:(half + 1) * LANES] = (
                            acc_ref[plane, pl.ds(r, tm // dil, stride=dil), :].astype(BF16))
    pool_off = 3 * ATTN_WIDTH
    for c in range(POOL_WIDTH // 256):
        u_ref[:, c * 256:(c + 1) * 256] = chunk(pool_off + c * 256, 256)
    ga_off = pool_off + POOL_WIDTH
    for gate_ref, off in ((ga_ref, ga_off), (gb_ref, ga_off + d_model)):
        for c in range(d_model // 256):
            gate_ref[:, c * 256:(c + 1) * 256] = jax.nn.sigmoid(chunk(off + c * 256, 256)).astype(BF16)


def _proj_call(x, w_in, b_in, b0, B, tm=512):
    _, S, D = x.shape
    in_width = w_in.shape[1]
    assert in_width == 3 * ATTN_WIDTH + POOL_WIDTH + 2 * D
    assert S % tm == 0
    grid = (B, S // tm)
    qkv_shapes, qkv_specs = [], []
    for _ in range(3):
        for (_, dil) in ATTN_CONFIGS:
            assert tm % (dil * 16) == 0
            qkv_shapes.append(jax.ShapeDtypeStruct((B, dil, S // dil, GROUP_WIDTH), BF16))
            qkv_specs.append(pl.BlockSpec((None, dil, tm // dil, GROUP_WIDTH), lambda b, i: (b, 0, i, 0)))
    row_spec = lambda width: pl.BlockSpec((None, tm, width), lambda b, i: (b, i, 0))
    out_shape = qkv_shapes + [jax.ShapeDtypeStruct((B, S, POOL_WIDTH), F32),
                              jax.ShapeDtypeStruct((B, S, D), BF16),
                              jax.ShapeDtypeStruct((B, S, D), BF16)]
    out_specs = qkv_specs + [row_spec(POOL_WIDTH), row_spec(D), row_spec(D)]
    outs = pl.pallas_call(
        functools.partial(_proj_kernel, tm=tm, d_model=D),
        grid=grid,
        in_specs=[pl.BlockSpec((None, tm, D), lambda b, i: (b + b0, i, 0)),
                  pl.BlockSpec((D, in_width), lambda b, i: (0, 0)),
                  pl.BlockSpec((1, in_width), lambda b, i: (0, 0))],
        out_specs=out_specs,
        out_shape=out_shape,
        scratch_shapes=[pltpu.VMEM((tm, D), BF16),
                        pltpu.VMEM((3 * sum(dil > 1 for _, dil in ATTN_CONFIGS) * HALVES, tm, LANES), F32)],
        compiler_params=pltpu.CompilerParams(dimension_semantics=("parallel", "parallel"),
                                             vmem_limit_bytes=VMEM_LIMIT_BYTES),
        name="proj",
    )(x, w_in.astype(BF16), b_in.reshape(1, in_width))
    qkv = [o.reshape(B, S, GROUP_WIDTH) for o in outs[:9]]
    return qkv, outs[9], outs[10], outs[11]


def _t5_causal_bucket(dist):
    max_exact = N_REL_BUCKETS // 2
    is_small = dist < max_exact
    d = jnp.maximum(dist, 1).astype(F32)
    large = max_exact + (jnp.log(d / max_exact) / math.log(REL_MAX_DISTANCE / max_exact)
                         * (N_REL_BUCKETS - max_exact)).astype(jnp.int32)
    large = jnp.minimum(large, N_REL_BUCKETS - 1)
    return jnp.where(is_small, dist, large)


def _attn_bias(rel_bias_table):
    full, first = [], []
    for g, (window, dil) in enumerate(ATTN_CONFIGS):
        span = window // dil
        table = rel_bias_table[:, g * HEADS_PER_GROUP:(g + 1) * HEADS_PER_GROUP].astype(F32)
        lq = np.arange(BLOCK)
        for lk, dst in ((np.arange(-BLOCK, BLOCK), full), (lq, first)):
            step = jnp.asarray(lq[:, None] - lk[None, :], jnp.int32)
            in_window = (step >= 0) & (step <= span)
            bucket = _t5_causal_bucket(jnp.clip(step, 0, span) * dil)
            bias = jnp.einsum('qkb,bh->hqk', jax.nn.one_hot(bucket, N_REL_BUCKETS, dtype=F32), table,
                              precision=lax.Precision.HIGHEST)
            bias = jnp.where(in_window[None], bias, NEG_INF)
            dst.append(bias.reshape(HEADS_PER_GROUP * BLOCK, lk.shape[0]))
    return jnp.stack(full), jnp.stack(first)


def _run_skewed(chains, n_stages):
    for step in range(n_stages + len(chains) - 1):
        for lag, chain in enumerate(chains):
            if 0 <= step - lag < n_stages:
                next(chain)


ATTN_STAGES = 4


def _attn_kernel(*refs, seq):
    qkv = refs[:9]
    bias_ref, bias_first_ref, out_ref, o_scr, l_scr = refs[9:]
    rows = HEADS_PER_GROUP * BLOCK
    row_head = lax.broadcasted_iota(jnp.int32, (rows, GROUP_WIDTH), 0) // BLOCK
    lane_head_r = lax.broadcasted_iota(jnp.int32, (rows, GROUP_WIDTH), 1) // HEAD_DIM
    head_mask = row_head == lane_head_r
    lane_head = lax.broadcasted_iota(jnp.int32, (BLOCK, GROUP_WIDTH), 1) // HEAD_DIM

    def per_head_to_lanes(stacked):
        out = stacked[0:BLOCK]
        for h in range(1, HEADS_PER_GROUP):
            out = jnp.where(lane_head == h, stacked[h * BLOCK:(h + 1) * BLOCK], out)
        return out

    for g, (_, dil) in enumerate(ATTN_CONFIGS):
        q_ref, k_ref, v_ref = qkv[3 * g:3 * g + 3]
        sub_len = seq // dil
        n_blocks = sub_len // BLOCK

        def block(r, n, first, g=g, dil=dil, q_ref=q_ref, k_ref=k_ref, v_ref=v_ref, sub_len=sub_len):
            base = pl.multiple_of(r * sub_len + n * BLOCK, BLOCK)
            qb = q_ref[pl.ds(base, BLOCK), :]
            if first:
                kk = k_ref[pl.ds(base, BLOCK), :]
                vv = v_ref[pl.ds(base, BLOCK), :]
                bias = bias_first_ref[g]
            else:
                kbase = pl.multiple_of(base - BLOCK, BLOCK)
                kk = k_ref[pl.ds(kbase, 2 * BLOCK), :]
                vv = v_ref[pl.ds(kbase, 2 * BLOCK), :]
                bias = bias_ref[g]
            qs = jnp.where(head_mask, jnp.concatenate([qb] * HEADS_PER_GROUP, axis=0), jnp.zeros((), BF16))
            logits = lax.dot_general(qs, kk, (((1,), (1,)), ((), ())), preferred_element_type=F32) + bias
            yield
            m = jnp.max(logits, axis=1, keepdims=True)
            p = jnp.exp(logits - m)
            s = jnp.sum(p, axis=1, keepdims=True)
            p = p.astype(BF16)
            yield
            pv = jnp.dot(p, vv, preferred_element_type=F32)
            yield
            o = per_head_to_lanes(pv * (1.0 / s))
            lse = per_head_to_lanes(jnp.broadcast_to(m + jnp.log(s), (rows, GROUP_WIDTH)))
            start = n * (BLOCK * dil) + r
            if dil == 1:
                dst = pl.ds(pl.multiple_of(start, BLOCK), BLOCK)
            else:
                dst = pl.ds(start, BLOCK, stride=dil)
            for half in range(HALVES):
                cols = slice(half * LANES, (half + 1) * LANES)
                o_scr[g * HALVES + half, dst, :] = o[:, cols]
                l_scr[g * HALVES + half, dst, :] = lse[:, cols]
            yield

        if n_blocks > 1:
            group = 3 if (n_blocks - 1) % 3 == 0 else 1

            _run_skewed([block(r, 0, True) for r in range(dil)], ATTN_STAGES)

            def per_subsequence(r, carry, block=block, n_blocks=n_blocks, group=group):
                def per_group(i, c):
                    _run_skewed([block(r, 1 + i * group + k, False) for k in range(group)], ATTN_STAGES)
                    return c
                lax.fori_loop(0, (n_blocks - 1) // group, per_group, 0)
                return carry
            lax.fori_loop(0, dil, per_subsequence, 0)
        else:
            group = 4 if dil % 4 == 0 else 1

            def per_group(i, carry, block=block, group=group):
                _run_skewed([block(i * group + k, 0, True) for k in range(group)], ATTN_STAGES)
                return carry
            lax.fori_loop(0, dil // group, per_group, 0)

    chunk = 256

    def merge(i, carry):
        sl = pl.ds(pl.multiple_of(i * chunk, chunk), chunk)
        for half in range(HALVES):
            ls = [l_scr[g * HALVES + half, sl, :] for g in range(N_GROUPS)]
            m = functools.reduce(jnp.maximum, ls)
            es = [jnp.exp(l - m) for l in ls]
            den = functools.reduce(lambda a, b: a + b, es)
            num = functools.reduce(lambda a, b: a + b,
                                   [e * o_scr[g * HALVES + half, sl, :] for g, e in enumerate(es)])
            out_ref[sl, half * LANES:(half + 1) * LANES] = (num / den).astype(BF16)
        return carry

    lax.fori_loop(0, seq // chunk, merge, 0)


def _attn_call(qkv, rel_bias_table):
    B, S, _ = qkv[0].shape
    for (_, dil) in ATTN_CONFIGS:
        assert S % (dil * BLOCK) == 0
    bias, bias_first = _attn_bias(rel_bias_table)
    seq_spec = pl.BlockSpec((None, S, GROUP_WIDTH), lambda b: (b, 0, 0))
    ordered = []
    for g in range(N_GROUPS):
        ordered += [qkv[g], qkv[3 + g], qkv[6 + g]]
    return pl.pallas_call(
        functools.partial(_attn_kernel, seq=S),
        grid=(B,),
        in_specs=[seq_spec] * 9 + [pl.BlockSpec(bias.shape, lambda b: (0, 0, 0)),
                                   pl.BlockSpec(bias_first.shape, lambda b: (0, 0, 0))],
        out_specs=seq_spec,
        out_shape=jax.ShapeDtypeStruct((B, S, GROUP_WIDTH), BF16),
        scratch_shapes=[pltpu.VMEM((N_GROUPS * HALVES, S, LANES), F32),
                        pltpu.VMEM((N_GROUPS * HALVES, S, LANES), F32)],
        compiler_params=pltpu.CompilerParams(dimension_semantics=("parallel",),
                                             vmem_limit_bytes=VMEM_LIMIT_BYTES),
        name="attn",
    )(*ordered, bias, bias_first)


PACK_CHUNKS = 4
SUBLANES = 8
ROUTER_LANES = LANES // 2


def _pack_bf16_pairs(rounded):
    w = rounded.shape[1] // 2
    bits = lax.bitcast_convert_type(rounded, jnp.int32)
    return bits[:, :w] | lax.shift_right_logical(bits[:, w:], jnp.full((), 16, jnp.int32))


def _unpack_bf16_pairs(words):
    hi = lax.bitcast_convert_type(words & jnp.int32(-65536), F32).astype(BF16)
    lo = lax.bitcast_convert_type(lax.shift_left(words, jnp.full((), 16, jnp.int32)), F32).astype(BF16)
    return jnp.concatenate([hi, lo], axis=1)


def _post_kernel(a_ref, u_ref, halo_ref, ga_ref, gb_ref, x_ref,
                 pa_ref, wpool_ref, pscale_ref, pb_ref, wout_ref, g1_ref, b1_ref,
                 wr_cat_ref, br_ref,
                 h_ref, hb_ref, ids_ref, wts_ref, cnt_ref, pool_scr, tmp_scr, *, tm, sub, alpha):
    i = pl.program_id(1)
    halo = halo_ref[...]
    pool_scr[0:POOL_HALO, :] = jnp.zeros_like(halo)
    pool_scr[POOL_HALO:2 * POOL_HALO, :] = jnp.where(i > 0, halo, jnp.zeros_like(halo))
    pool_scr[2 * POOL_HALO:, :] = u_ref[...]
    tmp_scr[:, 0:POOL_HALO, :] = jnp.zeros((tmp_scr.shape[0], POOL_HALO, POOL_GROUP_DIM), F32)
    head_pos = i * tm + lax.broadcasted_iota(jnp.int32, (POOL_HALO, POOL_GROUP_DIM), 0)

    def rows_chain(r0):
        rs = pl.ds(r0, sub)
        group_cols = [slice(gi * POOL_GROUP_DIM, (gi + 1) * POOL_GROUP_DIM) for gi in range(len(POOL_SIZES))]
        diffs = []
        ext = sub + 2 * POOL_HALO
        for gi, (cols, w) in enumerate(zip(group_cols, POOL_SIZES)):
            ug = u_ref[rs, cols]
            levels = w.bit_length() - 1
            for level in range(levels):
                shift = 1 << level
                dst = tmp_scr.at[(r0 // sub * len(POOL_SIZES) + gi) * 2 + level % 2]
                if level == 0:
                    cur = pool_scr[pl.ds(r0 + POOL_HALO, ext - POOL_HALO), cols]
                    back = pool_scr[pl.ds(r0 + POOL_HALO - shift, ext - POOL_HALO), cols]
                else:
                    src = tmp_scr.at[(r0 // sub * len(POOL_SIZES) + gi) * 2 + (level - 1) % 2]
                    cur = src[pl.ds(POOL_HALO, ext - POOL_HALO), :]
                    back = src[pl.ds(POOL_HALO - shift, ext - POOL_HALO), :]
                if level == levels - 1:
                    acc = (cur + back)[POOL_HALO:]
                else:
                    dst[pl.ds(POOL_HALO, ext - POOL_HALO), :] = cur + back
            inv_count = jnp.full((sub, POOL_GROUP_DIM), 1.0 / w, F32)
            if r0 == 0:
                inv_count = jnp.concatenate([1.0 / jnp.minimum(head_pos + 1, w).astype(F32),
                                             inv_count[POOL_HALO:]], axis=0)
            diffs.append((acc * inv_count - ug).astype(BF16))
        yield
        mixed_in = [(jnp.dot(diff, wpool_ref[gi], preferred_element_type=F32) * pscale_ref[:, cols]).astype(BF16)
                    for gi, (diff, cols) in enumerate(zip(diffs, group_cols))]
        y_pool = jnp.dot(jnp.concatenate(mixed_in, axis=1), pb_ref[...], preferred_element_type=F32)
        y_attn = jnp.dot(a_ref[rs, :], pa_ref[...], preferred_element_type=F32)
        yield
        mixed = ga_ref[rs, :] * y_attn.astype(BF16) + gb_ref[rs, :] * y_pool.astype(BF16)
        yield
        y = jnp.dot(mixed, wout_ref[...], preferred_element_type=F32)
        yield
        h = _layer_norm(alpha * x_ref[rs, :] + y, g1_ref[...], b1_ref[...])
        h_ref[rs, :] = h
        h_hi = h.astype(BF16)
        h_rounded = h_hi.astype(F32)
        packed = _pack_bf16_pairs(h_rounded)
        for c in range(PACK_CHUNKS):
            hb_ref[c, rs, :] = packed[:, c * LANES:(c + 1) * LANES]

        h_lo = (h - h_rounded).astype(BF16)
        yield
        nt = (((1,), (1,)), ((), ()))
        both = lax.dot_general(h_hi, wr_cat_ref[...], nt, preferred_element_type=F32)
        logits = (both + pltpu.roll(both, ROUTER_LANES, axis=1)
                  + lax.dot_general(h_lo, wr_cat_ref[...], nt, preferred_element_type=F32)) + br_ref[...]
        yield
        lane = lax.broadcasted_iota(jnp.int32, logits.shape, 1)
        big = jnp.int32(2 ** 30)
        is_group = lane < N_EXPERT_GROUPS
        gl = jnp.where(is_group, logits, -jnp.inf)
        gmax = jnp.max(gl, axis=1, keepdims=True)
        g_idx = jnp.min(jnp.where(gl == gmax, lane, big), axis=1, keepdims=True)
        g_prob = 1.0 / jnp.sum(jnp.exp(gl - gmax), axis=1, keepdims=True)
        expert = lane - N_EXPERT_GROUPS
        in_group = (expert >= g_idx * EXPERTS_PER_GROUP) & (expert < (g_idx + 1) * EXPERTS_PER_GROUP)
        el = jnp.where(in_group, logits, -jnp.inf)
        v1 = jnp.max(el, axis=1, keepdims=True)
        i1 = jnp.min(jnp.where(el == v1, expert, big), axis=1, keepdims=True)
        el2 = jnp.where(expert == i1, -jnp.inf, el)
        v2 = jnp.max(el2, axis=1, keepdims=True)
        i2 = jnp.min(jnp.where(el2 == v2, expert, big), axis=1, keepdims=True)
        e2 = jnp.exp(v2 - v1)
        den = 1.0 + e2
        ids_ref[rs, :] = jnp.concatenate([i1, i2], axis=1)
        wts_ref[rs, :] = jnp.concatenate([1.0 / den * g_prob, e2 / den * g_prob], axis=1)
        chosen = jnp.where((lane == i1) | (lane == i2), 1.0, 0.0)
        counts.append(jnp.sum(chosen, axis=0, keepdims=True))
        yield

    counts = []
    chains = [rows_chain(r0) for r0 in range(0, tm, sub)]
    n_stages = 7
    for step in range(n_stages + len(chains) - 1):
        for lag, chain in enumerate(chains):
            if 0 <= step - lag < n_stages:
                next(chain)
    cnt_ref[...] = jnp.broadcast_to(functools.reduce(lambda a, b: a + b, counts), cnt_ref.shape)


def _post_call(a, u, ga, gb, x, w_proj_attn, w_pool, pool_scale, w_proj_pool, w_out, gamma, beta,
               w_router_group, b_router_group, w_router_expert, b_router_expert, alpha, b0, tm=512):
    B = a.shape[0]
    _, S, D = x.shape
    assert S % tm == 0 and tm % POOL_HALO == 0
    n_logits = N_EXPERT_GROUPS + N_EXPERTS
    assert n_logits <= ROUTER_LANES
    wr = jnp.concatenate([w_router_group, w_router_expert], axis=1).T
    wr = jnp.pad(wr, ((0, ROUTER_LANES - n_logits), (0, 0)))
    wr_hi = wr.astype(BF16)
    wr_lo = (wr - wr_hi.astype(F32)).astype(BF16)
    wr_cat = jnp.concatenate([wr_hi, wr_lo], axis=0)
    br = jnp.pad(jnp.concatenate([b_router_group, b_router_expert]),
                 (0, 2 * ROUTER_LANES - n_logits)).reshape(1, 2 * ROUTER_LANES)
    n_tiles = B * (S // tm)
    row_spec = lambda width: pl.BlockSpec((None, tm, width), lambda b, i: (b, i, 0))
    full = lambda arr: pl.BlockSpec(arr.shape, lambda b, i: (0,) * arr.ndim)
    halo_blocks = tm // POOL_HALO
    halo_spec = pl.BlockSpec((None, POOL_HALO, POOL_WIDTH),
                             lambda b, i: (b, jnp.maximum(i * halo_blocks - 1, 0), 0))
    weights = [w_proj_attn.astype(BF16), w_pool.astype(BF16), pool_scale.reshape(1, POOL_WIDTH),
               w_proj_pool.astype(BF16), w_out.astype(BF16), gamma.reshape(1, D), beta.reshape(1, D),
               wr_cat, br]
    return pl.pallas_call(
        functools.partial(_post_kernel, tm=tm, sub=tm // 2, alpha=alpha),
        grid=(B, S // tm),
        in_specs=[row_spec(GROUP_WIDTH), row_spec(POOL_WIDTH), halo_spec, row_spec(D), row_spec(D),
                  pl.BlockSpec((None, tm, D), lambda b, i: (b + b0, i, 0))] + [full(w) for w in weights],
        out_specs=[row_spec(D),
                   pl.BlockSpec((PACK_CHUNKS, tm, LANES), lambda b, i: (0, b * (S // tm) + i, 0)),
                   row_spec(2), row_spec(2),
                   pl.BlockSpec((SUBLANES, LANES), lambda b, i: (b * (S // tm) + i, 0))],
        out_shape=[jax.ShapeDtypeStruct((B, S, D), F32),
                   jax.ShapeDtypeStruct((PACK_CHUNKS, B * S, LANES), jnp.int32),
                   jax.ShapeDtypeStruct((B, S, 2), jnp.int32), jax.ShapeDtypeStruct((B, S, 2), F32),
                   jax.ShapeDtypeStruct((n_tiles * SUBLANES, LANES), F32)],
        scratch_shapes=[pltpu.VMEM((tm + 2 * POOL_HALO, POOL_WIDTH), F32),
                        pltpu.VMEM((2 * len(POOL_SIZES) * 2, tm // 2 + 2 * POOL_HALO, POOL_GROUP_DIM), F32)],
        compiler_params=pltpu.CompilerParams(dimension_semantics=("parallel", "parallel"),
                                             vmem_limit_bytes=VMEM_LIMIT_BYTES),
        name="post",
    )(a, u, u, ga, gb, x, *weights)


def _rank_kernel(ids_ref, cnt_ref, pos_ref, ends_ref, run_ref, start_ref, *, tm, tile_rows):
    i = pl.program_id(0)

    @pl.when(i == 0)
    def _():
        total = jnp.sum(cnt_ref[...], axis=0, keepdims=True) / SUBLANES
        padded = jnp.broadcast_to(jnp.ceil(total / tile_rows) * tile_rows, run_ref.shape)
        lane1 = lax.broadcasted_iota(jnp.int32, padded.shape, 1)
        incl = padded
        shift = 1
        while shift < LANES:
            incl = incl + jnp.where(lane1 >= shift, pltpu.roll(incl, shift, axis=1), 0.0)
            shift *= 2
        start_ref[...] = incl - padded
        ends_ref[...] = incl
        run_ref[...] = jnp.zeros_like(run_ref)

    ids = ids_ref[...]
    lane = lax.broadcasted_iota(jnp.int32, (tm, LANES), 1)
    oh0 = lane == ids[:, 0:1]
    oh1 = lane == ids[:, 1:2]
    onehot = jnp.where(oh0 | oh1, 1.0, 0.0)
    before = jnp.dot(_strict_lower_ones(tm), onehot.astype(BF16), preferred_element_type=F32)
    slot = start_ref[0:1, :] + run_ref[0:1, :] + before
    p0 = jnp.sum(jnp.where(oh0, slot, 0.0), axis=1, keepdims=True)
    p1 = jnp.sum(jnp.where(oh1, slot, 0.0), axis=1, keepdims=True)
    pos_ref[...] = jnp.concatenate([p0, p1], axis=1).astype(jnp.int32)
    run_ref[...] += jnp.sum(onehot, axis=0, keepdims=True)


def _strict_lower_ones(tm):
    row = lax.broadcasted_iota(jnp.int32, (tm, tm), 0)
    col = lax.broadcasted_iota(jnp.int32, (tm, tm), 1)
    return jnp.where(row > col, 1.0, 0.0).astype(BF16)


def _rank_call(ids, counts, tile_rows, tm=1024):
    N = ids.shape[0]
    assert N % tm == 0 and SUBLANES * (2 * N + N_EXPERTS * tile_rows) < 2 ** 24
    return pl.pallas_call(
        functools.partial(_rank_kernel, tm=tm, tile_rows=tile_rows),
        grid=(N // tm,),
        in_specs=[pl.BlockSpec((tm, 2), lambda i: (i, 0)),
                  pl.BlockSpec(counts.shape, lambda i: (0, 0))],
        out_specs=[pl.BlockSpec((tm, 2), lambda i: (i, 0)),
                   pl.BlockSpec((SUBLANES, LANES), lambda i: (0, 0))],
        out_shape=[jax.ShapeDtypeStruct((N, 2), jnp.int32), jax.ShapeDtypeStruct((SUBLANES, LANES), F32)],
        scratch_shapes=[pltpu.VMEM((SUBLANES, LANES), F32), pltpu.VMEM((SUBLANES, LANES), F32)],
        compiler_params=pltpu.CompilerParams(dimension_semantics=("arbitrary",),
                                             vmem_limit_bytes=VMEM_LIMIT_BYTES),
        name="rank",
    )(ids, counts)


SC_CORES = 2
SC_SUBCORES = 16
SC_WORKERS = SC_CORES * SC_SUBCORES
SC_CHUNK = 128


def _sc_mesh():
    return plsc.VectorSubcoreMesh(core_axis_name="c", subcore_axis_name="s",
                                  num_cores=SC_CORES, num_subcores=SC_SUBCORES)


def _sc_dispatch(packed, pos_t, n_rows):
    n_chunks, n_tok, width = packed.shape
    per_worker = n_tok // SC_WORKERS
    assert n_tok % (SC_WORKERS * SC_CHUNK) == 0

    @functools.partial(
        pl.kernel, mesh=_sc_mesh(),
        out_type=jax.ShapeDtypeStruct((n_chunks, n_rows, width), packed.dtype),
        scratch_types=[pltpu.VMEM((SC_CHUNK,), jnp.int32), pltpu.VMEM((SC_CHUNK,), jnp.int32),
                       pltpu.VMEM((n_chunks, SC_CHUNK, width), packed.dtype)]
                      + [pltpu.SemaphoreType.DMA] * (3 * n_chunks),
        name="sc_dispatch")
    def run(packed_hbm, pos_hbm, out_hbm, idx0, idx1, bufs, *sems):
        load_sems, sems0, sems1 = sems[:n_chunks], sems[n_chunks:2 * n_chunks], sems[2 * n_chunks:]
        worker = lax.axis_index("s") * SC_CORES + lax.axis_index("c")

        @pl.loop(0, per_worker // SC_CHUNK)
        def _(j):
            base = worker * per_worker + j * SC_CHUNK
            loads = [pltpu.async_copy(packed_hbm.at[c, pl.ds(base, SC_CHUNK)], bufs.at[c], load_sems[c])
                     for c in range(n_chunks)]
            pltpu.sync_copy(pos_hbm.at[0, pl.ds(base, SC_CHUNK)], idx0)
            pltpu.sync_copy(pos_hbm.at[1, pl.ds(base, SC_CHUNK)], idx1)
            scatters = []
            for c in range(n_chunks):
                loads[c].wait()
                scatters.append(pltpu.async_copy(bufs.at[c], out_hbm.at[c].at[idx0], sems0[c]))
                scatters.append(pltpu.async_copy(bufs.at[c], out_hbm.at[c].at[idx1], sems1[c]))
            for s in scatters:
                s.wait()

    return run(packed, pos_t)


def _sc_combine(sorted_rows, pos_t):
    n_chunks, _, width = sorted_rows.shape
    n_tok = pos_t.shape[1]
    per_worker = n_tok // SC_WORKERS
    assert n_tok % (SC_WORKERS * SC_CHUNK) == 0

    @functools.partial(
        pl.kernel, mesh=_sc_mesh(),
        out_type=jax.ShapeDtypeStruct((2, n_chunks, n_tok, width), sorted_rows.dtype),
        scratch_types=[pltpu.VMEM((SC_CHUNK,), jnp.int32),
                       pltpu.VMEM((n_chunks, SC_CHUNK, width), sorted_rows.dtype)]
                      + [pltpu.SemaphoreType.DMA] * (2 * n_chunks),
        name="sc_combine")
    def run(rows_hbm, pos_hbm, out_hbm, idx, bufs, *sems):
        gather_sems, write_sems = sems[:n_chunks], sems[n_chunks:]
        worker = lax.axis_index("s") * SC_CORES + lax.axis_index("c")

        @pl.loop(0, per_worker // SC_CHUNK)
        def _(j):
            base = worker * per_worker + j * SC_CHUNK
            for k in range(2):
                pltpu.sync_copy(pos_hbm.at[k, pl.ds(base, SC_CHUNK)], idx)
                gathers = [pltpu.async_copy(rows_hbm.at[c].at[idx], bufs.at[c], gather_sems[c])
                           for c in range(n_chunks)]
                writes = []
                for c in range(n_chunks):
                    gathers[c].wait()
                    writes.append(pltpu.async_copy(bufs.at[c], out_hbm.at[k, c, pl.ds(base, SC_CHUNK)],
                                                   write_sems[c]))
                for w in writes:
                    w.wait()

    return run(sorted_rows, pos_t)


def _expert_kernel(tile_expert_ref, n_used_ref, xs_ref, wg_ref, wu_ref, wd_ref, ys_ref):
    del tile_expert_ref

    @pl.when(pl.program_id(0) < n_used_ref[0])
    def _():
        wg = wg_ref[...].astype(BF16)
        wu = wu_ref[...].astype(BF16)
        wd = wd_ref[...].astype(BF16)
        tile_rows = xs_ref.shape[1]
        sub = tile_rows // 2

        def rows_chain(r0):
            rs = pl.ds(r0, sub)
            x = _unpack_bf16_pairs(jnp.concatenate([xs_ref[c, rs, :] for c in range(PACK_CHUNKS)], axis=1))
            yield
            gate = jnp.dot(x, wg, preferred_element_type=F32)
            up = jnp.dot(x, wu, preferred_element_type=F32)
            yield
            hidden = (jax.nn.silu(gate) * up).astype(BF16)
            yield
            y = jnp.dot(hidden, wd, preferred_element_type=F32)
            yield
            y = _pack_bf16_pairs(y.astype(BF16).astype(F32))
            for c in range(PACK_CHUNKS):
                ys_ref[c, rs, :] = y[:, c * LANES:(c + 1) * LANES]
            yield

        chains = [rows_chain(r0) for r0 in range(0, tile_rows, sub)]
        n_stages = 5
        for step in range(n_stages + len(chains) - 1):
            for lag, chain in enumerate(chains):
                if 0 <= step - lag < n_stages:
                    next(chain)


def _expert_call(xs, tile_expert, n_used, w_gate, w_up, w_down, tile_rows):
    _, n_rows, _ = xs.shape
    D = w_gate.shape[1]
    row_block = pl.BlockSpec((PACK_CHUNKS, tile_rows, LANES),
                             lambda i, te, nu: (0, jnp.minimum(i, nu[0] - 1), 0))
    return pl.pallas_call(
        _expert_kernel,
        grid_spec=pltpu.PrefetchScalarGridSpec(
            num_scalar_prefetch=2,
            grid=(n_rows // tile_rows,),
            in_specs=[row_block,
                      pl.BlockSpec((None, D, D_EXPERT), lambda i, te, nu: (te[i], 0, 0)),
                      pl.BlockSpec((None, D, D_EXPERT), lambda i, te, nu: (te[i], 0, 0)),
                      pl.BlockSpec((None, D_EXPERT, D), lambda i, te, nu: (te[i], 0, 0))],
            out_specs=row_block),
        out_shape=jax.ShapeDtypeStruct(xs.shape, xs.dtype),
        compiler_params=pltpu.CompilerParams(dimension_semantics=("arbitrary",),
                                             vmem_limit_bytes=VMEM_LIMIT_BYTES),
        name="experts",
    )(tile_expert, n_used, xs, w_gate, w_up, w_down)


def _final_kernel(h_ref, y_ref, wts_ref, g2_ref, b2_ref, *rest, alpha):
    out_ref = rest[-1]
    wts = wts_ref[...]
    z = alpha * h_ref[...]
    for k in range(2):
        yk = _unpack_bf16_pairs(jnp.concatenate([y_ref[k, c] for c in range(PACK_CHUNKS)], axis=1))
        z = z + wts[:, k:k + 1] * yk.astype(F32)
    out_ref[...] = _layer_norm(z, g2_ref[...], b2_ref[...])


def _final_call(h, y, wts, gamma, beta, alpha, row0, n_total, earlier_out, tm=512):
    N, D = h.shape
    assert N % tm == 0 and row0 % tm == 0
    operands = [h, y, wts, gamma.reshape(1, D), beta.reshape(1, D)]
    in_specs = [pl.BlockSpec((tm, D), lambda i: (i, 0)),
                pl.BlockSpec((2, PACK_CHUNKS, tm, LANES), lambda i: (0, 0, i, 0)),
                pl.BlockSpec((tm, 2), lambda i: (i, 0)),
                pl.BlockSpec((1, D), lambda i: (0, 0)),
                pl.BlockSpec((1, D), lambda i: (0, 0))]
    aliases = {}
    if earlier_out is not None:
        aliases = {len(operands): 0}
        operands.append(earlier_out)
        in_specs.append(pl.BlockSpec(memory_space=pl.ANY))
    return pl.pallas_call(
        functools.partial(_final_kernel, alpha=alpha),
        grid=(N // tm,),
        in_specs=in_specs,
        out_specs=pl.BlockSpec((tm, D), lambda i: (i + row0 // tm, 0)),
        out_shape=jax.ShapeDtypeStruct((n_total, D), F32),
        input_output_aliases=aliases,
        compiler_params=pltpu.CompilerParams(dimension_semantics=("parallel",),
                                             vmem_limit_bytes=VMEM_LIMIT_BYTES),
        name="final",
    )(*operands)


def _moe(packed, ids, counts, w_gate, w_up, w_down, tile_rows=512):
    N = ids.shape[0]
    pos, ends = _rank_call(ids, counts, tile_rows)
    pos_t = pos.T
    n_tiles = 2 * N // tile_rows + N_EXPERTS
    seg_end = ends[0, :N_EXPERTS].astype(jnp.int32)
    tile_start = jnp.arange(n_tiles, dtype=jnp.int32) * tile_rows
    tile_expert = jnp.minimum(jnp.sum(seg_end[None, :] <= tile_start[:, None], axis=1),
                              N_EXPERTS - 1).astype(jnp.int32)
    n_used = (seg_end[N_EXPERTS - 1:] // tile_rows).astype(jnp.int32)
    xs = _sc_dispatch(packed, pos_t, n_tiles * tile_rows)
    ys = _expert_call(xs, tile_expert, n_used, w_gate, w_up, w_down, tile_rows)
    return _sc_combine(ys, pos_t)


@jax.jit
def kernel(x, w_in, b_in, rel_bias_table, w_pool, pool_scale, w_proj_attn, w_proj_pool, w_out, ln1_gamma, ln1_beta, w_router_group, b_router_group, w_router_expert, b_router_expert, w_expert_gate, w_expert_up, w_expert_down, ln2_gamma, ln2_beta):
    B, S, D = x.shape
    depth = w_in.shape[0]
    alpha = (2.0 * depth) ** 0.25
    n_parts = 2 if B % 2 == 0 else 1
    nb = B // n_parts
    for layer in range(depth):
        out = None
        for part in range(n_parts):
            b0 = part * nb
            qkv, u, ga, gb = _proj_call(x, w_in[layer], b_in[layer], b0, nb)
            a = _attn_call(qkv, rel_bias_table)
            h, packed, ids, wts, counts = _post_call(
                a, u, ga, gb, x, w_proj_attn[layer], w_pool[layer], pool_scale[layer],
                w_proj_pool[layer], w_out[layer], ln1_gamma[layer], ln1_beta[layer],
                w_router_group[layer], b_router_group[layer],
                w_router_expert[layer], b_router_expert[layer], alpha, b0)
            y = _moe(packed, ids.reshape(nb * S, 2), counts,
                     w_expert_gate[layer], w_expert_up[layer], w_expert_down[layer])
            out = _final_call(h.reshape(nb * S, D), y, wts.reshape(nb * S, 2), ln2_gamma[layer],
                              ln2_beta[layer], alpha, b0 * S, B * S, out)
        x = out.reshape(B, S, D)
    return x
```

```python
import functools
import math

import jax
import jax.numpy as jnp
import numpy as np
from jax import lax
from jax.experimental import pallas as pl
from jax.experimental.pallas import tpu as pltpu
from jax.experimental.pallas import tpu_sc as plsc

F32 = jnp.float32
BF16 = jnp.bfloat16

HEAD_DIM = 64
ATTN_CONFIGS = ((128, 1), (512, 4), (2048, 16))
N_GROUPS = len(ATTN_CONFIGS)
HEADS_PER_GROUP = 4
GROUP_WIDTH = HEADS_PER_GROUP * HEAD_DIM
ATTN_WIDTH = N_GROUPS * GROUP_WIDTH
BLOCK = 128
N_REL_BUCKETS = 32
REL_MAX_DISTANCE = 2048
NEG_INF = -1e30

POOL_SIZES = (2, 4, 8, 16)
POOL_GROUP_DIM = 128
POOL_WIDTH = POOL_GROUP_DIM * len(POOL_SIZES)
POOL_HALO = 16

N_EXPERT_GROUPS = 4
EXPERTS_PER_GROUP = 8
N_EXPERTS = N_EXPERT_GROUPS * EXPERTS_PER_GROUP
D_EXPERT = 256
LN_EPS = 1e-5

VMEM_LIMIT_BYTES = 56 * 1024 * 1024
LANES = 128
HALVES = GROUP_WIDTH // LANES


def _layer_norm(z, gamma, beta):
    mu = jnp.mean(z, axis=-1, keepdims=True)
    zc = z - mu
    var = jnp.mean(zc * zc, axis=-1, keepdims=True)
    return zc * lax.rsqrt(var + LN_EPS) * gamma + beta


def _proj_kernel(x_ref, w_ref, b_ref, *refs, tm, d_model):
    qkv_refs = refs[:9]
    u_ref, ga_ref, gb_ref, xb_ref, acc_ref = refs[9:]
    xb_ref[...] = x_ref[...].astype(BF16)

    def chunk(c0, width):
        acc = jnp.dot(xb_ref[...], w_ref[:, c0:c0 + width], preferred_element_type=F32)
        return acc + b_ref[:, c0:c0 + width]

    n_staged = 0
    for which in range(3):
        for g, (_, dil) in enumerate(ATTN_CONFIGS):
            out = qkv_refs[which * 3 + g]
            acc = chunk(which * ATTN_WIDTH + g * GROUP_WIDTH, GROUP_WIDTH)
            if which == 0:
                acc = acc * HEAD_DIM ** -0.5
            if dil == 1:
                out[0] = acc.astype(BF16)
            else:
                planes = [n_staged * HALVES + half for half in range(HALVES)]
                n_staged += 1
                for half, plane in enumerate(planes):
                    acc_ref[plane] = acc[:, half * LANES:(half + 1) * LANES]
                for r in range(dil):
                    for half, plane in enumerate(planes):
                        out[r, :, half * LANES:(half + 1) * LANES] = (
                            acc_ref[plane, pl.ds(r, tm // dil, stride=dil), :].astype(BF16))
    pool_off = 3 * ATTN_WIDTH
    for c in range(POOL_WIDTH // 256):
        u_ref[:, c * 256:(c + 1) * 256] = chunk(pool_off + c * 256, 256)
    ga_off = pool_off + POOL_WIDTH
    for gate_ref, off in ((ga_ref, ga_off), (gb_ref, ga_off + d_model)):
        for c in range(d_model // 256):
            gate_ref[:, c * 256:(c + 1) * 256] = jax.nn.sigmoid(chunk(off + c * 256, 256)).astype(BF16)


def _proj_call(x, w_in, b_in, b0, B, tm=1024):
    _, S, D = x.shape
    in_width = w_in.shape[1]
    assert in_width == 3 * ATTN_WIDTH + POOL_WIDTH + 2 * D
    assert S % tm == 0
    grid = (B, S // tm)
    qkv_shapes, qkv_specs = [], []
    for _ in range(3):
        for (_, dil) in ATTN_CONFIGS:
            assert tm % (dil * 16) == 0
            qkv_shapes.append(jax.ShapeDtypeStruct((B, dil, S // dil, GROUP_WIDTH), BF16))
            qkv_specs.append(pl.BlockSpec((None, dil, tm // dil, GROUP_WIDTH), lambda b, i: (b, 0, i, 0)))
    row_spec = lambda width: pl.BlockSpec((None, tm, width), lambda b, i: (b, i, 0))
    out_shape = qkv_shapes + [jax.ShapeDtypeStruct((B, S, POOL_WIDTH), F32),
                              jax.ShapeDtypeStruct((B, S, D), BF16),
                              jax.ShapeDtypeStruct((B, S, D), BF16)]
    out_specs = qkv_specs + [row_spec(POOL_WIDTH), row_spec(D), row_spec(D)]
    outs = pl.pallas_call(
        functools.partial(_proj_kernel, tm=tm, d_model=D),
        grid=grid,
        in_specs=[pl.BlockSpec((None, tm, D), lambda b, i: (b + b0, i, 0)),
                  pl.BlockSpec((D, in_width), lambda b, i: (0, 0), pipeline_mode=pl.Buffered(1)),
                  pl.BlockSpec((1, in_width), lambda b, i: (0, 0))],
        out_specs=out_specs,
        out_shape=out_shape,
        scratch_shapes=[pltpu.VMEM((tm, D), BF16),
                        pltpu.VMEM((3 * sum(dil > 1 for _, dil in ATTN_CONFIGS) * HALVES, tm, LANES), F32)],
        compiler_params=pltpu.CompilerParams(dimension_semantics=("parallel", "parallel"),
                                             vmem_limit_bytes=VMEM_LIMIT_BYTES),
        name="proj",
    )(x, w_in.astype(BF16), b_in.reshape(1, in_width))
    qkv = [o.reshape(B, S, GROUP_WIDTH) for o in outs[:9]]
    return qkv, outs[9], outs[10], outs[11]


def _t5_causal_bucket(dist):
    max_exact = N_REL_BUCKETS // 2
    is_small = dist < max_exact
    d = jnp.maximum(dist, 1).astype(F32)
    large = max_exact + (jnp.log(d / max_exact) / math.log(REL_MAX_DISTANCE / max_exact)
                         * (N_REL_BUCKETS - max_exact)).astype(jnp.int32)
    large = jnp.minimum(large, N_REL_BUCKETS - 1)
    return jnp.where(is_small, dist, large)


def _attn_bias(rel_bias_table):
    full, first = [], []
    for g, (window, dil) in enumerate(ATTN_CONFIGS):
        span = window // dil
        table = rel_bias_table[:, g * HEADS_PER_GROUP:(g + 1) * HEADS_PER_GROUP].astype(F32)
        lq = np.arange(BLOCK)
        for lk, dst in ((np.arange(-BLOCK, BLOCK), full), (lq, first)):
            step = jnp.asarray(lq[:, None] - lk[None, :], jnp.int32)
            in_window = (step >= 0) & (step <= span)
            bucket = _t5_causal_bucket(jnp.clip(step, 0, span) * dil)
            bias = jnp.einsum('qkb,bh->hqk', jax.nn.one_hot(bucket, N_REL_BUCKETS, dtype=F32), table,
                              precision=lax.Precision.HIGHEST)
            bias = jnp.where(in_window[None], bias, NEG_INF)
            dst.append(bias.reshape(HEADS_PER_GROUP * BLOCK, lk.shape[0]))
    return jnp.stack(full), jnp.stack(first)


def _run_skewed(chains, n_stages):
    for step in range(n_stages + len(chains) - 1):
        for lag, chain in enumerate(chains):
            if 0 <= step - lag < n_stages:
                next(chain)


ATTN_STAGES = 4


def _attn_kernel(*refs, seq):
    qkv = refs[:9]
    bias_ref, bias_first_ref, out_ref, o_scr, l_scr = refs[9:]
    rows = HEADS_PER_GROUP * BLOCK
    row_head = lax.broadcasted_iota(jnp.int32, (rows, GROUP_WIDTH), 0) // BLOCK
    lane_head_r = lax.broadcasted_iota(jnp.int32, (rows, GROUP_WIDTH), 1) // HEAD_DIM
    head_mask = row_head == lane_head_r
    lane_head = lax.broadcasted_iota(jnp.int32, (BLOCK, GROUP_WIDTH), 1) // HEAD_DIM

    def per_head_to_lanes(stacked):
        out = stacked[0:BLOCK]
        for h in range(1, HEADS_PER_GROUP):
            out = jnp.where(lane_head == h, stacked[h * BLOCK:(h + 1) * BLOCK], out)
        return out

    for g, (_, dil) in enumerate(ATTN_CONFIGS):
        q_ref, k_ref, v_ref = qkv[3 * g:3 * g + 3]
        sub_len = seq // dil
        n_blocks = sub_len // BLOCK

        def block(r, n, first, g=g, dil=dil, q_ref=q_ref, k_ref=k_ref, v_ref=v_ref, sub_len=sub_len):
            base = pl.multiple_of(r * sub_len + n * BLOCK, BLOCK)
            qb = q_ref[pl.ds(base, BLOCK), :]
            if first:
                kk = k_ref[pl.ds(base, BLOCK), :]
                vv = v_ref[pl.ds(base, BLOCK), :]
                bias = bias_first_ref[g]
            else:
                kbase = pl.multiple_of(base - BLOCK, BLOCK)
                kk = k_ref[pl.ds(kbase, 2 * BLOCK), :]
                vv = v_ref[pl.ds(kbase, 2 * BLOCK), :]
                bias = bias_ref[g]
            qs = jnp.where(head_mask, jnp.concatenate([qb] * HEADS_PER_GROUP, axis=0), jnp.zeros((), BF16))
            logits = lax.dot_general(qs, kk, (((1,), (1,)), ((), ())), preferred_element_type=F32) + bias
            yield
            m = jnp.max(logits, axis=1, keepdims=True)
            p = jnp.exp(logits - m)
            s = jnp.sum(p, axis=1, keepdims=True)
            p = p.astype(BF16)
            yield
            pv = jnp.dot(p, vv, preferred_element_type=F32)
            yield
            o = per_head_to_lanes(pv * (1.0 / s))
            lse = per_head_to_lanes(jnp.broadcast_to(m + jnp.log(s), (rows, GROUP_WIDTH)))
            start = n * (BLOCK * dil) + r
            if dil == 1:
                dst = pl.ds(pl.multiple_of(start, BLOCK), BLOCK)
            else:
                dst = pl.ds(start, BLOCK, stride=dil)
            for half in range(HALVES):
                cols = slice(half * LANES, (half + 1) * LANES)
                o_scr[g * HALVES + half, dst, :] = o[:, cols]
                l_scr[g * HALVES + half, dst, :] = lse[:, cols]
            yield

        if n_blocks > 1:
            group = 3 if (n_blocks - 1) % 3 == 0 else 1

            _run_skewed([block(r, 0, True) for r in range(dil)], ATTN_STAGES)

            def per_subsequence(r, carry, block=block, n_blocks=n_blocks, group=group):
                def per_group(i, c):
                    _run_skewed([block(r, 1 + i * group + k, False) for k in range(group)], ATTN_STAGES)
                    return c
                lax.fori_loop(0, (n_blocks - 1) // group, per_group, 0)
                return carry
            lax.fori_loop(0, dil, per_subsequence, 0)
        else:
            group = 4 if dil % 4 == 0 else 1

            def per_group(i, carry, block=block, group=group):
                _run_skewed([block(i * group + k, 0, True) for k in range(group)], ATTN_STAGES)
                return carry
            lax.fori_loop(0, dil // group, per_group, 0)

    chunk = 256

    def merge(i, carry):
        sl = pl.ds(pl.multiple_of(i * chunk, chunk), chunk)
        for half in range(HALVES):
            ls = [l_scr[g * HALVES + half, sl, :] for g in range(N_GROUPS)]
            m = functools.reduce(jnp.maximum, ls)
            es = [jnp.exp(l - m) for l in ls]
            den = functools.reduce(lambda a, b: a + b, es)
            num = functools.reduce(lambda a, b: a + b,
                                   [e * o_scr[g * HALVES + half, sl, :] for g, e in enumerate(es)])
            out_ref[sl, half * LANES:(half + 1) * LANES] = (num / den).astype(BF16)
        return carry

    lax.fori_loop(0, seq // chunk, merge, 0)


def _attn_call(qkv, rel_bias_table):
    B, S, _ = qkv[0].shape
    for (_, dil) in ATTN_CONFIGS:
        assert S % (dil * BLOCK) == 0
    bias, bias_first = _attn_bias(rel_bias_table)
    seq_spec = pl.BlockSpec((None, S, GROUP_WIDTH), lambda b: (b, 0, 0))
    ordered = []
    for g in range(N_GROUPS):
        ordered += [qkv[g], qkv[3 + g], qkv[6 + g]]
    return pl.pallas_call(
        functools.partial(_attn_kernel, seq=S),
        grid=(B,),
        in_specs=[seq_spec] * 9 + [pl.BlockSpec(bias.shape, lambda b: (0, 0, 0)),
                                   pl.BlockSpec(bias_first.shape, lambda b: (0, 0, 0))],
        out_specs=seq_spec,
        out_shape=jax.ShapeDtypeStruct((B, S, GROUP_WIDTH), BF16),
        scratch_shapes=[pltpu.VMEM((N_GROUPS * HALVES, S, LANES), F32),
                        pltpu.VMEM((N_GROUPS * HALVES, S, LANES), F32)],
        compiler_params=pltpu.CompilerParams(dimension_semantics=("parallel",),
                                             vmem_limit_bytes=VMEM_LIMIT_BYTES),
        name="attn",
    )(*ordered, bias, bias_first)


PACK_CHUNKS = 4
SUBLANES = 8
ROUTER_LANES = LANES // 2


def _pack_bf16_pairs(rounded):
    w = rounded.shape[1] // 2
    bits = lax.bitcast_convert_type(rounded, jnp.int32)
    return bits[:, :w] | lax.shift_right_logical(bits[:, w:], jnp.full((), 16, jnp.int32))


def _unpack_bf16_pairs(words):
    hi = lax.bitcast_convert_type(words & jnp.int32(-65536), F32).astype(BF16)
    lo = lax.bitcast_convert_type(lax.shift_left(words, jnp.full((), 16, jnp.int32)), F32).astype(BF16)
    return jnp.concatenate([hi, lo], axis=1)


def _post_kernel(a_ref, u_ref, halo_ref, ga_ref, gb_ref, x_ref,
                 pa_ref, wpool_ref, pscale_ref, pb_ref, wout_ref, g1_ref, b1_ref,
                 wr_cat_ref, br_ref,
                 h_ref, hb_ref, ids_ref, wts_ref, cnt_ref, pool_scr, tmp_scr, *, tm, sub, alpha):
    i = pl.program_id(1)
    halo = halo_ref[...]
    pool_scr[0:POOL_HALO, :] = jnp.zeros_like(halo)
    pool_scr[POOL_HALO:2 * POOL_HALO, :] = jnp.where(i > 0, halo, jnp.zeros_like(halo))
    pool_scr[2 * POOL_HALO:, :] = u_ref[...]
    tmp_scr[:, 0:POOL_HALO, :] = jnp.zeros((tmp_scr.shape[0], POOL_HALO, POOL_GROUP_DIM), F32)
    head_pos = i * tm + lax.broadcasted_iota(jnp.int32, (POOL_HALO, POOL_GROUP_DIM), 0)

    def rows_chain(r0):
        rs = pl.ds(r0, sub)
        group_cols = [slice(gi * POOL_GROUP_DIM, (gi + 1) * POOL_GROUP_DIM) for gi in range(len(POOL_SIZES))]
        diffs = []
        ext = sub + 2 * POOL_HALO
        for gi, (cols, w) in enumerate(zip(group_cols, POOL_SIZES)):
            ug = u_ref[rs, cols]
            levels = w.bit_length() - 1
            for level in range(levels):
                shift = 1 << level
                dst = tmp_scr.at[(r0 // sub * len(POOL_SIZES) + gi) * 2 + level % 2]
                if level == 0:
                    cur = pool_scr[pl.ds(r0 + POOL_HALO, ext - POOL_HALO), cols]
                    back = pool_scr[pl.ds(r0 + POOL_HALO - shift, ext - POOL_HALO), cols]
                else:
                    src = tmp_scr.at[(r0 // sub * len(POOL_SIZES) + gi) * 2 + (level - 1) % 2]
                    cur = src[pl.ds(POOL_HALO, ext - POOL_HALO), :]
                    back = src[pl.ds(POOL_HALO - shift, ext - POOL_HALO), :]
                if level == levels - 1:
                    acc = (cur + back)[POOL_HALO:]
                else:
                    dst[pl.ds(POOL_HALO, ext - POOL_HALO), :] = cur + back
            inv_count = jnp.full((sub, POOL_GROUP_DIM), 1.0 / w, F32)
            if r0 == 0:
                inv_count = jnp.concatenate([1.0 / jnp.minimum(head_pos + 1, w).astype(F32),
                                             inv_count[POOL_HALO:]], axis=0)
            diffs.append((acc * inv_count - ug).astype(BF16))
        yield
        mixed_in = [(jnp.dot(diff, wpool_ref[gi], preferred_element_type=F32) * pscale_ref[:, cols]).astype(BF16)
                    for gi, (diff, cols) in enumerate(zip(diffs, group_cols))]
        y_pool = jnp.dot(jnp.concatenate(mixed_in, axis=1), pb_ref[...], preferred_element_type=F32)
        y_attn = jnp.dot(a_ref[rs, :], pa_ref[...], preferred_element_type=F32)
        yield
        mixed = ga_ref[rs, :] * y_attn.astype(BF16) + gb_ref[rs, :] * y_pool.astype(BF16)
        yield
        y = jnp.dot(mixed, wout_ref[...], preferred_element_type=F32)
        yield
        h = _layer_norm(alpha * x_ref[rs, :] + y, g1_ref[...], b1_ref[...])
        h_ref[rs, :] = h
        h_hi = h.astype(BF16)
        h_rounded = h_hi.astype(F32)
        packed = _pack_bf16_pairs(h_rounded)
        for c in range(PACK_CHUNKS):
            hb_ref[c, rs, :] = packed[:, c * LANES:(c + 1) * LANES]

        h_lo = (h - h_rounded).astype(BF16)
        yield
        nt = (((1,), (1,)), ((), ()))
        both = lax.dot_general(h_hi, wr_cat_ref[...], nt, preferred_element_type=F32)
        logits = (both + pltpu.roll(both, ROUTER_LANES, axis=1)
                  + lax.dot_general(h_lo, wr_cat_ref[...], nt, preferred_element_type=F32)) + br_ref[...]
        yield
        lane = lax.broadcasted_iota(jnp.int32, logits.shape, 1)
        big = jnp.int32(2 ** 30)
        is_group = lane < N_EXPERT_GROUPS
        gl = jnp.where(is_group, logits, -jnp.inf)
        gmax = jnp.max(gl, axis=1, keepdims=True)
        g_idx = jnp.min(jnp.where(gl == gmax, lane, big), axis=1, keepdims=True)
        g_prob = 1.0 / jnp.sum(jnp.exp(gl - gmax), axis=1, keepdims=True)
        expert = lane - N_EXPERT_GROUPS
        in_group = (expert >= g_idx * EXPERTS_PER_GROUP) & (expert < (g_idx + 1) * EXPERTS_PER_GROUP)
        el = jnp.where(in_group, logits, -jnp.inf)
        v1 = jnp.max(el, axis=1, keepdims=True)
        i1 = jnp.min(jnp.where(el == v1, expert, big), axis=1, keepdims=True)
        el2 = jnp.where(expert == i1, -jnp.inf, el)
        v2 = jnp.max(el2, axis=1, keepdims=True)
        i2 = jnp.min(jnp.where(el2 == v2, expert, big), axis=1, keepdims=True)
        e2 = jnp.exp(v2 - v1)
        den = 1.0 + e2
        ids_ref[rs, :] = jnp.concatenate([i1, i2], axis=1)
        wts_ref[rs, :] = jnp.concatenate([1.0 / den * g_prob, e2 / den * g_prob], axis=1)
        chosen = jnp.where((lane == i1) | (lane == i2), 1.0, 0.0)
        counts.append(jnp.sum(chosen, axis=0, keepdims=True))
        yield

    counts = []
    chains = [rows_chain(r0) for r0 in range(0, tm, sub)]
    n_stages = 7
    for step in range(n_stages + len(chains) - 1):
        for lag, chain in enumerate(chains):
            if 0 <= step - lag < n_stages:
                next(chain)
    cnt_ref[...] = jnp.broadcast_to(functools.reduce(lambda a, b: a + b, counts), cnt_ref.shape)


def _post_call(a, u, ga, gb, x, w_proj_attn, w_pool, pool_scale, w_proj_pool, w_out, gamma, beta,
               w_router_group, b_router_group, w_router_expert, b_router_expert, alpha, b0, tm=1024, n_sub=4):
    B = a.shape[0]
    _, S, D = x.shape
    assert S % tm == 0 and tm % POOL_HALO == 0
    n_logits = N_EXPERT_GROUPS + N_EXPERTS
    assert n_logits <= ROUTER_LANES
    wr = jnp.concatenate([w_router_group, w_router_expert], axis=1).T
    wr = jnp.pad(wr, ((0, ROUTER_LANES - n_logits), (0, 0)))
    wr_hi = wr.astype(BF16)
    wr_lo = (wr - wr_hi.astype(F32)).astype(BF16)
    wr_cat = jnp.concatenate([wr_hi, wr_lo], axis=0)
    br = jnp.pad(jnp.concatenate([b_router_group, b_router_expert]),
                 (0, 2 * ROUTER_LANES - n_logits)).reshape(1, 2 * ROUTER_LANES)
    n_tiles = B * (S // tm)
    row_spec = lambda width: pl.BlockSpec((None, tm, width), lambda b, i: (b, i, 0))
    full = lambda arr: pl.BlockSpec(arr.shape, lambda b, i: (0,) * arr.ndim)
    halo_blocks = tm // POOL_HALO
    halo_spec = pl.BlockSpec((None, POOL_HALO, POOL_WIDTH),
                             lambda b, i: (b, jnp.maximum(i * halo_blocks - 1, 0), 0))
    weights = [w_proj_attn.astype(BF16), w_pool.astype(BF16), pool_scale.reshape(1, POOL_WIDTH),
               w_proj_pool.astype(BF16), w_out.astype(BF16), gamma.reshape(1, D), beta.reshape(1, D),
               wr_cat, br]
    return pl.pallas_call(
        functools.partial(_post_kernel, tm=tm, sub=tm // n_sub, alpha=alpha),
        grid=(B, S // tm),
        in_specs=[row_spec(GROUP_WIDTH), row_spec(POOL_WIDTH), halo_spec, row_spec(D), row_spec(D),
                  pl.BlockSpec((None, tm, D), lambda b, i: (b + b0, i, 0))] + [full(w) for w in weights],
        out_specs=[row_spec(D),
                   pl.BlockSpec((PACK_CHUNKS, tm, LANES), lambda b, i: (0, b * (S // tm) + i, 0)),
                   row_spec(2), row_spec(2),
                   pl.BlockSpec((SUBLANES, LANES), lambda b, i: (b * (S // tm) + i, 0))],
        out_shape=[jax.ShapeDtypeStruct((B, S, D), F32),
                   jax.ShapeDtypeStruct((PACK_CHUNKS, B * S, LANES), jnp.int32),
                   jax.ShapeDtypeStruct((B, S, 2), jnp.int32), jax.ShapeDtypeStruct((B, S, 2), F32),
                   jax.ShapeDtypeStruct((n_tiles * SUBLANES, LANES), F32)],
        scratch_shapes=[pltpu.VMEM((tm + 2 * POOL_HALO, POOL_WIDTH), F32),
                        pltpu.VMEM((n_sub * len(POOL_SIZES) * 2, tm // n_sub + 2 * POOL_HALO, POOL_GROUP_DIM), F32)],
        compiler_params=pltpu.CompilerParams(dimension_semantics=("parallel", "parallel"),
                                             vmem_limit_bytes=VMEM_LIMIT_BYTES),
        name="post",
    )(a, u, u, ga, gb, x, *weights)


def _rank_kernel(ids_ref, cnt_ref, pos_ref, ends_ref, run_ref, start_ref, *, tm, tile_rows):
    i = pl.program_id(0)

    @pl.when(i == 0)
    def _():
        total = jnp.sum(cnt_ref[...], axis=0, keepdims=True) / SUBLANES
        padded = jnp.broadcast_to(jnp.ceil(total / tile_rows) * tile_rows, run_ref.shape)
        lane1 = lax.broadcasted_iota(jnp.int32, padded.shape, 1)
        incl = padded
        shift = 1
        while shift < LANES:
            incl = incl + jnp.where(lane1 >= shift, pltpu.roll(incl, shift, axis=1), 0.0)
            shift *= 2
        start_ref[...] = incl - padded
        ends_ref[...] = incl
        run_ref[...] = jnp.zeros_like(run_ref)

    ids = ids_ref[...]
    lane = lax.broadcasted_iota(jnp.int32, (tm, LANES), 1)
    oh0 = lane == ids[:, 0:1]
    oh1 = lane == ids[:, 1:2]
    onehot = jnp.where(oh0 | oh1, 1.0, 0.0)
    before = jnp.dot(_strict_lower_ones(tm), onehot.astype(BF16), preferred_element_type=F32)
    slot = start_ref[0:1, :] + run_ref[0:1, :] + before
    p0 = jnp.sum(jnp.where(oh0, slot, 0.0), axis=1, keepdims=True)
    p1 = jnp.sum(jnp.where(oh1, slot, 0.0), axis=1, keepdims=True)
    pos_ref[...] = jnp.concatenate([p0, p1], axis=1).astype(jnp.int32)
    run_ref[...] += jnp.sum(onehot, axis=0, keepdims=True)


def _strict_lower_ones(tm):
    row = lax.broadcasted_iota(jnp.int32, (tm, tm), 0)
    col = lax.broadcasted_iota(jnp.int32, (tm, tm), 1)
    return jnp.where(row > col, 1.0, 0.0).astype(BF16)


def _rank_call(ids, counts, tile_rows, tm=1024):
    N = ids.shape[0]
    assert N % tm == 0 and SUBLANES * (2 * N + N_EXPERTS * tile_rows) < 2 ** 24
    return pl.pallas_call(
        functools.partial(_rank_kernel, tm=tm, tile_rows=tile_rows),
        grid=(N // tm,),
        in_specs=[pl.BlockSpec((tm, 2), lambda i: (i, 0)),
                  pl.BlockSpec(counts.shape, lambda i: (0, 0))],
        out_specs=[pl.BlockSpec((tm, 2), lambda i: (i, 0)),
                   pl.BlockSpec((SUBLANES, LANES), lambda i: (0, 0))],
        out_shape=[jax.ShapeDtypeStruct((N, 2), jnp.int32), jax.ShapeDtypeStruct((SUBLANES, LANES), F32)],
        scratch_shapes=[pltpu.VMEM((SUBLANES, LANES), F32), pltpu.VMEM((SUBLANES, LANES), F32)],
        compiler_params=pltpu.CompilerParams(dimension_semantics=("arbitrary",),
                                             vmem_limit_bytes=VMEM_LIMIT_BYTES),
        name="rank",
    )(ids, counts)


SC_CORES = 2
SC_SUBCORES = 16
SC_WORKERS = SC_CORES * SC_SUBCORES
SC_CHUNK = 128


def _sc_mesh():
    return plsc.VectorSubcoreMesh(core_axis_name="c", subcore_axis_name="s",
                                  num_cores=SC_CORES, num_subcores=SC_SUBCORES)


def _sc_dispatch(packed, pos_t, n_rows):
    n_chunks, n_tok, width = packed.shape
    per_worker = n_tok // SC_WORKERS
    assert n_tok % (SC_WORKERS * SC_CHUNK) == 0

    @functools.partial(
        pl.kernel, mesh=_sc_mesh(),
        out_type=jax.ShapeDtypeStruct((n_chunks, n_rows, width), packed.dtype),
        scratch_types=[pltpu.VMEM((SC_CHUNK,), jnp.int32), pltpu.VMEM((SC_CHUNK,), jnp.int32),
                       pltpu.VMEM((n_chunks, SC_CHUNK, width), packed.dtype)]
                      + [pltpu.SemaphoreType.DMA] * (3 * n_chunks),
        name="sc_dispatch")
    def run(packed_hbm, pos_hbm, out_hbm, idx0, idx1, bufs, *sems):
        load_sems, sems0, sems1 = sems[:n_chunks], sems[n_chunks:2 * n_chunks], sems[2 * n_chunks:]
        worker = lax.axis_index("s") * SC_CORES + lax.axis_index("c")

        @pl.loop(0, per_worker // SC_CHUNK)
        def _(j):
            base = worker * per_worker + j * SC_CHUNK
            loads = [pltpu.async_copy(packed_hbm.at[c, pl.ds(base, SC_CHUNK)], bufs.at[c], load_sems[c])
                     for c in range(n_chunks)]
            pltpu.sync_copy(pos_hbm.at[0, pl.ds(base, SC_CHUNK)], idx0)
            pltpu.sync_copy(pos_hbm.at[1, pl.ds(base, SC_CHUNK)], idx1)
            scatters = []
            for c in range(n_chunks):
                loads[c].wait()
                scatters.append(pltpu.async_copy(bufs.at[c], out_hbm.at[c].at[idx0], sems0[c]))
                scatters.append(pltpu.async_copy(bufs.at[c], out_hbm.at[c].at[idx1], sems1[c]))
            for s in scatters:
                s.wait()

    return run(packed, pos_t)


def _sc_combine(sorted_rows, pos_t):
    n_chunks, _, width = sorted_rows.shape
    n_tok = pos_t.shape[1]
    per_worker = n_tok // SC_WORKERS
    assert n_tok % (SC_WORKERS * SC_CHUNK) == 0

    @functools.partial(
        pl.kernel, mesh=_sc_mesh(),
        out_type=jax.ShapeDtypeStruct((2, n_chunks, n_tok, width), sorted_rows.dtype),
        scratch_types=[pltpu.VMEM((SC_CHUNK,), jnp.int32),
                       pltpu.VMEM((n_chunks, SC_CHUNK, width), sorted_rows.dtype)]
                      + [pltpu.SemaphoreType.DMA] * (2 * n_chunks),
        name="sc_combine")
    def run(rows_hbm, pos_hbm, out_hbm, idx, bufs, *sems):
        gather_sems, write_sems = sems[:n_chunks], sems[n_chunks:]
        worker = lax.axis_index("s") * SC_CORES + lax.axis_index("c")

        @pl.loop(0, per_worker // SC_CHUNK)
        def _(j):
            base = worker * per_worker + j * SC_CHUNK
            for k in range(2):
                pltpu.sync_copy(pos_hbm.at[k, pl.ds(base, SC_CHUNK)], idx)
                gathers = [pltpu.async_copy(rows_hbm.at[c].at[idx], bufs.at[c], gather_sems[c])
                           for c in range(n_chunks)]
                writes = []
                for c in range(n_chunks):
                    gathers[c].wait()
                    writes.append(pltpu.async_copy(bufs.at[c], out_hbm.at[k, c, pl.ds(base, SC_CHUNK)],
                                                   write_sems[c]))
                for w in writes:
                    w.wait()

    return run(sorted_rows, pos_t)


def _expert_kernel(tile_expert_ref, n_used_ref, xs_ref, wg_ref, wu_ref, wd_ref, ys_ref):
    del tile_expert_ref

    @pl.when(pl.program_id(0) < n_used_ref[0])
    def _():
        wg = wg_ref[...].astype(BF16)
        wu = wu_ref[...].astype(BF16)
        wd = wd_ref[...].astype(BF16)
        tile_rows = xs_ref.shape[1]
        sub = tile_rows // 2

        def rows_chain(r0):
            rs = pl.ds(r0, sub)
            x = _unpack_bf16_pairs(jnp.concatenate([xs_ref[c, rs, :] for c in range(PACK_CHUNKS)], axis=1))
            yield
            gate = jnp.dot(x, wg, preferred_element_type=F32)
            up = jnp.dot(x, wu, preferred_element_type=F32)
            yield
            hidden = (jax.nn.silu(gate) * up).astype(BF16)
            yield
            y = jnp.dot(hidden, wd, preferred_element_type=F32)
            yield
            y = _pack_bf16_pairs(y.astype(BF16).astype(F32))
            for c in range(PACK_CHUNKS):
                ys_ref[c, rs, :] = y[:, c * LANES:(c + 1) * LANES]
            yield

        chains = [rows_chain(r0) for r0 in range(0, tile_rows, sub)]
        n_stages = 5
        for step in range(n_stages + len(chains) - 1):
            for lag, chain in enumerate(chains):
                if 0 <= step - lag < n_stages:
                    next(chain)


def _expert_call(xs, tile_expert, n_used, w_gate, w_up, w_down, tile_rows):
    _, n_rows, _ = xs.shape
    D = w_gate.shape[1]
    row_block = pl.BlockSpec((PACK_CHUNKS, tile_rows, LANES),
                             lambda i, te, nu: (0, jnp.minimum(i, nu[0] - 1), 0))
    return pl.pallas_call(
        _expert_kernel,
        grid_spec=pltpu.PrefetchScalarGridSpec(
            num_scalar_prefetch=2,
            grid=(n_rows // tile_rows,),
            in_specs=[row_block,
                      pl.BlockSpec((None, D, D_EXPERT), lambda i, te, nu: (te[i], 0, 0)),
                      pl.BlockSpec((None, D, D_EXPERT), lambda i, te, nu: (te[i], 0, 0)),
                      pl.BlockSpec((None, D_EXPERT, D), lambda i, te, nu: (te[i], 0, 0))],
            out_specs=row_block),
        out_shape=jax.ShapeDtypeStruct(xs.shape, xs.dtype),
        compiler_params=pltpu.CompilerParams(dimension_semantics=("arbitrary",),
                                             vmem_limit_bytes=VMEM_LIMIT_BYTES),
        name="experts",
    )(tile_expert, n_used, xs, w_gate, w_up, w_down)


def _final_kernel(h_ref, y_ref, wts_ref, g2_ref, b2_ref, *rest, alpha):
    out_ref = rest[-1]
    wts = wts_ref[...]
    z = alpha * h_ref[...]
    for k in range(2):
        yk = _unpack_bf16_pairs(jnp.concatenate([y_ref[k, c] for c in range(PACK_CHUNKS)], axis=1))
        z = z + wts[:, k:k + 1] * yk.astype(F32)
    out_ref[...] = _layer_norm(z, g2_ref[...], b2_ref[...])


def _final_call(h, y, wts, gamma, beta, alpha, row0, n_total, earlier_out, tm=512):
    N, D = h.shape
    assert N % tm == 0 and row0 % tm == 0
    operands = [h, y, wts, gamma.reshape(1, D), beta.reshape(1, D)]
    in_specs = [pl.BlockSpec((tm, D), lambda i: (i, 0)),
                pl.BlockSpec((2, PACK_CHUNKS, tm, LANES), lambda i: (0, 0, i, 0)),
                pl.BlockSpec((tm, 2), lambda i: (i, 0)),
                pl.BlockSpec((1, D), lambda i: (0, 0)),
                pl.BlockSpec((1, D), lambda i: (0, 0))]
    aliases = {}
    if earlier_out is not None:
        aliases = {len(operands): 0}
        operands.append(earlier_out)
        in_specs.append(pl.BlockSpec(memory_space=pl.ANY))
    return pl.pallas_call(
        functools.partial(_final_kernel, alpha=alpha),
        grid=(N // tm,),
        in_specs=in_specs,
        out_specs=pl.BlockSpec((tm, D), lambda i: (i + row0 // tm, 0)),
        out_shape=jax.ShapeDtypeStruct((n_total, D), F32),
        input_output_aliases=aliases,
        compiler_params=pltpu.CompilerParams(dimension_semantics=("parallel",),
                                             vmem_limit_bytes=VMEM_LIMIT_BYTES),
        name="final",
    )(*operands)


def _moe(packed, ids, counts, w_gate, w_up, w_down, tile_rows=512):
    N = ids.shape[0]
    pos, ends = _rank_call(ids, counts, tile_rows)
    pos_t = pos.T
    n_tiles = 2 * N // tile_rows + N_EXPERTS
    seg_end = ends[0, :N_EXPERTS].astype(jnp.int32)
    tile_start = jnp.arange(n_tiles, dtype=jnp.int32) * tile_rows
    tile_expert = jnp.minimum(jnp.sum(seg_end[None, :] <= tile_start[:, None], axis=1),
                              N_EXPERTS - 1).astype(jnp.int32)
    n_used = (seg_end[N_EXPERTS - 1:] // tile_rows).astype(jnp.int32)
    xs = _sc_dispatch(packed, pos_t, n_tiles * tile_rows)
    ys = _expert_call(xs, tile_expert, n_used, w_gate, w_up, w_down, tile_rows)
    return _sc_combine(ys, pos_t)


@jax.jit
def kernel(x, w_in, b_in, rel_bias_table, w_pool, pool_scale, w_proj_attn, w_proj_pool, w_out, ln1_gamma, ln1_beta, w_router_group, b_router_group, w_router_expert, b_router_expert, w_expert_gate, w_expert_up, w_expert_down, ln2_gamma, ln2_beta):
    B, S, D = x.shape
    depth = w_in.shape[0]
    alpha = (2.0 * depth) ** 0.25
    n_parts = 2 if B % 2 == 0 else 1
    nb = B // n_parts
    for layer in range(depth):
        out = None
        for part in range(n_parts):
            b0 = part * nb
            qkv, u, ga, gb = _proj_call(x, w_in[layer], b_in[layer], b0, nb)
            a = _attn_call(qkv, rel_bias_table)
            h, packed, ids, wts, counts = _post_call(
                a, u, ga, gb, x, w_proj_attn[layer], w_pool[layer], pool_scale[layer],
                w_proj_pool[layer], w_out[layer], ln1_gamma[layer], ln1_beta[layer],
                w_router_group[layer], b_router_group[layer],
                w_router_expert[layer], b_router_expert[layer], alpha, b0)
            y = _moe(packed, ids.reshape(nb * S, 2), counts,
                     w_expert_gate[layer], w_expert_up[layer], w_expert_down[layer])
            out = _final_call(h.reshape(nb * S, D), y, wts.reshape(nb * S, 2), ln2_gamma[layer],
                              ln2_beta[layer], alpha, b0 * S, B * S, out)
        x = out.reshape(B, S, D)
    return x
```

```python
import functools
import math

import jax
import jax.numpy as jnp
import numpy as np
from jax import lax
from jax.experimental import pallas as pl
from jax.experimental.pallas import tpu as pltpu
from jax.experimental.pallas import tpu_sc as plsc

F32 = jnp.float32
BF16 = jnp.bfloat16

HEAD_DIM = 64
ATTN_CONFIGS = ((128, 1), (512, 4), (2048, 16))
N_GROUPS = len(ATTN_CONFIGS)
HEADS_PER_GROUP = 4
GROUP_WIDTH = HEADS_PER_GROUP * HEAD_DIM
ATTN_WIDTH = N_GROUPS * GROUP_WIDTH
BLOCK = 128
N_REL_BUCKETS = 32
REL_MAX_DISTANCE = 2048
NEG_INF = -1e30

POOL_SIZES = (2, 4, 8, 16)
POOL_GROUP_DIM = 128
POOL_WIDTH = POOL_GROUP_DIM * len(POOL_SIZES)
POOL_HALO = 16

N_EXPERT_GROUPS = 4
EXPERTS_PER_GROUP = 8
N_EXPERTS = N_EXPERT_GROUPS * EXPERTS_PER_GROUP
D_EXPERT = 256
LN_EPS = 1e-5

VMEM_LIMIT_BYTES = 56 * 1024 * 1024
LANES = 128
HALVES = GROUP_WIDTH // LANES


def _layer_norm(z, gamma, beta):
    mu = jnp.mean(z, axis=-1, keepdims=True)
    zc = z - mu
    var = jnp.mean(zc * zc, axis=-1, keepdims=True)
    return zc * lax.rsqrt(var + LN_EPS) * gamma + beta


def _proj_kernel(x_ref, w_ref, b_ref, *refs, tm, d_model):
    qkv_refs = refs[:9]
    u_ref, ga_ref, gb_ref, xb_ref, acc_ref = refs[9:]
    xb_ref[...] = x_ref[...].astype(BF16)

    def chunk(c0, width):
        acc = jnp.dot(xb_ref[...], w_ref[:, c0:c0 + width], preferred_element_type=F32)
        return acc + b_ref[:, c0:c0 + width]

    n_staged = 0
    for which in range(3):
        for g, (_, dil) in enumerate(ATTN_CONFIGS):
            out = qkv_refs[which * 3 + g]
            acc = chunk(which * ATTN_WIDTH + g * GROUP_WIDTH, GROUP_WIDTH)
            if which == 0:
                acc = acc * HEAD_DIM ** -0.5
            if dil == 1:
                out[0] = acc.astype(BF16)
            else:
                planes = [n_staged * HALVES + half for half in range(HALVES)]
                n_staged += 1
                for half, plane in enumerate(planes):
                    acc_ref[plane] = acc[:, half * LANES:(half + 1) * LANES]
                for r in range(dil):
                    for half, plane in enumerate(planes):
                        out[r, :, half * LANES:(half + 1) * LANES] = (
                            acc_ref[plane, pl.ds(r, tm // dil, stride=dil), :].astype(BF16))
    pool_off = 3 * ATTN_WIDTH
    for c in range(POOL_WIDTH // 256):
        u_ref[:, c * 256:(c + 1) * 256] = chunk(pool_off + c * 256, 256)
    ga_off = pool_off + POOL_WIDTH
    for gate_ref, off in ((ga_ref, ga_off), (gb_ref, ga_off + d_model)):
        for c in range(d_model // 256):
            gate_ref[:, c * 256:(c + 1) * 256] = jax.nn.sigmoid(chunk(off + c * 256, 256)).astype(BF16)


def _proj_call(x, w_in, b_in, b0, B, tm=1024):
    _, S, D = x.shape
    in_width = w_in.shape[1]
    assert in_width == 3 * ATTN_WIDTH + POOL_WIDTH + 2 * D
    assert S % tm == 0
    grid = (B, S // tm)
    qkv_shapes, qkv_specs = [], []
    for _ in range(3):
        for (_, dil) in ATTN_CONFIGS:
            assert tm % (dil * 16) == 0
            qkv_shapes.append(jax.ShapeDtypeStruct((B, dil, S // dil, GROUP_WIDTH), BF16))
            qkv_specs.append(pl.BlockSpec((None, dil, tm // dil, GROUP_WIDTH), lambda b, i: (b, 0, i, 0)))
    row_spec = lambda width: pl.BlockSpec((None, tm, width), lambda b, i: (b, i, 0))
    out_shape = qkv_shapes + [jax.ShapeDtypeStruct((B, S, POOL_WIDTH), F32),
                              jax.ShapeDtypeStruct((B, S, D), BF16),
                              jax.ShapeDtypeStruct((B, S, D), BF16)]
    out_specs = qkv_specs + [row_spec(POOL_WIDTH), row_spec(D), row_spec(D)]
    outs = pl.pallas_call(
        functools.partial(_proj_kernel, tm=tm, d_model=D),
        grid=grid,
        in_specs=[pl.BlockSpec((None, tm, D), lambda b, i: (b + b0, i, 0)),
                  pl.BlockSpec((D, in_width), lambda b, i: (0, 0), pipeline_mode=pl.Buffered(1)),
                  pl.BlockSpec((1, in_width), lambda b, i: (0, 0))],
        out_specs=out_specs,
        out_shape=out_shape,
        scratch_shapes=[pltpu.VMEM((tm, D), BF16),
                        pltpu.VMEM((3 * sum(dil > 1 for _, dil in ATTN_CONFIGS) * HALVES, tm, LANES), F32)],
        compiler_params=pltpu.CompilerParams(dimension_semantics=("parallel", "parallel"),
                                             vmem_limit_bytes=VMEM_LIMIT_BYTES),
        name="proj",
    )(x, w_in.astype(BF16), b_in.reshape(1, in_width))
    qkv = [o.reshape(B, S, GROUP_WIDTH) for o in outs[:9]]
    return qkv, outs[9], outs[10], outs[11]


def _t5_causal_bucket(dist):
    max_exact = N_REL_BUCKETS // 2
    is_small = dist < max_exact
    d = jnp.maximum(dist, 1).astype(F32)
    large = max_exact + (jnp.log(d / max_exact) / math.log(REL_MAX_DISTANCE / max_exact)
                         * (N_REL_BUCKETS - max_exact)).astype(jnp.int32)
    large = jnp.minimum(large, N_REL_BUCKETS - 1)
    return jnp.where(is_small, dist, large)


def _attn_bias(rel_bias_table):
    full, first = [], []
    for g, (window, dil) in enumerate(ATTN_CONFIGS):
        span = window // dil
        table = rel_bias_table[:, g * HEADS_PER_GROUP:(g + 1) * HEADS_PER_GROUP].astype(F32)
        lq = np.arange(BLOCK)
        for lk, dst in ((np.arange(-BLOCK, BLOCK), full), (lq, first)):
            step = jnp.asarray(lq[:, None] - lk[None, :], jnp.int32)
            in_window = (step >= 0) & (step <= span)
            bucket = _t5_causal_bucket(jnp.clip(step, 0, span) * dil)
            bias = jnp.einsum('qkb,bh->hqk', jax.nn.one_hot(bucket, N_REL_BUCKETS, dtype=F32), table,
                              precision=lax.Precision.HIGHEST)
            bias = jnp.where(in_window[None], bias, NEG_INF)
            dst.append(bias.reshape(HEADS_PER_GROUP * BLOCK, lk.shape[0]))
    return jnp.stack(full), jnp.stack(first)


def _run_skewed(chains, n_stages):
    for step in range(n_stages + len(chains) - 1):
        for lag, chain in enumerate(chains):
            if 0 <= step - lag < n_stages:
                next(chain)


ATTN_STAGES = 4


def _attn_kernel(*refs, seq):
    qkv = refs[:9]
    bias_ref, bias_first_ref, out_ref, o_scr, l_scr = refs[9:]
    rows = HEADS_PER_GROUP * BLOCK
    row_head = lax.broadcasted_iota(jnp.int32, (rows, GROUP_WIDTH), 0) // BLOCK
    lane_head_r = lax.broadcasted_iota(jnp.int32, (rows, GROUP_WIDTH), 1) // HEAD_DIM
    head_mask = row_head == lane_head_r
    lane_head = lax.broadcasted_iota(jnp.int32, (BLOCK, GROUP_WIDTH), 1) // HEAD_DIM

    def per_head_to_lanes(stacked):
        out = stacked[0:BLOCK]
        for h in range(1, HEADS_PER_GROUP):
            out = jnp.where(lane_head == h, stacked[h * BLOCK:(h + 1) * BLOCK], out)
        return out

    for g, (_, dil) in enumerate(ATTN_CONFIGS):
        q_ref, k_ref, v_ref = qkv[3 * g:3 * g + 3]
        sub_len = seq // dil
        n_blocks = sub_len // BLOCK

        def block(r, n, first, g=g, dil=dil, q_ref=q_ref, k_ref=k_ref, v_ref=v_ref, sub_len=sub_len):
            base = pl.multiple_of(r * sub_len + n * BLOCK, BLOCK)
            qb = q_ref[pl.ds(base, BLOCK), :]
            if first:
                kk = k_ref[pl.ds(base, BLOCK), :]
                vv = v_ref[pl.ds(base, BLOCK), :]
                bias = bias_first_ref[g]
            else:
                kbase = pl.multiple_of(base - BLOCK, BLOCK)
                kk = k_ref[pl.ds(kbase, 2 * BLOCK), :]
                vv = v_ref[pl.ds(kbase, 2 * BLOCK), :]
                bias = bias_ref[g]
            qs = jnp.where(head_mask, jnp.concatenate([qb] * HEADS_PER_GROUP, axis=0), jnp.zeros((), BF16))
            logits = lax.dot_general(qs, kk, (((1,), (1,)), ((), ())), preferred_element_type=F32) + bias
            yield
            m = jnp.max(logits, axis=1, keepdims=True)
            p = jnp.exp(logits - m)
            s = jnp.sum(p, axis=1, keepdims=True)
            p = p.astype(BF16)
            yield
            pv = jnp.dot(p, vv, preferred_element_type=F32)
            yield
            o = per_head_to_lanes(pv * (1.0 / s))
            lse = per_head_to_lanes(jnp.broadcast_to(m + jnp.log(s), (rows, GROUP_WIDTH)))
            start = n * (BLOCK * dil) + r
            if dil == 1:
                dst = pl.ds(pl.multiple_of(start, BLOCK), BLOCK)
            else:
                dst = pl.ds(start, BLOCK, stride=dil)
            for half in range(HALVES):
                cols = slice(half * LANES, (half + 1) * LANES)
                o_scr[g * HALVES + half, dst, :] = o[:, cols]
                l_scr[g * HALVES + half, dst, :] = lse[:, cols]
            yield

        if n_blocks > 1:
            group = 3 if (n_blocks - 1) % 3 == 0 else 1

            _run_skewed([block(r, 0, True) for r in range(dil)], ATTN_STAGES)

            def per_subsequence(r, carry, block=block, n_blocks=n_blocks, group=group):
                def per_group(i, c):
                    _run_skewed([block(r, 1 + i * group + k, False) for k in range(group)], ATTN_STAGES)
                    return c
                lax.fori_loop(0, (n_blocks - 1) // group, per_group, 0)
                return carry
            lax.fori_loop(0, dil, per_subsequence, 0)
        else:
            group = next(c for c in (8, 4, 2, 1) if dil % c == 0)

            def per_group(i, carry, block=block, group=group):
                _run_skewed([block(i * group + k, 0, True) for k in range(group)], ATTN_STAGES)
                return carry
            lax.fori_loop(0, dil // group, per_group, 0)

    chunk = 256

    def merge(i, carry):
        sl = pl.ds(pl.multiple_of(i * chunk, chunk), chunk)
        for half in range(HALVES):
            ls = [l_scr[g * HALVES + half, sl, :] for g in range(N_GROUPS)]
            m = functools.reduce(jnp.maximum, ls)
            es = [jnp.exp(l - m) for l in ls]
            den = functools.reduce(lambda a, b: a + b, es)
            num = functools.reduce(lambda a, b: a + b,
                                   [e * o_scr[g * HALVES + half, sl, :] for g, e in enumerate(es)])
            out_ref[sl, half * LANES:(half + 1) * LANES] = (num / den).astype(BF16)
        return carry

    lax.fori_loop(0, seq // chunk, merge, 0)


def _attn_call(qkv, rel_bias_table):
    B, S, _ = qkv[0].shape
    for (_, dil) in ATTN_CONFIGS:
        assert S % (dil * BLOCK) == 0
    bias, bias_first = _attn_bias(rel_bias_table)
    seq_spec = pl.BlockSpec((None, S, GROUP_WIDTH), lambda b: (b, 0, 0))
    ordered = []
    for g in range(N_GROUPS):
        ordered += [qkv[g], qkv[3 + g], qkv[6 + g]]
    return pl.pallas_call(
        functools.partial(_attn_kernel, seq=S),
        grid=(B,),
        in_specs=[seq_spec] * 9 + [pl.BlockSpec(bias.shape, lambda b: (0, 0, 0)),
                                   pl.BlockSpec(bias_first.shape, lambda b: (0, 0, 0))],
        out_specs=seq_spec,
        out_shape=jax.ShapeDtypeStruct((B, S, GROUP_WIDTH), BF16),
        scratch_shapes=[pltpu.VMEM((N_GROUPS * HALVES, S, LANES), F32),
                        pltpu.VMEM((N_GROUPS * HALVES, S, LANES), F32)],
        compiler_params=pltpu.CompilerParams(dimension_semantics=("parallel",),
                                             vmem_limit_bytes=VMEM_LIMIT_BYTES),
        name="attn",
    )(*ordered, bias, bias_first)


PACK_CHUNKS = 4
SUBLANES = 8
ROUTER_LANES = LANES // 2


def _pack_bf16_pairs(rounded):
    w = rounded.shape[1] // 2
    bits = lax.bitcast_convert_type(rounded, jnp.int32)
    return bits[:, :w] | lax.shift_right_logical(bits[:, w:], jnp.full((), 16, jnp.int32))


def _unpack_bf16_pairs(words):
    hi = lax.bitcast_convert_type(words & jnp.int32(-65536), F32).astype(BF16)
    lo = lax.bitcast_convert_type(lax.shift_left(words, jnp.full((), 16, jnp.int32)), F32).astype(BF16)
    return jnp.concatenate([hi, lo], axis=1)


def _post_kernel(a_ref, u_ref, halo_ref, ga_ref, gb_ref, x_ref,
                 pa_ref, wpool_ref, pscale_ref, pb_ref, wout_ref, g1_ref, b1_ref,
                 wr_cat_ref, br_ref,
                 h_ref, hb_ref, ids_ref, wts_ref, cnt_ref, pool_scr, tmp_scr, *, tm, sub, alpha):
    i = pl.program_id(1)
    halo = halo_ref[...]
    pool_scr[0:POOL_HALO, :] = jnp.zeros_like(halo)
    pool_scr[POOL_HALO:2 * POOL_HALO, :] = jnp.where(i > 0, halo, jnp.zeros_like(halo))
    pool_scr[2 * POOL_HALO:, :] = u_ref[...]
    tmp_scr[:, 0:POOL_HALO, :] = jnp.zeros((tmp_scr.shape[0], POOL_HALO, POOL_GROUP_DIM), F32)
    head_pos = i * tm + lax.broadcasted_iota(jnp.int32, (POOL_HALO, POOL_GROUP_DIM), 0)

    def rows_chain(r0):
        rs = pl.ds(r0, sub)
        group_cols = [slice(gi * POOL_GROUP_DIM, (gi + 1) * POOL_GROUP_DIM) for gi in range(len(POOL_SIZES))]
        diffs = []
        ext = sub + 2 * POOL_HALO
        for gi, (cols, w) in enumerate(zip(group_cols, POOL_SIZES)):
            ug = u_ref[rs, cols]
            levels = w.bit_length() - 1
            for level in range(levels):
                shift = 1 << level
                dst = tmp_scr.at[(r0 // sub * len(POOL_SIZES) + gi) * 2 + level % 2]
                if level == 0:
                    cur = pool_scr[pl.ds(r0 + POOL_HALO, ext - POOL_HALO), cols]
                    back = pool_scr[pl.ds(r0 + POOL_HALO - shift, ext - POOL_HALO), cols]
                else:
                    src = tmp_scr.at[(r0 // sub * len(POOL_SIZES) + gi) * 2 + (level - 1) % 2]
                    cur = src[pl.ds(POOL_HALO, ext - POOL_HALO), :]
                    back = src[pl.ds(POOL_HALO - shift, ext - POOL_HALO), :]
                if level == levels - 1:
                    acc = (cur + back)[POOL_HALO:]
                else:
                    dst[pl.ds(POOL_HALO, ext - POOL_HALO), :] = cur + back
            inv_count = jnp.full((sub, POOL_GROUP_DIM), 1.0 / w, F32)
            if r0 == 0:
                inv_count = jnp.concatenate([1.0 / jnp.minimum(head_pos + 1, w).astype(F32),
                                             inv_count[POOL_HALO:]], axis=0)
            diffs.append((acc * inv_count - ug).astype(BF16))
        yield
        mixed_in = [(jnp.dot(diff, wpool_ref[gi], preferred_element_type=F32) * pscale_ref[:, cols]).astype(BF16)
                    for gi, (diff, cols) in enumerate(zip(diffs, group_cols))]
        y_pool = jnp.dot(jnp.concatenate(mixed_in, axis=1), pb_ref[...], preferred_element_type=F32)
        y_attn = jnp.dot(a_ref[rs, :], pa_ref[...], preferred_element_type=F32)
        yield
        mixed = ga_ref[rs, :] * y_attn.astype(BF16) + gb_ref[rs, :] * y_pool.astype(BF16)
        yield
        y = jnp.dot(mixed, wout_ref[...], preferred_element_type=F32)
        yield
        h = _layer_norm(alpha * x_ref[rs, :] + y, g1_ref[...], b1_ref[...])
        h_ref[rs, :] = h
        h_hi = h.astype(BF16)
        h_rounded = h_hi.astype(F32)
        packed = _pack_bf16_pairs(h_rounded)
        for c in range(PACK_CHUNKS):
            hb_ref[c, rs, :] = packed[:, c * LANES:(c + 1) * LANES]

        h_lo = (h - h_rounded).astype(BF16)
        yield
        nt = (((1,), (1,)), ((), ()))
        both = lax.dot_general(wr_cat_ref[...], h_hi, nt, preferred_element_type=F32)
        lo_hi = lax.dot_general(wr_cat_ref[0:ROUTER_LANES, :], h_lo, nt, preferred_element_type=F32)
        logits = both[0:ROUTER_LANES] + both[ROUTER_LANES:] + lo_hi + br_ref[...]
        yield
        row = lax.broadcasted_iota(jnp.int32, logits.shape, 0)
        big = jnp.int32(2 ** 30)
        is_group = row < N_EXPERT_GROUPS
        gl = jnp.where(is_group, logits, -jnp.inf)
        gmax = jnp.max(gl, axis=0, keepdims=True)
        g_idx = jnp.min(jnp.where(gl == gmax, row, big), axis=0, keepdims=True)
        g_prob = 1.0 / jnp.sum(jnp.exp(gl - gmax), axis=0, keepdims=True)
        expert = row - N_EXPERT_GROUPS
        in_group = (expert >= g_idx * EXPERTS_PER_GROUP) & (expert < (g_idx + 1) * EXPERTS_PER_GROUP)
        el = jnp.where(in_group, logits, -jnp.inf)
        v1 = jnp.max(el, axis=0, keepdims=True)
        i1 = jnp.min(jnp.where(el == v1, expert, big), axis=0, keepdims=True)
        el2 = jnp.where(expert == i1, -jnp.inf, el)
        v2 = jnp.max(el2, axis=0, keepdims=True)
        i2 = jnp.min(jnp.where(el2 == v2, expert, big), axis=0, keepdims=True)
        e2 = jnp.exp(v2 - v1)
        den = 1.0 + e2
        first_row = lax.broadcasted_iota(jnp.int32, (SUBLANES, sub), 0) == 0
        wts_ref[:, rs] = jnp.where(first_row, 1.0 / den * g_prob, e2 / den * g_prob)
        ids_ref[:, rs] = jnp.where(first_row, i1, i2).astype(F32)
        chosen = jnp.where((expert == i1) | (expert == i2), 1.0, 0.0)
        per_expert = jnp.broadcast_to(jnp.sum(chosen, axis=1, keepdims=True), (ROUTER_LANES, LANES))
        to_lane = (lax.broadcasted_iota(jnp.int32, (ROUTER_LANES, LANES), 0) - N_EXPERT_GROUPS
                   == lax.broadcasted_iota(jnp.int32, (ROUTER_LANES, LANES), 1))
        counts.append(jnp.sum(jnp.where(to_lane, per_expert, 0.0), axis=0, keepdims=True))
        yield

    counts = []
    chains = [rows_chain(r0) for r0 in range(0, tm, sub)]
    n_stages = 7
    for step in range(n_stages + len(chains) - 1):
        for lag, chain in enumerate(chains):
            if 0 <= step - lag < n_stages:
                next(chain)
    cnt_ref[...] = jnp.broadcast_to(functools.reduce(lambda a, b: a + b, counts), cnt_ref.shape)


def _post_call(a, u, ga, gb, x, w_proj_attn, w_pool, pool_scale, w_proj_pool, w_out, gamma, beta,
               w_router_group, b_router_group, w_router_expert, b_router_expert, alpha, b0, tm=1024, n_sub=4):
    B = a.shape[0]
    _, S, D = x.shape
    assert S % tm == 0 and tm % POOL_HALO == 0
    n_logits = N_EXPERT_GROUPS + N_EXPERTS
    assert n_logits <= ROUTER_LANES
    wr = jnp.concatenate([w_router_group, w_router_expert], axis=1).T
    wr = jnp.pad(wr, ((0, ROUTER_LANES - n_logits), (0, 0)))
    wr_hi = wr.astype(BF16)
    wr_lo = (wr - wr_hi.astype(F32)).astype(BF16)
    wr_cat = jnp.concatenate([wr_hi, wr_lo], axis=0)
    br = jnp.pad(jnp.concatenate([b_router_group, b_router_expert]),
                 (0, ROUTER_LANES - n_logits)).reshape(ROUTER_LANES, 1)
    n_tiles = B * (S // tm)
    row_spec = lambda width: pl.BlockSpec((None, tm, width), lambda b, i: (b, i, 0))
    full = lambda arr: pl.BlockSpec(arr.shape, lambda b, i: (0,) * arr.ndim)
    halo_blocks = tm // POOL_HALO
    halo_spec = pl.BlockSpec((None, POOL_HALO, POOL_WIDTH),
                             lambda b, i: (b, jnp.maximum(i * halo_blocks - 1, 0), 0))
    weights = [w_proj_attn.astype(BF16), w_pool.astype(BF16), pool_scale.reshape(1, POOL_WIDTH),
               w_proj_pool.astype(BF16), w_out.astype(BF16), gamma.reshape(1, D), beta.reshape(1, D),
               wr_cat, br]
    return pl.pallas_call(
        functools.partial(_post_kernel, tm=tm, sub=tm // n_sub, alpha=alpha),
        grid=(B, S // tm),
        in_specs=[row_spec(GROUP_WIDTH), row_spec(POOL_WIDTH), halo_spec, row_spec(D), row_spec(D),
                  pl.BlockSpec((None, tm, D), lambda b, i: (b + b0, i, 0))] + [full(w) for w in weights],
        out_specs=[row_spec(D),
                   pl.BlockSpec((PACK_CHUNKS, tm, LANES), lambda b, i: (0, b * (S // tm) + i, 0)),
                   pl.BlockSpec((SUBLANES, tm), lambda b, i: (0, b * (S // tm) + i)),
                   pl.BlockSpec((SUBLANES, tm), lambda b, i: (0, b * (S // tm) + i)),
                   pl.BlockSpec((SUBLANES, LANES), lambda b, i: (b * (S // tm) + i, 0))],
        out_shape=[jax.ShapeDtypeStruct((B, S, D), F32),
                   jax.ShapeDtypeStruct((PACK_CHUNKS, B * S, LANES), jnp.int32),
                   jax.ShapeDtypeStruct((SUBLANES, B * S), F32), jax.ShapeDtypeStruct((SUBLANES, B * S), F32),
                   jax.ShapeDtypeStruct((n_tiles * SUBLANES, LANES), F32)],
        scratch_shapes=[pltpu.VMEM((tm + 2 * POOL_HALO, POOL_WIDTH), F32),
                        pltpu.VMEM((n_sub * len(POOL_SIZES) * 2, tm // n_sub + 2 * POOL_HALO, POOL_GROUP_DIM), F32)],
        compiler_params=pltpu.CompilerParams(dimension_semantics=("parallel", "parallel"),
                                             vmem_limit_bytes=VMEM_LIMIT_BYTES),
        name="post",
    )(a, u, u, ga, gb, x, *weights)


def _rank_kernel(ids_ref, cnt_ref, pos_ref, ends_ref, run_ref, start_ref, earlier_ref, *, tm, tile_rows):
    i = pl.program_id(0)

    @pl.when(i == 0)
    def _():
        total = jnp.sum(cnt_ref[...], axis=0, keepdims=True) / SUBLANES
        padded = jnp.broadcast_to(jnp.ceil(total / tile_rows) * tile_rows, ends_ref.shape)
        lane1 = lax.broadcasted_iota(jnp.int32, padded.shape, 1)
        incl = padded
        shift = 1
        while shift < LANES:
            incl = incl + jnp.where(lane1 >= shift, pltpu.roll(incl, shift, axis=1), 0.0)
            shift *= 2
        ends_ref[...] = incl
        start_row = jnp.broadcast_to((incl - padded)[0:1, :], start_ref.shape)
        on_diagonal = (lax.broadcasted_iota(jnp.int32, start_ref.shape, 0)
                       == lax.broadcasted_iota(jnp.int32, start_ref.shape, 1))
        start_col = jnp.sum(jnp.where(on_diagonal, start_row, 0.0), axis=1, keepdims=True)
        start_ref[...] = jnp.broadcast_to(start_col, start_ref.shape)
        run_ref[...] = jnp.zeros_like(run_ref)
        row = lax.broadcasted_iota(jnp.int32, (tm, tm), 0)
        col = lax.broadcasted_iota(jnp.int32, (tm, tm), 1)
        earlier_ref[...] = jnp.where(row < col, 1.0, 0.0).astype(BF16)

    ids = ids_ref[...]
    expert = lax.broadcasted_iota(jnp.int32, (N_EXPERTS, tm), 0).astype(F32)
    oh0 = expert == ids[0:1, :]
    oh1 = expert == ids[1:2, :]
    onehot = jnp.where(oh0 | oh1, 1.0, 0.0)
    before = jnp.dot(onehot.astype(BF16), earlier_ref[...], preferred_element_type=F32)
    slot = start_ref[:, 0:1] + run_ref[:, 0:1] + before
    p0 = jnp.sum(jnp.where(oh0, slot, 0.0), axis=0, keepdims=True)
    p1 = jnp.sum(jnp.where(oh1, slot, 0.0), axis=0, keepdims=True)
    first_row = lax.broadcasted_iota(jnp.int32, pos_ref.shape, 0) == 0
    pos_ref[...] = jnp.where(first_row, p0, p1).astype(jnp.int32)
    run_ref[...] += jnp.broadcast_to(jnp.sum(onehot, axis=1, keepdims=True), run_ref.shape)


def _rank_call(ids_t, counts, tile_rows, tm=1024):
    N = ids_t.shape[1]
    assert N % tm == 0 and SUBLANES * (2 * N + N_EXPERTS * tile_rows) < 2 ** 24
    assert N_EXPERTS <= LANES
    return pl.pallas_call(
        functools.partial(_rank_kernel, tm=tm, tile_rows=tile_rows),
        grid=(N // tm,),
        in_specs=[pl.BlockSpec((SUBLANES, tm), lambda i: (0, i)),
                  pl.BlockSpec(counts.shape, lambda i: (0, 0))],
        out_specs=[pl.BlockSpec((SUBLANES, tm), lambda i: (0, i)),
                   pl.BlockSpec((SUBLANES, LANES), lambda i: (0, 0))],
        out_shape=[jax.ShapeDtypeStruct((SUBLANES, N), jnp.int32),
                   jax.ShapeDtypeStruct((SUBLANES, LANES), F32)],
        scratch_shapes=[pltpu.VMEM((N_EXPERTS, LANES), F32), pltpu.VMEM((N_EXPERTS, LANES), F32),
                        pltpu.VMEM((tm, tm), BF16)],
        compiler_params=pltpu.CompilerParams(dimension_semantics=("arbitrary",),
                                             vmem_limit_bytes=VMEM_LIMIT_BYTES),
        name="rank",
    )(ids_t, counts)


SC_CORES = 2
SC_SUBCORES = 16
SC_WORKERS = SC_CORES * SC_SUBCORES
SC_CHUNK = 128


def _sc_mesh():
    return plsc.VectorSubcoreMesh(core_axis_name="c", subcore_axis_name="s",
                                  num_cores=SC_CORES, num_subcores=SC_SUBCORES)


def _sc_dispatch(packed, pos_t, n_rows):
    n_chunks, n_tok, width = packed.shape
    per_worker = n_tok // SC_WORKERS
    assert n_tok % (SC_WORKERS * SC_CHUNK) == 0

    @functools.partial(
        pl.kernel, mesh=_sc_mesh(),
        out_type=jax.ShapeDtypeStruct((n_chunks, n_rows, width), packed.dtype),
        scratch_types=[pltpu.VMEM((SC_CHUNK,), jnp.int32), pltpu.VMEM((SC_CHUNK,), jnp.int32),
                       pltpu.VMEM((n_chunks, SC_CHUNK, width), packed.dtype)]
                      + [pltpu.SemaphoreType.DMA] * (3 * n_chunks),
        name="sc_dispatch")
    def run(packed_hbm, pos_hbm, out_hbm, idx0, idx1, bufs, *sems):
        load_sems, sems0, sems1 = sems[:n_chunks], sems[n_chunks:2 * n_chunks], sems[2 * n_chunks:]
        worker = lax.axis_index("s") * SC_CORES + lax.axis_index("c")

        @pl.loop(0, per_worker // SC_CHUNK)
        def _(j):
            base = worker * per_worker + j * SC_CHUNK
            loads = [pltpu.async_copy(packed_hbm.at[c, pl.ds(base, SC_CHUNK)], bufs.at[c], load_sems[c])
                     for c in range(n_chunks)]
            pltpu.sync_copy(pos_hbm.at[0, pl.ds(base, SC_CHUNK)], idx0)
            pltpu.sync_copy(pos_hbm.at[1, pl.ds(base, SC_CHUNK)], idx1)
            scatters = []
            for c in range(n_chunks):
                loads[c].wait()
                scatters.append(pltpu.async_copy(bufs.at[c], out_hbm.at[c].at[idx0], sems0[c]))
                scatters.append(pltpu.async_copy(bufs.at[c], out_hbm.at[c].at[idx1], sems1[c]))
            for s in scatters:
                s.wait()

    return run(packed, pos_t)


def _sc_combine(sorted_rows, pos_t):
    n_chunks, _, width = sorted_rows.shape
    n_tok = pos_t.shape[1]
    per_worker = n_tok // SC_WORKERS
    assert n_tok % (SC_WORKERS * SC_CHUNK) == 0

    @functools.partial(
        pl.kernel, mesh=_sc_mesh(),
        out_type=jax.ShapeDtypeStruct((2, n_chunks, n_tok, width), sorted_rows.dtype),
        scratch_types=[pltpu.VMEM((SC_CHUNK,), jnp.int32),
                       pltpu.VMEM((n_chunks, SC_CHUNK, width), sorted_rows.dtype)]
                      + [pltpu.SemaphoreType.DMA] * (2 * n_chunks),
        name="sc_combine")
    def run(rows_hbm, pos_hbm, out_hbm, idx, bufs, *sems):
        gather_sems, write_sems = sems[:n_chunks], sems[n_chunks:]
        worker = lax.axis_index("s") * SC_CORES + lax.axis_index("c")

        @pl.loop(0, per_worker // SC_CHUNK)
        def _(j):
            base = worker * per_worker + j * SC_CHUNK
            for k in range(2):
                pltpu.sync_copy(pos_hbm.at[k, pl.ds(base, SC_CHUNK)], idx)
                gathers = [pltpu.async_copy(rows_hbm.at[c].at[idx], bufs.at[c], gather_sems[c])
                           for c in range(n_chunks)]
                writes = []
                for c in range(n_chunks):
                    gathers[c].wait()
                    writes.append(pltpu.async_copy(bufs.at[c], out_hbm.at[k, c, pl.ds(base, SC_CHUNK)],
                                                   write_sems[c]))
                for w in writes:
                    w.wait()

    return run(sorted_rows, pos_t)


def _expert_kernel(tile_expert_ref, n_used_ref, xs_ref, wg_ref, wu_ref, wd_ref, ys_ref):
    del tile_expert_ref

    @pl.when(pl.program_id(0) < n_used_ref[0])
    def _():
        wg = wg_ref[...].astype(BF16)
        wu = wu_ref[...].astype(BF16)
        wd = wd_ref[...].astype(BF16)
        tile_rows = xs_ref.shape[1]
        sub = tile_rows // 2

        def rows_chain(r0):
            rs = pl.ds(r0, sub)
            x = _unpack_bf16_pairs(jnp.concatenate([xs_ref[c, rs, :] for c in range(PACK_CHUNKS)], axis=1))
            yield
            gate = jnp.dot(x, wg, preferred_element_type=F32)
            up = jnp.dot(x, wu, preferred_element_type=F32)
            yield
            hidden = (jax.nn.silu(gate) * up).astype(BF16)
            yield
            y = jnp.dot(hidden, wd, preferred_element_type=F32)
            yield
            y = _pack_bf16_pairs(y.astype(BF16).astype(F32))
            for c in range(PACK_CHUNKS):
                ys_ref[c, rs, :] = y[:, c * LANES:(c + 1) * LANES]
            yield

        chains = [rows_chain(r0) for r0 in range(0, tile_rows, sub)]
        n_stages = 5
        for step in range(n_stages + len(chains) - 1):
            for lag, chain in enumerate(chains):
                if 0 <= step - lag < n_stages:
                    next(chain)


def _expert_call(xs, tile_expert, n_used, w_gate, w_up, w_down, tile_rows):
    _, n_rows, _ = xs.shape
    D = w_gate.shape[1]
    row_block = pl.BlockSpec((PACK_CHUNKS, tile_rows, LANES),
                             lambda i, te, nu: (0, jnp.minimum(i, nu[0] - 1), 0))
    return pl.pallas_call(
        _expert_kernel,
        grid_spec=pltpu.PrefetchScalarGridSpec(
            num_scalar_prefetch=2,
            grid=(n_rows // tile_rows,),
            in_specs=[row_block,
                      pl.BlockSpec((None, D, D_EXPERT), lambda i, te, nu: (te[i], 0, 0)),
                      pl.BlockSpec((None, D, D_EXPERT), lambda i, te, nu: (te[i], 0, 0)),
                      pl.BlockSpec((None, D_EXPERT, D), lambda i, te, nu: (te[i], 0, 0))],
            out_specs=row_block),
        out_shape=jax.ShapeDtypeStruct(xs.shape, xs.dtype),
        compiler_params=pltpu.CompilerParams(dimension_semantics=("arbitrary",),
                                             vmem_limit_bytes=VMEM_LIMIT_BYTES),
        name="experts",
    )(tile_expert, n_used, xs, w_gate, w_up, w_down)


def _final_kernel(h_ref, y_ref, wts_ref, g2_ref, b2_ref, *rest, alpha):
    out_ref = rest[-1]
    tm = h_ref.shape[0]
    on_diagonal = (lax.broadcasted_iota(jnp.int32, (tm, tm), 0)
                   == lax.broadcasted_iota(jnp.int32, (tm, tm), 1))
    z = alpha * h_ref[...]
    for k in range(2):
        w_col = jnp.sum(jnp.where(on_diagonal, jnp.broadcast_to(wts_ref[k:k + 1, :], (tm, tm)), 0.0),
                        axis=1, keepdims=True)
        yk = _unpack_bf16_pairs(jnp.concatenate([y_ref[k, c] for c in range(PACK_CHUNKS)], axis=1))
        z = z + w_col * yk.astype(F32)
    out_ref[...] = _layer_norm(z, g2_ref[...], b2_ref[...])


def _final_call(h, y, wts, gamma, beta, alpha, row0, n_total, earlier_out, tm=512):
    N, D = h.shape
    assert N % tm == 0 and row0 % tm == 0
    operands = [h, y, wts, gamma.reshape(1, D), beta.reshape(1, D)]
    in_specs = [pl.BlockSpec((tm, D), lambda i: (i, 0)),
                pl.BlockSpec((2, PACK_CHUNKS, tm, LANES), lambda i: (0, 0, i, 0)),
                pl.BlockSpec((SUBLANES, tm), lambda i: (0, i)),
                pl.BlockSpec((1, D), lambda i: (0, 0)),
                pl.BlockSpec((1, D), lambda i: (0, 0))]
    aliases = {}
    if earlier_out is not None:
        aliases = {len(operands): 0}
        operands.append(earlier_out)
        in_specs.append(pl.BlockSpec(memory_space=pl.ANY))
    return pl.pallas_call(
        functools.partial(_final_kernel, alpha=alpha),
        grid=(N // tm,),
        in_specs=in_specs,
        out_specs=pl.BlockSpec((tm, D), lambda i: (i + row0 // tm, 0)),
        out_shape=jax.ShapeDtypeStruct((n_total, D), F32),
        input_output_aliases=aliases,
        compiler_params=pltpu.CompilerParams(dimension_semantics=("parallel",),
                                             vmem_limit_bytes=VMEM_LIMIT_BYTES),
        name="final",
    )(*operands)


def _moe(packed, ids_t, counts, w_gate, w_up, w_down, tile_rows=512):
    N = ids_t.shape[1]
    pos_t, ends = _rank_call(ids_t, counts, tile_rows)
    n_tiles = 2 * N // tile_rows + N_EXPERTS
    seg_end = ends[0, :N_EXPERTS].astype(jnp.int32)
    tile_start = jnp.arange(n_tiles, dtype=jnp.int32) * tile_rows
    tile_expert = jnp.minimum(jnp.sum(seg_end[None, :] <= tile_start[:, None], axis=1),
                              N_EXPERTS - 1).astype(jnp.int32)
    n_used = (seg_end[N_EXPERTS - 1:] // tile_rows).astype(jnp.int32)
    xs = _sc_dispatch(packed, pos_t, n_tiles * tile_rows)
    ys = _expert_call(xs, tile_expert, n_used, w_gate, w_up, w_down, tile_rows)
    return _sc_combine(ys, pos_t)


@jax.jit
def kernel(x, w_in, b_in, rel_bias_table, w_pool, pool_scale, w_proj_attn, w_proj_pool, w_out, ln1_gamma, ln1_beta, w_router_group, b_router_group, w_router_expert, b_router_expert, w_expert_gate, w_expert_up, w_expert_down, ln2_gamma, ln2_beta):
    B, S, D = x.shape
    depth = w_in.shape[0]
    alpha = (2.0 * depth) ** 0.25
    n_parts = 2 if B % 2 == 0 else 1
    nb = B // n_parts
    for layer in range(depth):
        out = None
        for part in range(n_parts):
            b0 = part * nb
            qkv, u, ga, gb = _proj_call(x, w_in[layer], b_in[layer], b0, nb)
            a = _attn_call(qkv, rel_bias_table)
            h, packed, ids, wts, counts = _post_call(
                a, u, ga, gb, x, w_proj_attn[layer], w_pool[layer], pool_scale[layer],
                w_proj_pool[layer], w_out[layer], ln1_gamma[layer], ln1_beta[layer],
                w_router_group[layer], b_router_group[layer],
                w_router_expert[layer], b_router_expert[layer], alpha, b0)
            y = _moe(packed, ids, counts,
                     w_expert_gate[layer], w_expert_up[layer], w_expert_down[layer])
            out = _final_call(h.reshape(nb * S, D), y, wts, ln2_gamma[layer],
                              ln2_beta[layer], alpha, b0 * S, B * S, out)
        x = out.reshape(B, S, D)
    return x
```

```python
import functools
import math

import jax
import jax.numpy as jnp
import numpy as np
from jax import lax
from jax.experimental import pallas as pl
from jax.experimental.pallas import tpu as pltpu
from jax.experimental.pallas import tpu_sc as plsc

F32 = jnp.float32
BF16 = jnp.bfloat16

HEAD_DIM = 64
ATTN_CONFIGS = ((128, 1), (512, 4), (2048, 16))
N_GROUPS = len(ATTN_CONFIGS)
HEADS_PER_GROUP = 4
GROUP_WIDTH = HEADS_PER_GROUP * HEAD_DIM
ATTN_WIDTH = N_GROUPS * GROUP_WIDTH
BLOCK = 128
N_REL_BUCKETS = 32
REL_MAX_DISTANCE = 2048
NEG_INF = -1e30

POOL_SIZES = (2, 4, 8, 16)
POOL_GROUP_DIM = 128
POOL_WIDTH = POOL_GROUP_DIM * len(POOL_SIZES)
POOL_HALO = 16

N_EXPERT_GROUPS = 4
EXPERTS_PER_GROUP = 8
N_EXPERTS = N_EXPERT_GROUPS * EXPERTS_PER_GROUP
D_EXPERT = 256
LN_EPS = 1e-5

VMEM_LIMIT_BYTES = 56 * 1024 * 1024
LANES = 128
HALVES = GROUP_WIDTH // LANES


def _layer_norm(z, gamma, beta):
    mu = jnp.mean(z, axis=-1, keepdims=True)
    zc = z - mu
    var = jnp.mean(zc * zc, axis=-1, keepdims=True)
    return zc * lax.rsqrt(var + LN_EPS) * gamma + beta


def _proj_kernel(x_ref, w_ref, b_ref, *refs, tm, d_model):
    qkv_refs = refs[:9]
    u_ref, ga_ref, gb_ref, xb_ref, acc_ref = refs[9:]
    xb_ref[...] = x_ref[...].astype(BF16)

    def chunk(c0, width):
        acc = jnp.dot(xb_ref[...], w_ref[:, c0:c0 + width], preferred_element_type=F32)
        return acc + b_ref[:, c0:c0 + width]

    n_staged = 0
    for which in range(3):
        for g, (_, dil) in enumerate(ATTN_CONFIGS):
            out = qkv_refs[which * 3 + g]
            acc = chunk(which * ATTN_WIDTH + g * GROUP_WIDTH, GROUP_WIDTH)
            if which == 0:
                acc = acc * HEAD_DIM ** -0.5
            if dil == 1:
                out[0] = acc.astype(BF16)
            else:
                planes = [n_staged * HALVES + half for half in range(HALVES)]
                n_staged += 1
                for half, plane in enumerate(planes):
                    acc_ref[plane] = acc[:, half * LANES:(half + 1) * LANES]
                for r in range(dil):
                    for half, plane in enumerate(planes):
                        out[r, :, half * LANES:(half + 1) * LANES] = (
                            acc_ref[plane, pl.ds(r, tm // dil, stride=dil), :].astype(BF16))
    pool_off = 3 * ATTN_WIDTH
    for c in range(POOL_WIDTH // 256):
        u_ref[:, c * 256:(c + 1) * 256] = chunk(pool_off + c * 256, 256)
    ga_off = pool_off + POOL_WIDTH
    for gate_ref, off in ((ga_ref, ga_off), (gb_ref, ga_off + d_model)):
        for c in range(d_model // 256):
            gate_ref[:, c * 256:(c + 1) * 256] = jax.nn.sigmoid(chunk(off + c * 256, 256)).astype(BF16)


def _proj_call(x, w_in, b_in, b0, B, tm=1024):
    _, S, D = x.shape
    in_width = w_in.shape[1]
    assert in_width == 3 * ATTN_WIDTH + POOL_WIDTH + 2 * D
    assert S % tm == 0
    grid = (B, S // tm)
    qkv_shapes, qkv_specs = [], []
    for _ in range(3):
        for (_, dil) in ATTN_CONFIGS:
            assert tm % (dil * 16) == 0
            qkv_shapes.append(jax.ShapeDtypeStruct((B, dil, S // dil, GROUP_WIDTH), BF16))
            qkv_specs.append(pl.BlockSpec((None, dil, tm // dil, GROUP_WIDTH), lambda b, i: (b, 0, i, 0)))
    row_spec = lambda width: pl.BlockSpec((None, tm, width), lambda b, i: (b, i, 0))
    out_shape = qkv_shapes + [jax.ShapeDtypeStruct((B, S, POOL_WIDTH), F32),
                              jax.ShapeDtypeStruct((B, S, D), BF16),
                              jax.ShapeDtypeStruct((B, S, D), BF16)]
    out_specs = qkv_specs + [row_spec(POOL_WIDTH), row_spec(D), row_spec(D)]
    outs = pl.pallas_call(
        functools.partial(_proj_kernel, tm=tm, d_model=D),
        grid=grid,
        in_specs=[pl.BlockSpec((None, tm, D), lambda b, i: (b + b0, i, 0)),
                  pl.BlockSpec((D, in_width), lambda b, i: (0, 0), pipeline_mode=pl.Buffered(1)),
                  pl.BlockSpec((1, in_width), lambda b, i: (0, 0))],
        out_specs=out_specs,
        out_shape=out_shape,
        scratch_shapes=[pltpu.VMEM((tm, D), BF16),
                        pltpu.VMEM((3 * sum(dil > 1 for _, dil in ATTN_CONFIGS) * HALVES, tm, LANES), F32)],
        compiler_params=pltpu.CompilerParams(dimension_semantics=("parallel", "parallel"),
                                             vmem_limit_bytes=VMEM_LIMIT_BYTES),
        name="proj",
    )(x, w_in.astype(BF16), b_in.reshape(1, in_width))
    qkv = [o.reshape(B, S, GROUP_WIDTH) for o in outs[:9]]
    return qkv, outs[9], outs[10], outs[11]


def _t5_causal_bucket(dist):
    max_exact = N_REL_BUCKETS // 2
    is_small = dist < max_exact
    d = jnp.maximum(dist, 1).astype(F32)
    large = max_exact + (jnp.log(d / max_exact) / math.log(REL_MAX_DISTANCE / max_exact)
                         * (N_REL_BUCKETS - max_exact)).astype(jnp.int32)
    large = jnp.minimum(large, N_REL_BUCKETS - 1)
    return jnp.where(is_small, dist, large)


def _attn_bias(rel_bias_table):
    full, first = [], []
    for g, (window, dil) in enumerate(ATTN_CONFIGS):
        span = window // dil
        table = rel_bias_table[:, g * HEADS_PER_GROUP:(g + 1) * HEADS_PER_GROUP].astype(F32)
        lq = np.arange(BLOCK)
        for lk, dst in ((np.arange(-BLOCK, BLOCK), full), (lq, first)):
            step = jnp.asarray(lq[:, None] - lk[None, :], jnp.int32)
            in_window = (step >= 0) & (step <= span)
            bucket = _t5_causal_bucket(jnp.clip(step, 0, span) * dil)
            bias = jnp.einsum('qkb,bh->hqk', jax.nn.one_hot(bucket, N_REL_BUCKETS, dtype=F32), table,
                              precision=lax.Precision.HIGHEST)
            bias = jnp.where(in_window[None], bias, NEG_INF)
            dst.append(bias.reshape(HEADS_PER_GROUP * BLOCK, lk.shape[0]))
    return jnp.stack(full), jnp.stack(first)


def _run_skewed(chains, n_stages):
    for step in range(n_stages + len(chains) - 1):
        for lag, chain in enumerate(chains):
            if 0 <= step - lag < n_stages:
                next(chain)


ATTN_STAGES = 4


def _attn_kernel(*refs, seq):
    qkv = refs[:9]
    bias_ref, bias_first_ref, out_ref, o_scr, l_scr = refs[9:]
    rows = HEADS_PER_GROUP * BLOCK
    row_head = lax.broadcasted_iota(jnp.int32, (rows, GROUP_WIDTH), 0) // BLOCK
    lane_head_r = lax.broadcasted_iota(jnp.int32, (rows, GROUP_WIDTH), 1) // HEAD_DIM
    head_mask = row_head == lane_head_r
    heads_per_half = LANES // HEAD_DIM
    lane_head = lax.broadcasted_iota(jnp.int32, (BLOCK, LANES), 1) // HEAD_DIM

    def heads_to_lanes(per_head):
        out = per_head[0]
        for h in range(1, heads_per_half):
            out = jnp.where(lane_head == h, per_head[h], out)
        return out

    for g, (_, dil) in enumerate(ATTN_CONFIGS):
        q_ref, k_ref, v_ref = qkv[3 * g:3 * g + 3]
        sub_len = seq // dil
        n_blocks = sub_len // BLOCK

        def block(r, n, first, g=g, dil=dil, q_ref=q_ref, k_ref=k_ref, v_ref=v_ref, sub_len=sub_len):
            base = pl.multiple_of(r * sub_len + n * BLOCK, BLOCK)
            qb = q_ref[pl.ds(base, BLOCK), :]
            if first:
                kk = k_ref[pl.ds(base, BLOCK), :]
                vv = v_ref[pl.ds(base, BLOCK), :]
                bias = bias_first_ref[g]
            else:
                kbase = pl.multiple_of(base - BLOCK, BLOCK)
                kk = k_ref[pl.ds(kbase, 2 * BLOCK), :]
                vv = v_ref[pl.ds(kbase, 2 * BLOCK), :]
                bias = bias_ref[g]
            qs = jnp.where(head_mask, jnp.concatenate([qb] * HEADS_PER_GROUP, axis=0), jnp.zeros((), BF16))
            logits = lax.dot_general(qs, kk, (((1,), (1,)), ((), ())), preferred_element_type=F32) + bias
            yield
            m = jnp.max(logits, axis=1, keepdims=True)
            p = jnp.exp(logits - m)
            s = jnp.sum(p, axis=1, keepdims=True)
            p = p.astype(BF16)
            yield
            pv = jnp.dot(p, vv, preferred_element_type=F32)
            yield
            inv_s = 1.0 / s
            lse = m + jnp.log(s)
            start = n * (BLOCK * dil) + r
            if dil == 1:
                dst = pl.ds(pl.multiple_of(start, BLOCK), BLOCK)
            else:
                dst = pl.ds(start, BLOCK, stride=dil)
            for half in range(HALVES):
                cols = slice(half * LANES, (half + 1) * LANES)
                o_heads, lse_heads = [], []
                for h in range(half * heads_per_half, (half + 1) * heads_per_half):
                    head_rows = slice(h * BLOCK, (h + 1) * BLOCK)
                    o_heads.append(pv[head_rows, cols] * inv_s[head_rows])
                    lse_heads.append(jnp.broadcast_to(lse[head_rows], (BLOCK, LANES)))
                o_scr[g * HALVES + half, dst, :] = heads_to_lanes(o_heads)
                l_scr[g * HALVES + half, dst, :] = heads_to_lanes(lse_heads)
            yield

        if n_blocks > 1:
            group = 3 if (n_blocks - 1) % 3 == 0 else 1

            _run_skewed([block(r, 0, True) for r in range(dil)], ATTN_STAGES)

            def per_subsequence(r, carry, block=block, n_blocks=n_blocks, group=group):
                def per_group(i, c):
                    _run_skewed([block(r, 1 + i * group + k, False) for k in range(group)], ATTN_STAGES)
                    return c
                lax.fori_loop(0, (n_blocks - 1) // group, per_group, 0)
                return carry
            lax.fori_loop(0, dil, per_subsequence, 0)
        else:
            group = next(c for c in (8, 4, 2, 1) if dil % c == 0)

            def per_group(i, carry, block=block, group=group):
                _run_skewed([block(i * group + k, 0, True) for k in range(group)], ATTN_STAGES)
                return carry
            lax.fori_loop(0, dil // group, per_group, 0)

    chunk = 256

    def merge(i, carry):
        sl = pl.ds(pl.multiple_of(i * chunk, chunk), chunk)
        for half in range(HALVES):
            ls = [l_scr[g * HALVES + half, sl, :] for g in range(N_GROUPS)]
            m = functools.reduce(jnp.maximum, ls)
            es = [jnp.exp(l - m) for l in ls]
            den = functools.reduce(lambda a, b: a + b, es)
            num = functools.reduce(lambda a, b: a + b,
                                   [e * o_scr[g * HALVES + half, sl, :] for g, e in enumerate(es)])
            out_ref[sl, half * LANES:(half + 1) * LANES] = (num / den).astype(BF16)
        return carry

    lax.fori_loop(0, seq // chunk, merge, 0)


def _attn_call(qkv, rel_bias_table):
    B, S, _ = qkv[0].shape
    for (_, dil) in ATTN_CONFIGS:
        assert S % (dil * BLOCK) == 0
    bias, bias_first = _attn_bias(rel_bias_table)
    seq_spec = pl.BlockSpec((None, S, GROUP_WIDTH), lambda b: (b, 0, 0))
    ordered = []
    for g in range(N_GROUPS):
        ordered += [qkv[g], qkv[3 + g], qkv[6 + g]]
    return pl.pallas_call(
        functools.partial(_attn_kernel, seq=S),
        grid=(B,),
        in_specs=[seq_spec] * 9 + [pl.BlockSpec(bias.shape, lambda b: (0, 0, 0)),
                                   pl.BlockSpec(bias_first.shape, lambda b: (0, 0, 0))],
        out_specs=seq_spec,
        out_shape=jax.ShapeDtypeStruct((B, S, GROUP_WIDTH), BF16),
        scratch_shapes=[pltpu.VMEM((N_GROUPS * HALVES, S, LANES), F32),
                        pltpu.VMEM((N_GROUPS * HALVES, S, LANES), F32)],
        compiler_params=pltpu.CompilerParams(dimension_semantics=("parallel",),
                                             vmem_limit_bytes=VMEM_LIMIT_BYTES),
        name="attn",
    )(*ordered, bias, bias_first)


PACK_CHUNKS = 4
SUBLANES = 8
ROUTER_LANES = LANES // 2


def _pack_bf16_pairs(rounded):
    w = rounded.shape[1] // 2
    bits = lax.bitcast_convert_type(rounded, jnp.int32)
    return bits[:, :w] | lax.shift_right_logical(bits[:, w:], jnp.full((), 16, jnp.int32))


def _unpack_bf16_pairs(words):
    hi = lax.bitcast_convert_type(words & jnp.int32(-65536), F32).astype(BF16)
    lo = lax.bitcast_convert_type(lax.shift_left(words, jnp.full((), 16, jnp.int32)), F32).astype(BF16)
    return jnp.concatenate([hi, lo], axis=1)


def _fold_pool_kernel(wp_ref, scale_ref, pb_ref, out_ref):
    a = wp_ref[...] * scale_ref[...]
    b = pb_ref[...]
    a_hi = a.astype(BF16)
    a_lo = (a - a_hi.astype(F32)).astype(BF16)
    b_hi = b.astype(BF16)
    b_lo = (b - b_hi.astype(F32)).astype(BF16)
    out_ref[...] = (jnp.dot(a_hi, b_hi, preferred_element_type=F32)
                    + jnp.dot(a_lo, b_hi, preferred_element_type=F32)
                    + jnp.dot(a_hi, b_lo, preferred_element_type=F32)).astype(BF16)


def _fold_pool_call(w_pool, pool_scale, w_proj_pool):
    n_groups, gd, _ = w_pool.shape
    D = w_proj_pool.shape[1]
    return pl.pallas_call(
        _fold_pool_kernel,
        grid=(n_groups,),
        in_specs=[pl.BlockSpec((None, gd, gd), lambda g: (g, 0, 0)),
                  pl.BlockSpec((None, 1, gd), lambda g: (g, 0, 0)),
                  pl.BlockSpec((gd, D), lambda g: (g, 0))],
        out_specs=pl.BlockSpec((gd, D), lambda g: (g, 0)),
        out_shape=jax.ShapeDtypeStruct((n_groups * gd, D), BF16),
        name="fold_pool",
    )(w_pool, pool_scale.reshape(n_groups, 1, gd), w_proj_pool)


def _post_kernel(a_ref, u_ref, halo_ref, ga_ref, gb_ref, x_ref,
                 pa_ref, pb_ref, wout_ref, g1_ref, b1_ref,
                 wr_cat_ref, br_ref,
                 h_ref, hb_ref, ids_ref, wts_ref, cnt_ref, pool_scr, tmp_scr, *, tm, sub, alpha):
    i = pl.program_id(1)
    halo = halo_ref[...]
    pool_scr[0:POOL_HALO, :] = jnp.zeros_like(halo)
    pool_scr[POOL_HALO:2 * POOL_HALO, :] = jnp.where(i > 0, halo, jnp.zeros_like(halo))
    pool_scr[2 * POOL_HALO:, :] = u_ref[...]
    tmp_scr[:, 0:POOL_HALO, :] = jnp.zeros((tmp_scr.shape[0], POOL_HALO, POOL_GROUP_DIM), F32)
    head_pos = i * tm + lax.broadcasted_iota(jnp.int32, (POOL_HALO, POOL_GROUP_DIM), 0)

    def rows_chain(r0):
        rs = pl.ds(r0, sub)
        group_cols = [slice(gi * POOL_GROUP_DIM, (gi + 1) * POOL_GROUP_DIM) for gi in range(len(POOL_SIZES))]
        diffs = []
        ext = sub + 2 * POOL_HALO
        for gi, (cols, w) in enumerate(zip(group_cols, POOL_SIZES)):
            ug = u_ref[rs, cols]
            levels = w.bit_length() - 1
            for level in range(levels):
                shift = 1 << level
                dst = tmp_scr.at[(r0 // sub * len(POOL_SIZES) + gi) * 2 + level % 2]
                if level == 0:
                    cur = pool_scr[pl.ds(r0 + POOL_HALO, ext - POOL_HALO), cols]
                    back = pool_scr[pl.ds(r0 + POOL_HALO - shift, ext - POOL_HALO), cols]
                else:
                    src = tmp_scr.at[(r0 // sub * len(POOL_SIZES) + gi) * 2 + (level - 1) % 2]
                    cur = src[pl.ds(POOL_HALO, ext - POOL_HALO), :]
                    back = src[pl.ds(POOL_HALO - shift, ext - POOL_HALO), :]
                if level == levels - 1:
                    acc = (cur + back)[POOL_HALO:]
                else:
                    dst[pl.ds(POOL_HALO, ext - POOL_HALO), :] = cur + back
            inv_count = jnp.full((sub, POOL_GROUP_DIM), 1.0 / w, F32)
            if r0 == 0:
                inv_count = jnp.concatenate([1.0 / jnp.minimum(head_pos + 1, w).astype(F32),
                                             inv_count[POOL_HALO:]], axis=0)
            diffs.append((acc * inv_count - ug).astype(BF16))
        yield
        y_pool = jnp.dot(jnp.concatenate(diffs, axis=1), pb_ref[...], preferred_element_type=F32)
        y_attn = jnp.dot(a_ref[rs, :], pa_ref[...], preferred_element_type=F32)
        yield
        mixed = ga_ref[rs, :] * y_attn.astype(BF16) + gb_ref[rs, :] * y_pool.astype(BF16)
        yield
        y = jnp.dot(mixed, wout_ref[...], preferred_element_type=F32)
        yield
        h = _layer_norm(alpha * x_ref[rs, :] + y, g1_ref[...], b1_ref[...])
        h_ref[rs, :] = h
        h_hi = h.astype(BF16)
        h_rounded = h_hi.astype(F32)
        packed = _pack_bf16_pairs(h_rounded)
        for c in range(PACK_CHUNKS):
            hb_ref[c, rs, :] = packed[:, c * LANES:(c + 1) * LANES]

        h_lo = (h - h_rounded).astype(BF16)
        yield
        nt = (((1,), (1,)), ((), ()))
        both = lax.dot_general(wr_cat_ref[...], h_hi, nt, preferred_element_type=F32)
        lo_hi = lax.dot_general(wr_cat_ref[0:ROUTER_LANES, :], h_lo, nt, preferred_element_type=F32)
        logits = both[0:ROUTER_LANES] + both[ROUTER_LANES:] + lo_hi + br_ref[...]
        yield
        row = lax.broadcasted_iota(jnp.int32, logits.shape, 0)
        big = jnp.int32(2 ** 30)
        is_group = row < N_EXPERT_GROUPS
        gl = jnp.where(is_group, logits, -jnp.inf)
        gmax = jnp.max(gl, axis=0, keepdims=True)
        g_idx = jnp.min(jnp.where(gl == gmax, row, big), axis=0, keepdims=True)
        g_prob = 1.0 / jnp.sum(jnp.exp(gl - gmax), axis=0, keepdims=True)
        expert = row - N_EXPERT_GROUPS
        in_group = (expert >= g_idx * EXPERTS_PER_GROUP) & (expert < (g_idx + 1) * EXPERTS_PER_GROUP)
        el = jnp.where(in_group, logits, -jnp.inf)
        v1 = jnp.max(el, axis=0, keepdims=True)
        i1 = jnp.min(jnp.where(el == v1, expert, big), axis=0, keepdims=True)
        el2 = jnp.where(expert == i1, -jnp.inf, el)
        v2 = jnp.max(el2, axis=0, keepdims=True)
        i2 = jnp.min(jnp.where(el2 == v2, expert, big), axis=0, keepdims=True)
        e2 = jnp.exp(v2 - v1)
        den = 1.0 + e2
        first_row = lax.broadcasted_iota(jnp.int32, (SUBLANES, sub), 0) == 0
        wts_ref[:, rs] = jnp.where(first_row, 1.0 / den * g_prob, e2 / den * g_prob)
        ids_ref[:, rs] = jnp.where(first_row, i1, i2).astype(F32)
        chosen = jnp.where((expert == i1) | (expert == i2), 1.0, 0.0)
        per_expert = jnp.broadcast_to(jnp.sum(chosen, axis=1, keepdims=True), (ROUTER_LANES, LANES))
        to_lane = (lax.broadcasted_iota(jnp.int32, (ROUTER_LANES, LANES), 0) - N_EXPERT_GROUPS
                   == lax.broadcasted_iota(jnp.int32, (ROUTER_LANES, LANES), 1))
        counts.append(jnp.sum(jnp.where(to_lane, per_expert, 0.0), axis=0, keepdims=True))
        yield

    counts = []
    chains = [rows_chain(r0) for r0 in range(0, tm, sub)]
    n_stages = 7
    for step in range(n_stages + len(chains) - 1):
        for lag, chain in enumerate(chains):
            if 0 <= step - lag < n_stages:
                next(chain)
    cnt_ref[...] = jnp.broadcast_to(functools.reduce(lambda a, b: a + b, counts), cnt_ref.shape)


def _post_call(a, u, ga, gb, x, w_proj_attn, w_pool, pool_scale, w_proj_pool, w_out, gamma, beta,
               w_router_group, b_router_group, w_router_expert, b_router_expert, alpha, b0, tm=1024, n_sub=4):
    B = a.shape[0]
    _, S, D = x.shape
    assert S % tm == 0 and tm % POOL_HALO == 0
    n_logits = N_EXPERT_GROUPS + N_EXPERTS
    assert n_logits <= ROUTER_LANES
    wr = jnp.concatenate([w_router_group, w_router_expert], axis=1).T
    wr = jnp.pad(wr, ((0, ROUTER_LANES - n_logits), (0, 0)))
    wr_hi = wr.astype(BF16)
    wr_lo = (wr - wr_hi.astype(F32)).astype(BF16)
    wr_cat = jnp.concatenate([wr_hi, wr_lo], axis=0)
    br = jnp.pad(jnp.concatenate([b_router_group, b_router_expert]),
                 (0, ROUTER_LANES - n_logits)).reshape(ROUTER_LANES, 1)
    n_tiles = B * (S // tm)
    row_spec = lambda width: pl.BlockSpec((None, tm, width), lambda b, i: (b, i, 0))
    full = lambda arr: pl.BlockSpec(arr.shape, lambda b, i: (0,) * arr.ndim)
    halo_blocks = tm // POOL_HALO
    halo_spec = pl.BlockSpec((None, POOL_HALO, POOL_WIDTH),
                             lambda b, i: (b, jnp.maximum(i * halo_blocks - 1, 0), 0))
    weights = [w_proj_attn.astype(BF16), _fold_pool_call(w_pool, pool_scale, w_proj_pool),
               w_out.astype(BF16), gamma.reshape(1, D), beta.reshape(1, D), wr_cat, br]
    return pl.pallas_call(
        functools.partial(_post_kernel, tm=tm, sub=tm // n_sub, alpha=alpha),
        grid=(B, S // tm),
        in_specs=[row_spec(GROUP_WIDTH), row_spec(POOL_WIDTH), halo_spec, row_spec(D), row_spec(D),
                  pl.BlockSpec((None, tm, D), lambda b, i: (b + b0, i, 0))] + [full(w) for w in weights],
        out_specs=[row_spec(D),
                   pl.BlockSpec((PACK_CHUNKS, tm, LANES), lambda b, i: (0, b * (S // tm) + i, 0)),
                   pl.BlockSpec((SUBLANES, tm), lambda b, i: (0, b * (S // tm) + i)),
                   pl.BlockSpec((SUBLANES, tm), lambda b, i: (0, b * (S // tm) + i)),
                   pl.BlockSpec((SUBLANES, LANES), lambda b, i: (b * (S // tm) + i, 0))],
        out_shape=[jax.ShapeDtypeStruct((B, S, D), F32),
                   jax.ShapeDtypeStruct((PACK_CHUNKS, B * S, LANES), jnp.int32),
                   jax.ShapeDtypeStruct((SUBLANES, B * S), F32), jax.ShapeDtypeStruct((SUBLANES, B * S), F32),
                   jax.ShapeDtypeStruct((n_tiles * SUBLANES, LANES), F32)],
        scratch_shapes=[pltpu.VMEM((tm + 2 * POOL_HALO, POOL_WIDTH), F32),
                        pltpu.VMEM((n_sub * len(POOL_SIZES) * 2, tm // n_sub + 2 * POOL_HALO, POOL_GROUP_DIM), F32)],
        compiler_params=pltpu.CompilerParams(dimension_semantics=("parallel", "parallel"),
                                             vmem_limit_bytes=VMEM_LIMIT_BYTES),
        name="post",
    )(a, u, u, ga, gb, x, *weights)


def _rank_kernel(ids_ref, cnt_ref, pos_ref, ends_ref, run_ref, start_ref, earlier_ref, *, tm, tile_rows):
    i = pl.program_id(0)

    @pl.when(i == 0)
    def _():
        total = jnp.sum(cnt_ref[...], axis=0, keepdims=True) / SUBLANES
        padded = jnp.broadcast_to(jnp.ceil(total / tile_rows) * tile_rows, ends_ref.shape)
        lane1 = lax.broadcasted_iota(jnp.int32, padded.shape, 1)
        incl = padded
        shift = 1
        while shift < LANES:
            incl = incl + jnp.where(lane1 >= shift, pltpu.roll(incl, shift, axis=1), 0.0)
            shift *= 2
        ends_ref[...] = incl
        start_row = jnp.broadcast_to((incl - padded)[0:1, :], start_ref.shape)
        on_diagonal = (lax.broadcasted_iota(jnp.int32, start_ref.shape, 0)
                       == lax.broadcasted_iota(jnp.int32, start_ref.shape, 1))
        start_col = jnp.sum(jnp.where(on_diagonal, start_row, 0.0), axis=1, keepdims=True)
        start_ref[...] = jnp.broadcast_to(start_col, start_ref.shape)
        run_ref[...] = jnp.zeros_like(run_ref)
        row = lax.broadcasted_iota(jnp.int32, (tm, tm), 0)
        col = lax.broadcasted_iota(jnp.int32, (tm, tm), 1)
        earlier_ref[...] = jnp.where(row < col, 1.0, 0.0).astype(BF16)

    ids = ids_ref[...]
    expert = lax.broadcasted_iota(jnp.int32, (N_EXPERTS, tm), 0).astype(F32)
    oh0 = expert == ids[0:1, :]
    oh1 = expert == ids[1:2, :]
    onehot = jnp.where(oh0 | oh1, 1.0, 0.0)
    before = jnp.dot(onehot.astype(BF16), earlier_ref[...], preferred_element_type=F32)
    slot = start_ref[:, 0:1] + run_ref[:, 0:1] + before
    p0 = jnp.sum(jnp.where(oh0, slot, 0.0), axis=0, keepdims=True)
    p1 = jnp.sum(jnp.where(oh1, slot, 0.0), axis=0, keepdims=True)
    first_row = lax.broadcasted_iota(jnp.int32, pos_ref.shape, 0) == 0
    pos_ref[...] = jnp.where(first_row, p0, p1).astype(jnp.int32)
    run_ref[...] += jnp.broadcast_to(jnp.sum(onehot, axis=1, keepdims=True), run_ref.shape)


def _rank_call(ids_t, counts, tile_rows, tm=1024):
    N = ids_t.shape[1]
    assert N % tm == 0 and SUBLANES * (2 * N + N_EXPERTS * tile_rows) < 2 ** 24
    assert N_EXPERTS <= LANES
    return pl.pallas_call(
        functools.partial(_rank_kernel, tm=tm, tile_rows=tile_rows),
        grid=(N // tm,),
        in_specs=[pl.BlockSpec((SUBLANES, tm), lambda i: (0, i)),
                  pl.BlockSpec(counts.shape, lambda i: (0, 0))],
        out_specs=[pl.BlockSpec((SUBLANES, tm), lambda i: (0, i)),
                   pl.BlockSpec((SUBLANES, LANES), lambda i: (0, 0))],
        out_shape=[jax.ShapeDtypeStruct((SUBLANES, N), jnp.int32),
                   jax.ShapeDtypeStruct((SUBLANES, LANES), F32)],
        scratch_shapes=[pltpu.VMEM((N_EXPERTS, LANES), F32), pltpu.VMEM((N_EXPERTS, LANES), F32),
                        pltpu.VMEM((tm, tm), BF16)],
        compiler_params=pltpu.CompilerParams(dimension_semantics=("arbitrary",),
                                             vmem_limit_bytes=VMEM_LIMIT_BYTES),
        name="rank",
    )(ids_t, counts)


SC_CORES = 2
SC_SUBCORES = 16
SC_WORKERS = SC_CORES * SC_SUBCORES
SC_CHUNK = 128


def _sc_mesh():
    return plsc.VectorSubcoreMesh(core_axis_name="c", subcore_axis_name="s",
                                  num_cores=SC_CORES, num_subcores=SC_SUBCORES)


def _sc_dispatch(packed, pos_t, n_rows):
    n_chunks, n_tok, width = packed.shape
    per_worker = n_tok // SC_WORKERS
    assert n_tok % (SC_WORKERS * SC_CHUNK) == 0

    @functools.partial(
        pl.kernel, mesh=_sc_mesh(),
        out_type=jax.ShapeDtypeStruct((n_chunks, n_rows, width), packed.dtype),
        scratch_types=[pltpu.VMEM((SC_CHUNK,), jnp.int32), pltpu.VMEM((SC_CHUNK,), jnp.int32),
                       pltpu.VMEM((n_chunks, SC_CHUNK, width), packed.dtype)]
                      + [pltpu.SemaphoreType.DMA] * (3 * n_chunks),
        name="sc_dispatch")
    def run(packed_hbm, pos_hbm, out_hbm, idx0, idx1, bufs, *sems):
        load_sems, sems0, sems1 = sems[:n_chunks], sems[n_chunks:2 * n_chunks], sems[2 * n_chunks:]
        worker = lax.axis_index("s") * SC_CORES + lax.axis_index("c")

        @pl.loop(0, per_worker // SC_CHUNK)
        def _(j):
            base = worker * per_worker + j * SC_CHUNK
            loads = [pltpu.async_copy(packed_hbm.at[c, pl.ds(base, SC_CHUNK)], bufs.at[c], load_sems[c])
                     for c in range(n_chunks)]
            pltpu.sync_copy(pos_hbm.at[0, pl.ds(base, SC_CHUNK)], idx0)
            pltpu.sync_copy(pos_hbm.at[1, pl.ds(base, SC_CHUNK)], idx1)
            scatters = []
            for c in range(n_chunks):
                loads[c].wait()
                scatters.append(pltpu.async_copy(bufs.at[c], out_hbm.at[c].at[idx0], sems0[c]))
                scatters.append(pltpu.async_copy(bufs.at[c], out_hbm.at[c].at[idx1], sems1[c]))
            for s in scatters:
                s.wait()

    return run(packed, pos_t)


def _sc_combine(sorted_rows, pos_t):
    n_chunks, _, width = sorted_rows.shape
    n_tok = pos_t.shape[1]
    per_worker = n_tok // SC_WORKERS
    assert n_tok % (SC_WORKERS * SC_CHUNK) == 0

    @functools.partial(
        pl.kernel, mesh=_sc_mesh(),
        out_type=jax.ShapeDtypeStruct((2, n_chunks, n_tok, width), sorted_rows.dtype),
        scratch_types=[pltpu.VMEM((SC_CHUNK,), jnp.int32),
                       pltpu.VMEM((n_chunks, SC_CHUNK, width), sorted_rows.dtype)]
                      + [pltpu.SemaphoreType.DMA] * (2 * n_chunks),
        name="sc_combine")
    def run(rows_hbm, pos_hbm, out_hbm, idx, bufs, *sems):
        gather_sems, write_sems = sems[:n_chunks], sems[n_chunks:]
        worker = lax.axis_index("s") * SC_CORES + lax.axis_index("c")

        @pl.loop(0, per_worker // SC_CHUNK)
        def _(j):
            base = worker * per_worker + j * SC_CHUNK
            for k in range(2):
                pltpu.sync_copy(pos_hbm.at[k, pl.ds(base, SC_CHUNK)], idx)
                gathers = [pltpu.async_copy(rows_hbm.at[c].at[idx], bufs.at[c], gather_sems[c])
                           for c in range(n_chunks)]
                writes = []
                for c in range(n_chunks):
                    gathers[c].wait()
                    writes.append(pltpu.async_copy(bufs.at[c], out_hbm.at[k, c, pl.ds(base, SC_CHUNK)],
                                                   write_sems[c]))
                for w in writes:
                    w.wait()

    return run(sorted_rows, pos_t)


def _expert_kernel(tile_expert_ref, n_used_ref, xs_ref, wg_ref, wu_ref, wd_ref, ys_ref):
    del tile_expert_ref

    @pl.when(pl.program_id(0) < n_used_ref[0])
    def _():
        wg = wg_ref[...].astype(BF16)
        wu = wu_ref[...].astype(BF16)
        wd = wd_ref[...].astype(BF16)
        tile_rows = xs_ref.shape[1]
        sub = tile_rows // 2

        def rows_chain(r0):
            rs = pl.ds(r0, sub)
            x = _unpack_bf16_pairs(jnp.concatenate([xs_ref[c, rs, :] for c in range(PACK_CHUNKS)], axis=1))
            yield
            gate = jnp.dot(x, wg, preferred_element_type=F32)
            up = jnp.dot(x, wu, preferred_element_type=F32)
            yield
            hidden = (jax.nn.silu(gate) * up).astype(BF16)
            yield
            y = jnp.dot(hidden, wd, preferred_element_type=F32)
            yield
            y = _pack_bf16_pairs(y.astype(BF16).astype(F32))
            for c in range(PACK_CHUNKS):
                ys_ref[c, rs, :] = y[:, c * LANES:(c + 1) * LANES]
            yield

        chains = [rows_chain(r0) for r0 in range(0, tile_rows, sub)]
        n_stages = 5
        for step in range(n_stages + len(chains) - 1):
            for lag, chain in enumerate(chains):
                if 0 <= step - lag < n_stages:
                    next(chain)


def _expert_call(xs, tile_expert, n_used, w_gate, w_up, w_down, tile_rows):
    _, n_rows, _ = xs.shape
    D = w_gate.shape[1]
    row_block = pl.BlockSpec((PACK_CHUNKS, tile_rows, LANES),
                             lambda i, te, nu: (0, jnp.minimum(i, nu[0] - 1), 0))
    return pl.pallas_call(
        _expert_kernel,
        grid_spec=pltpu.PrefetchScalarGridSpec(
            num_scalar_prefetch=2,
            grid=(n_rows // tile_rows,),
            in_specs=[row_block,
                      pl.BlockSpec((None, D, D_EXPERT), lambda i, te, nu: (te[i], 0, 0)),
                      pl.BlockSpec((None, D, D_EXPERT), lambda i, te, nu: (te[i], 0, 0)),
                      pl.BlockSpec((None, D_EXPERT, D), lambda i, te, nu: (te[i], 0, 0))],
            out_specs=row_block),
        out_shape=jax.ShapeDtypeStruct(xs.shape, xs.dtype),
        compiler_params=pltpu.CompilerParams(dimension_semantics=("arbitrary",),
                                             vmem_limit_bytes=VMEM_LIMIT_BYTES),
        name="experts",
    )(tile_expert, n_used, xs, w_gate, w_up, w_down)


def _final_kernel(h_ref, y_ref, wts_ref, g2_ref, b2_ref, *rest, alpha):
    out_ref = rest[-1]
    tm = h_ref.shape[0]
    on_diagonal = (lax.broadcasted_iota(jnp.int32, (tm, tm), 0)
                   == lax.broadcasted_iota(jnp.int32, (tm, tm), 1))
    z = alpha * h_ref[...]
    for k in range(2):
        w_col = jnp.sum(jnp.where(on_diagonal, jnp.broadcast_to(wts_ref[k:k + 1, :], (tm, tm)), 0.0),
                        axis=1, keepdims=True)
        yk = _unpack_bf16_pairs(jnp.concatenate([y_ref[k, c] for c in range(PACK_CHUNKS)], axis=1))
        z = z + w_col * yk.astype(F32)
    out_ref[...] = _layer_norm(z, g2_ref[...], b2_ref[...])


def _final_call(h, y, wts, gamma, beta, alpha, row0, n_total, earlier_out, tm=512):
    N, D = h.shape
    assert N % tm == 0 and row0 % tm == 0
    operands = [h, y, wts, gamma.reshape(1, D), beta.reshape(1, D)]
    in_specs = [pl.BlockSpec((tm, D), lambda i: (i, 0)),
                pl.BlockSpec((2, PACK_CHUNKS, tm, LANES), lambda i: (0, 0, i, 0)),
                pl.BlockSpec((SUBLANES, tm), lambda i: (0, i)),
                pl.BlockSpec((1, D), lambda i: (0, 0)),
                pl.BlockSpec((1, D), lambda i: (0, 0))]
    aliases = {}
    if earlier_out is not None:
        aliases = {len(operands): 0}
        operands.append(earlier_out)
        in_specs.append(pl.BlockSpec(memory_space=pl.ANY))
    return pl.pallas_call(
        functools.partial(_final_kernel, alpha=alpha),
        grid=(N // tm,),
        in_specs=in_specs,
        out_specs=pl.BlockSpec((tm, D), lambda i: (i + row0 // tm, 0)),
        out_shape=jax.ShapeDtypeStruct((n_total, D), F32),
        input_output_aliases=aliases,
        compiler_params=pltpu.CompilerParams(dimension_semantics=("parallel",),
                                             vmem_limit_bytes=VMEM_LIMIT_BYTES),
        name="final",
    )(*operands)


def _moe(packed, ids_t, counts, w_gate, w_up, w_down, tile_rows=512):
    N = ids_t.shape[1]
    pos_t, ends = _rank_call(ids_t, counts, tile_rows)
    n_tiles = 2 * N // tile_rows + N_EXPERTS
    seg_end = ends[0, :N_EXPERTS].astype(jnp.int32)
    tile_start = jnp.arange(n_tiles, dtype=jnp.int32) * tile_rows
    tile_expert = jnp.minimum(jnp.sum(seg_end[None, :] <= tile_start[:, None], axis=1),
                              N_EXPERTS - 1).astype(jnp.int32)
    n_used = (seg_end[N_EXPERTS - 1:] // tile_rows).astype(jnp.int32)
    xs = _sc_dispatch(packed, pos_t, n_tiles * tile_rows)
    ys = _expert_call(xs, tile_expert, n_used, w_gate, w_up, w_down, tile_rows)
    return _sc_combine(ys, pos_t)


@jax.jit
def kernel(x, w_in, b_in, rel_bias_table, w_pool, pool_scale, w_proj_attn, w_proj_pool, w_out, ln1_gamma, ln1_beta, w_router_group, b_router_group, w_router_expert, b_router_expert, w_expert_gate, w_expert_up, w_expert_down, ln2_gamma, ln2_beta):
    B, S, D = x.shape
    depth = w_in.shape[0]
    alpha = (2.0 * depth) ** 0.25
    n_parts = 2 if B % 2 == 0 else 1
    nb = B // n_parts
    for layer in range(depth):
        out = None
        for part in range(n_parts):
            b0 = part * nb
            qkv, u, ga, gb = _proj_call(x, w_in[layer], b_in[layer], b0, nb)
            a = _attn_call(qkv, rel_bias_table)
            h, packed, ids, wts, counts = _post_call(
                a, u, ga, gb, x, w_proj_attn[layer], w_pool[layer], pool_scale[layer],
                w_proj_pool[layer], w_out[layer], ln1_gamma[layer], ln1_beta[layer],
                w_router_group[layer], b_router_group[layer],
                w_router_expert[layer], b_router_expert[layer], alpha, b0)
            y = _moe(packed, ids, counts,
                     w_expert_gate[layer], w_expert_up[layer], w_expert_down[layer])
            out = _final_call(h.reshape(nb * S, D), y, wts, ln2_gamma[layer],
                              ln2_beta[layer], alpha, b0 * S, B * S, out)
        x = out.reshape(B, S, D)
    return x
```

```python
import functools
import math

import jax
import jax.numpy as jnp
import numpy as np
from jax import lax
from jax.experimental import pallas as pl
from jax.experimental.pallas import tpu as pltpu
from jax.experimental.pallas import tpu_sc as plsc

F32 = jnp.float32
BF16 = jnp.bfloat16

HEAD_DIM = 64
ATTN_CONFIGS = ((128, 1), (512, 4), (2048, 16))
N_GROUPS = len(ATTN_CONFIGS)
HEADS_PER_GROUP = 4
GROUP_WIDTH = HEADS_PER_GROUP * HEAD_DIM
ATTN_WIDTH = N_GROUPS * GROUP_WIDTH
BLOCK = 128
N_REL_BUCKETS = 32
REL_MAX_DISTANCE = 2048
NEG_INF = -1e30

POOL_SIZES = (2, 4, 8, 16)
POOL_GROUP_DIM = 128
POOL_WIDTH = POOL_GROUP_DIM * len(POOL_SIZES)
POOL_HALO = 16

N_EXPERT_GROUPS = 4
EXPERTS_PER_GROUP = 8
N_EXPERTS = N_EXPERT_GROUPS * EXPERTS_PER_GROUP
D_EXPERT = 256
LN_EPS = 1e-5

VMEM_LIMIT_BYTES = 56 * 1024 * 1024
LANES = 128
HALVES = GROUP_WIDTH // LANES


def _layer_norm(z, gamma, beta):
    mu = jnp.mean(z, axis=-1, keepdims=True)
    zc = z - mu
    var = jnp.mean(zc * zc, axis=-1, keepdims=True)
    return zc * lax.rsqrt(var + LN_EPS) * gamma + beta


def _proj_kernel(x_ref, w_ref, b_ref, *refs, tm, d_model):
    qkv_refs = refs[:9]
    u_ref, ga_ref, gb_ref, xb_ref, acc_ref = refs[9:]
    xb_ref[...] = x_ref[...].astype(BF16)

    def chunk(c0, width):
        acc = jnp.dot(xb_ref[...], w_ref[:, c0:c0 + width], preferred_element_type=F32)
        return acc + b_ref[:, c0:c0 + width]

    pool_off = 3 * ATTN_WIDTH
    ga_off = pool_off + POOL_WIDTH

    def gate_job(gate_ref, off, c):
        def run():
            gate_ref[:, c * 256:(c + 1) * 256] = jax.nn.sigmoid(chunk(off + c * 256, 256)).astype(BF16)
        return run

    def pool_job(c):
        def run():
            u_ref[:, c * 256:(c + 1) * 256] = chunk(pool_off + c * 256, 256)
        return run

    def qkv_job(which, g, staged):
        def run():
            dil = ATTN_CONFIGS[g][1]
            out = qkv_refs[which * 3 + g]
            acc = chunk(which * ATTN_WIDTH + g * GROUP_WIDTH, GROUP_WIDTH)
            if which == 0:
                acc = acc * HEAD_DIM ** -0.5
            if dil == 1:
                out[0] = acc.astype(BF16)
            else:
                planes = [staged * HALVES + half for half in range(HALVES)]
                for half, plane in enumerate(planes):
                    acc_ref[plane] = acc[:, half * LANES:(half + 1) * LANES]
                for r in range(dil):
                    for half, plane in enumerate(planes):
                        out[r, :, half * LANES:(half + 1) * LANES] = (
                            acc_ref[plane, pl.ds(r, tm // dil, stride=dil), :].astype(BF16))
        return run

    gates = [gate_job(ref, off, c) for ref, off in ((ga_ref, ga_off), (gb_ref, ga_off + d_model))
             for c in range(d_model // 256)]
    dilated = [(which, g) for g in range(N_GROUPS) if ATTN_CONFIGS[g][1] > 1 for which in range(3)]
    strided = [qkv_job(which, g, k) for k, (which, g) in enumerate(dilated)]
    plain = [qkv_job(which, g, 0) for g in range(N_GROUPS) if ATTN_CONFIGS[g][1] == 1 for which in range(3)]
    pools = [pool_job(c) for c in range(POOL_WIDTH // 256)]
    order = []
    while gates or strided:
        if gates:
            order.append(gates.pop(0))
        if strided:
            order.append(strided.pop(0))
    for job in order + plain + pools:
        job()


def _proj_call(x, w_in, b_in, b0, B, tm=1024):
    _, S, D = x.shape
    in_width = w_in.shape[1]
    assert in_width == 3 * ATTN_WIDTH + POOL_WIDTH + 2 * D
    assert S % tm == 0
    grid = (B, S // tm)
    qkv_shapes, qkv_specs = [], []
    for _ in range(3):
        for (_, dil) in ATTN_CONFIGS:
            assert tm % (dil * 16) == 0
            qkv_shapes.append(jax.ShapeDtypeStruct((B, dil, S // dil, GROUP_WIDTH), BF16))
            qkv_specs.append(pl.BlockSpec((None, dil, tm // dil, GROUP_WIDTH), lambda b, i: (b, 0, i, 0)))
    row_spec = lambda width: pl.BlockSpec((None, tm, width), lambda b, i: (b, i, 0))
    out_shape = qkv_shapes + [jax.ShapeDtypeStruct((B, S, POOL_WIDTH), F32),
                              jax.ShapeDtypeStruct((B, S, D), BF16),
                              jax.ShapeDtypeStruct((B, S, D), BF16)]
    out_specs = qkv_specs + [row_spec(POOL_WIDTH), row_spec(D), row_spec(D)]
    outs = pl.pallas_call(
        functools.partial(_proj_kernel, tm=tm, d_model=D),
        grid=grid,
        in_specs=[pl.BlockSpec((None, tm, D), lambda b, i: (b + b0, i, 0)),
                  pl.BlockSpec((D, in_width), lambda b, i: (0, 0), pipeline_mode=pl.Buffered(1)),
                  pl.BlockSpec((1, in_width), lambda b, i: (0, 0))],
        out_specs=out_specs,
        out_shape=out_shape,
        scratch_shapes=[pltpu.VMEM((tm, D), BF16),
                        pltpu.VMEM((3 * sum(dil > 1 for _, dil in ATTN_CONFIGS) * HALVES, tm, LANES), F32)],
        compiler_params=pltpu.CompilerParams(dimension_semantics=("parallel", "parallel"),
                                             vmem_limit_bytes=VMEM_LIMIT_BYTES),
        name="proj",
    )(x, w_in.astype(BF16), b_in.reshape(1, in_width))
    qkv = [o.reshape(B, S, GROUP_WIDTH) for o in outs[:9]]
    return qkv, outs[9], outs[10], outs[11]


def _t5_causal_bucket(dist):
    max_exact = N_REL_BUCKETS // 2
    is_small = dist < max_exact
    d = jnp.maximum(dist, 1).astype(F32)
    large = max_exact + (jnp.log(d / max_exact) / math.log(REL_MAX_DISTANCE / max_exact)
                         * (N_REL_BUCKETS - max_exact)).astype(jnp.int32)
    large = jnp.minimum(large, N_REL_BUCKETS - 1)
    return jnp.where(is_small, dist, large)


def _attn_bias(rel_bias_table):
    full, first = [], []
    for g, (window, dil) in enumerate(ATTN_CONFIGS):
        span = window // dil
        table = rel_bias_table[:, g * HEADS_PER_GROUP:(g + 1) * HEADS_PER_GROUP].astype(F32)
        lq = np.arange(BLOCK)
        for lk, dst in ((np.arange(-BLOCK, BLOCK), full), (lq, first)):
            step = jnp.asarray(lq[:, None] - lk[None, :], jnp.int32)
            in_window = (step >= 0) & (step <= span)
            bucket = _t5_causal_bucket(jnp.clip(step, 0, span) * dil)
            bias = jnp.einsum('qkb,bh->hqk', jax.nn.one_hot(bucket, N_REL_BUCKETS, dtype=F32), table,
                              precision=lax.Precision.HIGHEST)
            bias = jnp.where(in_window[None], bias, NEG_INF)
            dst.append(bias.reshape(HEADS_PER_GROUP * BLOCK, lk.shape[0]))
    return jnp.stack(full), jnp.stack(first)


def _run_skewed(chains, n_stages):
    for step in range(n_stages + len(chains) - 1):
        for lag, chain in enumerate(chains):
            if 0 <= step - lag < n_stages:
                next(chain)


ATTN_STAGES = 4


def _attn_kernel(*refs, seq):
    qkv = refs[:9]
    bias_ref, bias_first_ref, out_ref, o_scr, l_scr = refs[9:]
    rows = HEADS_PER_GROUP * BLOCK
    row_head = lax.broadcasted_iota(jnp.int32, (rows, GROUP_WIDTH), 0) // BLOCK
    lane_head_r = lax.broadcasted_iota(jnp.int32, (rows, GROUP_WIDTH), 1) // HEAD_DIM
    head_mask = row_head == lane_head_r
    heads_per_half = LANES // HEAD_DIM
    lane_head = lax.broadcasted_iota(jnp.int32, (BLOCK, LANES), 1) // HEAD_DIM

    def heads_to_lanes(per_head):
        out = per_head[0]
        for h in range(1, heads_per_half):
            out = jnp.where(lane_head == h, per_head[h], out)
        return out

    for g, (_, dil) in enumerate(ATTN_CONFIGS):
        q_ref, k_ref, v_ref = qkv[3 * g:3 * g + 3]
        sub_len = seq // dil
        n_blocks = sub_len // BLOCK

        def block(r, n, first, g=g, dil=dil, q_ref=q_ref, k_ref=k_ref, v_ref=v_ref, sub_len=sub_len):
            base = pl.multiple_of(r * sub_len + n * BLOCK, BLOCK)
            qb = q_ref[pl.ds(base, BLOCK), :]
            if first:
                kk = k_ref[pl.ds(base, BLOCK), :]
                vv = v_ref[pl.ds(base, BLOCK), :]
                bias = bias_first_ref[g]
            else:
                kbase = pl.multiple_of(base - BLOCK, BLOCK)
                kk = k_ref[pl.ds(kbase, 2 * BLOCK), :]
                vv = v_ref[pl.ds(kbase, 2 * BLOCK), :]
                bias = bias_ref[g]
            qs = jnp.where(head_mask, jnp.concatenate([qb] * HEADS_PER_GROUP, axis=0), jnp.zeros((), BF16))
            logits = lax.dot_general(qs, kk, (((1,), (1,)), ((), ())), preferred_element_type=F32) + bias
            yield
            m = jnp.max(logits, axis=1, keepdims=True)
            p = jnp.exp(logits - m)
            s = jnp.sum(p, axis=1, keepdims=True)
            p = p.astype(BF16)
            yield
            pv = jnp.dot(p, vv, preferred_element_type=F32)
            yield
            inv_s = 1.0 / s
            lse = m + jnp.log(s)
            start = n * (BLOCK * dil) + r
            if dil == 1:
                dst = pl.ds(pl.multiple_of(start, BLOCK), BLOCK)
            else:
                dst = pl.ds(start, BLOCK, stride=dil)
            for half in range(HALVES):
                cols = slice(half * LANES, (half + 1) * LANES)
                o_heads, lse_heads = [], []
                for h in range(half * heads_per_half, (half + 1) * heads_per_half):
                    head_rows = slice(h * BLOCK, (h + 1) * BLOCK)
                    o_heads.append(pv[head_rows, cols] * inv_s[head_rows])
                    lse_heads.append(jnp.broadcast_to(lse[head_rows], (BLOCK, LANES)))
                o_scr[g * HALVES + half, dst, :] = heads_to_lanes(o_heads)
                l_scr[g * HALVES + half, dst, :] = heads_to_lanes(lse_heads)
            yield

        if n_blocks > 1:
            group = 3 if (n_blocks - 1) % 3 == 0 else 1

            _run_skewed([block(r, 0, True) for r in range(dil)], ATTN_STAGES)

            def per_subsequence(r, carry, block=block, n_blocks=n_blocks, group=group):
                def per_group(i, c):
                    _run_skewed([block(r, 1 + i * group + k, False) for k in range(group)], ATTN_STAGES)
                    return c
                lax.fori_loop(0, (n_blocks - 1) // group, per_group, 0)
                return carry
            lax.fori_loop(0, dil, per_subsequence, 0)
        else:
            group = next(c for c in (8, 4, 2, 1) if dil % c == 0)

            def per_group(i, carry, block=block, group=group):
                _run_skewed([block(i * group + k, 0, True) for k in range(group)], ATTN_STAGES)
                return carry
            lax.fori_loop(0, dil // group, per_group, 0)

    chunk = 256

    def merge(i, carry):
        sl = pl.ds(pl.multiple_of(i * chunk, chunk), chunk)
        for half in range(HALVES):
            ls = [l_scr[g * HALVES + half, sl, :] for g in range(N_GROUPS)]
            m = functools.reduce(jnp.maximum, ls)
            es = [jnp.exp(l - m) for l in ls]
            den = functools.reduce(lambda a, b: a + b, es)
            num = functools.reduce(lambda a, b: a + b,
                                   [e * o_scr[g * HALVES + half, sl, :] for g, e in enumerate(es)])
            out_ref[sl, half * LANES:(half + 1) * LANES] = (num / den).astype(BF16)
        return carry

    lax.fori_loop(0, seq // chunk, merge, 0)


def _attn_call(qkv, rel_bias_table):
    B, S, _ = qkv[0].shape
    for (_, dil) in ATTN_CONFIGS:
        assert S % (dil * BLOCK) == 0
    bias, bias_first = _attn_bias(rel_bias_table)
    seq_spec = pl.BlockSpec((None, S, GROUP_WIDTH), lambda b: (b, 0, 0))
    ordered = []
    for g in range(N_GROUPS):
        ordered += [qkv[g], qkv[3 + g], qkv[6 + g]]
    return pl.pallas_call(
        functools.partial(_attn_kernel, seq=S),
        grid=(B,),
        in_specs=[seq_spec] * 9 + [pl.BlockSpec(bias.shape, lambda b: (0, 0, 0)),
                                   pl.BlockSpec(bias_first.shape, lambda b: (0, 0, 0))],
        out_specs=seq_spec,
        out_shape=jax.ShapeDtypeStruct((B, S, GROUP_WIDTH), BF16),
        scratch_shapes=[pltpu.VMEM((N_GROUPS * HALVES, S, LANES), F32),
                        pltpu.VMEM((N_GROUPS * HALVES, S, LANES), F32)],
        compiler_params=pltpu.CompilerParams(dimension_semantics=("parallel",),
                                             vmem_limit_bytes=VMEM_LIMIT_BYTES),
        name="attn",
    )(*ordered, bias, bias_first)


PACK_CHUNKS = 4
SUBLANES = 8
ROUTER_LANES = LANES // 2


def _pack_bf16_pairs(rounded):
    w = rounded.shape[1] // 2
    bits = lax.bitcast_convert_type(rounded, jnp.int32)
    return bits[:, :w] | lax.shift_right_logical(bits[:, w:], jnp.full((), 16, jnp.int32))


def _unpack_bf16_pairs(words):
    hi = lax.bitcast_convert_type(words & jnp.int32(-65536), F32).astype(BF16)
    lo = lax.bitcast_convert_type(lax.shift_left(words, jnp.full((), 16, jnp.int32)), F32).astype(BF16)
    return jnp.concatenate([hi, lo], axis=1)


def _fold_pool_kernel(wp_ref, scale_ref, pb_ref, out_ref):
    a = wp_ref[...] * scale_ref[...]
    b = pb_ref[...]
    a_hi = a.astype(BF16)
    a_lo = (a - a_hi.astype(F32)).astype(BF16)
    b_hi = b.astype(BF16)
    b_lo = (b - b_hi.astype(F32)).astype(BF16)
    out_ref[...] = (jnp.dot(a_hi, b_hi, preferred_element_type=F32)
                    + jnp.dot(a_lo, b_hi, preferred_element_type=F32)
                    + jnp.dot(a_hi, b_lo, preferred_element_type=F32)).astype(BF16)


def _fold_pool_call(w_pool, pool_scale, w_proj_pool):
    n_groups, gd, _ = w_pool.shape
    D = w_proj_pool.shape[1]
    return pl.pallas_call(
        _fold_pool_kernel,
        grid=(n_groups,),
        in_specs=[pl.BlockSpec((None, gd, gd), lambda g: (g, 0, 0)),
                  pl.BlockSpec((None, 1, gd), lambda g: (g, 0, 0)),
                  pl.BlockSpec((gd, D), lambda g: (g, 0))],
        out_specs=pl.BlockSpec((gd, D), lambda g: (g, 0)),
        out_shape=jax.ShapeDtypeStruct((n_groups * gd, D), BF16),
        name="fold_pool",
    )(w_pool, pool_scale.reshape(n_groups, 1, gd), w_proj_pool)


def _post_kernel(a_ref, u_ref, halo_ref, ga_ref, gb_ref, x_ref,
                 pa_ref, pb_ref, wout_ref, g1_ref, b1_ref,
                 wr_cat_ref, br_ref,
                 h_ref, hb_ref, ids_ref, wts_ref, cnt_ref, pool_scr, tmp_scr, *, tm, sub, alpha):
    i = pl.program_id(1)
    halo = halo_ref[...]
    pool_scr[0:POOL_HALO, :] = jnp.zeros_like(halo)
    pool_scr[POOL_HALO:2 * POOL_HALO, :] = jnp.where(i > 0, halo, jnp.zeros_like(halo))
    pool_scr[2 * POOL_HALO:, :] = u_ref[...]
    tmp_scr[:, 0:POOL_HALO, :] = jnp.zeros((tmp_scr.shape[0], POOL_HALO, POOL_GROUP_DIM), F32)
    head_pos = i * tm + lax.broadcasted_iota(jnp.int32, (POOL_HALO, POOL_GROUP_DIM), 0)

    def rows_chain(r0):
        rs = pl.ds(r0, sub)
        group_cols = [slice(gi * POOL_GROUP_DIM, (gi + 1) * POOL_GROUP_DIM) for gi in range(len(POOL_SIZES))]
        diffs = []
        ext = sub + 2 * POOL_HALO
        for gi, (cols, w) in enumerate(zip(group_cols, POOL_SIZES)):
            ug = u_ref[rs, cols]
            levels = w.bit_length() - 1
            for level in range(levels):
                shift = 1 << level
                dst = tmp_scr.at[(r0 // sub * len(POOL_SIZES) + gi) * 2 + level % 2]
                if level == 0:
                    cur = pool_scr[pl.ds(r0 + POOL_HALO, ext - POOL_HALO), cols]
                    back = pool_scr[pl.ds(r0 + POOL_HALO - shift, ext - POOL_HALO), cols]
                else:
                    src = tmp_scr.at[(r0 // sub * len(POOL_SIZES) + gi) * 2 + (level - 1) % 2]
                    cur = src[pl.ds(POOL_HALO, ext - POOL_HALO), :]
                    back = src[pl.ds(POOL_HALO - shift, ext - POOL_HALO), :]
                if level == levels - 1:
                    acc = (cur + back)[POOL_HALO:]
                else:
                    dst[pl.ds(POOL_HALO, ext - POOL_HALO), :] = cur + back
            inv_count = jnp.full((sub, POOL_GROUP_DIM), 1.0 / w, F32)
            if r0 == 0:
                inv_count = jnp.concatenate([1.0 / jnp.minimum(head_pos + 1, w).astype(F32),
                                             inv_count[POOL_HALO:]], axis=0)
            diffs.append((acc * inv_count - ug).astype(BF16))
        yield
        y_pool = jnp.dot(jnp.concatenate(diffs, axis=1), pb_ref[...], preferred_element_type=F32)
        y_attn = jnp.dot(a_ref[rs, :], pa_ref[...], preferred_element_type=F32)
        yield
        mixed = ga_ref[rs, :] * y_attn.astype(BF16) + gb_ref[rs, :] * y_pool.astype(BF16)
        yield
        y = jnp.dot(mixed, wout_ref[...], preferred_element_type=F32)
        yield
        h = _layer_norm(alpha * x_ref[rs, :] + y, g1_ref[...], b1_ref[...])
        h_ref[rs, :] = h
        h_hi = h.astype(BF16)
        h_rounded = h_hi.astype(F32)
        packed = _pack_bf16_pairs(h_rounded)
        for c in range(PACK_CHUNKS):
            hb_ref[c, rs, :] = packed[:, c * LANES:(c + 1) * LANES]

        h_lo = (h - h_rounded).astype(BF16)
        yield
        nt = (((1,), (1,)), ((), ()))
        both = lax.dot_general(wr_cat_ref[...], h_hi, nt, preferred_element_type=F32)
        lo_hi = lax.dot_general(wr_cat_ref[0:ROUTER_LANES, :], h_lo, nt, preferred_element_type=F32)
        logits = both[0:ROUTER_LANES] + both[ROUTER_LANES:] + lo_hi + br_ref[...]
        yield
        row = lax.broadcasted_iota(jnp.int32, logits.shape, 0)
        big = jnp.int32(2 ** 30)
        is_group = row < N_EXPERT_GROUPS
        gl = jnp.where(is_group, logits, -jnp.inf)
        gmax = jnp.max(gl, axis=0, keepdims=True)
        g_idx = jnp.min(jnp.where(gl == gmax, row, big), axis=0, keepdims=True)
        g_prob = 1.0 / jnp.sum(jnp.exp(gl - gmax), axis=0, keepdims=True)
        expert = row - N_EXPERT_GROUPS
        in_group = (expert >= g_idx * EXPERTS_PER_GROUP) & (expert < (g_idx + 1) * EXPERTS_PER_GROUP)
        el = jnp.where(in_group, logits, -jnp.inf)
        v1 = jnp.max(el, axis=0, keepdims=True)
        i1 = jnp.min(jnp.where(el == v1, expert, big), axis=0, keepdims=True)
        el2 = jnp.where(expert == i1, -jnp.inf, el)
        v2 = jnp.max(el2, axis=0, keepdims=True)
        i2 = jnp.min(jnp.where(el2 == v2, expert, big), axis=0, keepdims=True)
        e2 = jnp.exp(v2 - v1)
        den = 1.0 + e2
        first_row = lax.broadcasted_iota(jnp.int32, (SUBLANES, sub), 0) == 0
        wts_ref[:, rs] = jnp.where(first_row, 1.0 / den * g_prob, e2 / den * g_prob)
        ids_ref[:, rs] = jnp.where(first_row, i1, i2).astype(F32)
        chosen = jnp.where((expert == i1) | (expert == i2), 1.0, 0.0)
        per_expert = jnp.broadcast_to(jnp.sum(chosen, axis=1, keepdims=True), (ROUTER_LANES, LANES))
        to_lane = (lax.broadcasted_iota(jnp.int32, (ROUTER_LANES, LANES), 0) - N_EXPERT_GROUPS
                   == lax.broadcasted_iota(jnp.int32, (ROUTER_LANES, LANES), 1))
        counts.append(jnp.sum(jnp.where(to_lane, per_expert, 0.0), axis=0, keepdims=True))
        yield

    counts = []
    chains = [rows_chain(r0) for r0 in range(0, tm, sub)]
    n_stages = 7
    for step in range(n_stages + len(chains) - 1):
        for lag, chain in enumerate(chains):
            if 0 <= step - lag < n_stages:
                next(chain)
    cnt_ref[...] = jnp.broadcast_to(functools.reduce(lambda a, b: a + b, counts), cnt_ref.shape)


def _post_call(a, u, ga, gb, x, w_proj_attn, w_pool, pool_scale, w_proj_pool, w_out, gamma, beta,
               w_router_group, b_router_group, w_router_expert, b_router_expert, alpha, b0, tm=1024, n_sub=4):
    B = a.shape[0]
    _, S, D = x.shape
    assert S % tm == 0 and tm % POOL_HALO == 0
    n_logits = N_EXPERT_GROUPS + N_EXPERTS
    assert n_logits <= ROUTER_LANES
    wr = jnp.concatenate([w_router_group, w_router_expert], axis=1).T
    wr = jnp.pad(wr, ((0, ROUTER_LANES - n_logits), (0, 0)))
    wr_hi = wr.astype(BF16)
    wr_lo = (wr - wr_hi.astype(F32)).astype(BF16)
    wr_cat = jnp.concatenate([wr_hi, wr_lo], axis=0)
    br = jnp.pad(jnp.concatenate([b_router_group, b_router_expert]),
                 (0, ROUTER_LANES - n_logits)).reshape(ROUTER_LANES, 1)
    n_tiles = B * (S // tm)
    row_spec = lambda width: pl.BlockSpec((None, tm, width), lambda b, i: (b, i, 0))
    full = lambda arr: pl.BlockSpec(arr.shape, lambda b, i: (0,) * arr.ndim)
    halo_blocks = tm // POOL_HALO
    halo_spec = pl.BlockSpec((None, POOL_HALO, POOL_WIDTH),
                             lambda b, i: (b, jnp.maximum(i * halo_blocks - 1, 0), 0))
    weights = [w_proj_attn.astype(BF16), _fold_pool_call(w_pool, pool_scale, w_proj_pool),
               w_out.astype(BF16), gamma.reshape(1, D), beta.reshape(1, D), wr_cat, br]
    return pl.pallas_call(
        functools.partial(_post_kernel, tm=tm, sub=tm // n_sub, alpha=alpha),
        grid=(B, S // tm),
        in_specs=[row_spec(GROUP_WIDTH), row_spec(POOL_WIDTH), halo_spec, row_spec(D), row_spec(D),
                  pl.BlockSpec((None, tm, D), lambda b, i: (b + b0, i, 0))] + [full(w) for w in weights],
        out_specs=[row_spec(D),
                   pl.BlockSpec((PACK_CHUNKS, tm, LANES), lambda b, i: (0, b * (S // tm) + i, 0)),
                   pl.BlockSpec((SUBLANES, tm), lambda b, i: (0, b * (S // tm) + i)),
                   pl.BlockSpec((SUBLANES, tm), lambda b, i: (0, b * (S // tm) + i)),
                   pl.BlockSpec((SUBLANES, LANES), lambda b, i: (b * (S // tm) + i, 0))],
        out_shape=[jax.ShapeDtypeStruct((B, S, D), F32),
                   jax.ShapeDtypeStruct((PACK_CHUNKS, B * S, LANES), jnp.int32),
                   jax.ShapeDtypeStruct((SUBLANES, B * S), F32), jax.ShapeDtypeStruct((SUBLANES, B * S), F32),
                   jax.ShapeDtypeStruct((n_tiles * SUBLANES, LANES), F32)],
        scratch_shapes=[pltpu.VMEM((tm + 2 * POOL_HALO, POOL_WIDTH), F32),
                        pltpu.VMEM((n_sub * len(POOL_SIZES) * 2, tm // n_sub + 2 * POOL_HALO, POOL_GROUP_DIM), F32)],
        compiler_params=pltpu.CompilerParams(dimension_semantics=("parallel", "parallel"),
                                             vmem_limit_bytes=VMEM_LIMIT_BYTES),
        name="post",
    )(a, u, u, ga, gb, x, *weights)


def _rank_kernel(ids_ref, cnt_ref, pos_ref, ends_ref, run_ref, start_ref, earlier_ref, *, tm, tile_rows):
    i = pl.program_id(0)

    @pl.when(i == 0)
    def _():
        total = jnp.sum(cnt_ref[...], axis=0, keepdims=True) / SUBLANES
        padded = jnp.broadcast_to(jnp.ceil(total / tile_rows) * tile_rows, ends_ref.shape)
        lane1 = lax.broadcasted_iota(jnp.int32, padded.shape, 1)
        incl = padded
        shift = 1
        while shift < LANES:
            incl = incl + jnp.where(lane1 >= shift, pltpu.roll(incl, shift, axis=1), 0.0)
            shift *= 2
        ends_ref[...] = incl
        start_row = jnp.broadcast_to((incl - padded)[0:1, :], start_ref.shape)
        on_diagonal = (lax.broadcasted_iota(jnp.int32, start_ref.shape, 0)
                       == lax.broadcasted_iota(jnp.int32, start_ref.shape, 1))
        start_col = jnp.sum(jnp.where(on_diagonal, start_row, 0.0), axis=1, keepdims=True)
        start_ref[...] = jnp.broadcast_to(start_col, start_ref.shape)
        run_ref[...] = jnp.zeros_like(run_ref)
        row = lax.broadcasted_iota(jnp.int32, (tm, tm), 0)
        col = lax.broadcasted_iota(jnp.int32, (tm, tm), 1)
        earlier_ref[...] = jnp.where(row < col, 1.0, 0.0).astype(BF16)

    ids = ids_ref[...]
    expert = lax.broadcasted_iota(jnp.int32, (N_EXPERTS, tm), 0).astype(F32)
    oh0 = expert == ids[0:1, :]
    oh1 = expert == ids[1:2, :]
    onehot = jnp.where(oh0 | oh1, 1.0, 0.0)
    before = jnp.dot(onehot.astype(BF16), earlier_ref[...], preferred_element_type=F32)
    slot = start_ref[:, 0:1] + run_ref[:, 0:1] + before
    p0 = jnp.sum(jnp.where(oh0, slot, 0.0), axis=0, keepdims=True)
    p1 = jnp.sum(jnp.where(oh1, slot, 0.0), axis=0, keepdims=True)
    first_row = lax.broadcasted_iota(jnp.int32, pos_ref.shape, 0) == 0
    pos_ref[...] = jnp.where(first_row, p0, p1).astype(jnp.int32)
    run_ref[...] += jnp.broadcast_to(jnp.sum(onehot, axis=1, keepdims=True), run_ref.shape)


def _rank_call(ids_t, counts, tile_rows, tm=1024):
    N = ids_t.shape[1]
    assert N % tm == 0 and SUBLANES * (2 * N + N_EXPERTS * tile_rows) < 2 ** 24
    assert N_EXPERTS <= LANES
    return pl.pallas_call(
        functools.partial(_rank_kernel, tm=tm, tile_rows=tile_rows),
        grid=(N // tm,),
        in_specs=[pl.BlockSpec((SUBLANES, tm), lambda i: (0, i)),
                  pl.BlockSpec(counts.shape, lambda i: (0, 0))],
        out_specs=[pl.BlockSpec((SUBLANES, tm), lambda i: (0, i)),
                   pl.BlockSpec((SUBLANES, LANES), lambda i: (0, 0))],
        out_shape=[jax.ShapeDtypeStruct((SUBLANES, N), jnp.int32),
                   jax.ShapeDtypeStruct((SUBLANES, LANES), F32)],
        scratch_shapes=[pltpu.VMEM((N_EXPERTS, LANES), F32), pltpu.VMEM((N_EXPERTS, LANES), F32),
                        pltpu.VMEM((tm, tm), BF16)],
        compiler_params=pltpu.CompilerParams(dimension_semantics=("arbitrary",),
                                             vmem_limit_bytes=VMEM_LIMIT_BYTES),
        name="rank",
    )(ids_t, counts)


SC_CORES = 2
SC_SUBCORES = 16
SC_WORKERS = SC_CORES * SC_SUBCORES
SC_CHUNK = 128


def _sc_mesh():
    return plsc.VectorSubcoreMesh(core_axis_name="c", subcore_axis_name="s",
                                  num_cores=SC_CORES, num_subcores=SC_SUBCORES)


def _sc_dispatch(packed, pos_t, n_rows):
    n_chunks, n_tok, width = packed.shape
    per_worker = n_tok // SC_WORKERS
    assert n_tok % (SC_WORKERS * SC_CHUNK) == 0

    @functools.partial(
        pl.kernel, mesh=_sc_mesh(),
        out_type=jax.ShapeDtypeStruct((n_chunks, n_rows, width), packed.dtype),
        scratch_types=[pltpu.VMEM((SC_CHUNK,), jnp.int32), pltpu.VMEM((SC_CHUNK,), jnp.int32),
                       pltpu.VMEM((n_chunks, SC_CHUNK, width), packed.dtype)]
                      + [pltpu.SemaphoreType.DMA] * (3 * n_chunks),
        name="sc_dispatch")
    def run(packed_hbm, pos_hbm, out_hbm, idx0, idx1, bufs, *sems):
        load_sems, sems0, sems1 = sems[:n_chunks], sems[n_chunks:2 * n_chunks], sems[2 * n_chunks:]
        worker = lax.axis_index("s") * SC_CORES + lax.axis_index("c")

        @pl.loop(0, per_worker // SC_CHUNK)
        def _(j):
            base = worker * per_worker + j * SC_CHUNK
            loads = [pltpu.async_copy(packed_hbm.at[c, pl.ds(base, SC_CHUNK)], bufs.at[c], load_sems[c])
                     for c in range(n_chunks)]
            pltpu.sync_copy(pos_hbm.at[0, pl.ds(base, SC_CHUNK)], idx0)
            pltpu.sync_copy(pos_hbm.at[1, pl.ds(base, SC_CHUNK)], idx1)
            scatters = []
            for c in range(n_chunks):
                loads[c].wait()
                scatters.append(pltpu.async_copy(bufs.at[c], out_hbm.at[c].at[idx0], sems0[c]))
                scatters.append(pltpu.async_copy(bufs.at[c], out_hbm.at[c].at[idx1], sems1[c]))
            for s in scatters:
                s.wait()

    return run(packed, pos_t)


def _sc_combine(sorted_rows, pos_t):
    n_chunks, _, width = sorted_rows.shape
    n_tok = pos_t.shape[1]
    per_worker = n_tok // SC_WORKERS
    assert n_tok % (SC_WORKERS * SC_CHUNK) == 0

    @functools.partial(
        pl.kernel, mesh=_sc_mesh(),
        out_type=jax.ShapeDtypeStruct((2, n_chunks, n_tok, width), sorted_rows.dtype),
        scratch_types=[pltpu.VMEM((SC_CHUNK,), jnp.int32),
                       pltpu.VMEM((n_chunks, SC_CHUNK, width), sorted_rows.dtype)]
                      + [pltpu.SemaphoreType.DMA] * (2 * n_chunks),
        name="sc_combine")
    def run(rows_hbm, pos_hbm, out_hbm, idx, bufs, *sems):
        gather_sems, write_sems = sems[:n_chunks], sems[n_chunks:]
        worker = lax.axis_index("s") * SC_CORES + lax.axis_index("c")

        @pl.loop(0, per_worker // SC_CHUNK)
        def _(j):
            base = worker * per_worker + j * SC_CHUNK
            for k in range(2):
                pltpu.sync_copy(pos_hbm.at[k, pl.ds(base, SC_CHUNK)], idx)
                gathers = [pltpu.async_copy(rows_hbm.at[c].at[idx], bufs.at[c], gather_sems[c])
                           for c in range(n_chunks)]
                writes = []
                for c in range(n_chunks):
                    gathers[c].wait()
                    writes.append(pltpu.async_copy(bufs.at[c], out_hbm.at[k, c, pl.ds(base, SC_CHUNK)],
                                                   write_sems[c]))
                for w in writes:
                    w.wait()

    return run(sorted_rows, pos_t)


def _expert_kernel(tile_expert_ref, n_used_ref, xs_ref, wg_ref, wu_ref, wd_ref, ys_ref):
    del tile_expert_ref

    @pl.when(pl.program_id(0) < n_used_ref[0])
    def _():
        wg = wg_ref[...].astype(BF16)
        wu = wu_ref[...].astype(BF16)
        wd = wd_ref[...].astype(BF16)
        tile_rows = xs_ref.shape[1]
        sub = tile_rows // 2

        def rows_chain(r0):
            rs = pl.ds(r0, sub)
            x = _unpack_bf16_pairs(jnp.concatenate([xs_ref[c, rs, :] for c in range(PACK_CHUNKS)], axis=1))
            yield
            gate = jnp.dot(x, wg, preferred_element_type=F32)
            up = jnp.dot(x, wu, preferred_element_type=F32)
            yield
            hidden = (jax.nn.silu(gate) * up).astype(BF16)
            yield
            y = jnp.dot(hidden, wd, preferred_element_type=F32)
            yield
            y = _pack_bf16_pairs(y.astype(BF16).astype(F32))
            for c in range(PACK_CHUNKS):
                ys_ref[c, rs, :] = y[:, c * LANES:(c + 1) * LANES]
            yield

        chains = [rows_chain(r0) for r0 in range(0, tile_rows, sub)]
        n_stages = 5
        for step in range(n_stages + len(chains) - 1):
            for lag, chain in enumerate(chains):
                if 0 <= step - lag < n_stages:
                    next(chain)


def _expert_call(xs, tile_expert, n_used, w_gate, w_up, w_down, tile_rows):
    _, n_rows, _ = xs.shape
    D = w_gate.shape[1]
    row_block = pl.BlockSpec((PACK_CHUNKS, tile_rows, LANES),
                             lambda i, te, nu: (0, jnp.minimum(i, nu[0] - 1), 0))
    return pl.pallas_call(
        _expert_kernel,
        grid_spec=pltpu.PrefetchScalarGridSpec(
            num_scalar_prefetch=2,
            grid=(n_rows // tile_rows,),
            in_specs=[row_block,
                      pl.BlockSpec((None, D, D_EXPERT), lambda i, te, nu: (te[i], 0, 0)),
                      pl.BlockSpec((None, D, D_EXPERT), lambda i, te, nu: (te[i], 0, 0)),
                      pl.BlockSpec((None, D_EXPERT, D), lambda i, te, nu: (te[i], 0, 0))],
            out_specs=row_block),
        out_shape=jax.ShapeDtypeStruct(xs.shape, xs.dtype),
        compiler_params=pltpu.CompilerParams(dimension_semantics=("arbitrary",),
                                             vmem_limit_bytes=VMEM_LIMIT_BYTES),
        name="experts",
    )(tile_expert, n_used, xs, w_gate, w_up, w_down)


def _final_kernel(h_ref, y_ref, wts_ref, g2_ref, b2_ref, *rest, alpha):
    out_ref = rest[-1]
    tm = h_ref.shape[0]
    on_diagonal = (lax.broadcasted_iota(jnp.int32, (tm, tm), 0)
                   == lax.broadcasted_iota(jnp.int32, (tm, tm), 1))
    z = alpha * h_ref[...]
    for k in range(2):
        w_col = jnp.sum(jnp.where(on_diagonal, jnp.broadcast_to(wts_ref[k:k + 1, :], (tm, tm)), 0.0),
                        axis=1, keepdims=True)
        yk = _unpack_bf16_pairs(jnp.concatenate([y_ref[k, c] for c in range(PACK_CHUNKS)], axis=1))
        z = z + w_col * yk.astype(F32)
    out_ref[...] = _layer_norm(z, g2_ref[...], b2_ref[...])


def _final_call(h, y, wts, gamma, beta, alpha, row0, n_total, earlier_out, tm=1024):
    N, D = h.shape
    assert N % tm == 0 and row0 % tm == 0
    operands = [h, y, wts, gamma.reshape(1, D), beta.reshape(1, D)]
    in_specs = [pl.BlockSpec((tm, D), lambda i: (i, 0)),
                pl.BlockSpec((2, PACK_CHUNKS, tm, LANES), lambda i: (0, 0, i, 0)),
                pl.BlockSpec((SUBLANES, tm), lambda i: (0, i)),
                pl.BlockSpec((1, D), lambda i: (0, 0)),
                pl.BlockSpec((1, D), lambda i: (0, 0))]
    aliases = {}
    if earlier_out is not None:
        aliases = {len(operands): 0}
        operands.append(earlier_out)
        in_specs.append(pl.BlockSpec(memory_space=pl.ANY))
    return pl.pallas_call(
        functools.partial(_final_kernel, alpha=alpha),
        grid=(N // tm,),
        in_specs=in_specs,
        out_specs=pl.BlockSpec((tm, D), lambda i: (i + row0 // tm, 0)),
        out_shape=jax.ShapeDtypeStruct((n_total, D), F32),
        input_output_aliases=aliases,
        compiler_params=pltpu.CompilerParams(dimension_semantics=("parallel",),
                                             vmem_limit_bytes=VMEM_LIMIT_BYTES),
        name="final",
    )(*operands)


def _moe(packed, ids_t, counts, w_gate, w_up, w_down, tile_rows=512):
    N = ids_t.shape[1]
    pos_t, ends = _rank_call(ids_t, counts, tile_rows)
    n_tiles = 2 * N // tile_rows + N_EXPERTS
    seg_end = ends[0, :N_EXPERTS].astype(jnp.int32)
    tile_start = jnp.arange(n_tiles, dtype=jnp.int32) * tile_rows
    tile_expert = jnp.minimum(jnp.sum(seg_end[None, :] <= tile_start[:, None], axis=1),
                              N_EXPERTS - 1).astype(jnp.int32)
    n_used = (seg_end[N_EXPERTS - 1:] // tile_rows).astype(jnp.int32)
    xs = _sc_dispatch(packed, pos_t, n_tiles * tile_rows)
    ys = _expert_call(xs, tile_expert, n_used, w_gate, w_up, w_down, tile_rows)
    return _sc_combine(ys, pos_t)


@jax.jit
def kernel(x, w_in, b_in, rel_bias_table, w_pool, pool_scale, w_proj_attn, w_proj_pool, w_out, ln1_gamma, ln1_beta, w_router_group, b_router_group, w_router_expert, b_router_expert, w_expert_gate, w_expert_up, w_expert_down, ln2_gamma, ln2_beta):
    B, S, D = x.shape
    depth = w_in.shape[0]
    alpha = (2.0 * depth) ** 0.25
    n_parts = 2 if B % 2 == 0 else 1
    nb = B // n_parts
    for layer in range(depth):
        out = None
        for part in range(n_parts):
            b0 = part * nb
            qkv, u, ga, gb = _proj_call(x, w_in[layer], b_in[layer], b0, nb)
            a = _attn_call(qkv, rel_bias_table)
            h, packed, ids, wts, counts = _post_call(
                a, u, ga, gb, x, w_proj_attn[layer], w_pool[layer], pool_scale[layer],
                w_proj_pool[layer], w_out[layer], ln1_gamma[layer], ln1_beta[layer],
                w_router_group[layer], b_router_group[layer],
                w_router_expert[layer], b_router_expert[layer], alpha, b0)
            y = _moe(packed, ids, counts,
                     w_expert_gate[layer], w_expert_up[layer], w_expert_down[layer])
            out = _final_call(h.reshape(nb * S, D), y, wts, ln2_gamma[layer],
                              ln2_beta[layer], alpha, b0 * S, B * S, out)
        x = out.reshape(B, S, D)
    return x
```

```python
import functools
import math

import jax
import jax.numpy as jnp
import numpy as np
from jax import lax
from jax.experimental import pallas as pl
from jax.experimental.pallas import tpu as pltpu
from jax.experimental.pallas import tpu_sc as plsc

F32 = jnp.float32
BF16 = jnp.bfloat16

HEAD_DIM = 64
ATTN_CONFIGS = ((128, 1), (512, 4), (2048, 16))
N_GROUPS = len(ATTN_CONFIGS)
HEADS_PER_GROUP = 4
GROUP_WIDTH = HEADS_PER_GROUP * HEAD_DIM
ATTN_WIDTH = N_GROUPS * GROUP_WIDTH
BLOCK = 128
N_REL_BUCKETS = 32
REL_MAX_DISTANCE = 2048
NEG_INF = -1e30

POOL_SIZES = (2, 4, 8, 16)
POOL_GROUP_DIM = 128
POOL_WIDTH = POOL_GROUP_DIM * len(POOL_SIZES)
POOL_HALO = 16

N_EXPERT_GROUPS = 4
EXPERTS_PER_GROUP = 8
N_EXPERTS = N_EXPERT_GROUPS * EXPERTS_PER_GROUP
D_EXPERT = 256
LN_EPS = 1e-5

VMEM_LIMIT_BYTES = 56 * 1024 * 1024
LANES = 128
HALVES = GROUP_WIDTH // LANES


def _layer_norm(z, gamma, beta):
    mu = jnp.mean(z, axis=-1, keepdims=True)
    zc = z - mu
    var = jnp.mean(zc * zc, axis=-1, keepdims=True)
    return zc * lax.rsqrt(var + LN_EPS) * gamma + beta


def _proj_kernel(x_ref, w_ref, b_ref, *refs, tm, d_model):
    qkv_refs = refs[:9]
    u_ref, ga_ref, gb_ref, xb_ref, acc_ref = refs[9:]
    xb_ref[...] = x_ref[...].astype(BF16)

    def chunk(c0, width):
        acc = jnp.dot(xb_ref[...], w_ref[:, c0:c0 + width], preferred_element_type=F32)
        return acc + b_ref[:, c0:c0 + width]

    pool_off = 3 * ATTN_WIDTH
    ga_off = pool_off + POOL_WIDTH

    def gate_job(gate_ref, off, c):
        def run():
            gate_ref[:, c * 256:(c + 1) * 256] = jax.nn.sigmoid(chunk(off + c * 256, 256)).astype(BF16)
        return run

    def pool_job(c):
        def run():
            u_ref[:, c * 256:(c + 1) * 256] = chunk(pool_off + c * 256, 256)
        return run

    def qkv_job(which, g, staged):
        def run():
            dil = ATTN_CONFIGS[g][1]
            out = qkv_refs[which * 3 + g]
            acc = chunk(which * ATTN_WIDTH + g * GROUP_WIDTH, GROUP_WIDTH)
            if which == 0:
                acc = acc * HEAD_DIM ** -0.5
            if dil == 1:
                out[0] = acc.astype(BF16)
            else:
                planes = [staged * HALVES + half for half in range(HALVES)]
                for half, plane in enumerate(planes):
                    acc_ref[plane] = acc[:, half * LANES:(half + 1) * LANES]
                for r in range(dil):
                    for half, plane in enumerate(planes):
                        out[r, :, half * LANES:(half + 1) * LANES] = (
                            acc_ref[plane, pl.ds(r, tm // dil, stride=dil), :].astype(BF16))
        return run

    gates = [gate_job(ref, off, c) for ref, off in ((ga_ref, ga_off), (gb_ref, ga_off + d_model))
             for c in range(d_model // 256)]
    dilated = [(which, g) for g in range(N_GROUPS) if ATTN_CONFIGS[g][1] > 1 for which in range(3)]
    strided = [qkv_job(which, g, k) for k, (which, g) in enumerate(dilated)]
    plain = [qkv_job(which, g, 0) for g in range(N_GROUPS) if ATTN_CONFIGS[g][1] == 1 for which in range(3)]
    pools = [pool_job(c) for c in range(POOL_WIDTH // 256)]
    order = []
    while gates or strided:
        if gates:
            order.append(gates.pop(0))
        if strided:
            order.append(strided.pop(0))
    for job in order + plain + pools:
        job()


def _proj_call(x, w_in, b_in, b0, B, tm=1024):
    _, S, D = x.shape
    in_width = w_in.shape[1]
    assert in_width == 3 * ATTN_WIDTH + POOL_WIDTH + 2 * D
    assert S % tm == 0
    grid = (B, S // tm)
    qkv_shapes, qkv_specs = [], []
    for _ in range(3):
        for (_, dil) in ATTN_CONFIGS:
            assert tm % (dil * 16) == 0
            qkv_shapes.append(jax.ShapeDtypeStruct((B, dil, S // dil, GROUP_WIDTH), BF16))
            qkv_specs.append(pl.BlockSpec((None, dil, tm // dil, GROUP_WIDTH), lambda b, i: (b, 0, i, 0)))
    row_spec = lambda width: pl.BlockSpec((None, tm, width), lambda b, i: (b, i, 0))
    out_shape = qkv_shapes + [jax.ShapeDtypeStruct((B, S, POOL_WIDTH), F32),
                              jax.ShapeDtypeStruct((B, S, D), BF16),
                              jax.ShapeDtypeStruct((B, S, D), BF16)]
    out_specs = qkv_specs + [row_spec(POOL_WIDTH), row_spec(D), row_spec(D)]
    outs = pl.pallas_call(
        functools.partial(_proj_kernel, tm=tm, d_model=D),
        grid=grid,
        in_specs=[pl.BlockSpec((None, tm, D), lambda b, i: (b + b0, i, 0)),
                  pl.BlockSpec((D, in_width), lambda b, i: (0, 0), pipeline_mode=pl.Buffered(1)),
                  pl.BlockSpec((1, in_width), lambda b, i: (0, 0))],
        out_specs=out_specs,
        out_shape=out_shape,
        scratch_shapes=[pltpu.VMEM((tm, D), BF16),
                        pltpu.VMEM((3 * sum(dil > 1 for _, dil in ATTN_CONFIGS) * HALVES, tm, LANES), F32)],
        compiler_params=pltpu.CompilerParams(dimension_semantics=("parallel", "parallel"),
                                             vmem_limit_bytes=VMEM_LIMIT_BYTES),
        name="proj",
    )(x, w_in.astype(BF16), b_in.reshape(1, in_width))
    qkv = [o.reshape(B, S, GROUP_WIDTH) for o in outs[:9]]
    return qkv, outs[9], outs[10], outs[11]


def _t5_causal_bucket(dist):
    max_exact = N_REL_BUCKETS // 2
    is_small = dist < max_exact
    d = jnp.maximum(dist, 1).astype(F32)
    large = max_exact + (jnp.log(d / max_exact) / math.log(REL_MAX_DISTANCE / max_exact)
                         * (N_REL_BUCKETS - max_exact)).astype(jnp.int32)
    large = jnp.minimum(large, N_REL_BUCKETS - 1)
    return jnp.where(is_small, dist, large)


def _attn_bias(rel_bias_table):
    full, first = [], []
    for g, (window, dil) in enumerate(ATTN_CONFIGS):
        span = window // dil
        table = rel_bias_table[:, g * HEADS_PER_GROUP:(g + 1) * HEADS_PER_GROUP].astype(F32)
        lq = np.arange(BLOCK)
        for lk, dst in ((np.arange(-BLOCK, BLOCK), full), (lq, first)):
            step = jnp.asarray(lq[:, None] - lk[None, :], jnp.int32)
            in_window = (step >= 0) & (step <= span)
            bucket = _t5_causal_bucket(jnp.clip(step, 0, span) * dil)
            bias = jnp.einsum('qkb,bh->hqk', jax.nn.one_hot(bucket, N_REL_BUCKETS, dtype=F32), table,
                              precision=lax.Precision.HIGHEST)
            bias = jnp.where(in_window[None], bias, NEG_INF)
            dst.append(bias.reshape(HEADS_PER_GROUP * BLOCK, lk.shape[0]))
    return jnp.stack(full), jnp.stack(first)


def _run_skewed(chains, n_stages):
    for step in range(n_stages + len(chains) - 1):
        for lag, chain in enumerate(chains):
            if 0 <= step - lag < n_stages:
                next(chain)


ATTN_STAGES = 4


def _attn_kernel(*refs, seq):
    qkv = refs[:9]
    bias_ref, bias_first_ref, out_ref, o_scr, l_scr = refs[9:]
    rows = HEADS_PER_GROUP * BLOCK
    row_head = lax.broadcasted_iota(jnp.int32, (rows, GROUP_WIDTH), 0) // BLOCK
    lane_head_r = lax.broadcasted_iota(jnp.int32, (rows, GROUP_WIDTH), 1) // HEAD_DIM
    head_mask = row_head == lane_head_r
    heads_per_half = LANES // HEAD_DIM
    lane_head = lax.broadcasted_iota(jnp.int32, (BLOCK, LANES), 1) // HEAD_DIM

    def heads_to_lanes(per_head):
        out = per_head[0]
        for h in range(1, heads_per_half):
            out = jnp.where(lane_head == h, per_head[h], out)
        return out

    for g, (_, dil) in enumerate(ATTN_CONFIGS):
        q_ref, k_ref, v_ref = qkv[3 * g:3 * g + 3]
        sub_len = seq // dil
        n_blocks = sub_len // BLOCK

        def block(r, n, first, g=g, dil=dil, q_ref=q_ref, k_ref=k_ref, v_ref=v_ref, sub_len=sub_len):
            base = pl.multiple_of(r * sub_len + n * BLOCK, BLOCK)
            qb = q_ref[pl.ds(base, BLOCK), :]
            if first:
                kk = k_ref[pl.ds(base, BLOCK), :]
                vv = v_ref[pl.ds(base, BLOCK), :]
                bias = bias_first_ref[g]
            else:
                kbase = pl.multiple_of(base - BLOCK, BLOCK)
                kk = k_ref[pl.ds(kbase, 2 * BLOCK), :]
                vv = v_ref[pl.ds(kbase, 2 * BLOCK), :]
                bias = bias_ref[g]
            qs = jnp.where(head_mask, jnp.concatenate([qb] * HEADS_PER_GROUP, axis=0), jnp.zeros((), BF16))
            logits = lax.dot_general(qs, kk, (((1,), (1,)), ((), ())), preferred_element_type=F32) + bias
            yield
            m = jnp.max(logits, axis=1, keepdims=True)
            p = jnp.exp(logits - m)
            s = jnp.sum(p, axis=1, keepdims=True)
            p = p.astype(BF16)
            yield
            pv = jnp.dot(p, vv, preferred_element_type=F32)
            yield
            inv_s = 1.0 / s
            lse = m + jnp.log(s)
            start = n * (BLOCK * dil) + r
            if dil == 1:
                dst = pl.ds(pl.multiple_of(start, BLOCK), BLOCK)
            else:
                dst = pl.ds(start, BLOCK, stride=dil)
            for half in range(HALVES):
                cols = slice(half * LANES, (half + 1) * LANES)
                o_heads, lse_heads = [], []
                for h in range(half * heads_per_half, (half + 1) * heads_per_half):
                    head_rows = slice(h * BLOCK, (h + 1) * BLOCK)
                    o_heads.append(pv[head_rows, cols] * inv_s[head_rows])
                    lse_heads.append(jnp.broadcast_to(lse[head_rows], (BLOCK, LANES)))
                o_scr[g * HALVES + half, dst, :] = heads_to_lanes(o_heads)
                l_scr[g * HALVES + half, dst, :] = heads_to_lanes(lse_heads)
            yield

        if n_blocks > 1:
            group = 3 if (n_blocks - 1) % 3 == 0 else 1

            _run_skewed([block(r, 0, True) for r in range(dil)], ATTN_STAGES)

            def per_subsequence(r, carry, block=block, n_blocks=n_blocks, group=group):
                def per_group(i, c):
                    _run_skewed([block(r, 1 + i * group + k, False) for k in range(group)], ATTN_STAGES)
                    return c
                lax.fori_loop(0, (n_blocks - 1) // group, per_group, 0)
                return carry
            lax.fori_loop(0, dil, per_subsequence, 0)
        else:
            group = next(c for c in (8, 4, 2, 1) if dil % c == 0)

            def per_group(i, carry, block=block, group=group):
                _run_skewed([block(i * group + k, 0, True) for k in range(group)], ATTN_STAGES)
                return carry
            lax.fori_loop(0, dil // group, per_group, 0)

    chunk = 256

    def merge(i, carry):
        sl = pl.ds(pl.multiple_of(i * chunk, chunk), chunk)
        for half in range(HALVES):
            ls = [l_scr[g * HALVES + half, sl, :] for g in range(N_GROUPS)]
            m = functools.reduce(jnp.maximum, ls)
            es = [jnp.exp(l - m) for l in ls]
            den = functools.reduce(lambda a, b: a + b, es)
            num = functools.reduce(lambda a, b: a + b,
                                   [e * o_scr[g * HALVES + half, sl, :] for g, e in enumerate(es)])
            out_ref[sl, half * LANES:(half + 1) * LANES] = (num / den).astype(BF16)
        return carry

    lax.fori_loop(0, seq // chunk, merge, 0)


def _attn_call(qkv, rel_bias_table):
    B, S, _ = qkv[0].shape
    for (_, dil) in ATTN_CONFIGS:
        assert S % (dil * BLOCK) == 0
    bias, bias_first = _attn_bias(rel_bias_table)
    seq_spec = pl.BlockSpec((None, S, GROUP_WIDTH), lambda b: (b, 0, 0))
    ordered = []
    for g in range(N_GROUPS):
        ordered += [qkv[g], qkv[3 + g], qkv[6 + g]]
    return pl.pallas_call(
        functools.partial(_attn_kernel, seq=S),
        grid=(B,),
        in_specs=[seq_spec] * 9 + [pl.BlockSpec(bias.shape, lambda b: (0, 0, 0)),
                                   pl.BlockSpec(bias_first.shape, lambda b: (0, 0, 0))],
        out_specs=seq_spec,
        out_shape=jax.ShapeDtypeStruct((B, S, GROUP_WIDTH), BF16),
        scratch_shapes=[pltpu.VMEM((N_GROUPS * HALVES, S, LANES), F32),
                        pltpu.VMEM((N_GROUPS * HALVES, S, LANES), F32)],
        compiler_params=pltpu.CompilerParams(dimension_semantics=("parallel",),
                                             vmem_limit_bytes=VMEM_LIMIT_BYTES),
        name="attn",
    )(*ordered, bias, bias_first)


PACK_CHUNKS = 4
SUBLANES = 8
ROUTER_LANES = LANES // 2


def _pack_bf16_pairs(rounded):
    w = rounded.shape[1] // 2
    bits = lax.bitcast_convert_type(rounded, jnp.int32)
    return bits[:, :w] | lax.shift_right_logical(bits[:, w:], jnp.full((), 16, jnp.int32))


def _unpack_bf16_pairs(words):
    hi = lax.bitcast_convert_type(words & jnp.int32(-65536), F32).astype(BF16)
    lo = lax.bitcast_convert_type(lax.shift_left(words, jnp.full((), 16, jnp.int32)), F32).astype(BF16)
    return jnp.concatenate([hi, lo], axis=1)


def _fold_pool_kernel(wp_ref, scale_ref, pb_ref, out_ref):
    a = wp_ref[...] * scale_ref[...]
    b = pb_ref[...]
    a_hi = a.astype(BF16)
    a_lo = (a - a_hi.astype(F32)).astype(BF16)
    b_hi = b.astype(BF16)
    b_lo = (b - b_hi.astype(F32)).astype(BF16)
    out_ref[...] = (jnp.dot(a_hi, b_hi, preferred_element_type=F32)
                    + jnp.dot(a_lo, b_hi, preferred_element_type=F32)
                    + jnp.dot(a_hi, b_lo, preferred_element_type=F32)).astype(BF16)


def _fold_pool_call(w_pool, pool_scale, w_proj_pool):
    n_groups, gd, _ = w_pool.shape
    D = w_proj_pool.shape[1]
    return pl.pallas_call(
        _fold_pool_kernel,
        grid=(n_groups,),
        in_specs=[pl.BlockSpec((None, gd, gd), lambda g: (g, 0, 0)),
                  pl.BlockSpec((None, 1, gd), lambda g: (g, 0, 0)),
                  pl.BlockSpec((gd, D), lambda g: (g, 0))],
        out_specs=pl.BlockSpec((gd, D), lambda g: (g, 0)),
        out_shape=jax.ShapeDtypeStruct((n_groups * gd, D), BF16),
        name="fold_pool",
    )(w_pool, pool_scale.reshape(n_groups, 1, gd), w_proj_pool)


def _post_kernel(a_ref, u_ref, halo_ref, ga_ref, gb_ref, x_ref,
                 pa_ref, pb_ref, wout_ref, g1_ref, b1_ref,
                 wr_cat_ref, br_ref,
                 h_ref, hb_ref, ids_ref, wts_ref, cnt_ref, pool_scr, tmp_scr, *, tm, sub, alpha):
    i = pl.program_id(1)
    halo = halo_ref[...]
    pool_scr[0:POOL_HALO, :] = jnp.zeros_like(halo)
    pool_scr[POOL_HALO:2 * POOL_HALO, :] = jnp.where(i > 0, halo, jnp.zeros_like(halo))
    pool_scr[2 * POOL_HALO:, :] = u_ref[...]
    tmp_scr[:, 0:POOL_HALO, :] = jnp.zeros((tmp_scr.shape[0], POOL_HALO, POOL_GROUP_DIM), F32)
    head_pos = i * tm + lax.broadcasted_iota(jnp.int32, (POOL_HALO, POOL_GROUP_DIM), 0)

    def rows_chain(r0):
        rs = pl.ds(r0, sub)
        group_cols = [slice(gi * POOL_GROUP_DIM, (gi + 1) * POOL_GROUP_DIM) for gi in range(len(POOL_SIZES))]
        diffs = []
        ext = sub + 2 * POOL_HALO
        for gi, (cols, w) in enumerate(zip(group_cols, POOL_SIZES)):
            ug = u_ref[rs, cols]
            levels = w.bit_length() - 1
            for level in range(levels):
                shift = 1 << level
                dst = tmp_scr.at[(r0 // sub * len(POOL_SIZES) + gi) * 2 + level % 2]
                if level == 0:
                    cur = pool_scr[pl.ds(r0 + POOL_HALO, ext - POOL_HALO), cols]
                    back = pool_scr[pl.ds(r0 + POOL_HALO - shift, ext - POOL_HALO), cols]
                else:
                    src = tmp_scr.at[(r0 // sub * len(POOL_SIZES) + gi) * 2 + (level - 1) % 2]
                    cur = src[pl.ds(POOL_HALO, ext - POOL_HALO), :]
                    back = src[pl.ds(POOL_HALO - shift, ext - POOL_HALO), :]
                if level == levels - 1:
                    acc = (cur + back)[POOL_HALO:]
                else:
                    dst[pl.ds(POOL_HALO, ext - POOL_HALO), :] = cur + back
            inv_count = jnp.full((sub, POOL_GROUP_DIM), 1.0 / w, F32)
            if r0 == 0:
                inv_count = jnp.concatenate([1.0 / jnp.minimum(head_pos + 1, w).astype(F32),
                                             inv_count[POOL_HALO:]], axis=0)
            diffs.append((acc * inv_count - ug).astype(BF16))
        yield
        y_pool = jnp.dot(jnp.concatenate(diffs, axis=1), pb_ref[...], preferred_element_type=F32)
        y_attn = jnp.dot(a_ref[rs, :], pa_ref[...], preferred_element_type=F32)
        yield
        mixed = ga_ref[rs, :] * y_attn.astype(BF16) + gb_ref[rs, :] * y_pool.astype(BF16)
        yield
        y = jnp.dot(mixed, wout_ref[...], preferred_element_type=F32)
        yield
        h = _layer_norm(alpha * x_ref[rs, :] + y, g1_ref[...], b1_ref[...])
        h_ref[rs, :] = h
        h_hi = h.astype(BF16)
        h_rounded = h_hi.astype(F32)
        packed = _pack_bf16_pairs(h_rounded)
        for c in range(PACK_CHUNKS):
            hb_ref[c, rs, :] = packed[:, c * LANES:(c + 1) * LANES]

        h_lo = (h - h_rounded).astype(BF16)
        yield
        nt = (((1,), (1,)), ((), ()))
        both = lax.dot_general(wr_cat_ref[...], h_hi, nt, preferred_element_type=F32)
        lo_hi = lax.dot_general(wr_cat_ref[0:ROUTER_LANES, :], h_lo, nt, preferred_element_type=F32)
        logits = both[0:ROUTER_LANES] + both[ROUTER_LANES:] + lo_hi + br_ref[...]
        yield
        row = lax.broadcasted_iota(jnp.int32, logits.shape, 0)
        big = jnp.int32(2 ** 30)
        is_group = row < N_EXPERT_GROUPS
        gl = jnp.where(is_group, logits, -jnp.inf)
        gmax = jnp.max(gl, axis=0, keepdims=True)
        g_idx = jnp.min(jnp.where(gl == gmax, row, big), axis=0, keepdims=True)
        g_prob = 1.0 / jnp.sum(jnp.exp(gl - gmax), axis=0, keepdims=True)
        expert = row - N_EXPERT_GROUPS
        in_group = (expert >= g_idx * EXPERTS_PER_GROUP) & (expert < (g_idx + 1) * EXPERTS_PER_GROUP)
        el = jnp.where(in_group, logits, -jnp.inf)
        v1 = jnp.max(el, axis=0, keepdims=True)
        i1 = jnp.min(jnp.where(el == v1, expert, big), axis=0, keepdims=True)
        el2 = jnp.where(expert == i1, -jnp.inf, el)
        v2 = jnp.max(el2, axis=0, keepdims=True)
        i2 = jnp.min(jnp.where(el2 == v2, expert, big), axis=0, keepdims=True)
        e2 = jnp.exp(v2 - v1)
        den = 1.0 + e2
        first_row = lax.broadcasted_iota(jnp.int32, (SUBLANES, sub), 0) == 0
        wts_ref[:, rs] = jnp.where(first_row, 1.0 / den * g_prob, e2 / den * g_prob)
        ids_ref[:, rs] = jnp.where(first_row, i1, i2).astype(F32)
        chosen = jnp.where((expert == i1) | (expert == i2), 1.0, 0.0)
        per_expert = jnp.broadcast_to(jnp.sum(chosen, axis=1, keepdims=True), (ROUTER_LANES, LANES))
        to_lane = (lax.broadcasted_iota(jnp.int32, (ROUTER_LANES, LANES), 0) - N_EXPERT_GROUPS
                   == lax.broadcasted_iota(jnp.int32, (ROUTER_LANES, LANES), 1))
        counts.append(jnp.sum(jnp.where(to_lane, per_expert, 0.0), axis=0, keepdims=True))
        yield

    counts = []
    _run_skewed([rows_chain(r0) for r0 in range(0, tm, sub)], 7)
    cnt_ref[...] = jnp.broadcast_to(functools.reduce(lambda a, b: a + b, counts), cnt_ref.shape)


def _post_call(a, u, ga, gb, x, w_proj_attn, w_pool, pool_scale, w_proj_pool, w_out, gamma, beta,
               w_router_group, b_router_group, w_router_expert, b_router_expert, alpha, b0, tm=1024, n_sub=2):
    B = a.shape[0]
    _, S, D = x.shape
    assert S % tm == 0 and tm % POOL_HALO == 0
    n_logits = N_EXPERT_GROUPS + N_EXPERTS
    assert n_logits <= ROUTER_LANES
    wr = jnp.concatenate([w_router_group, w_router_expert], axis=1).T
    wr = jnp.pad(wr, ((0, ROUTER_LANES - n_logits), (0, 0)))
    wr_hi = wr.astype(BF16)
    wr_lo = (wr - wr_hi.astype(F32)).astype(BF16)
    wr_cat = jnp.concatenate([wr_hi, wr_lo], axis=0)
    br = jnp.pad(jnp.concatenate([b_router_group, b_router_expert]),
                 (0, ROUTER_LANES - n_logits)).reshape(ROUTER_LANES, 1)
    n_tiles = B * (S // tm)
    row_spec = lambda width: pl.BlockSpec((None, tm, width), lambda b, i: (b, i, 0))
    full = lambda arr: pl.BlockSpec(arr.shape, lambda b, i: (0,) * arr.ndim)
    halo_blocks = tm // POOL_HALO
    halo_spec = pl.BlockSpec((None, POOL_HALO, POOL_WIDTH),
                             lambda b, i: (b, jnp.maximum(i * halo_blocks - 1, 0), 0))
    weights = [w_proj_attn.astype(BF16), _fold_pool_call(w_pool, pool_scale, w_proj_pool),
               w_out.astype(BF16), gamma.reshape(1, D), beta.reshape(1, D), wr_cat, br]
    return pl.pallas_call(
        functools.partial(_post_kernel, tm=tm, sub=tm // n_sub, alpha=alpha),
        grid=(B, S // tm),
        in_specs=[row_spec(GROUP_WIDTH), row_spec(POOL_WIDTH), halo_spec, row_spec(D), row_spec(D),
                  pl.BlockSpec((None, tm, D), lambda b, i: (b + b0, i, 0))] + [full(w) for w in weights],
        out_specs=[row_spec(D),
                   pl.BlockSpec((PACK_CHUNKS, tm, LANES), lambda b, i: (0, b * (S // tm) + i, 0)),
                   pl.BlockSpec((SUBLANES, tm), lambda b, i: (0, b * (S // tm) + i)),
                   pl.BlockSpec((SUBLANES, tm), lambda b, i: (0, b * (S // tm) + i)),
                   pl.BlockSpec((SUBLANES, LANES), lambda b, i: (b * (S // tm) + i, 0))],
        out_shape=[jax.ShapeDtypeStruct((B, S, D), F32),
                   jax.ShapeDtypeStruct((PACK_CHUNKS, B * S, LANES), jnp.int32),
                   jax.ShapeDtypeStruct((SUBLANES, B * S), F32), jax.ShapeDtypeStruct((SUBLANES, B * S), F32),
                   jax.ShapeDtypeStruct((n_tiles * SUBLANES, LANES), F32)],
        scratch_shapes=[pltpu.VMEM((tm + 2 * POOL_HALO, POOL_WIDTH), F32),
                        pltpu.VMEM((n_sub * len(POOL_SIZES) * 2, tm // n_sub + 2 * POOL_HALO, POOL_GROUP_DIM), F32)],
        compiler_params=pltpu.CompilerParams(dimension_semantics=("parallel", "parallel"),
                                             vmem_limit_bytes=VMEM_LIMIT_BYTES),
        name="post",
    )(a, u, u, ga, gb, x, *weights)


def _rank_kernel(ids_ref, cnt_ref, pos_ref, ends_ref, run_ref, start_ref, earlier_ref, *, tm, tile_rows):
    i = pl.program_id(0)

    @pl.when(i == 0)
    def _():
        total = jnp.sum(cnt_ref[...], axis=0, keepdims=True) / SUBLANES
        padded = jnp.broadcast_to(jnp.ceil(total / tile_rows) * tile_rows, ends_ref.shape)
        lane1 = lax.broadcasted_iota(jnp.int32, padded.shape, 1)
        incl = padded
        shift = 1
        while shift < LANES:
            incl = incl + jnp.where(lane1 >= shift, pltpu.roll(incl, shift, axis=1), 0.0)
            shift *= 2
        ends_ref[...] = incl
        start_row = jnp.broadcast_to((incl - padded)[0:1, :], start_ref.shape)
        on_diagonal = (lax.broadcasted_iota(jnp.int32, start_ref.shape, 0)
                       == lax.broadcasted_iota(jnp.int32, start_ref.shape, 1))
        start_col = jnp.sum(jnp.where(on_diagonal, start_row, 0.0), axis=1, keepdims=True)
        start_ref[...] = jnp.broadcast_to(start_col, start_ref.shape)
        run_ref[...] = jnp.zeros_like(run_ref)
        row = lax.broadcasted_iota(jnp.int32, (tm, tm), 0)
        col = lax.broadcasted_iota(jnp.int32, (tm, tm), 1)
        earlier_ref[...] = jnp.where(row < col, 1.0, 0.0).astype(BF16)

    ids = ids_ref[...]
    expert = lax.broadcasted_iota(jnp.int32, (N_EXPERTS, tm), 0).astype(F32)
    oh0 = expert == ids[0:1, :]
    oh1 = expert == ids[1:2, :]
    onehot = jnp.where(oh0 | oh1, 1.0, 0.0)
    before = jnp.dot(onehot.astype(BF16), earlier_ref[...], preferred_element_type=F32)
    slot = start_ref[:, 0:1] + run_ref[:, 0:1] + before
    p0 = jnp.sum(jnp.where(oh0, slot, 0.0), axis=0, keepdims=True)
    p1 = jnp.sum(jnp.where(oh1, slot, 0.0), axis=0, keepdims=True)
    first_row = lax.broadcasted_iota(jnp.int32, pos_ref.shape, 0) == 0
    pos_ref[...] = jnp.where(first_row, p0, p1).astype(jnp.int32)
    run_ref[...] += jnp.broadcast_to(jnp.sum(onehot, axis=1, keepdims=True), run_ref.shape)


def _rank_call(ids_t, counts, tile_rows, tm=1024):
    N = ids_t.shape[1]
    assert N % tm == 0 and SUBLANES * (2 * N + N_EXPERTS * tile_rows) < 2 ** 24
    assert N_EXPERTS <= LANES
    return pl.pallas_call(
        functools.partial(_rank_kernel, tm=tm, tile_rows=tile_rows),
        grid=(N // tm,),
        in_specs=[pl.BlockSpec((SUBLANES, tm), lambda i: (0, i)),
                  pl.BlockSpec(counts.shape, lambda i: (0, 0))],
        out_specs=[pl.BlockSpec((SUBLANES, tm), lambda i: (0, i)),
                   pl.BlockSpec((SUBLANES, LANES), lambda i: (0, 0))],
        out_shape=[jax.ShapeDtypeStruct((SUBLANES, N), jnp.int32),
                   jax.ShapeDtypeStruct((SUBLANES, LANES), F32)],
        scratch_shapes=[pltpu.VMEM((N_EXPERTS, LANES), F32), pltpu.VMEM((N_EXPERTS, LANES), F32),
                        pltpu.VMEM((tm, tm), BF16)],
        compiler_params=pltpu.CompilerParams(dimension_semantics=("arbitrary",),
                                             vmem_limit_bytes=VMEM_LIMIT_BYTES),
        name="rank",
    )(ids_t, counts)


SC_CORES = 2
SC_SUBCORES = 16
SC_WORKERS = SC_CORES * SC_SUBCORES
SC_CHUNK = 128


def _sc_mesh():
    return plsc.VectorSubcoreMesh(core_axis_name="c", subcore_axis_name="s",
                                  num_cores=SC_CORES, num_subcores=SC_SUBCORES)


def _sc_dispatch(packed, pos_t, n_rows):
    n_chunks, n_tok, width = packed.shape
    per_worker = n_tok // SC_WORKERS
    assert n_tok % (SC_WORKERS * SC_CHUNK) == 0

    @functools.partial(
        pl.kernel, mesh=_sc_mesh(),
        out_type=jax.ShapeDtypeStruct((n_chunks, n_rows, width), packed.dtype),
        scratch_types=[pltpu.VMEM((SC_CHUNK,), jnp.int32), pltpu.VMEM((SC_CHUNK,), jnp.int32),
                       pltpu.VMEM((n_chunks, SC_CHUNK, width), packed.dtype)]
                      + [pltpu.SemaphoreType.DMA] * (3 * n_chunks),
        name="sc_dispatch")
    def run(packed_hbm, pos_hbm, out_hbm, idx0, idx1, bufs, *sems):
        load_sems, sems0, sems1 = sems[:n_chunks], sems[n_chunks:2 * n_chunks], sems[2 * n_chunks:]
        worker = lax.axis_index("s") * SC_CORES + lax.axis_index("c")

        @pl.loop(0, per_worker // SC_CHUNK)
        def _(j):
            base = worker * per_worker + j * SC_CHUNK
            loads = [pltpu.async_copy(packed_hbm.at[c, pl.ds(base, SC_CHUNK)], bufs.at[c], load_sems[c])
                     for c in range(n_chunks)]
            pltpu.sync_copy(pos_hbm.at[0, pl.ds(base, SC_CHUNK)], idx0)
            pltpu.sync_copy(pos_hbm.at[1, pl.ds(base, SC_CHUNK)], idx1)
            scatters = []
            for c in range(n_chunks):
                loads[c].wait()
                scatters.append(pltpu.async_copy(bufs.at[c], out_hbm.at[c].at[idx0], sems0[c]))
                scatters.append(pltpu.async_copy(bufs.at[c], out_hbm.at[c].at[idx1], sems1[c]))
            for s in scatters:
                s.wait()

    return run(packed, pos_t)


def _sc_combine(sorted_rows, pos_t):
    n_chunks, _, width = sorted_rows.shape
    n_tok = pos_t.shape[1]
    per_worker = n_tok // SC_WORKERS
    assert n_tok % (SC_WORKERS * SC_CHUNK) == 0

    @functools.partial(
        pl.kernel, mesh=_sc_mesh(),
        out_type=jax.ShapeDtypeStruct((2, n_chunks, n_tok, width), sorted_rows.dtype),
        scratch_types=[pltpu.VMEM((SC_CHUNK,), jnp.int32),
                       pltpu.VMEM((n_chunks, SC_CHUNK, width), sorted_rows.dtype)]
                      + [pltpu.SemaphoreType.DMA] * (2 * n_chunks),
        name="sc_combine")
    def run(rows_hbm, pos_hbm, out_hbm, idx, bufs, *sems):
        gather_sems, write_sems = sems[:n_chunks], sems[n_chunks:]
        worker = lax.axis_index("s") * SC_CORES + lax.axis_index("c")

        @pl.loop(0, per_worker // SC_CHUNK)
        def _(j):
            base = worker * per_worker + j * SC_CHUNK
            for k in range(2):
                pltpu.sync_copy(pos_hbm.at[k, pl.ds(base, SC_CHUNK)], idx)
                gathers = [pltpu.async_copy(rows_hbm.at[c].at[idx], bufs.at[c], gather_sems[c])
                           for c in range(n_chunks)]
                writes = []
                for c in range(n_chunks):
                    gathers[c].wait()
                    writes.append(pltpu.async_copy(bufs.at[c], out_hbm.at[k, c, pl.ds(base, SC_CHUNK)],
                                                   write_sems[c]))
                for w in writes:
                    w.wait()

    return run(sorted_rows, pos_t)


def _expert_kernel(tile_expert_ref, n_used_ref, xs_ref, wg_ref, wu_ref, wd_ref, ys_ref):
    del tile_expert_ref

    @pl.when(pl.program_id(0) < n_used_ref[0])
    def _():
        wg = wg_ref[...].astype(BF16)
        wu = wu_ref[...].astype(BF16)
        wd = wd_ref[...].astype(BF16)
        tile_rows = xs_ref.shape[1]
        sub = tile_rows // 2

        def rows_chain(r0):
            rs = pl.ds(r0, sub)
            x = _unpack_bf16_pairs(jnp.concatenate([xs_ref[c, rs, :] for c in range(PACK_CHUNKS)], axis=1))
            yield
            gate = jnp.dot(x, wg, preferred_element_type=F32)
            up = jnp.dot(x, wu, preferred_element_type=F32)
            yield
            hidden = (jax.nn.silu(gate) * up).astype(BF16)
            yield
            y = jnp.dot(hidden, wd, preferred_element_type=F32)
            yield
            y = _pack_bf16_pairs(y.astype(BF16).astype(F32))
            for c in range(PACK_CHUNKS):
                ys_ref[c, rs, :] = y[:, c * LANES:(c + 1) * LANES]
            yield

        chains = [rows_chain(r0) for r0 in range(0, tile_rows, sub)]
        n_stages = 5
        for step in range(n_stages + len(chains) - 1):
            for lag, chain in enumerate(chains):
                if 0 <= step - lag < n_stages:
                    next(chain)


def _expert_call(xs, tile_expert, n_used, w_gate, w_up, w_down, tile_rows):
    _, n_rows, _ = xs.shape
    D = w_gate.shape[1]
    row_block = pl.BlockSpec((PACK_CHUNKS, tile_rows, LANES),
                             lambda i, te, nu: (0, jnp.minimum(i, nu[0] - 1), 0))
    return pl.pallas_call(
        _expert_kernel,
        grid_spec=pltpu.PrefetchScalarGridSpec(
            num_scalar_prefetch=2,
            grid=(n_rows // tile_rows,),
            in_specs=[row_block,
                      pl.BlockSpec((None, D, D_EXPERT), lambda i, te, nu: (te[i], 0, 0)),
                      pl.BlockSpec((None, D, D_EXPERT), lambda i, te, nu: (te[i], 0, 0)),
                      pl.BlockSpec((None, D_EXPERT, D), lambda i, te, nu: (te[i], 0, 0))],
            out_specs=row_block),
        out_shape=jax.ShapeDtypeStruct(xs.shape, xs.dtype),
        compiler_params=pltpu.CompilerParams(dimension_semantics=("arbitrary",),
                                             vmem_limit_bytes=VMEM_LIMIT_BYTES),
        name="experts",
    )(tile_expert, n_used, xs, w_gate.astype(BF16), w_up.astype(BF16), w_down.astype(BF16))


def _final_kernel(h_ref, y_ref, wts_ref, g2_ref, b2_ref, *rest, alpha):
    out_ref = rest[-1]
    tm = h_ref.shape[0]
    on_diagonal = (lax.broadcasted_iota(jnp.int32, (tm, tm), 0)
                   == lax.broadcasted_iota(jnp.int32, (tm, tm), 1))
    z = alpha * h_ref[...]
    for k in range(2):
        w_col = jnp.sum(jnp.where(on_diagonal, jnp.broadcast_to(wts_ref[k:k + 1, :], (tm, tm)), 0.0),
                        axis=1, keepdims=True)
        yk = _unpack_bf16_pairs(jnp.concatenate([y_ref[k, c] for c in range(PACK_CHUNKS)], axis=1))
        z = z + w_col * yk.astype(F32)
    out_ref[...] = _layer_norm(z, g2_ref[...], b2_ref[...])


def _final_call(h, y, wts, gamma, beta, alpha, row0, n_total, earlier_out, tm=1024):
    N, D = h.shape
    assert N % tm == 0 and row0 % tm == 0
    operands = [h, y, wts, gamma.reshape(1, D), beta.reshape(1, D)]
    in_specs = [pl.BlockSpec((tm, D), lambda i: (i, 0)),
                pl.BlockSpec((2, PACK_CHUNKS, tm, LANES), lambda i: (0, 0, i, 0)),
                pl.BlockSpec((SUBLANES, tm), lambda i: (0, i)),
                pl.BlockSpec((1, D), lambda i: (0, 0)),
                pl.BlockSpec((1, D), lambda i: (0, 0))]
    aliases = {}
    if earlier_out is not None:
        aliases = {len(operands): 0}
        operands.append(earlier_out)
        in_specs.append(pl.BlockSpec(memory_space=pl.ANY))
    return pl.pallas_call(
        functools.partial(_final_kernel, alpha=alpha),
        grid=(N // tm,),
        in_specs=in_specs,
        out_specs=pl.BlockSpec((tm, D), lambda i: (i + row0 // tm, 0)),
        out_shape=jax.ShapeDtypeStruct((n_total, D), F32),
        input_output_aliases=aliases,
        compiler_params=pltpu.CompilerParams(dimension_semantics=("parallel",),
                                             vmem_limit_bytes=VMEM_LIMIT_BYTES),
        name="final",
    )(*operands)


def _moe(packed, ids_t, counts, w_gate, w_up, w_down, tile_rows=512):
    N = ids_t.shape[1]
    pos_t, ends = _rank_call(ids_t, counts, tile_rows)
    n_tiles = 2 * N // tile_rows + N_EXPERTS
    seg_end = ends[0, :N_EXPERTS].astype(jnp.int32)
    tile_start = jnp.arange(n_tiles, dtype=jnp.int32) * tile_rows
    tile_expert = jnp.minimum(jnp.sum(seg_end[None, :] <= tile_start[:, None], axis=1),
                              N_EXPERTS - 1).astype(jnp.int32)
    n_used = (seg_end[N_EXPERTS - 1:] // tile_rows).astype(jnp.int32)
    xs = _sc_dispatch(packed, pos_t, n_tiles * tile_rows)
    ys = _expert_call(xs, tile_expert, n_used, w_gate, w_up, w_down, tile_rows)
    return _sc_combine(ys, pos_t)


@jax.jit
def kernel(x, w_in, b_in, rel_bias_table, w_pool, pool_scale, w_proj_attn, w_proj_pool, w_out, ln1_gamma, ln1_beta, w_router_group, b_router_group, w_router_expert, b_router_expert, w_expert_gate, w_expert_up, w_expert_down, ln2_gamma, ln2_beta):
    B, S, D = x.shape
    depth = w_in.shape[0]
    alpha = (2.0 * depth) ** 0.25
    n_parts = 2 if B % 2 == 0 else 1
    nb = B // n_parts
    for layer in range(depth):
        out = None
        for part in range(n_parts):
            b0 = part * nb
            qkv, u, ga, gb = _proj_call(x, w_in[layer], b_in[layer], b0, nb)
            a = _attn_call(qkv, rel_bias_table)
            h, packed, ids, wts, counts = _post_call(
                a, u, ga, gb, x, w_proj_attn[layer], w_pool[layer], pool_scale[layer],
                w_proj_pool[layer], w_out[layer], ln1_gamma[layer], ln1_beta[layer],
                w_router_group[layer], b_router_group[layer],
                w_router_expert[layer], b_router_expert[layer], alpha, b0)
            y = _moe(packed, ids, counts,
                     w_expert_gate[layer], w_expert_up[layer], w_expert_down[layer])
            out = _final_call(h.reshape(nb * S, D), y, wts, ln2_gamma[layer],
                              ln2_beta[layer], alpha, b0 * S, B * S, out)
        x = out.reshape(B, S, D)
    return x
```

```python
import functools
import math

import jax
import jax.numpy as jnp
import numpy as np
from jax import lax
from jax.experimental import pallas as pl
from jax.experimental.pallas import tpu as pltpu
from jax.experimental.pallas import tpu_sc as plsc

F32 = jnp.float32
BF16 = jnp.bfloat16

HEAD_DIM = 64
ATTN_CONFIGS = ((128, 1), (512, 4), (2048, 16))
N_GROUPS = len(ATTN_CONFIGS)
HEADS_PER_GROUP = 4
GROUP_WIDTH = HEADS_PER_GROUP * HEAD_DIM
ATTN_WIDTH = N_GROUPS * GROUP_WIDTH
BLOCK = 128
N_REL_BUCKETS = 32
REL_MAX_DISTANCE = 2048
NEG_INF = -1e30

POOL_SIZES = (2, 4, 8, 16)
POOL_GROUP_DIM = 128
POOL_WIDTH = POOL_GROUP_DIM * len(POOL_SIZES)
POOL_HALO = 16

N_EXPERT_GROUPS = 4
EXPERTS_PER_GROUP = 8
N_EXPERTS = N_EXPERT_GROUPS * EXPERTS_PER_GROUP
D_EXPERT = 256
LN_EPS = 1e-5

VMEM_LIMIT_BYTES = 56 * 1024 * 1024
LANES = 128
HALVES = GROUP_WIDTH // LANES


def _layer_norm(z, gamma, beta):
    mu = jnp.mean(z, axis=-1, keepdims=True)
    zc = z - mu
    var = jnp.mean(zc * zc, axis=-1, keepdims=True)
    return zc * lax.rsqrt(var + LN_EPS) * gamma + beta


def _proj_kernel(x_ref, w_ref, b_ref, *refs, tm, d_model):
    qkv_refs = refs[:9]
    u_ref, ga_ref, gb_ref, xb_ref, acc_ref = refs[9:]
    xb_ref[...] = x_ref[...].astype(BF16)

    def chunk(c0, width):
        acc = jnp.dot(xb_ref[...], w_ref[:, c0:c0 + width], preferred_element_type=F32)
        return acc + b_ref[:, c0:c0 + width]

    pool_off = 3 * ATTN_WIDTH
    ga_off = pool_off + POOL_WIDTH

    def gate_job(gate_ref, off, c):
        def run():
            gate_ref[:, c * 256:(c + 1) * 256] = jax.nn.sigmoid(chunk(off + c * 256, 256)).astype(BF16)
        return run

    def pool_job(c):
        def run():
            u_ref[:, c * 256:(c + 1) * 256] = chunk(pool_off + c * 256, 256)
        return run

    def qkv_job(which, g, staged):
        def run():
            dil = ATTN_CONFIGS[g][1]
            out = qkv_refs[which * 3 + g]
            acc = chunk(which * ATTN_WIDTH + g * GROUP_WIDTH, GROUP_WIDTH)
            if which == 0:
                acc = acc * HEAD_DIM ** -0.5
            if dil == 1:
                out[0] = acc.astype(BF16)
            else:
                planes = [staged * HALVES + half for half in range(HALVES)]
                for half, plane in enumerate(planes):
                    acc_ref[plane] = acc[:, half * LANES:(half + 1) * LANES]
                for r in range(dil):
                    for half, plane in enumerate(planes):
                        out[r, :, half * LANES:(half + 1) * LANES] = (
                            acc_ref[plane, pl.ds(r, tm // dil, stride=dil), :].astype(BF16))
        return run

    gates = [gate_job(ref, off, c) for ref, off in ((ga_ref, ga_off), (gb_ref, ga_off + d_model))
             for c in range(d_model // 256)]
    dilated = [(which, g) for g in range(N_GROUPS) if ATTN_CONFIGS[g][1] > 1 for which in range(3)]
    strided = [qkv_job(which, g, k) for k, (which, g) in enumerate(dilated)]
    plain = [qkv_job(which, g, 0) for g in range(N_GROUPS) if ATTN_CONFIGS[g][1] == 1 for which in range(3)]
    pools = [pool_job(c) for c in range(POOL_WIDTH // 256)]
    order = []
    while gates or strided:
        if gates:
            order.append(gates.pop(0))
        if strided:
            order.append(strided.pop(0))
    for job in order + plain + pools:
        job()


def _proj_call(x, w_in, b_in, b0, B, tm=1024):
    _, S, D = x.shape
    in_width = w_in.shape[1]
    assert in_width == 3 * ATTN_WIDTH + POOL_WIDTH + 2 * D
    assert S % tm == 0
    grid = (B, S // tm)
    qkv_shapes, qkv_specs = [], []
    for _ in range(3):
        for (_, dil) in ATTN_CONFIGS:
            assert tm % (dil * 16) == 0
            qkv_shapes.append(jax.ShapeDtypeStruct((B, dil, S // dil, GROUP_WIDTH), BF16))
            qkv_specs.append(pl.BlockSpec((None, dil, tm // dil, GROUP_WIDTH), lambda b, i: (b, 0, i, 0)))
    row_spec = lambda width: pl.BlockSpec((None, tm, width), lambda b, i: (b, i, 0))
    out_shape = qkv_shapes + [jax.ShapeDtypeStruct((B, S, POOL_WIDTH), F32),
                              jax.ShapeDtypeStruct((B, S, D), BF16),
                              jax.ShapeDtypeStruct((B, S, D), BF16)]
    out_specs = qkv_specs + [row_spec(POOL_WIDTH), row_spec(D), row_spec(D)]
    outs = pl.pallas_call(
        functools.partial(_proj_kernel, tm=tm, d_model=D),
        grid=grid,
        in_specs=[pl.BlockSpec((None, tm, D), lambda b, i: (b + b0, i, 0)),
                  pl.BlockSpec((D, in_width), lambda b, i: (0, 0), pipeline_mode=pl.Buffered(1)),
                  pl.BlockSpec((1, in_width), lambda b, i: (0, 0))],
        out_specs=out_specs,
        out_shape=out_shape,
        scratch_shapes=[pltpu.VMEM((tm, D), BF16),
                        pltpu.VMEM((3 * sum(dil > 1 for _, dil in ATTN_CONFIGS) * HALVES, tm, LANES), F32)],
        compiler_params=pltpu.CompilerParams(dimension_semantics=("parallel", "parallel"),
                                             vmem_limit_bytes=VMEM_LIMIT_BYTES),
        name="proj",
    )(x, w_in.astype(BF16), b_in.reshape(1, in_width))
    qkv = [o.reshape(B, S, GROUP_WIDTH) for o in outs[:9]]
    return qkv, outs[9], outs[10], outs[11]


def _t5_causal_bucket(dist):
    max_exact = N_REL_BUCKETS // 2
    is_small = dist < max_exact
    d = jnp.maximum(dist, 1).astype(F32)
    large = max_exact + (jnp.log(d / max_exact) / math.log(REL_MAX_DISTANCE / max_exact)
                         * (N_REL_BUCKETS - max_exact)).astype(jnp.int32)
    large = jnp.minimum(large, N_REL_BUCKETS - 1)
    return jnp.where(is_small, dist, large)


def _attn_bias(rel_bias_table):
    full, first = [], []
    for g, (window, dil) in enumerate(ATTN_CONFIGS):
        span = window // dil
        table = rel_bias_table[:, g * HEADS_PER_GROUP:(g + 1) * HEADS_PER_GROUP].astype(F32)
        lq = np.arange(BLOCK)
        for lk, dst in ((np.arange(-BLOCK, BLOCK), full), (lq, first)):
            step = jnp.asarray(lq[:, None] - lk[None, :], jnp.int32)
            in_window = (step >= 0) & (step <= span)
            bucket = _t5_causal_bucket(jnp.clip(step, 0, span) * dil)
            bias = jnp.einsum('qkb,bh->hqk', jax.nn.one_hot(bucket, N_REL_BUCKETS, dtype=F32), table,
                              precision=lax.Precision.HIGHEST)
            bias = jnp.where(in_window[None], bias, NEG_INF)
            dst.append(bias.reshape(HEADS_PER_GROUP * BLOCK, lk.shape[0]))
    return jnp.stack(full), jnp.stack(first)


def _run_skewed(chains, n_stages):
    for step in range(n_stages + len(chains) - 1):
        for lag, chain in enumerate(chains):
            if 0 <= step - lag < n_stages:
                next(chain)


ATTN_STAGES = 4


def _attn_kernel(*refs, seq, cast_per_step):
    qkv = refs[:9]
    if cast_per_step:
        (bias_ref, bias_first_ref, w32_g, w32_u, w32_d, out_ref, w16_g, w16_u, w16_d,
         o_scr, l_scr, in_g, in_u, in_d, cast_g, cast_u, cast_d, in_sems, out_sems) = refs[9:]
        w32 = (w32_g, w32_u, w32_d)
        w16 = (w16_g, w16_u, w16_d)
        stage_in = (in_g, in_u, in_d)
        stage_out = (cast_g, cast_u, cast_d)
        first_expert = pl.program_id(0) * cast_per_step

        def load(k):
            return [pltpu.make_async_copy(w32[a].at[first_expert + k], stage_in[a], in_sems.at[a])
                    for a in range(3)]

        def store(k):
            return [pltpu.make_async_copy(stage_out[a], w16[a].at[first_expert + k], out_sems.at[a])
                    for a in range(3)]

        def cast_boundary(k):
            if k > cast_per_step:
                return
            if k > 0:
                for copy in load(k - 1):
                    copy.wait()
                if k > 1:
                    for copy in store(k - 2):
                        copy.wait()
                for a in range(3):
                    stage_out[a][...] = stage_in[a][...].astype(BF16)
                for copy in store(k - 1):
                    copy.start()
            if k < cast_per_step:
                for copy in load(k):
                    copy.start()
            else:
                for copy in store(k - 1):
                    copy.wait()
    else:
        bias_ref, bias_first_ref, out_ref, o_scr, l_scr = refs[9:]

        def cast_boundary(k):
            del k

    rows = HEADS_PER_GROUP * BLOCK
    row_head = lax.broadcasted_iota(jnp.int32, (rows, GROUP_WIDTH), 0) // BLOCK
    lane_head_r = lax.broadcasted_iota(jnp.int32, (rows, GROUP_WIDTH), 1) // HEAD_DIM
    head_mask = row_head == lane_head_r
    heads_per_half = LANES // HEAD_DIM
    lane_head = lax.broadcasted_iota(jnp.int32, (BLOCK, LANES), 1) // HEAD_DIM

    def heads_to_lanes(per_head):
        out = per_head[0]
        for h in range(1, heads_per_half):
            out = jnp.where(lane_head == h, per_head[h], out)
        return out

    for g, (_, dil) in enumerate(ATTN_CONFIGS):
        cast_boundary(g)
        q_ref, k_ref, v_ref = qkv[3 * g:3 * g + 3]
        sub_len = seq // dil
        n_blocks = sub_len // BLOCK

        def block(r, n, first, g=g, dil=dil, q_ref=q_ref, k_ref=k_ref, v_ref=v_ref, sub_len=sub_len):
            base = pl.multiple_of(r * sub_len + n * BLOCK, BLOCK)
            qb = q_ref[pl.ds(base, BLOCK), :]
            if first:
                kk = k_ref[pl.ds(base, BLOCK), :]
                vv = v_ref[pl.ds(base, BLOCK), :]
                bias = bias_first_ref[g]
            else:
                kbase = pl.multiple_of(base - BLOCK, BLOCK)
                kk = k_ref[pl.ds(kbase, 2 * BLOCK), :]
                vv = v_ref[pl.ds(kbase, 2 * BLOCK), :]
                bias = bias_ref[g]
            qs = jnp.where(head_mask, jnp.concatenate([qb] * HEADS_PER_GROUP, axis=0), jnp.zeros((), BF16))
            logits = lax.dot_general(qs, kk, (((1,), (1,)), ((), ())), preferred_element_type=F32) + bias
            yield
            m = jnp.max(logits, axis=1, keepdims=True)
            p = jnp.exp(logits - m)
            s = jnp.sum(p, axis=1, keepdims=True)
            p = p.astype(BF16)
            yield
            pv = jnp.dot(p, vv, preferred_element_type=F32)
            yield
            inv_s = 1.0 / s
            lse = m + jnp.log(s)
            start = n * (BLOCK * dil) + r
            if dil == 1:
                dst = pl.ds(pl.multiple_of(start, BLOCK), BLOCK)
            else:
                dst = pl.ds(start, BLOCK, stride=dil)
            for half in range(HALVES):
                cols = slice(half * LANES, (half + 1) * LANES)
                o_heads, lse_heads = [], []
                for h in range(half * heads_per_half, (half + 1) * heads_per_half):
                    head_rows = slice(h * BLOCK, (h + 1) * BLOCK)
                    o_heads.append(pv[head_rows, cols] * inv_s[head_rows])
                    lse_heads.append(jnp.broadcast_to(lse[head_rows], (BLOCK, LANES)))
                o_scr[g * HALVES + half, dst, :] = heads_to_lanes(o_heads)
                l_scr[g * HALVES + half, dst, :] = heads_to_lanes(lse_heads)
            yield

        if n_blocks > 1:
            group = 3 if (n_blocks - 1) % 3 == 0 else 1

            _run_skewed([block(r, 0, True) for r in range(dil)], ATTN_STAGES)

            def per_subsequence(r, carry, block=block, n_blocks=n_blocks, group=group):
                def per_group(i, c):
                    _run_skewed([block(r, 1 + i * group + k, False) for k in range(group)], ATTN_STAGES)
                    return c
                lax.fori_loop(0, (n_blocks - 1) // group, per_group, 0)
                return carry
            lax.fori_loop(0, dil, per_subsequence, 0)
        else:
            group = next(c for c in (8, 4, 2, 1) if dil % c == 0)

            def per_group(i, carry, block=block, group=group):
                _run_skewed([block(i * group + k, 0, True) for k in range(group)], ATTN_STAGES)
                return carry
            lax.fori_loop(0, dil // group, per_group, 0)

    cast_boundary(N_GROUPS)
    chunk = 256

    def merge(i, carry):
        sl = pl.ds(pl.multiple_of(i * chunk, chunk), chunk)
        for half in range(HALVES):
            ls = [l_scr[g * HALVES + half, sl, :] for g in range(N_GROUPS)]
            m = functools.reduce(jnp.maximum, ls)
            es = [jnp.exp(l - m) for l in ls]
            den = functools.reduce(lambda a, b: a + b, es)
            num = functools.reduce(lambda a, b: a + b,
                                   [e * o_scr[g * HALVES + half, sl, :] for g, e in enumerate(es)])
            out_ref[sl, half * LANES:(half + 1) * LANES] = (num / den).astype(BF16)
        return carry

    lax.fori_loop(0, seq // chunk, merge, 0)
    cast_boundary(N_GROUPS + 1)


def _attn_call(qkv, rel_bias_table, expert_weights=None):
    B, S, _ = qkv[0].shape
    for (_, dil) in ATTN_CONFIGS:
        assert S % (dil * BLOCK) == 0
    bias, bias_first = _attn_bias(rel_bias_table)
    seq_spec = pl.BlockSpec((None, S, GROUP_WIDTH), lambda b: (b, 0, 0))
    ordered = []
    for g in range(N_GROUPS):
        ordered += [qkv[g], qkv[3 + g], qkv[6 + g]]
    operands = ordered + [bias, bias_first]
    in_specs = [seq_spec] * 9 + [pl.BlockSpec(bias.shape, lambda b: (0, 0, 0)),
                                 pl.BlockSpec(bias_first.shape, lambda b: (0, 0, 0))]
    out_specs = [seq_spec]
    out_shape = [jax.ShapeDtypeStruct((B, S, GROUP_WIDTH), BF16)]
    scratch_shapes = [pltpu.VMEM((N_GROUPS * HALVES, S, LANES), F32),
                      pltpu.VMEM((N_GROUPS * HALVES, S, LANES), F32)]
    cast_per_step = 0
    if expert_weights is not None:
        n_experts = expert_weights[0].shape[0]
        assert n_experts % B == 0 and n_experts // B <= N_GROUPS + 1
        cast_per_step = n_experts // B
        operands += list(expert_weights)
        in_specs += [pl.BlockSpec(memory_space=pl.ANY)] * 3
        out_specs += [pl.BlockSpec(memory_space=pl.ANY)] * 3
        out_shape += [jax.ShapeDtypeStruct(w.shape, BF16) for w in expert_weights]
        scratch_shapes += ([pltpu.VMEM(w.shape[1:], F32) for w in expert_weights]
                           + [pltpu.VMEM(w.shape[1:], BF16) for w in expert_weights]
                           + [pltpu.SemaphoreType.DMA((3,)), pltpu.SemaphoreType.DMA((3,))])
    outs = pl.pallas_call(
        functools.partial(_attn_kernel, seq=S, cast_per_step=cast_per_step),
        grid=(B,),
        in_specs=in_specs,
        out_specs=out_specs,
        out_shape=out_shape,
        scratch_shapes=scratch_shapes,
        compiler_params=pltpu.CompilerParams(dimension_semantics=("arbitrary",),
                                             vmem_limit_bytes=VMEM_LIMIT_BYTES),
        name="attn",
    )(*operands)
    return outs[0], (tuple(outs[1:]) if expert_weights is not None else None)


PACK_CHUNKS = 4
SUBLANES = 8
ROUTER_LANES = LANES // 2


def _pack_bf16_pairs(rounded):
    w = rounded.shape[1] // 2
    bits = lax.bitcast_convert_type(rounded, jnp.int32)
    return bits[:, :w] | lax.shift_right_logical(bits[:, w:], jnp.full((), 16, jnp.int32))


def _unpack_bf16_pairs(words):
    hi = lax.bitcast_convert_type(words & jnp.int32(-65536), F32).astype(BF16)
    lo = lax.bitcast_convert_type(lax.shift_left(words, jnp.full((), 16, jnp.int32)), F32).astype(BF16)
    return jnp.concatenate([hi, lo], axis=1)


def _fold_pool_kernel(wp_ref, scale_ref, pb_ref, out_ref):
    a = wp_ref[...] * scale_ref[...]
    b = pb_ref[...]
    a_hi = a.astype(BF16)
    a_lo = (a - a_hi.astype(F32)).astype(BF16)
    b_hi = b.astype(BF16)
    b_lo = (b - b_hi.astype(F32)).astype(BF16)
    out_ref[...] = (jnp.dot(a_hi, b_hi, preferred_element_type=F32)
                    + jnp.dot(a_lo, b_hi, preferred_element_type=F32)
                    + jnp.dot(a_hi, b_lo, preferred_element_type=F32)).astype(BF16)


def _fold_pool_call(w_pool, pool_scale, w_proj_pool):
    n_groups, gd, _ = w_pool.shape
    D = w_proj_pool.shape[1]
    return pl.pallas_call(
        _fold_pool_kernel,
        grid=(n_groups,),
        in_specs=[pl.BlockSpec((None, gd, gd), lambda g: (g, 0, 0)),
                  pl.BlockSpec((None, 1, gd), lambda g: (g, 0, 0)),
                  pl.BlockSpec((gd, D), lambda g: (g, 0))],
        out_specs=pl.BlockSpec((gd, D), lambda g: (g, 0)),
        out_shape=jax.ShapeDtypeStruct((n_groups * gd, D), BF16),
        name="fold_pool",
    )(w_pool, pool_scale.reshape(n_groups, 1, gd), w_proj_pool)


def _post_kernel(a_ref, u_ref, halo_ref, ga_ref, gb_ref, x_ref,
                 pa_ref, pb_ref, wout_ref, g1_ref, b1_ref,
                 wr_cat_ref, br_ref,
                 h_ref, hb_ref, ids_ref, wts_ref, cnt_ref, pool_scr, tmp_scr, *, tm, sub, alpha):
    i = pl.program_id(1)
    halo = halo_ref[...]
    pool_scr[0:POOL_HALO, :] = jnp.zeros_like(halo)
    pool_scr[POOL_HALO:2 * POOL_HALO, :] = jnp.where(i > 0, halo, jnp.zeros_like(halo))
    pool_scr[2 * POOL_HALO:, :] = u_ref[...]
    tmp_scr[:, 0:POOL_HALO, :] = jnp.zeros((tmp_scr.shape[0], POOL_HALO, POOL_GROUP_DIM), F32)
    head_pos = i * tm + lax.broadcasted_iota(jnp.int32, (POOL_HALO, POOL_GROUP_DIM), 0)

    def rows_chain(r0):
        rs = pl.ds(r0, sub)
        group_cols = [slice(gi * POOL_GROUP_DIM, (gi + 1) * POOL_GROUP_DIM) for gi in range(len(POOL_SIZES))]
        diffs = []
        ext = sub + 2 * POOL_HALO
        for gi, (cols, w) in enumerate(zip(group_cols, POOL_SIZES)):
            ug = u_ref[rs, cols]
            levels = w.bit_length() - 1
            for level in range(levels):
                shift = 1 << level
                dst = tmp_scr.at[(r0 // sub * len(POOL_SIZES) + gi) * 2 + level % 2]
                if level == 0:
                    cur = pool_scr[pl.ds(r0 + POOL_HALO, ext - POOL_HALO), cols]
                    back = pool_scr[pl.ds(r0 + POOL_HALO - shift, ext - POOL_HALO), cols]
                else:
                    src = tmp_scr.at[(r0 // sub * len(POOL_SIZES) + gi) * 2 + (level - 1) % 2]
                    cur = src[pl.ds(POOL_HALO, ext - POOL_HALO), :]
                    back = src[pl.ds(POOL_HALO - shift, ext - POOL_HALO), :]
                if level == levels - 1:
                    acc = (cur + back)[POOL_HALO:]
                else:
                    dst[pl.ds(POOL_HALO, ext - POOL_HALO), :] = cur + back
            inv_count = jnp.full((sub, POOL_GROUP_DIM), 1.0 / w, F32)
            if r0 == 0:
                inv_count = jnp.concatenate([1.0 / jnp.minimum(head_pos + 1, w).astype(F32),
                                             inv_count[POOL_HALO:]], axis=0)
            diffs.append((acc * inv_count - ug).astype(BF16))
        yield
        y_pool = jnp.dot(jnp.concatenate(diffs, axis=1), pb_ref[...], preferred_element_type=F32)
        y_attn = jnp.dot(a_ref[rs, :], pa_ref[...], preferred_element_type=F32)
        yield
        mixed = ga_ref[rs, :] * y_attn.astype(BF16) + gb_ref[rs, :] * y_pool.astype(BF16)
        yield
        y = jnp.dot(mixed, wout_ref[...], preferred_element_type=F32)
        yield
        h = _layer_norm(alpha * x_ref[rs, :] + y, g1_ref[...], b1_ref[...])
        h_ref[rs, :] = h
        h_hi = h.astype(BF16)
        h_rounded = h_hi.astype(F32)
        packed = _pack_bf16_pairs(h_rounded)
        for c in range(PACK_CHUNKS):
            hb_ref[c, rs, :] = packed[:, c * LANES:(c + 1) * LANES]

        h_lo = (h - h_rounded).astype(BF16)
        yield
        nt = (((1,), (1,)), ((), ()))
        both = lax.dot_general(wr_cat_ref[...], h_hi, nt, preferred_element_type=F32)
        lo_hi = lax.dot_general(wr_cat_ref[0:ROUTER_LANES, :], h_lo, nt, preferred_element_type=F32)
        logits = both[0:ROUTER_LANES] + both[ROUTER_LANES:] + lo_hi + br_ref[...]
        yield
        row = lax.broadcasted_iota(jnp.int32, logits.shape, 0)
        big = jnp.int32(2 ** 30)
        is_group = row < N_EXPERT_GROUPS
        gl = jnp.where(is_group, logits, -jnp.inf)
        gmax = jnp.max(gl, axis=0, keepdims=True)
        g_idx = jnp.min(jnp.where(gl == gmax, row, big), axis=0, keepdims=True)
        g_prob = 1.0 / jnp.sum(jnp.exp(gl - gmax), axis=0, keepdims=True)
        expert = row - N_EXPERT_GROUPS
        in_group = (expert >= g_idx * EXPERTS_PER_GROUP) & (expert < (g_idx + 1) * EXPERTS_PER_GROUP)
        el = jnp.where(in_group, logits, -jnp.inf)
        v1 = jnp.max(el, axis=0, keepdims=True)
        i1 = jnp.min(jnp.where(el == v1, expert, big), axis=0, keepdims=True)
        el2 = jnp.where(expert == i1, -jnp.inf, el)
        v2 = jnp.max(el2, axis=0, keepdims=True)
        i2 = jnp.min(jnp.where(el2 == v2, expert, big), axis=0, keepdims=True)
        e2 = jnp.exp(v2 - v1)
        den = 1.0 + e2
        first_row = lax.broadcasted_iota(jnp.int32, (SUBLANES, sub), 0) == 0
        wts_ref[:, rs] = jnp.where(first_row, 1.0 / den * g_prob, e2 / den * g_prob)
        ids_ref[:, rs] = jnp.where(first_row, i1, i2).astype(F32)
        chosen = jnp.where((expert == i1) | (expert == i2), 1.0, 0.0)
        per_expert = jnp.broadcast_to(jnp.sum(chosen, axis=1, keepdims=True), (ROUTER_LANES, LANES))
        to_lane = (lax.broadcasted_iota(jnp.int32, (ROUTER_LANES, LANES), 0) - N_EXPERT_GROUPS
                   == lax.broadcasted_iota(jnp.int32, (ROUTER_LANES, LANES), 1))
        counts.append(jnp.sum(jnp.where(to_lane, per_expert, 0.0), axis=0, keepdims=True))
        yield

    counts = []
    _run_skewed([rows_chain(r0) for r0 in range(0, tm, sub)], 7)
    cnt_ref[...] = jnp.broadcast_to(functools.reduce(lambda a, b: a + b, counts), cnt_ref.shape)


def _post_call(a, u, ga, gb, x, w_proj_attn, w_pool, pool_scale, w_proj_pool, w_out, gamma, beta,
               w_router_group, b_router_group, w_router_expert, b_router_expert, alpha, b0, tm=1024, n_sub=2):
    B = a.shape[0]
    _, S, D = x.shape
    assert S % tm == 0 and tm % POOL_HALO == 0
    n_logits = N_EXPERT_GROUPS + N_EXPERTS
    assert n_logits <= ROUTER_LANES
    wr = jnp.concatenate([w_router_group, w_router_expert], axis=1).T
    wr = jnp.pad(wr, ((0, ROUTER_LANES - n_logits), (0, 0)))
    wr_hi = wr.astype(BF16)
    wr_lo = (wr - wr_hi.astype(F32)).astype(BF16)
    wr_cat = jnp.concatenate([wr_hi, wr_lo], axis=0)
    br = jnp.pad(jnp.concatenate([b_router_group, b_router_expert]),
                 (0, ROUTER_LANES - n_logits)).reshape(ROUTER_LANES, 1)
    n_tiles = B * (S // tm)
    row_spec = lambda width: pl.BlockSpec((None, tm, width), lambda b, i: (b, i, 0))
    full = lambda arr: pl.BlockSpec(arr.shape, lambda b, i: (0,) * arr.ndim)
    halo_blocks = tm // POOL_HALO
    halo_spec = pl.BlockSpec((None, POOL_HALO, POOL_WIDTH),
                             lambda b, i: (b, jnp.maximum(i * halo_blocks - 1, 0), 0))
    weights = [w_proj_attn.astype(BF16), _fold_pool_call(w_pool, pool_scale, w_proj_pool),
               w_out.astype(BF16), gamma.reshape(1, D), beta.reshape(1, D), wr_cat, br]
    return pl.pallas_call(
        functools.partial(_post_kernel, tm=tm, sub=tm // n_sub, alpha=alpha),
        grid=(B, S // tm),
        in_specs=[row_spec(GROUP_WIDTH), row_spec(POOL_WIDTH), halo_spec, row_spec(D), row_spec(D),
                  pl.BlockSpec((None, tm, D), lambda b, i: (b + b0, i, 0))] + [full(w) for w in weights],
        out_specs=[row_spec(D),
                   pl.BlockSpec((PACK_CHUNKS, tm, LANES), lambda b, i: (0, b * (S // tm) + i, 0)),
                   pl.BlockSpec((SUBLANES, tm), lambda b, i: (0, b * (S // tm) + i)),
                   pl.BlockSpec((SUBLANES, tm), lambda b, i: (0, b * (S // tm) + i)),
                   pl.BlockSpec((SUBLANES, LANES), lambda b, i: (b * (S // tm) + i, 0))],
        out_shape=[jax.ShapeDtypeStruct((B, S, D), F32),
                   jax.ShapeDtypeStruct((PACK_CHUNKS, B * S, LANES), jnp.int32),
                   jax.ShapeDtypeStruct((SUBLANES, B * S), F32), jax.ShapeDtypeStruct((SUBLANES, B * S), F32),
                   jax.ShapeDtypeStruct((n_tiles * SUBLANES, LANES), F32)],
        scratch_shapes=[pltpu.VMEM((tm + 2 * POOL_HALO, POOL_WIDTH), F32),
                        pltpu.VMEM((n_sub * len(POOL_SIZES) * 2, tm // n_sub + 2 * POOL_HALO, POOL_GROUP_DIM), F32)],
        compiler_params=pltpu.CompilerParams(dimension_semantics=("parallel", "parallel"),
                                             vmem_limit_bytes=VMEM_LIMIT_BYTES),
        name="post",
    )(a, u, u, ga, gb, x, *weights)


def _rank_kernel(ids_ref, cnt_ref, pos_ref, ends_ref, run_ref, start_ref, earlier_ref, *, tm, tile_rows):
    i = pl.program_id(0)

    @pl.when(i == 0)
    def _():
        total = jnp.sum(cnt_ref[...], axis=0, keepdims=True) / SUBLANES
        padded = jnp.broadcast_to(jnp.ceil(total / tile_rows) * tile_rows, ends_ref.shape)
        lane1 = lax.broadcasted_iota(jnp.int32, padded.shape, 1)
        incl = padded
        shift = 1
        while shift < LANES:
            incl = incl + jnp.where(lane1 >= shift, pltpu.roll(incl, shift, axis=1), 0.0)
            shift *= 2
        ends_ref[...] = incl
        start_row = jnp.broadcast_to((incl - padded)[0:1, :], start_ref.shape)
        on_diagonal = (lax.broadcasted_iota(jnp.int32, start_ref.shape, 0)
                       == lax.broadcasted_iota(jnp.int32, start_ref.shape, 1))
        start_col = jnp.sum(jnp.where(on_diagonal, start_row, 0.0), axis=1, keepdims=True)
        start_ref[...] = jnp.broadcast_to(start_col, start_ref.shape)
        run_ref[...] = jnp.zeros_like(run_ref)
        row = lax.broadcasted_iota(jnp.int32, (tm, tm), 0)
        col = lax.broadcasted_iota(jnp.int32, (tm, tm), 1)
        earlier_ref[...] = jnp.where(row < col, 1.0, 0.0).astype(BF16)

    ids = ids_ref[...]
    expert = lax.broadcasted_iota(jnp.int32, (N_EXPERTS, tm), 0).astype(F32)
    oh0 = expert == ids[0:1, :]
    oh1 = expert == ids[1:2, :]
    onehot = jnp.where(oh0 | oh1, 1.0, 0.0)
    before = jnp.dot(onehot.astype(BF16), earlier_ref[...], preferred_element_type=F32)
    slot = start_ref[:, 0:1] + run_ref[:, 0:1] + before
    p0 = jnp.sum(jnp.where(oh0, slot, 0.0), axis=0, keepdims=True)
    p1 = jnp.sum(jnp.where(oh1, slot, 0.0), axis=0, keepdims=True)
    first_row = lax.broadcasted_iota(jnp.int32, pos_ref.shape, 0) == 0
    pos_ref[...] = jnp.where(first_row, p0, p1).astype(jnp.int32)
    run_ref[...] += jnp.broadcast_to(jnp.sum(onehot, axis=1, keepdims=True), run_ref.shape)


def _rank_call(ids_t, counts, tile_rows, tm=1024):
    N = ids_t.shape[1]
    assert N % tm == 0 and SUBLANES * (2 * N + N_EXPERTS * tile_rows) < 2 ** 24
    assert N_EXPERTS <= LANES
    return pl.pallas_call(
        functools.partial(_rank_kernel, tm=tm, tile_rows=tile_rows),
        grid=(N // tm,),
        in_specs=[pl.BlockSpec((SUBLANES, tm), lambda i: (0, i)),
                  pl.BlockSpec(counts.shape, lambda i: (0, 0))],
        out_specs=[pl.BlockSpec((SUBLANES, tm), lambda i: (0, i)),
                   pl.BlockSpec((SUBLANES, LANES), lambda i: (0, 0))],
        out_shape=[jax.ShapeDtypeStruct((SUBLANES, N), jnp.int32),
                   jax.ShapeDtypeStruct((SUBLANES, LANES), F32)],
        scratch_shapes=[pltpu.VMEM((N_EXPERTS, LANES), F32), pltpu.VMEM((N_EXPERTS, LANES), F32),
                        pltpu.VMEM((tm, tm), BF16)],
        compiler_params=pltpu.CompilerParams(dimension_semantics=("arbitrary",),
                                             vmem_limit_bytes=VMEM_LIMIT_BYTES),
        name="rank",
    )(ids_t, counts)


SC_CORES = 2
SC_SUBCORES = 16
SC_WORKERS = SC_CORES * SC_SUBCORES
SC_CHUNK = 128


def _sc_mesh():
    return plsc.VectorSubcoreMesh(core_axis_name="c", subcore_axis_name="s",
                                  num_cores=SC_CORES, num_subcores=SC_SUBCORES)


def _sc_dispatch(packed, pos_t, n_rows):
    n_chunks, n_tok, width = packed.shape
    per_worker = n_tok // SC_WORKERS
    assert n_tok % (SC_WORKERS * SC_CHUNK) == 0

    @functools.partial(
        pl.kernel, mesh=_sc_mesh(),
        out_type=jax.ShapeDtypeStruct((n_chunks, n_rows, width), packed.dtype),
        scratch_types=[pltpu.VMEM((SC_CHUNK,), jnp.int32), pltpu.VMEM((SC_CHUNK,), jnp.int32),
                       pltpu.VMEM((n_chunks, SC_CHUNK, width), packed.dtype)]
                      + [pltpu.SemaphoreType.DMA] * (3 * n_chunks),
        name="sc_dispatch")
    def run(packed_hbm, pos_hbm, out_hbm, idx0, idx1, bufs, *sems):
        load_sems, sems0, sems1 = sems[:n_chunks], sems[n_chunks:2 * n_chunks], sems[2 * n_chunks:]
        worker = lax.axis_index("s") * SC_CORES + lax.axis_index("c")

        @pl.loop(0, per_worker // SC_CHUNK)
        def _(j):
            base = worker * per_worker + j * SC_CHUNK
            loads = [pltpu.async_copy(packed_hbm.at[c, pl.ds(base, SC_CHUNK)], bufs.at[c], load_sems[c])
                     for c in range(n_chunks)]
            pltpu.sync_copy(pos_hbm.at[0, pl.ds(base, SC_CHUNK)], idx0)
            pltpu.sync_copy(pos_hbm.at[1, pl.ds(base, SC_CHUNK)], idx1)
            scatters = []
            for c in range(n_chunks):
                loads[c].wait()
                scatters.append(pltpu.async_copy(bufs.at[c], out_hbm.at[c].at[idx0], sems0[c]))
                scatters.append(pltpu.async_copy(bufs.at[c], out_hbm.at[c].at[idx1], sems1[c]))
            for s in scatters:
                s.wait()

    return run(packed, pos_t)


def _sc_combine(sorted_rows, pos_t):
    n_chunks, _, width = sorted_rows.shape
    n_tok = pos_t.shape[1]
    per_worker = n_tok // SC_WORKERS
    assert n_tok % (SC_WORKERS * SC_CHUNK) == 0

    @functools.partial(
        pl.kernel, mesh=_sc_mesh(),
        out_type=jax.ShapeDtypeStruct((2, n_chunks, n_tok, width), sorted_rows.dtype),
        scratch_types=[pltpu.VMEM((SC_CHUNK,), jnp.int32),
                       pltpu.VMEM((n_chunks, SC_CHUNK, width), sorted_rows.dtype)]
                      + [pltpu.SemaphoreType.DMA] * (2 * n_chunks),
        name="sc_combine")
    def run(rows_hbm, pos_hbm, out_hbm, idx, bufs, *sems):
        gather_sems, write_sems = sems[:n_chunks], sems[n_chunks:]
        worker = lax.axis_index("s") * SC_CORES + lax.axis_index("c")

        @pl.loop(0, per_worker // SC_CHUNK)
        def _(j):
            base = worker * per_worker + j * SC_CHUNK
            for k in range(2):
                pltpu.sync_copy(pos_hbm.at[k, pl.ds(base, SC_CHUNK)], idx)
                gathers = [pltpu.async_copy(rows_hbm.at[c].at[idx], bufs.at[c], gather_sems[c])
                           for c in range(n_chunks)]
                writes = []
                for c in range(n_chunks):
                    gathers[c].wait()
                    writes.append(pltpu.async_copy(bufs.at[c], out_hbm.at[k, c, pl.ds(base, SC_CHUNK)],
                                                   write_sems[c]))
                for w in writes:
                    w.wait()

    return run(sorted_rows, pos_t)


def _expert_kernel(tile_expert_ref, n_used_ref, xs_ref, wg_ref, wu_ref, wd_ref, ys_ref):
    del tile_expert_ref

    @pl.when(pl.program_id(0) < n_used_ref[0])
    def _():
        wg = wg_ref[...].astype(BF16)
        wu = wu_ref[...].astype(BF16)
        wd = wd_ref[...].astype(BF16)
        tile_rows = xs_ref.shape[1]
        sub = tile_rows // 2

        def rows_chain(r0):
            rs = pl.ds(r0, sub)
            x = _unpack_bf16_pairs(jnp.concatenate([xs_ref[c, rs, :] for c in range(PACK_CHUNKS)], axis=1))
            yield
            gate = jnp.dot(x, wg, preferred_element_type=F32)
            up = jnp.dot(x, wu, preferred_element_type=F32)
            yield
            hidden = (jax.nn.silu(gate) * up).astype(BF16)
            yield
            y = jnp.dot(hidden, wd, preferred_element_type=F32)
            yield
            y = _pack_bf16_pairs(y.astype(BF16).astype(F32))
            for c in range(PACK_CHUNKS):
                ys_ref[c, rs, :] = y[:, c * LANES:(c + 1) * LANES]
            yield

        chains = [rows_chain(r0) for r0 in range(0, tile_rows, sub)]
        n_stages = 5
        for step in range(n_stages + len(chains) - 1):
            for lag, chain in enumerate(chains):
                if 0 <= step - lag < n_stages:
                    next(chain)


def _expert_call(xs, tile_expert, n_used, w_gate, w_up, w_down, tile_rows):
    _, n_rows, _ = xs.shape
    D = w_gate.shape[1]
    row_block = pl.BlockSpec((PACK_CHUNKS, tile_rows, LANES),
                             lambda i, te, nu: (0, jnp.minimum(i, nu[0] - 1), 0))
    return pl.pallas_call(
        _expert_kernel,
        grid_spec=pltpu.PrefetchScalarGridSpec(
            num_scalar_prefetch=2,
            grid=(n_rows // tile_rows,),
            in_specs=[row_block,
                      pl.BlockSpec((None, D, D_EXPERT), lambda i, te, nu: (te[i], 0, 0)),
                      pl.BlockSpec((None, D, D_EXPERT), lambda i, te, nu: (te[i], 0, 0)),
                      pl.BlockSpec((None, D_EXPERT, D), lambda i, te, nu: (te[i], 0, 0))],
            out_specs=row_block),
        out_shape=jax.ShapeDtypeStruct(xs.shape, xs.dtype),
        compiler_params=pltpu.CompilerParams(dimension_semantics=("arbitrary",),
                                             vmem_limit_bytes=VMEM_LIMIT_BYTES),
        name="experts",
    )(tile_expert, n_used, xs, w_gate, w_up, w_down)


def _final_kernel(h_ref, y_ref, wts_ref, g2_ref, b2_ref, *rest, alpha):
    out_ref = rest[-1]
    tm = h_ref.shape[0]
    on_diagonal = (lax.broadcasted_iota(jnp.int32, (tm, tm), 0)
                   == lax.broadcasted_iota(jnp.int32, (tm, tm), 1))
    z = alpha * h_ref[...]
    for k in range(2):
        w_col = jnp.sum(jnp.where(on_diagonal, jnp.broadcast_to(wts_ref[k:k + 1, :], (tm, tm)), 0.0),
                        axis=1, keepdims=True)
        yk = _unpack_bf16_pairs(jnp.concatenate([y_ref[k, c] for c in range(PACK_CHUNKS)], axis=1))
        z = z + w_col * yk.astype(F32)
    out_ref[...] = _layer_norm(z, g2_ref[...], b2_ref[...])


def _final_call(h, y, wts, gamma, beta, alpha, row0, n_total, earlier_out, tm=1024):
    N, D = h.shape
    assert N % tm == 0 and row0 % tm == 0
    operands = [h, y, wts, gamma.reshape(1, D), beta.reshape(1, D)]
    in_specs = [pl.BlockSpec((tm, D), lambda i: (i, 0)),
                pl.BlockSpec((2, PACK_CHUNKS, tm, LANES), lambda i: (0, 0, i, 0)),
                pl.BlockSpec((SUBLANES, tm), lambda i: (0, i)),
                pl.BlockSpec((1, D), lambda i: (0, 0)),
                pl.BlockSpec((1, D), lambda i: (0, 0))]
    aliases = {}
    if earlier_out is not None:
        aliases = {len(operands): 0}
        operands.append(earlier_out)
        in_specs.append(pl.BlockSpec(memory_space=pl.ANY))
    return pl.pallas_call(
        functools.partial(_final_kernel, alpha=alpha),
        grid=(N // tm,),
        in_specs=in_specs,
        out_specs=pl.BlockSpec((tm, D), lambda i: (i + row0 // tm, 0)),
        out_shape=jax.ShapeDtypeStruct((n_total, D), F32),
        input_output_aliases=aliases,
        compiler_params=pltpu.CompilerParams(dimension_semantics=("parallel",),
                                             vmem_limit_bytes=VMEM_LIMIT_BYTES),
        name="final",
    )(*operands)


def _moe(packed, ids_t, counts, w_gate, w_up, w_down, tile_rows=512):
    N = ids_t.shape[1]
    pos_t, ends = _rank_call(ids_t, counts, tile_rows)
    n_tiles = 2 * N // tile_rows + N_EXPERTS
    seg_end = ends[0, :N_EXPERTS].astype(jnp.int32)
    tile_start = jnp.arange(n_tiles, dtype=jnp.int32) * tile_rows
    tile_expert = jnp.minimum(jnp.sum(seg_end[None, :] <= tile_start[:, None], axis=1),
                              N_EXPERTS - 1).astype(jnp.int32)
    n_used = (seg_end[N_EXPERTS - 1:] // tile_rows).astype(jnp.int32)
    xs = _sc_dispatch(packed, pos_t, n_tiles * tile_rows)
    ys = _expert_call(xs, tile_expert, n_used, w_gate, w_up, w_down, tile_rows)
    return _sc_combine(ys, pos_t)


@jax.jit
def kernel(x, w_in, b_in, rel_bias_table, w_pool, pool_scale, w_proj_attn, w_proj_pool, w_out, ln1_gamma, ln1_beta, w_router_group, b_router_group, w_router_expert, b_router_expert, w_expert_gate, w_expert_up, w_expert_down, ln2_gamma, ln2_beta):
    B, S, D = x.shape
    depth = w_in.shape[0]
    alpha = (2.0 * depth) ** 0.25
    n_parts = 2 if B % 2 == 0 else 1
    nb = B // n_parts
    for layer in range(depth):
        out = None
        experts32 = (w_expert_gate[layer], w_expert_up[layer], w_expert_down[layer])
        n_experts = experts32[0].shape[0]
        cast_in_attn = n_experts % nb == 0 and n_experts // nb <= N_GROUPS + 1
        experts16 = None if cast_in_attn else tuple(w.astype(BF16) for w in experts32)
        for part in range(n_parts):
            b0 = part * nb
            qkv, u, ga, gb = _proj_call(x, w_in[layer], b_in[layer], b0, nb)
            a, converted = _attn_call(qkv, rel_bias_table, experts32 if experts16 is None else None)
            if converted is not None:
                experts16 = converted
            h, packed, ids, wts, counts = _post_call(
                a, u, ga, gb, x, w_proj_attn[layer], w_pool[layer], pool_scale[layer],
                w_proj_pool[layer], w_out[layer], ln1_gamma[layer], ln1_beta[layer],
                w_router_group[layer], b_router_group[layer],
                w_router_expert[layer], b_router_expert[layer], alpha, b0)
            y = _moe(packed, ids, counts, *experts16)
            out = _final_call(h.reshape(nb * S, D), y, wts, ln2_gamma[layer],
                              ln2_beta[layer], alpha, b0 * S, B * S, out)
        x = out.reshape(B, S, D)
    return x
```

```python
import functools
import math

import jax
import jax.numpy as jnp
import numpy as np
from jax import lax
from jax.experimental import pallas as pl
from jax.experimental.pallas import tpu as pltpu
from jax.experimental.pallas import tpu_sc as plsc

F32 = jnp.float32
BF16 = jnp.bfloat16

HEAD_DIM = 64
ATTN_CONFIGS = ((128, 1), (512, 4), (2048, 16))
N_GROUPS = len(ATTN_CONFIGS)
HEADS_PER_GROUP = 4
GROUP_WIDTH = HEADS_PER_GROUP * HEAD_DIM
ATTN_WIDTH = N_GROUPS * GROUP_WIDTH
BLOCK = 128
N_REL_BUCKETS = 32
REL_MAX_DISTANCE = 2048
NEG_INF = -1e30

POOL_SIZES = (2, 4, 8, 16)
POOL_GROUP_DIM = 128
POOL_WIDTH = POOL_GROUP_DIM * len(POOL_SIZES)
POOL_HALO = 16

N_EXPERT_GROUPS = 4
EXPERTS_PER_GROUP = 8
N_EXPERTS = N_EXPERT_GROUPS * EXPERTS_PER_GROUP
D_EXPERT = 256
LN_EPS = 1e-5

VMEM_LIMIT_BYTES = 56 * 1024 * 1024
LANES = 128
HALVES = GROUP_WIDTH // LANES


def _layer_norm(z, gamma, beta):
    mu = jnp.mean(z, axis=-1, keepdims=True)
    zc = z - mu
    var = jnp.mean(zc * zc, axis=-1, keepdims=True)
    return zc * lax.rsqrt(var + LN_EPS) * gamma + beta


def _proj_kernel(x_ref, w_ref, b_ref, *refs, tm, d_model):
    qkv_refs = refs[:9]
    u_ref, ga_ref, gb_ref, xb_ref, acc_ref = refs[9:]
    xb_ref[...] = x_ref[...].astype(BF16)

    def chunk(c0, width):
        acc = jnp.dot(xb_ref[...], w_ref[:, c0:c0 + width], preferred_element_type=F32)
        return acc + b_ref[:, c0:c0 + width]

    pool_off = 3 * ATTN_WIDTH
    ga_off = pool_off + POOL_WIDTH

    def gate_job(gate_ref, off, c):
        def run():
            gate_ref[:, c * 256:(c + 1) * 256] = jax.nn.sigmoid(chunk(off + c * 256, 256)).astype(BF16)
        return run

    def pool_job(c):
        def run():
            u_ref[:, c * 256:(c + 1) * 256] = chunk(pool_off + c * 256, 256)
        return run

    def qkv_job(which, g, staged):
        def run():
            dil = ATTN_CONFIGS[g][1]
            out = qkv_refs[which * 3 + g]
            acc = chunk(which * ATTN_WIDTH + g * GROUP_WIDTH, GROUP_WIDTH)
            if which == 0:
                acc = acc * HEAD_DIM ** -0.5
            if dil == 1:
                out[0] = acc.astype(BF16)
            else:
                planes = [staged * HALVES + half for half in range(HALVES)]
                for half, plane in enumerate(planes):
                    acc_ref[plane] = acc[:, half * LANES:(half + 1) * LANES]
                for r in range(dil):
                    for half, plane in enumerate(planes):
                        out[r, :, half * LANES:(half + 1) * LANES] = (
                            acc_ref[plane, pl.ds(r, tm // dil, stride=dil), :].astype(BF16))
        return run

    gates = [gate_job(ref, off, c) for ref, off in ((ga_ref, ga_off), (gb_ref, ga_off + d_model))
             for c in range(d_model // 256)]
    dilated = [(which, g) for g in range(N_GROUPS) if ATTN_CONFIGS[g][1] > 1 for which in range(3)]
    strided = [qkv_job(which, g, k) for k, (which, g) in enumerate(dilated)]
    plain = [qkv_job(which, g, 0) for g in range(N_GROUPS) if ATTN_CONFIGS[g][1] == 1 for which in range(3)]
    pools = [pool_job(c) for c in range(POOL_WIDTH // 256)]
    order = []
    while gates or strided:
        if gates:
            order.append(gates.pop(0))
        if strided:
            order.append(strided.pop(0))
    for job in order + plain + pools:
        job()


def _proj_call(x, w_in, b_in, b0, B, tm=1024):
    _, S, D = x.shape
    in_width = w_in.shape[1]
    assert in_width == 3 * ATTN_WIDTH + POOL_WIDTH + 2 * D
    assert S % tm == 0
    grid = (B, S // tm)
    qkv_shapes, qkv_specs = [], []
    for _ in range(3):
        for (_, dil) in ATTN_CONFIGS:
            assert tm % (dil * 16) == 0
            qkv_shapes.append(jax.ShapeDtypeStruct((B, dil, S // dil, GROUP_WIDTH), BF16))
            qkv_specs.append(pl.BlockSpec((None, dil, tm // dil, GROUP_WIDTH), lambda b, i: (b, 0, i, 0)))
    row_spec = lambda width: pl.BlockSpec((None, tm, width), lambda b, i: (b, i, 0))
    out_shape = qkv_shapes + [jax.ShapeDtypeStruct((B, S, POOL_WIDTH), F32),
                              jax.ShapeDtypeStruct((B, S, D), BF16),
                              jax.ShapeDtypeStruct((B, S, D), BF16)]
    out_specs = qkv_specs + [row_spec(POOL_WIDTH), row_spec(D), row_spec(D)]
    outs = pl.pallas_call(
        functools.partial(_proj_kernel, tm=tm, d_model=D),
        grid=grid,
        in_specs=[pl.BlockSpec((None, tm, D), lambda b, i: (b + b0, i, 0)),
                  pl.BlockSpec((D, in_width), lambda b, i: (0, 0), pipeline_mode=pl.Buffered(1)),
                  pl.BlockSpec((1, in_width), lambda b, i: (0, 0))],
        out_specs=out_specs,
        out_shape=out_shape,
        scratch_shapes=[pltpu.VMEM((tm, D), BF16),
                        pltpu.VMEM((3 * sum(dil > 1 for _, dil in ATTN_CONFIGS) * HALVES, tm, LANES), F32)],
        compiler_params=pltpu.CompilerParams(dimension_semantics=("parallel", "parallel"),
                                             vmem_limit_bytes=VMEM_LIMIT_BYTES),
        name="proj",
    )(x, w_in.astype(BF16), b_in.reshape(1, in_width))
    qkv = [o.reshape(B, S, GROUP_WIDTH) for o in outs[:9]]
    return qkv, outs[9], outs[10], outs[11]


def _t5_causal_bucket(dist):
    max_exact = N_REL_BUCKETS // 2
    is_small = dist < max_exact
    d = jnp.maximum(dist, 1).astype(F32)
    large = max_exact + (jnp.log(d / max_exact) / math.log(REL_MAX_DISTANCE / max_exact)
                         * (N_REL_BUCKETS - max_exact)).astype(jnp.int32)
    large = jnp.minimum(large, N_REL_BUCKETS - 1)
    return jnp.where(is_small, dist, large)


def _attn_bias(rel_bias_table):
    full, first = [], []
    for g, (window, dil) in enumerate(ATTN_CONFIGS):
        span = window // dil
        table = rel_bias_table[:, g * HEADS_PER_GROUP:(g + 1) * HEADS_PER_GROUP].astype(F32)
        lq = np.arange(BLOCK)
        for lk, dst in ((np.arange(-BLOCK, BLOCK), full), (lq, first)):
            step = jnp.asarray(lq[:, None] - lk[None, :], jnp.int32)
            in_window = (step >= 0) & (step <= span)
            bucket = _t5_causal_bucket(jnp.clip(step, 0, span) * dil)
            bias = jnp.einsum('qkb,bh->hqk', jax.nn.one_hot(bucket, N_REL_BUCKETS, dtype=F32), table,
                              precision=lax.Precision.HIGHEST)
            bias = jnp.where(in_window[None], bias, NEG_INF)
            dst.append(bias.reshape(HEADS_PER_GROUP * BLOCK, lk.shape[0]))
    return jnp.stack(full), jnp.stack(first)


def _run_skewed(chains, n_stages):
    for step in range(n_stages + len(chains) - 1):
        for lag, chain in enumerate(chains):
            if 0 <= step - lag < n_stages:
                next(chain)


ATTN_STAGES = 4


def _attn_kernel(*refs, seq, cast_per_step):
    qkv = refs[:9]
    if cast_per_step:
        (bias_ref, bias_first_ref, w32_g, w32_u, w32_d, out_ref, w16_g, w16_u, w16_d,
         o_scr, l_scr, in_g, in_u, in_d, cast_g, cast_u, cast_d, in_sems, out_sems) = refs[9:]
        w32 = (w32_g, w32_u, w32_d)
        w16 = (w16_g, w16_u, w16_d)
        stage_in = (in_g, in_u, in_d)
        stage_out = (cast_g, cast_u, cast_d)
        first_expert = pl.program_id(0) * cast_per_step

        def load(k):
            return [pltpu.make_async_copy(w32[a].at[first_expert + k], stage_in[a], in_sems.at[a])
                    for a in range(3)]

        def store(k):
            return [pltpu.make_async_copy(stage_out[a], w16[a].at[first_expert + k], out_sems.at[a])
                    for a in range(3)]

        def cast_boundary(k):
            if k > cast_per_step:
                return
            if k > 0:
                for copy in load(k - 1):
                    copy.wait()
                if k > 1:
                    for copy in store(k - 2):
                        copy.wait()
                for a in range(3):
                    stage_out[a][...] = stage_in[a][...].astype(BF16)
                for copy in store(k - 1):
                    copy.start()
            if k < cast_per_step:
                for copy in load(k):
                    copy.start()
            else:
                for copy in store(k - 1):
                    copy.wait()
    else:
        bias_ref, bias_first_ref, out_ref, o_scr, l_scr = refs[9:]

        def cast_boundary(k):
            del k

    rows = HEADS_PER_GROUP * BLOCK
    row_head = lax.broadcasted_iota(jnp.int32, (rows, GROUP_WIDTH), 0) // BLOCK
    lane_head_r = lax.broadcasted_iota(jnp.int32, (rows, GROUP_WIDTH), 1) // HEAD_DIM
    head_mask = row_head == lane_head_r
    heads_per_half = LANES // HEAD_DIM
    lane_head = lax.broadcasted_iota(jnp.int32, (BLOCK, LANES), 1) // HEAD_DIM

    def heads_to_lanes(per_head):
        out = per_head[0]
        for h in range(1, heads_per_half):
            out = jnp.where(lane_head == h, per_head[h], out)
        return out

    for g, (_, dil) in enumerate(ATTN_CONFIGS):
        cast_boundary(g)
        q_ref, k_ref, v_ref = qkv[3 * g:3 * g + 3]
        sub_len = seq // dil
        n_blocks = sub_len // BLOCK

        def block(r, n, first, g=g, dil=dil, q_ref=q_ref, k_ref=k_ref, v_ref=v_ref, sub_len=sub_len):
            base = pl.multiple_of(r * sub_len + n * BLOCK, BLOCK)
            qb = q_ref[pl.ds(base, BLOCK), :]
            if first:
                kk = k_ref[pl.ds(base, BLOCK), :]
                vv = v_ref[pl.ds(base, BLOCK), :]
                bias = bias_first_ref[g]
            else:
                kbase = pl.multiple_of(base - BLOCK, BLOCK)
                kk = k_ref[pl.ds(kbase, 2 * BLOCK), :]
                vv = v_ref[pl.ds(kbase, 2 * BLOCK), :]
                bias = bias_ref[g]
            qs = jnp.where(head_mask, jnp.concatenate([qb] * HEADS_PER_GROUP, axis=0), jnp.zeros((), BF16))
            logits = lax.dot_general(qs, kk, (((1,), (1,)), ((), ())), preferred_element_type=F32) + bias
            yield
            m = jnp.max(logits, axis=1, keepdims=True)
            p = jnp.exp(logits - m)
            s = jnp.sum(p, axis=1, keepdims=True)
            p = p.astype(BF16)
            yield
            pv = jnp.dot(p, vv, preferred_element_type=F32)
            yield
            inv_s = 1.0 / s
            lse = m + jnp.log(s)
            start = n * (BLOCK * dil) + r
            if dil == 1:
                dst = pl.ds(pl.multiple_of(start, BLOCK), BLOCK)
            else:
                dst = pl.ds(start, BLOCK, stride=dil)
            for half in range(HALVES):
                cols = slice(half * LANES, (half + 1) * LANES)
                o_heads, lse_heads = [], []
                for h in range(half * heads_per_half, (half + 1) * heads_per_half):
                    head_rows = slice(h * BLOCK, (h + 1) * BLOCK)
                    o_heads.append(pv[head_rows, cols] * inv_s[head_rows])
                    lse_heads.append(jnp.broadcast_to(lse[head_rows], (BLOCK, LANES)))
                o_scr[g * HALVES + half, dst, :] = heads_to_lanes(o_heads)
                l_scr[g * HALVES + half, dst, :] = heads_to_lanes(lse_heads)
            yield

        if n_blocks > 1:
            group = 3 if (n_blocks - 1) % 3 == 0 else 1

            _run_skewed([block(r, 0, True) for r in range(dil)], ATTN_STAGES)

            def per_subsequence(r, carry, block=block, n_blocks=n_blocks, group=group):
                def per_group(i, c):
                    _run_skewed([block(r, 1 + i * group + k, False) for k in range(group)], ATTN_STAGES)
                    return c
                lax.fori_loop(0, (n_blocks - 1) // group, per_group, 0)
                return carry
            lax.fori_loop(0, dil, per_subsequence, 0)
        else:
            group = next(c for c in (8, 4, 2, 1) if dil % c == 0)

            def per_group(i, carry, block=block, group=group):
                _run_skewed([block(i * group + k, 0, True) for k in range(group)], ATTN_STAGES)
                return carry
            lax.fori_loop(0, dil // group, per_group, 0)

    cast_boundary(N_GROUPS)
    chunk = 256

    def merge(i, carry):
        sl = pl.ds(pl.multiple_of(i * chunk, chunk), chunk)
        for half in range(HALVES):
            ls = [l_scr[g * HALVES + half, sl, :] for g in range(N_GROUPS)]
            m = functools.reduce(jnp.maximum, ls)
            es = [jnp.exp(l - m) for l in ls]
            den = functools.reduce(lambda a, b: a + b, es)
            num = functools.reduce(lambda a, b: a + b,
                                   [e * o_scr[g * HALVES + half, sl, :] for g, e in enumerate(es)])
            out_ref[sl, half * LANES:(half + 1) * LANES] = (num / den).astype(BF16)
        return carry

    lax.fori_loop(0, seq // chunk, merge, 0)
    cast_boundary(N_GROUPS + 1)


def _attn_call(qkv, rel_bias_table, expert_weights=None):
    B, S, _ = qkv[0].shape
    for (_, dil) in ATTN_CONFIGS:
        assert S % (dil * BLOCK) == 0
    bias, bias_first = _attn_bias(rel_bias_table)
    seq_spec = pl.BlockSpec((None, S, GROUP_WIDTH), lambda b: (b, 0, 0))
    ordered = []
    for g in range(N_GROUPS):
        ordered += [qkv[g], qkv[3 + g], qkv[6 + g]]
    operands = ordered + [bias, bias_first]
    in_specs = [seq_spec] * 9 + [pl.BlockSpec(bias.shape, lambda b: (0, 0, 0)),
                                 pl.BlockSpec(bias_first.shape, lambda b: (0, 0, 0))]
    out_specs = [seq_spec]
    out_shape = [jax.ShapeDtypeStruct((B, S, GROUP_WIDTH), BF16)]
    scratch_shapes = [pltpu.VMEM((N_GROUPS * HALVES, S, LANES), F32),
                      pltpu.VMEM((N_GROUPS * HALVES, S, LANES), F32)]
    cast_per_step = 0
    if expert_weights is not None:
        n_experts = expert_weights[0].shape[0]
        assert n_experts % B == 0 and n_experts // B <= N_GROUPS + 1
        cast_per_step = n_experts // B
        operands += list(expert_weights)
        in_specs += [pl.BlockSpec(memory_space=pl.ANY)] * 3
        out_specs += [pl.BlockSpec(memory_space=pl.ANY)] * 3
        out_shape += [jax.ShapeDtypeStruct(w.shape, BF16) for w in expert_weights]
        scratch_shapes += ([pltpu.VMEM(w.shape[1:], F32) for w in expert_weights]
                           + [pltpu.VMEM(w.shape[1:], BF16) for w in expert_weights]
                           + [pltpu.SemaphoreType.DMA((3,)), pltpu.SemaphoreType.DMA((3,))])
    outs = pl.pallas_call(
        functools.partial(_attn_kernel, seq=S, cast_per_step=cast_per_step),
        grid=(B,),
        in_specs=in_specs,
        out_specs=out_specs,
        out_shape=out_shape,
        scratch_shapes=scratch_shapes,
        compiler_params=pltpu.CompilerParams(dimension_semantics=("arbitrary",),
                                             vmem_limit_bytes=VMEM_LIMIT_BYTES),
        name="attn",
    )(*operands)
    return outs[0], (tuple(outs[1:]) if expert_weights is not None else None)


PACK_CHUNKS = 4
SUBLANES = 8
ROUTER_LANES = LANES // 2


def _pack_bf16_pairs(rounded):
    w = rounded.shape[1] // 2
    bits = lax.bitcast_convert_type(rounded, jnp.int32)
    return bits[:, :w] | lax.shift_right_logical(bits[:, w:], jnp.full((), 16, jnp.int32))


def _unpack_bf16_pairs(words):
    hi = lax.bitcast_convert_type(words & jnp.int32(-65536), F32).astype(BF16)
    lo = lax.bitcast_convert_type(lax.shift_left(words, jnp.full((), 16, jnp.int32)), F32).astype(BF16)
    return jnp.concatenate([hi, lo], axis=1)


def _fold_pool_kernel(wp_ref, scale_ref, pb_ref, out_ref):
    a = wp_ref[...] * scale_ref[...]
    b = pb_ref[...]
    a_hi = a.astype(BF16)
    a_lo = (a - a_hi.astype(F32)).astype(BF16)
    b_hi = b.astype(BF16)
    b_lo = (b - b_hi.astype(F32)).astype(BF16)
    out_ref[...] = (jnp.dot(a_hi, b_hi, preferred_element_type=F32)
                    + jnp.dot(a_lo, b_hi, preferred_element_type=F32)
                    + jnp.dot(a_hi, b_lo, preferred_element_type=F32)).astype(BF16)


def _fold_pool_call(w_pool, pool_scale, w_proj_pool):
    n_groups, gd, _ = w_pool.shape
    D = w_proj_pool.shape[1]
    return pl.pallas_call(
        _fold_pool_kernel,
        grid=(n_groups,),
        in_specs=[pl.BlockSpec((None, gd, gd), lambda g: (g, 0, 0)),
                  pl.BlockSpec((None, 1, gd), lambda g: (g, 0, 0)),
                  pl.BlockSpec((gd, D), lambda g: (g, 0))],
        out_specs=pl.BlockSpec((gd, D), lambda g: (g, 0)),
        out_shape=jax.ShapeDtypeStruct((n_groups * gd, D), BF16),
        name="fold_pool",
    )(w_pool, pool_scale.reshape(n_groups, 1, gd), w_proj_pool)


def _post_kernel(a_ref, u_ref, halo_ref, ga_ref, gb_ref, x_ref,
                 pa_ref, pb_ref, wout_ref, g1_ref, b1_ref,
                 wr_cat_ref, br_ref,
                 h_ref, hb_ref, ids_ref, wts_ref, cnt_ref, pool_scr, tmp_scr, *, tm, sub, alpha):
    i = pl.program_id(1)
    halo = halo_ref[...]
    pool_scr[0:POOL_HALO, :] = jnp.zeros_like(halo)
    pool_scr[POOL_HALO:2 * POOL_HALO, :] = jnp.where(i > 0, halo, jnp.zeros_like(halo))
    pool_scr[2 * POOL_HALO:, :] = u_ref[...]
    tmp_scr[:, 0:POOL_HALO, :] = jnp.zeros((tmp_scr.shape[0], POOL_HALO, POOL_GROUP_DIM), F32)
    head_pos = i * tm + lax.broadcasted_iota(jnp.int32, (POOL_HALO, POOL_GROUP_DIM), 0)

    def rows_chain(r0):
        rs = pl.ds(r0, sub)
        group_cols = [slice(gi * POOL_GROUP_DIM, (gi + 1) * POOL_GROUP_DIM) for gi in range(len(POOL_SIZES))]
        diffs = []
        ext = sub + 2 * POOL_HALO
        for gi, (cols, w) in enumerate(zip(group_cols, POOL_SIZES)):
            ug = u_ref[rs, cols]
            levels = w.bit_length() - 1
            for level in range(levels):
                shift = 1 << level
                dst = tmp_scr.at[(r0 // sub * len(POOL_SIZES) + gi) * 2 + level % 2]
                if level == 0:
                    cur = pool_scr[pl.ds(r0 + POOL_HALO, ext - POOL_HALO), cols]
                    back = pool_scr[pl.ds(r0 + POOL_HALO - shift, ext - POOL_HALO), cols]
                else:
                    src = tmp_scr.at[(r0 // sub * len(POOL_SIZES) + gi) * 2 + (level - 1) % 2]
                    cur = src[pl.ds(POOL_HALO, ext - POOL_HALO), :]
                    back = src[pl.ds(POOL_HALO - shift, ext - POOL_HALO), :]
                if level == levels - 1:
                    acc = (cur + back)[POOL_HALO:]
                else:
                    dst[pl.ds(POOL_HALO, ext - POOL_HALO), :] = cur + back
            inv_count = jnp.full((sub, POOL_GROUP_DIM), 1.0 / w, F32)
            if r0 == 0:
                inv_count = jnp.concatenate([1.0 / jnp.minimum(head_pos + 1, w).astype(F32),
                                             inv_count[POOL_HALO:]], axis=0)
            diffs.append((acc * inv_count - ug).astype(BF16))
        yield
        y_pool = jnp.dot(jnp.concatenate(diffs, axis=1), pb_ref[...], preferred_element_type=F32)
        y_attn = jnp.dot(a_ref[rs, :], pa_ref[...], preferred_element_type=F32)
        yield
        mixed = ga_ref[rs, :] * y_attn.astype(BF16) + gb_ref[rs, :] * y_pool.astype(BF16)
        yield
        y = jnp.dot(mixed, wout_ref[...], preferred_element_type=F32)
        yield
        h = _layer_norm(alpha * x_ref[rs, :] + y, g1_ref[...], b1_ref[...])
        h_ref[rs, :] = h
        h_hi = h.astype(BF16)
        h_rounded = h_hi.astype(F32)
        packed = _pack_bf16_pairs(h_rounded)
        for c in range(PACK_CHUNKS):
            hb_ref[c, rs, :] = packed[:, c * LANES:(c + 1) * LANES]

        h_lo = (h - h_rounded).astype(BF16)
        yield
        nt = (((1,), (1,)), ((), ()))
        both = lax.dot_general(wr_cat_ref[...], h_hi, nt, preferred_element_type=F32)
        lo_hi = lax.dot_general(wr_cat_ref[0:ROUTER_LANES, :], h_lo, nt, preferred_element_type=F32)
        logits = both[0:ROUTER_LANES] + both[ROUTER_LANES:] + lo_hi + br_ref[...]
        yield
        row = lax.broadcasted_iota(jnp.int32, logits.shape, 0)
        big = jnp.int32(2 ** 30)
        is_group = row < N_EXPERT_GROUPS
        gl = jnp.where(is_group, logits, -jnp.inf)
        gmax = jnp.max(gl, axis=0, keepdims=True)
        g_idx = jnp.min(jnp.where(gl == gmax, row, big), axis=0, keepdims=True)
        g_prob = 1.0 / jnp.sum(jnp.exp(gl - gmax), axis=0, keepdims=True)
        expert = row - N_EXPERT_GROUPS
        in_group = (expert >= g_idx * EXPERTS_PER_GROUP) & (expert < (g_idx + 1) * EXPERTS_PER_GROUP)
        el = jnp.where(in_group, logits, -jnp.inf)
        v1 = jnp.max(el, axis=0, keepdims=True)
        i1 = jnp.min(jnp.where(el == v1, expert, big), axis=0, keepdims=True)
        el2 = jnp.where(expert == i1, -jnp.inf, el)
        v2 = jnp.max(el2, axis=0, keepdims=True)
        i2 = jnp.min(jnp.where(el2 == v2, expert, big), axis=0, keepdims=True)
        e2 = jnp.exp(v2 - v1)
        den = 1.0 + e2
        first_row = lax.broadcasted_iota(jnp.int32, (SUBLANES, sub), 0) == 0
        wts_ref[:, rs] = jnp.where(first_row, 1.0 / den * g_prob, e2 / den * g_prob)
        ids_ref[:, rs] = jnp.where(first_row, i1, i2).astype(F32)
        chosen = jnp.where((expert == i1) | (expert == i2), 1.0, 0.0)
        per_expert = jnp.broadcast_to(jnp.sum(chosen, axis=1, keepdims=True), (ROUTER_LANES, LANES))
        to_lane = (lax.broadcasted_iota(jnp.int32, (ROUTER_LANES, LANES), 0) - N_EXPERT_GROUPS
                   == lax.broadcasted_iota(jnp.int32, (ROUTER_LANES, LANES), 1))
        counts.append(jnp.sum(jnp.where(to_lane, per_expert, 0.0), axis=0, keepdims=True))
        yield

    counts = []
    _run_skewed([rows_chain(r0) for r0 in range(0, tm, sub)], 7)
    cnt_ref[...] = jnp.broadcast_to(functools.reduce(lambda a, b: a + b, counts), cnt_ref.shape)


def _post_call(a, u, ga, gb, x, w_proj_attn, w_pool, pool_scale, w_proj_pool, w_out, gamma, beta,
               w_router_group, b_router_group, w_router_expert, b_router_expert, alpha, b0, tm=1024, n_sub=2):
    B = a.shape[0]
    _, S, D = x.shape
    assert S % tm == 0 and tm % POOL_HALO == 0
    n_logits = N_EXPERT_GROUPS + N_EXPERTS
    assert n_logits <= ROUTER_LANES
    wr = jnp.concatenate([w_router_group, w_router_expert], axis=1).T
    wr = jnp.pad(wr, ((0, ROUTER_LANES - n_logits), (0, 0)))
    wr_hi = wr.astype(BF16)
    wr_lo = (wr - wr_hi.astype(F32)).astype(BF16)
    wr_cat = jnp.concatenate([wr_hi, wr_lo], axis=0)
    br = jnp.pad(jnp.concatenate([b_router_group, b_router_expert]),
                 (0, ROUTER_LANES - n_logits)).reshape(ROUTER_LANES, 1)
    n_tiles = B * (S // tm)
    row_spec = lambda width: pl.BlockSpec((None, tm, width), lambda b, i: (b, i, 0))
    full = lambda arr: pl.BlockSpec(arr.shape, lambda b, i: (0,) * arr.ndim)
    halo_blocks = tm // POOL_HALO
    halo_spec = pl.BlockSpec((None, POOL_HALO, POOL_WIDTH),
                             lambda b, i: (b, jnp.maximum(i * halo_blocks - 1, 0), 0))
    weights = [w_proj_attn.astype(BF16), _fold_pool_call(w_pool, pool_scale, w_proj_pool),
               w_out.astype(BF16), gamma.reshape(1, D), beta.reshape(1, D), wr_cat, br]
    return pl.pallas_call(
        functools.partial(_post_kernel, tm=tm, sub=tm // n_sub, alpha=alpha),
        grid=(B, S // tm),
        in_specs=[row_spec(GROUP_WIDTH), row_spec(POOL_WIDTH), halo_spec, row_spec(D), row_spec(D),
                  pl.BlockSpec((None, tm, D), lambda b, i: (b + b0, i, 0))] + [full(w) for w in weights],
        out_specs=[row_spec(D),
                   pl.BlockSpec((PACK_CHUNKS, tm, LANES), lambda b, i: (0, b * (S // tm) + i, 0)),
                   pl.BlockSpec((SUBLANES, tm), lambda b, i: (0, b * (S // tm) + i)),
                   pl.BlockSpec((SUBLANES, tm), lambda b, i: (0, b * (S // tm) + i)),
                   pl.BlockSpec((SUBLANES, LANES), lambda b, i: (b * (S // tm) + i, 0))],
        out_shape=[jax.ShapeDtypeStruct((B, S, D), F32),
                   jax.ShapeDtypeStruct((PACK_CHUNKS, B * S, LANES), jnp.int32),
                   jax.ShapeDtypeStruct((SUBLANES, B * S), F32), jax.ShapeDtypeStruct((SUBLANES, B * S), F32),
                   jax.ShapeDtypeStruct((n_tiles * SUBLANES, LANES), F32)],
        scratch_shapes=[pltpu.VMEM((tm + 2 * POOL_HALO, POOL_WIDTH), F32),
                        pltpu.VMEM((n_sub * len(POOL_SIZES) * 2, tm // n_sub + 2 * POOL_HALO, POOL_GROUP_DIM), F32)],
        compiler_params=pltpu.CompilerParams(dimension_semantics=("parallel", "parallel"),
                                             vmem_limit_bytes=VMEM_LIMIT_BYTES),
        name="post",
    )(a, u, u, ga, gb, x, *weights)


def _rank_kernel(ids_ref, cnt_ref, pos_ref, ends_ref, run_ref, start_ref, earlier_ref, *, tm, tile_rows):
    i = pl.program_id(0)

    @pl.when(i == 0)
    def _():
        total = jnp.sum(cnt_ref[...], axis=0, keepdims=True) / SUBLANES
        padded = jnp.broadcast_to(jnp.ceil(total / tile_rows) * tile_rows, ends_ref.shape)
        lane1 = lax.broadcasted_iota(jnp.int32, padded.shape, 1)
        incl = padded
        shift = 1
        while shift < LANES:
            incl = incl + jnp.where(lane1 >= shift, pltpu.roll(incl, shift, axis=1), 0.0)
            shift *= 2
        ends_ref[...] = incl
        start_row = jnp.broadcast_to((incl - padded)[0:1, :], start_ref.shape)
        on_diagonal = (lax.broadcasted_iota(jnp.int32, start_ref.shape, 0)
                       == lax.broadcasted_iota(jnp.int32, start_ref.shape, 1))
        start_col = jnp.sum(jnp.where(on_diagonal, start_row, 0.0), axis=1, keepdims=True)
        start_ref[...] = jnp.broadcast_to(start_col, start_ref.shape)
        run_ref[...] = jnp.zeros_like(run_ref)
        row = lax.broadcasted_iota(jnp.int32, (tm, tm), 0)
        col = lax.broadcasted_iota(jnp.int32, (tm, tm), 1)
        earlier_ref[...] = jnp.where(row < col, 1.0, 0.0).astype(BF16)

    ids = ids_ref[...]
    expert = lax.broadcasted_iota(jnp.int32, (N_EXPERTS, tm), 0).astype(F32)
    oh0 = expert == ids[0:1, :]
    oh1 = expert == ids[1:2, :]
    onehot = jnp.where(oh0 | oh1, 1.0, 0.0)
    before = jnp.dot(onehot.astype(BF16), earlier_ref[...], preferred_element_type=F32)
    slot = start_ref[:, 0:1] + run_ref[:, 0:1] + before
    p0 = jnp.sum(jnp.where(oh0, slot, 0.0), axis=0, keepdims=True)
    p1 = jnp.sum(jnp.where(oh1, slot, 0.0), axis=0, keepdims=True)
    first_row = lax.broadcasted_iota(jnp.int32, pos_ref.shape, 0) == 0
    pos_ref[...] = jnp.where(first_row, p0, p1).astype(jnp.int32)
    run_ref[...] += jnp.broadcast_to(jnp.sum(onehot, axis=1, keepdims=True), run_ref.shape)


def _rank_call(ids_t, counts, tile_rows, tm=1024):
    N = ids_t.shape[1]
    assert N % tm == 0 and SUBLANES * (2 * N + N_EXPERTS * tile_rows) < 2 ** 24
    assert N_EXPERTS <= LANES
    return pl.pallas_call(
        functools.partial(_rank_kernel, tm=tm, tile_rows=tile_rows),
        grid=(N // tm,),
        in_specs=[pl.BlockSpec((SUBLANES, tm), lambda i: (0, i)),
                  pl.BlockSpec(counts.shape, lambda i: (0, 0))],
        out_specs=[pl.BlockSpec((SUBLANES, tm), lambda i: (0, i)),
                   pl.BlockSpec((SUBLANES, LANES), lambda i: (0, 0))],
        out_shape=[jax.ShapeDtypeStruct((SUBLANES, N), jnp.int32),
                   jax.ShapeDtypeStruct((SUBLANES, LANES), F32)],
        scratch_shapes=[pltpu.VMEM((N_EXPERTS, LANES), F32), pltpu.VMEM((N_EXPERTS, LANES), F32),
                        pltpu.VMEM((tm, tm), BF16)],
        compiler_params=pltpu.CompilerParams(dimension_semantics=("arbitrary",),
                                             vmem_limit_bytes=VMEM_LIMIT_BYTES),
        name="rank",
    )(ids_t, counts)


SC_CORES = 2
SC_SUBCORES = 16
SC_WORKERS = SC_CORES * SC_SUBCORES
SC_CHUNK = 128


def _sc_mesh():
    return plsc.VectorSubcoreMesh(core_axis_name="c", subcore_axis_name="s",
                                  num_cores=SC_CORES, num_subcores=SC_SUBCORES)


def _sc_dispatch(packed, pos_t, n_rows):
    n_chunks, n_tok, width = packed.shape
    per_worker = n_tok // SC_WORKERS
    assert n_tok % (SC_WORKERS * SC_CHUNK) == 0

    @functools.partial(
        pl.kernel, mesh=_sc_mesh(),
        out_type=jax.ShapeDtypeStruct((n_chunks, n_rows, width), packed.dtype),
        scratch_types=[pltpu.VMEM((SC_CHUNK,), jnp.int32), pltpu.VMEM((SC_CHUNK,), jnp.int32),
                       pltpu.VMEM((n_chunks, SC_CHUNK, width), packed.dtype)]
                      + [pltpu.SemaphoreType.DMA] * (3 * n_chunks),
        name="sc_dispatch")
    def run(packed_hbm, pos_hbm, out_hbm, idx0, idx1, bufs, *sems):
        load_sems, sems0, sems1 = sems[:n_chunks], sems[n_chunks:2 * n_chunks], sems[2 * n_chunks:]
        worker = lax.axis_index("s") * SC_CORES + lax.axis_index("c")

        @pl.loop(0, per_worker // SC_CHUNK)
        def _(j):
            base = worker * per_worker + j * SC_CHUNK
            loads = [pltpu.async_copy(packed_hbm.at[c, pl.ds(base, SC_CHUNK)], bufs.at[c], load_sems[c])
                     for c in range(n_chunks)]
            pltpu.sync_copy(pos_hbm.at[0, pl.ds(base, SC_CHUNK)], idx0)
            pltpu.sync_copy(pos_hbm.at[1, pl.ds(base, SC_CHUNK)], idx1)
            scatters = []
            for c in range(n_chunks):
                loads[c].wait()
                scatters.append(pltpu.async_copy(bufs.at[c], out_hbm.at[c].at[idx0], sems0[c]))
                scatters.append(pltpu.async_copy(bufs.at[c], out_hbm.at[c].at[idx1], sems1[c]))
            for s in scatters:
                s.wait()

    return run(packed, pos_t)


def _sc_combine(sorted_rows, pos_t):
    n_chunks, _, width = sorted_rows.shape
    n_tok = pos_t.shape[1]
    per_worker = n_tok // SC_WORKERS
    assert n_tok % (SC_WORKERS * SC_CHUNK) == 0

    @functools.partial(
        pl.kernel, mesh=_sc_mesh(),
        out_type=jax.ShapeDtypeStruct((2, n_chunks, n_tok, width), sorted_rows.dtype),
        scratch_types=[pltpu.VMEM((SC_CHUNK,), jnp.int32),
                       pltpu.VMEM((n_chunks, SC_CHUNK, width), sorted_rows.dtype)]
                      + [pltpu.SemaphoreType.DMA] * (2 * n_chunks),
        name="sc_combine")
    def run(rows_hbm, pos_hbm, out_hbm, idx, bufs, *sems):
        gather_sems, write_sems = sems[:n_chunks], sems[n_chunks:]
        worker = lax.axis_index("s") * SC_CORES + lax.axis_index("c")

        @pl.loop(0, per_worker // SC_CHUNK)
        def _(j):
            base = worker * per_worker + j * SC_CHUNK
            for k in range(2):
                pltpu.sync_copy(pos_hbm.at[k, pl.ds(base, SC_CHUNK)], idx)
                gathers = [pltpu.async_copy(rows_hbm.at[c].at[idx], bufs.at[c], gather_sems[c])
                           for c in range(n_chunks)]
                writes = []
                for c in range(n_chunks):
                    gathers[c].wait()
                    writes.append(pltpu.async_copy(bufs.at[c], out_hbm.at[k, c, pl.ds(base, SC_CHUNK)],
                                                   write_sems[c]))
                for w in writes:
                    w.wait()

    return run(sorted_rows, pos_t)


def _expert_kernel(tile_expert_ref, n_used_ref, xs_ref, wg_ref, wu_ref, wd_ref, ys_ref):
    del tile_expert_ref

    @pl.when(pl.program_id(0) < n_used_ref[0])
    def _():
        wg = wg_ref[...].astype(BF16)
        wu = wu_ref[...].astype(BF16)
        wd = wd_ref[...].astype(BF16)
        tile_rows = xs_ref.shape[1]
        sub = tile_rows // 2 if tile_rows >= 512 else tile_rows

        def rows_chain(r0):
            rs = pl.ds(r0, sub)
            x = _unpack_bf16_pairs(jnp.concatenate([xs_ref[c, rs, :] for c in range(PACK_CHUNKS)], axis=1))
            yield
            gate = jnp.dot(x, wg, preferred_element_type=F32)
            up = jnp.dot(x, wu, preferred_element_type=F32)
            yield
            hidden = (jax.nn.silu(gate) * up).astype(BF16)
            yield
            y = jnp.dot(hidden, wd, preferred_element_type=F32)
            yield
            y = _pack_bf16_pairs(y.astype(BF16).astype(F32))
            for c in range(PACK_CHUNKS):
                ys_ref[c, rs, :] = y[:, c * LANES:(c + 1) * LANES]
            yield

        chains = [rows_chain(r0) for r0 in range(0, tile_rows, sub)]
        n_stages = 5
        for step in range(n_stages + len(chains) - 1):
            for lag, chain in enumerate(chains):
                if 0 <= step - lag < n_stages:
                    next(chain)


def _expert_call(xs, tile_expert, n_used, w_gate, w_up, w_down, tile_rows):
    _, n_rows, _ = xs.shape
    D = w_gate.shape[1]
    row_block = pl.BlockSpec((PACK_CHUNKS, tile_rows, LANES),
                             lambda i, te, nu: (0, jnp.minimum(i, nu[0] - 1), 0))
    return pl.pallas_call(
        _expert_kernel,
        grid_spec=pltpu.PrefetchScalarGridSpec(
            num_scalar_prefetch=2,
            grid=(n_rows // tile_rows,),
            in_specs=[row_block,
                      pl.BlockSpec((None, D, D_EXPERT), lambda i, te, nu: (te[i], 0, 0)),
                      pl.BlockSpec((None, D, D_EXPERT), lambda i, te, nu: (te[i], 0, 0)),
                      pl.BlockSpec((None, D_EXPERT, D), lambda i, te, nu: (te[i], 0, 0))],
            out_specs=row_block),
        out_shape=jax.ShapeDtypeStruct(xs.shape, xs.dtype),
        compiler_params=pltpu.CompilerParams(dimension_semantics=("arbitrary",),
                                             vmem_limit_bytes=VMEM_LIMIT_BYTES),
        name="experts",
    )(tile_expert, n_used, xs, w_gate, w_up, w_down)


def _final_kernel(h_ref, y_ref, wts_ref, g2_ref, b2_ref, *rest, alpha):
    out_ref = rest[-1]
    tm = h_ref.shape[0]
    on_diagonal = (lax.broadcasted_iota(jnp.int32, (tm, tm), 0)
                   == lax.broadcasted_iota(jnp.int32, (tm, tm), 1))
    z = alpha * h_ref[...]
    for k in range(2):
        w_col = jnp.sum(jnp.where(on_diagonal, jnp.broadcast_to(wts_ref[k:k + 1, :], (tm, tm)), 0.0),
                        axis=1, keepdims=True)
        yk = _unpack_bf16_pairs(jnp.concatenate([y_ref[k, c] for c in range(PACK_CHUNKS)], axis=1))
        z = z + w_col * yk.astype(F32)
    out_ref[...] = _layer_norm(z, g2_ref[...], b2_ref[...])


def _final_call(h, y, wts, gamma, beta, alpha, row0, n_total, earlier_out, tm=1024):
    N, D = h.shape
    assert N % tm == 0 and row0 % tm == 0
    operands = [h, y, wts, gamma.reshape(1, D), beta.reshape(1, D)]
    in_specs = [pl.BlockSpec((tm, D), lambda i: (i, 0)),
                pl.BlockSpec((2, PACK_CHUNKS, tm, LANES), lambda i: (0, 0, i, 0)),
                pl.BlockSpec((SUBLANES, tm), lambda i: (0, i)),
                pl.BlockSpec((1, D), lambda i: (0, 0)),
                pl.BlockSpec((1, D), lambda i: (0, 0))]
    aliases = {}
    if earlier_out is not None:
        aliases = {len(operands): 0}
        operands.append(earlier_out)
        in_specs.append(pl.BlockSpec(memory_space=pl.ANY))
    return pl.pallas_call(
        functools.partial(_final_kernel, alpha=alpha),
        grid=(N // tm,),
        in_specs=in_specs,
        out_specs=pl.BlockSpec((tm, D), lambda i: (i + row0 // tm, 0)),
        out_shape=jax.ShapeDtypeStruct((n_total, D), F32),
        input_output_aliases=aliases,
        compiler_params=pltpu.CompilerParams(dimension_semantics=("parallel",),
                                             vmem_limit_bytes=VMEM_LIMIT_BYTES),
        name="final",
    )(*operands)


def _moe(packed, ids_t, counts, w_gate, w_up, w_down, tile_rows=256):
    N = ids_t.shape[1]
    pos_t, ends = _rank_call(ids_t, counts, tile_rows)
    n_tiles = 2 * N // tile_rows + N_EXPERTS
    seg_end = ends[0, :N_EXPERTS].astype(jnp.int32)
    tile_start = jnp.arange(n_tiles, dtype=jnp.int32) * tile_rows
    tile_expert = jnp.minimum(jnp.sum(seg_end[None, :] <= tile_start[:, None], axis=1),
                              N_EXPERTS - 1).astype(jnp.int32)
    n_used = (seg_end[N_EXPERTS - 1:] // tile_rows).astype(jnp.int32)
    xs = _sc_dispatch(packed, pos_t, n_tiles * tile_rows)
    ys = _expert_call(xs, tile_expert, n_used, w_gate, w_up, w_down, tile_rows)
    return _sc_combine(ys, pos_t)


@jax.jit
def kernel(x, w_in, b_in, rel_bias_table, w_pool, pool_scale, w_proj_attn, w_proj_pool, w_out, ln1_gamma, ln1_beta, w_router_group, b_router_group, w_router_expert, b_router_expert, w_expert_gate, w_expert_up, w_expert_down, ln2_gamma, ln2_beta):
    B, S, D = x.shape
    depth = w_in.shape[0]
    alpha = (2.0 * depth) ** 0.25
    n_parts = 2 if B % 2 == 0 else 1
    nb = B // n_parts
    for layer in range(depth):
        out = None
        experts32 = (w_expert_gate[layer], w_expert_up[layer], w_expert_down[layer])
        n_experts = experts32[0].shape[0]
        cast_in_attn = n_experts % nb == 0 and n_experts // nb <= N_GROUPS + 1
        experts16 = None if cast_in_attn else tuple(w.astype(BF16) for w in experts32)
        for part in range(n_parts):
            b0 = part * nb
            qkv, u, ga, gb = _proj_call(x, w_in[layer], b_in[layer], b0, nb)
            a, converted = _attn_call(qkv, rel_bias_table, experts32 if experts16 is None else None)
            if converted is not None:
                experts16 = converted
            h, packed, ids, wts, counts = _post_call(
                a, u, ga, gb, x, w_proj_attn[layer], w_pool[layer], pool_scale[layer],
                w_proj_pool[layer], w_out[layer], ln1_gamma[layer], ln1_beta[layer],
                w_router_group[layer], b_router_group[layer],
                w_router_expert[layer], b_router_expert[layer], alpha, b0)
            y = _moe(packed, ids, counts, *experts16)
            out = _final_call(h.reshape(nb * S, D), y, wts, ln2_gamma[layer],
                              ln2_beta[layer], alpha, b0 * S, B * S, out)
        x = out.reshape(B, S, D)
    return x
```

```python
import functools
import math

import jax
import jax.numpy as jnp
import numpy as np
from jax import lax
from jax.experimental import pallas as pl
from jax.experimental.pallas import tpu as pltpu
from jax.experimental.pallas import tpu_sc as plsc

F32 = jnp.float32
BF16 = jnp.bfloat16

HEAD_DIM = 64
ATTN_CONFIGS = ((128, 1), (512, 4), (2048, 16))
N_GROUPS = len(ATTN_CONFIGS)
HEADS_PER_GROUP = 4
GROUP_WIDTH = HEADS_PER_GROUP * HEAD_DIM
ATTN_WIDTH = N_GROUPS * GROUP_WIDTH
BLOCK = 128
N_REL_BUCKETS = 32
REL_MAX_DISTANCE = 2048
NEG_INF = -1e30

POOL_SIZES = (2, 4, 8, 16)
POOL_GROUP_DIM = 128
POOL_WIDTH = POOL_GROUP_DIM * len(POOL_SIZES)
POOL_HALO = 16

N_EXPERT_GROUPS = 4
EXPERTS_PER_GROUP = 8
N_EXPERTS = N_EXPERT_GROUPS * EXPERTS_PER_GROUP
D_EXPERT = 256
LN_EPS = 1e-5

VMEM_LIMIT_BYTES = 56 * 1024 * 1024
LANES = 128
HALVES = GROUP_WIDTH // LANES


def _layer_norm(z, gamma, beta):
    mu = jnp.mean(z, axis=-1, keepdims=True)
    zc = z - mu
    var = jnp.mean(zc * zc, axis=-1, keepdims=True)
    return zc * lax.rsqrt(var + LN_EPS) * gamma + beta


def _proj_kernel(x_ref, w_ref, b_ref, *refs, tm, d_model):
    qkv_refs = refs[:9]
    u_ref, ga_ref, gb_ref, xb_ref, acc_ref = refs[9:]
    xb_ref[...] = x_ref[...].astype(BF16)

    def chunk(c0, width):
        acc = jnp.dot(xb_ref[...], w_ref[:, c0:c0 + width], preferred_element_type=F32)
        return acc + b_ref[:, c0:c0 + width]

    pool_off = 3 * ATTN_WIDTH
    ga_off = pool_off + POOL_WIDTH

    def gate_job(gate_ref, off, c):
        def run():
            gate_ref[:, c * 256:(c + 1) * 256] = jax.nn.sigmoid(chunk(off + c * 256, 256)).astype(BF16)
        return run

    def pool_job(c):
        def run():
            u_ref[:, c * 256:(c + 1) * 256] = chunk(pool_off + c * 256, 256)
        return run

    def qkv_job(which, g, staged):
        def run():
            dil = ATTN_CONFIGS[g][1]
            out = qkv_refs[which * 3 + g]
            acc = chunk(which * ATTN_WIDTH + g * GROUP_WIDTH, GROUP_WIDTH)
            if which == 0:
                acc = acc * HEAD_DIM ** -0.5
            if dil == 1:
                out[0] = acc.astype(BF16)
            else:
                planes = [staged * HALVES + half for half in range(HALVES)]
                for half, plane in enumerate(planes):
                    acc_ref[plane] = acc[:, half * LANES:(half + 1) * LANES]
                for r in range(dil):
                    for half, plane in enumerate(planes):
                        out[r, :, half * LANES:(half + 1) * LANES] = (
                            acc_ref[plane, pl.ds(r, tm // dil, stride=dil), :].astype(BF16))
        return run

    gates = [gate_job(ref, off, c) for ref, off in ((ga_ref, ga_off), (gb_ref, ga_off + d_model))
             for c in range(d_model // 256)]
    dilated = [(which, g) for g in range(N_GROUPS) if ATTN_CONFIGS[g][1] > 1 for which in range(3)]
    strided = [qkv_job(which, g, k) for k, (which, g) in enumerate(dilated)]
    plain = [qkv_job(which, g, 0) for g in range(N_GROUPS) if ATTN_CONFIGS[g][1] == 1 for which in range(3)]
    pools = [pool_job(c) for c in range(POOL_WIDTH // 256)]
    order = []
    while gates or strided:
        if gates:
            order.append(gates.pop(0))
        if strided:
            order.append(strided.pop(0))
    for job in order + plain + pools:
        job()


def _proj_call(x, w_in, b_in, b0, B, tm=1024):
    _, S, D = x.shape
    in_width = w_in.shape[1]
    assert in_width == 3 * ATTN_WIDTH + POOL_WIDTH + 2 * D
    assert S % tm == 0
    grid = (B, S // tm)
    qkv_shapes, qkv_specs = [], []
    for _ in range(3):
        for (_, dil) in ATTN_CONFIGS:
            assert tm % (dil * 16) == 0
            qkv_shapes.append(jax.ShapeDtypeStruct((B, dil, S // dil, GROUP_WIDTH), BF16))
            qkv_specs.append(pl.BlockSpec((None, dil, tm // dil, GROUP_WIDTH), lambda b, i: (b, 0, i, 0)))
    row_spec = lambda width: pl.BlockSpec((None, tm, width), lambda b, i: (b, i, 0))
    out_shape = qkv_shapes + [jax.ShapeDtypeStruct((B, S, POOL_WIDTH), F32),
                              jax.ShapeDtypeStruct((B, S, D), BF16),
                              jax.ShapeDtypeStruct((B, S, D), BF16)]
    out_specs = qkv_specs + [row_spec(POOL_WIDTH), row_spec(D), row_spec(D)]
    outs = pl.pallas_call(
        functools.partial(_proj_kernel, tm=tm, d_model=D),
        grid=grid,
        in_specs=[pl.BlockSpec((None, tm, D), lambda b, i: (b + b0, i, 0)),
                  pl.BlockSpec((D, in_width), lambda b, i: (0, 0), pipeline_mode=pl.Buffered(1)),
                  pl.BlockSpec((1, in_width), lambda b, i: (0, 0))],
        out_specs=out_specs,
        out_shape=out_shape,
        scratch_shapes=[pltpu.VMEM((tm, D), BF16),
                        pltpu.VMEM((3 * sum(dil > 1 for _, dil in ATTN_CONFIGS) * HALVES, tm, LANES), F32)],
        compiler_params=pltpu.CompilerParams(dimension_semantics=("parallel", "parallel"),
                                             vmem_limit_bytes=VMEM_LIMIT_BYTES),
        name="proj",
    )(x, w_in.astype(BF16), b_in.reshape(1, in_width))
    qkv = [o.reshape(B, S, GROUP_WIDTH) for o in outs[:9]]
    return qkv, outs[9], outs[10], outs[11]


def _t5_causal_bucket(dist):
    max_exact = N_REL_BUCKETS // 2
    is_small = dist < max_exact
    d = jnp.maximum(dist, 1).astype(F32)
    large = max_exact + (jnp.log(d / max_exact) / math.log(REL_MAX_DISTANCE / max_exact)
                         * (N_REL_BUCKETS - max_exact)).astype(jnp.int32)
    large = jnp.minimum(large, N_REL_BUCKETS - 1)
    return jnp.where(is_small, dist, large)


def _attn_bias(rel_bias_table):
    full, first = [], []
    for g, (window, dil) in enumerate(ATTN_CONFIGS):
        span = window // dil
        table = rel_bias_table[:, g * HEADS_PER_GROUP:(g + 1) * HEADS_PER_GROUP].astype(F32)
        lq = np.arange(BLOCK)
        for lk, dst in ((np.arange(-BLOCK, BLOCK), full), (lq, first)):
            step = jnp.asarray(lq[:, None] - lk[None, :], jnp.int32)
            in_window = (step >= 0) & (step <= span)
            bucket = _t5_causal_bucket(jnp.clip(step, 0, span) * dil)
            bias = jnp.einsum('qkb,bh->hqk', jax.nn.one_hot(bucket, N_REL_BUCKETS, dtype=F32), table,
                              precision=lax.Precision.HIGHEST)
            bias = jnp.where(in_window[None], bias, NEG_INF)
            dst.append(bias.reshape(HEADS_PER_GROUP * BLOCK, lk.shape[0]))
    return jnp.stack(full), jnp.stack(first)


def _run_skewed(chains, n_stages):
    for step in range(n_stages + len(chains) - 1):
        for lag, chain in enumerate(chains):
            if 0 <= step - lag < n_stages:
                next(chain)


ATTN_STAGES = 4


def _attn_kernel(*refs, seq, cast_per_step):
    qkv = refs[:9]
    if cast_per_step:
        (bias_ref, bias_first_ref, w32_g, w32_u, w32_d, out_ref, w16_g, w16_u, w16_d,
         o_scr, l_scr, in_g, in_u, in_d, cast_g, cast_u, cast_d, in_sems, out_sems) = refs[9:]
        w32 = (w32_g, w32_u, w32_d)
        w16 = (w16_g, w16_u, w16_d)
        stage_in = (in_g, in_u, in_d)
        stage_out = (cast_g, cast_u, cast_d)
        first_expert = pl.program_id(0) * cast_per_step

        def load(k):
            return [pltpu.make_async_copy(w32[a].at[first_expert + k], stage_in[a], in_sems.at[a])
                    for a in range(3)]

        def store(k):
            return [pltpu.make_async_copy(stage_out[a], w16[a].at[first_expert + k], out_sems.at[a])
                    for a in range(3)]

        def cast_boundary(k):
            if k > cast_per_step:
                return
            if k > 0:
                for copy in load(k - 1):
                    copy.wait()
                if k > 1:
                    for copy in store(k - 2):
                        copy.wait()
                for a in range(3):
                    stage_out[a][...] = stage_in[a][...].astype(BF16)
                for copy in store(k - 1):
                    copy.start()
            if k < cast_per_step:
                for copy in load(k):
                    copy.start()
            else:
                for copy in store(k - 1):
                    copy.wait()
    else:
        bias_ref, bias_first_ref, out_ref, o_scr, l_scr = refs[9:]

        def cast_boundary(k):
            del k

    rows = HEADS_PER_GROUP * BLOCK
    row_head = lax.broadcasted_iota(jnp.int32, (rows, GROUP_WIDTH), 0) // BLOCK
    lane_head_r = lax.broadcasted_iota(jnp.int32, (rows, GROUP_WIDTH), 1) // HEAD_DIM
    head_mask = row_head == lane_head_r
    heads_per_half = LANES // HEAD_DIM
    lane_head = lax.broadcasted_iota(jnp.int32, (BLOCK, LANES), 1) // HEAD_DIM

    def heads_to_lanes(per_head):
        out = per_head[0]
        for h in range(1, heads_per_half):
            out = jnp.where(lane_head == h, per_head[h], out)
        return out

    for g, (_, dil) in enumerate(ATTN_CONFIGS):
        cast_boundary(g)
        q_ref, k_ref, v_ref = qkv[3 * g:3 * g + 3]
        sub_len = seq // dil
        n_blocks = sub_len // BLOCK

        def block(r, n, first, g=g, dil=dil, q_ref=q_ref, k_ref=k_ref, v_ref=v_ref, sub_len=sub_len):
            base = pl.multiple_of(r * sub_len + n * BLOCK, BLOCK)
            qb = q_ref[pl.ds(base, BLOCK), :]
            if first:
                kk = k_ref[pl.ds(base, BLOCK), :]
                vv = v_ref[pl.ds(base, BLOCK), :]
                bias = bias_first_ref[g]
            else:
                kbase = pl.multiple_of(base - BLOCK, BLOCK)
                kk = k_ref[pl.ds(kbase, 2 * BLOCK), :]
                vv = v_ref[pl.ds(kbase, 2 * BLOCK), :]
                bias = bias_ref[g]
            qs = jnp.where(head_mask, jnp.concatenate([qb] * HEADS_PER_GROUP, axis=0), jnp.zeros((), BF16))
            logits = lax.dot_general(qs, kk, (((1,), (1,)), ((), ())), preferred_element_type=F32) + bias
            yield
            m = jnp.max(logits, axis=1, keepdims=True)
            p = jnp.exp(logits - m)
            s = jnp.sum(p, axis=1, keepdims=True)
            p = p.astype(BF16)
            yield
            pv = jnp.dot(p, vv, preferred_element_type=F32)
            yield
            inv_s = 1.0 / s
            lse = m + jnp.log(s)
            start = n * (BLOCK * dil) + r
            if dil == 1:
                dst = pl.ds(pl.multiple_of(start, BLOCK), BLOCK)
            else:
                dst = pl.ds(start, BLOCK, stride=dil)
            for half in range(HALVES):
                cols = slice(half * LANES, (half + 1) * LANES)
                o_heads, lse_heads = [], []
                for h in range(half * heads_per_half, (half + 1) * heads_per_half):
                    head_rows = slice(h * BLOCK, (h + 1) * BLOCK)
                    o_heads.append(pv[head_rows, cols] * inv_s[head_rows])
                    lse_heads.append(jnp.broadcast_to(lse[head_rows], (BLOCK, LANES)))
                o_scr[g * HALVES + half, dst, :] = heads_to_lanes(o_heads)
                l_scr[g * HALVES + half, dst, :] = heads_to_lanes(lse_heads)
            yield

        if n_blocks > 1:
            group = 3 if (n_blocks - 1) % 3 == 0 else 1

            _run_skewed([block(r, 0, True) for r in range(dil)], ATTN_STAGES)

            def per_subsequence(r, carry, block=block, n_blocks=n_blocks, group=group):
                def per_group(i, c):
                    _run_skewed([block(r, 1 + i * group + k, False) for k in range(group)], ATTN_STAGES)
                    return c
                lax.fori_loop(0, (n_blocks - 1) // group, per_group, 0)
                return carry
            lax.fori_loop(0, dil, per_subsequence, 0)
        else:
            group = next(c for c in (8, 4, 2, 1) if dil % c == 0)

            def per_group(i, carry, block=block, group=group):
                _run_skewed([block(i * group + k, 0, True) for k in range(group)], ATTN_STAGES)
                return carry
            lax.fori_loop(0, dil // group, per_group, 0)

    cast_boundary(N_GROUPS)
    chunk = 256

    def merge(i, carry):
        sl = pl.ds(pl.multiple_of(i * chunk, chunk), chunk)
        for half in range(HALVES):
            ls = [l_scr[g * HALVES + half, sl, :] for g in range(N_GROUPS)]
            m = functools.reduce(jnp.maximum, ls)
            es = [jnp.exp(l - m) for l in ls]
            den = functools.reduce(lambda a, b: a + b, es)
            num = functools.reduce(lambda a, b: a + b,
                                   [e * o_scr[g * HALVES + half, sl, :] for g, e in enumerate(es)])
            out_ref[sl, half * LANES:(half + 1) * LANES] = (num / den).astype(BF16)
        return carry

    lax.fori_loop(0, seq // chunk, merge, 0)
    cast_boundary(N_GROUPS + 1)


def _attn_call(qkv, rel_bias_table, expert_weights=None):
    B, S, _ = qkv[0].shape
    for (_, dil) in ATTN_CONFIGS:
        assert S % (dil * BLOCK) == 0
    bias, bias_first = _attn_bias(rel_bias_table)
    seq_spec = pl.BlockSpec((None, S, GROUP_WIDTH), lambda b: (b, 0, 0))
    ordered = []
    for g in range(N_GROUPS):
        ordered += [qkv[g], qkv[3 + g], qkv[6 + g]]
    operands = ordered + [bias, bias_first]
    in_specs = [seq_spec] * 9 + [pl.BlockSpec(bias.shape, lambda b: (0, 0, 0)),
                                 pl.BlockSpec(bias_first.shape, lambda b: (0, 0, 0))]
    out_specs = [seq_spec]
    out_shape = [jax.ShapeDtypeStruct((B, S, GROUP_WIDTH), BF16)]
    scratch_shapes = [pltpu.VMEM((N_GROUPS * HALVES, S, LANES), F32),
                      pltpu.VMEM((N_GROUPS * HALVES, S, LANES), F32)]
    cast_per_step = 0
    if expert_weights is not None:
        n_experts = expert_weights[0].shape[0]
        assert n_experts % B == 0 and n_experts // B <= N_GROUPS + 1
        cast_per_step = n_experts // B
        operands += list(expert_weights)
        in_specs += [pl.BlockSpec(memory_space=pl.ANY)] * 3
        out_specs += [pl.BlockSpec(memory_space=pl.ANY)] * 3
        out_shape += [jax.ShapeDtypeStruct(w.shape, BF16) for w in expert_weights]
        scratch_shapes += ([pltpu.VMEM(w.shape[1:], F32) for w in expert_weights]
                           + [pltpu.VMEM(w.shape[1:], BF16) for w in expert_weights]
                           + [pltpu.SemaphoreType.DMA((3,)), pltpu.SemaphoreType.DMA((3,))])
    outs = pl.pallas_call(
        functools.partial(_attn_kernel, seq=S, cast_per_step=cast_per_step),
        grid=(B,),
        in_specs=in_specs,
        out_specs=out_specs,
        out_shape=out_shape,
        scratch_shapes=scratch_shapes,
        compiler_params=pltpu.CompilerParams(dimension_semantics=("arbitrary",),
                                             vmem_limit_bytes=VMEM_LIMIT_BYTES),
        name="attn",
    )(*operands)
    return outs[0], (tuple(outs[1:]) if expert_weights is not None else None)


PACK_CHUNKS = 4
SUBLANES = 8
ROUTER_LANES = LANES // 2


def _pack_bf16_pairs(rounded):
    w = rounded.shape[1] // 2
    bits = lax.bitcast_convert_type(rounded, jnp.int32)
    return bits[:, :w] | lax.shift_right_logical(bits[:, w:], jnp.full((), 16, jnp.int32))


def _unpack_bf16_pairs(words):
    hi = lax.bitcast_convert_type(words & jnp.int32(-65536), F32).astype(BF16)
    lo = lax.bitcast_convert_type(lax.shift_left(words, jnp.full((), 16, jnp.int32)), F32).astype(BF16)
    return jnp.concatenate([hi, lo], axis=1)


def _fold_pool_kernel(wp_ref, scale_ref, pb_ref, out_ref):
    a = wp_ref[...] * scale_ref[...]
    b = pb_ref[...]
    a_hi = a.astype(BF16)
    a_lo = (a - a_hi.astype(F32)).astype(BF16)
    b_hi = b.astype(BF16)
    b_lo = (b - b_hi.astype(F32)).astype(BF16)
    out_ref[...] = (jnp.dot(a_hi, b_hi, preferred_element_type=F32)
                    + jnp.dot(a_lo, b_hi, preferred_element_type=F32)
                    + jnp.dot(a_hi, b_lo, preferred_element_type=F32)).astype(BF16)


def _fold_pool_call(w_pool, pool_scale, w_proj_pool):
    n_groups, gd, _ = w_pool.shape
    D = w_proj_pool.shape[1]
    return pl.pallas_call(
        _fold_pool_kernel,
        grid=(n_groups,),
        in_specs=[pl.BlockSpec((None, gd, gd), lambda g: (g, 0, 0)),
                  pl.BlockSpec((None, 1, gd), lambda g: (g, 0, 0)),
                  pl.BlockSpec((gd, D), lambda g: (g, 0))],
        out_specs=pl.BlockSpec((gd, D), lambda g: (g, 0)),
        out_shape=jax.ShapeDtypeStruct((n_groups * gd, D), BF16),
        name="fold_pool",
    )(w_pool, pool_scale.reshape(n_groups, 1, gd), w_proj_pool)


def _post_kernel(a_ref, u_ref, halo_ref, ga_ref, gb_ref, x_ref,
                 pa_ref, pb_ref, wout_ref, g1_ref, b1_ref,
                 wr_cat_ref, br_ref,
                 h_ref, hb_ref, ids_ref, wts_ref, cnt_ref, pool_scr, tmp_scr, *, tm, sub, alpha):
    i = pl.program_id(1)
    halo = halo_ref[...]
    pool_scr[0:POOL_HALO, :] = jnp.zeros_like(halo)
    pool_scr[POOL_HALO:2 * POOL_HALO, :] = jnp.where(i > 0, halo, jnp.zeros_like(halo))
    pool_scr[2 * POOL_HALO:, :] = u_ref[...]
    tmp_scr[:, 0:POOL_HALO, :] = jnp.zeros((tmp_scr.shape[0], POOL_HALO, POOL_GROUP_DIM), F32)
    head_pos = i * tm + lax.broadcasted_iota(jnp.int32, (POOL_HALO, POOL_GROUP_DIM), 0)

    def rows_chain(r0):
        rs = pl.ds(r0, sub)
        group_cols = [slice(gi * POOL_GROUP_DIM, (gi + 1) * POOL_GROUP_DIM) for gi in range(len(POOL_SIZES))]
        diffs = []
        ext = sub + 2 * POOL_HALO
        for gi, (cols, w) in enumerate(zip(group_cols, POOL_SIZES)):
            ug = u_ref[rs, cols]
            levels = w.bit_length() - 1
            for level in range(levels):
                shift = 1 << level
                dst = tmp_scr.at[(r0 // sub * len(POOL_SIZES) + gi) * 2 + level % 2]
                if level == 0:
                    cur = pool_scr[pl.ds(r0 + POOL_HALO, ext - POOL_HALO), cols]
                    back = pool_scr[pl.ds(r0 + POOL_HALO - shift, ext - POOL_HALO), cols]
                else:
                    src = tmp_scr.at[(r0 // sub * len(POOL_SIZES) + gi) * 2 + (level - 1) % 2]
                    cur = src[pl.ds(POOL_HALO, ext - POOL_HALO), :]
                    back = src[pl.ds(POOL_HALO - shift, ext - POOL_HALO), :]
                if level == levels - 1:
                    acc = (cur + back)[POOL_HALO:]
                else:
                    dst[pl.ds(POOL_HALO, ext - POOL_HALO), :] = cur + back
            inv_count = jnp.full((sub, POOL_GROUP_DIM), 1.0 / w, F32)
            if r0 == 0:
                inv_count = jnp.concatenate([1.0 / jnp.minimum(head_pos + 1, w).astype(F32),
                                             inv_count[POOL_HALO:]], axis=0)
            diffs.append((acc * inv_count - ug).astype(BF16))
        yield
        y_pool = jnp.dot(jnp.concatenate(diffs, axis=1), pb_ref[...], preferred_element_type=F32)
        y_attn = jnp.dot(a_ref[rs, :], pa_ref[...], preferred_element_type=F32)
        yield
        mixed = ga_ref[rs, :] * y_attn.astype(BF16) + gb_ref[rs, :] * y_pool.astype(BF16)
        yield
        y = jnp.dot(mixed, wout_ref[...], preferred_element_type=F32)
        yield
        h = _layer_norm(alpha * x_ref[rs, :] + y, g1_ref[...], b1_ref[...])
        h_ref[rs, :] = h
        h_hi = h.astype(BF16)
        h_rounded = h_hi.astype(F32)
        packed = _pack_bf16_pairs(h_rounded)
        for c in range(PACK_CHUNKS):
            hb_ref[c, rs, :] = packed[:, c * LANES:(c + 1) * LANES]

        h_lo = (h - h_rounded).astype(BF16)
        yield
        nt = (((1,), (1,)), ((), ()))
        both = lax.dot_general(wr_cat_ref[...], h_hi, nt, preferred_element_type=F32)
        lo_hi = lax.dot_general(wr_cat_ref[0:ROUTER_LANES, :], h_lo, nt, preferred_element_type=F32)
        logits = both[0:ROUTER_LANES] + both[ROUTER_LANES:] + lo_hi + br_ref[...]
        yield
        row = lax.broadcasted_iota(jnp.int32, logits.shape, 0)
        big = jnp.int32(2 ** 30)
        is_group = row < N_EXPERT_GROUPS
        gl = jnp.where(is_group, logits, -jnp.inf)
        gmax = jnp.max(gl, axis=0, keepdims=True)
        g_idx = jnp.min(jnp.where(gl == gmax, row, big), axis=0, keepdims=True)
        g_prob = 1.0 / jnp.sum(jnp.exp(gl - gmax), axis=0, keepdims=True)
        expert = row - N_EXPERT_GROUPS
        in_group = (expert >= g_idx * EXPERTS_PER_GROUP) & (expert < (g_idx + 1) * EXPERTS_PER_GROUP)
        el = jnp.where(in_group, logits, -jnp.inf)
        v1 = jnp.max(el, axis=0, keepdims=True)
        i1 = jnp.min(jnp.where(el == v1, expert, big), axis=0, keepdims=True)
        el2 = jnp.where(expert == i1, -jnp.inf, el)
        v2 = jnp.max(el2, axis=0, keepdims=True)
        i2 = jnp.min(jnp.where(el2 == v2, expert, big), axis=0, keepdims=True)
        e2 = jnp.exp(v2 - v1)
        den = 1.0 + e2
        first_row = lax.broadcasted_iota(jnp.int32, (SUBLANES, sub), 0) == 0
        wts_ref[:, rs] = jnp.where(first_row, 1.0 / den * g_prob, e2 / den * g_prob)
        ids_ref[:, rs] = jnp.where(first_row, i1, i2).astype(F32)
        chosen = jnp.where((expert == i1) | (expert == i2), 1.0, 0.0)
        per_expert = jnp.broadcast_to(jnp.sum(chosen, axis=1, keepdims=True), (ROUTER_LANES, LANES))
        to_lane = (lax.broadcasted_iota(jnp.int32, (ROUTER_LANES, LANES), 0) - N_EXPERT_GROUPS
                   == lax.broadcasted_iota(jnp.int32, (ROUTER_LANES, LANES), 1))
        counts.append(jnp.sum(jnp.where(to_lane, per_expert, 0.0), axis=0, keepdims=True))
        yield

    counts = []
    _run_skewed([rows_chain(r0) for r0 in range(0, tm, sub)], 7)
    cnt_ref[...] = jnp.broadcast_to(functools.reduce(lambda a, b: a + b, counts), cnt_ref.shape)


def _post_call(a, u, ga, gb, x, w_proj_attn, w_pool, pool_scale, w_proj_pool, w_out, gamma, beta,
               w_router_group, b_router_group, w_router_expert, b_router_expert, alpha, b0, tm=1024, n_sub=2):
    B = a.shape[0]
    _, S, D = x.shape
    assert S % tm == 0 and tm % POOL_HALO == 0
    n_logits = N_EXPERT_GROUPS + N_EXPERTS
    assert n_logits <= ROUTER_LANES
    wr = jnp.concatenate([w_router_group, w_router_expert], axis=1).T
    wr = jnp.pad(wr, ((0, ROUTER_LANES - n_logits), (0, 0)))
    wr_hi = wr.astype(BF16)
    wr_lo = (wr - wr_hi.astype(F32)).astype(BF16)
    wr_cat = jnp.concatenate([wr_hi, wr_lo], axis=0)
    br = jnp.pad(jnp.concatenate([b_router_group, b_router_expert]),
                 (0, ROUTER_LANES - n_logits)).reshape(ROUTER_LANES, 1)
    n_tiles = B * (S // tm)
    row_spec = lambda width: pl.BlockSpec((None, tm, width), lambda b, i: (b, i, 0))
    full = lambda arr: pl.BlockSpec(arr.shape, lambda b, i: (0,) * arr.ndim)
    halo_blocks = tm // POOL_HALO
    halo_spec = pl.BlockSpec((None, POOL_HALO, POOL_WIDTH),
                             lambda b, i: (b, jnp.maximum(i * halo_blocks - 1, 0), 0))
    weights = [w_proj_attn.astype(BF16), _fold_pool_call(w_pool, pool_scale, w_proj_pool),
               w_out.astype(BF16), gamma.reshape(1, D), beta.reshape(1, D), wr_cat, br]
    return pl.pallas_call(
        functools.partial(_post_kernel, tm=tm, sub=tm // n_sub, alpha=alpha),
        grid=(B, S // tm),
        in_specs=[row_spec(GROUP_WIDTH), row_spec(POOL_WIDTH), halo_spec, row_spec(D), row_spec(D),
                  pl.BlockSpec((None, tm, D), lambda b, i: (b + b0, i, 0))] + [full(w) for w in weights],
        out_specs=[row_spec(D),
                   pl.BlockSpec((PACK_CHUNKS, tm, LANES), lambda b, i: (0, b * (S // tm) + i, 0)),
                   pl.BlockSpec((SUBLANES, tm), lambda b, i: (0, b * (S // tm) + i)),
                   pl.BlockSpec((SUBLANES, tm), lambda b, i: (0, b * (S // tm) + i)),
                   pl.BlockSpec((SUBLANES, LANES), lambda b, i: (b * (S // tm) + i, 0))],
        out_shape=[jax.ShapeDtypeStruct((B, S, D), F32),
                   jax.ShapeDtypeStruct((PACK_CHUNKS, B * S, LANES), jnp.int32),
                   jax.ShapeDtypeStruct((SUBLANES, B * S), F32), jax.ShapeDtypeStruct((SUBLANES, B * S), F32),
                   jax.ShapeDtypeStruct((n_tiles * SUBLANES, LANES), F32)],
        scratch_shapes=[pltpu.VMEM((tm + 2 * POOL_HALO, POOL_WIDTH), F32),
                        pltpu.VMEM((n_sub * len(POOL_SIZES) * 2, tm // n_sub + 2 * POOL_HALO, POOL_GROUP_DIM), F32)],
        compiler_params=pltpu.CompilerParams(dimension_semantics=("parallel", "parallel"),
                                             vmem_limit_bytes=VMEM_LIMIT_BYTES),
        name="post",
    )(a, u, u, ga, gb, x, *weights)


def _rank_kernel(ids_ref, cnt_ref, pos_ref, ends_ref, run_ref, start_ref, earlier_ref, *, tm, tile_rows):
    i = pl.program_id(0)

    @pl.when(i == 0)
    def _():
        total = jnp.sum(cnt_ref[...], axis=0, keepdims=True) / SUBLANES
        padded = jnp.broadcast_to(jnp.ceil(total / tile_rows) * tile_rows, ends_ref.shape)
        lane1 = lax.broadcasted_iota(jnp.int32, padded.shape, 1)
        incl = padded
        shift = 1
        while shift < LANES:
            incl = incl + jnp.where(lane1 >= shift, pltpu.roll(incl, shift, axis=1), 0.0)
            shift *= 2
        ends_ref[...] = incl
        start_row = jnp.broadcast_to((incl - padded)[0:1, :], start_ref.shape)
        on_diagonal = (lax.broadcasted_iota(jnp.int32, start_ref.shape, 0)
                       == lax.broadcasted_iota(jnp.int32, start_ref.shape, 1))
        start_col = jnp.sum(jnp.where(on_diagonal, start_row, 0.0), axis=1, keepdims=True)
        start_ref[...] = jnp.broadcast_to(start_col, start_ref.shape)
        run_ref[...] = jnp.zeros_like(run_ref)
        row = lax.broadcasted_iota(jnp.int32, (tm, tm), 0)
        col = lax.broadcasted_iota(jnp.int32, (tm, tm), 1)
        earlier_ref[...] = jnp.where(row < col, 1.0, 0.0).astype(BF16)

    ids = ids_ref[...]
    expert = lax.broadcasted_iota(jnp.int32, (N_EXPERTS, tm), 0).astype(F32)
    oh0 = expert == ids[0:1, :]
    oh1 = expert == ids[1:2, :]
    onehot = jnp.where(oh0 | oh1, 1.0, 0.0)
    before = jnp.dot(onehot.astype(BF16), earlier_ref[...], preferred_element_type=F32)
    slot = start_ref[:, 0:1] + run_ref[:, 0:1] + before
    p0 = jnp.sum(jnp.where(oh0, slot, 0.0), axis=0, keepdims=True)
    p1 = jnp.sum(jnp.where(oh1, slot, 0.0), axis=0, keepdims=True)
    first_row = lax.broadcasted_iota(jnp.int32, pos_ref.shape, 0) == 0
    pos_ref[...] = jnp.where(first_row, p0, p1).astype(jnp.int32)
    run_ref[...] += jnp.broadcast_to(jnp.sum(onehot, axis=1, keepdims=True), run_ref.shape)


def _rank_call(ids_t, counts, tile_rows, tm=1024):
    N = ids_t.shape[1]
    assert N % tm == 0 and SUBLANES * (2 * N + N_EXPERTS * tile_rows) < 2 ** 24
    assert N_EXPERTS <= LANES
    return pl.pallas_call(
        functools.partial(_rank_kernel, tm=tm, tile_rows=tile_rows),
        grid=(N // tm,),
        in_specs=[pl.BlockSpec((SUBLANES, tm), lambda i: (0, i)),
                  pl.BlockSpec(counts.shape, lambda i: (0, 0))],
        out_specs=[pl.BlockSpec((SUBLANES, tm), lambda i: (0, i)),
                   pl.BlockSpec((SUBLANES, LANES), lambda i: (0, 0))],
        out_shape=[jax.ShapeDtypeStruct((SUBLANES, N), jnp.int32),
                   jax.ShapeDtypeStruct((SUBLANES, LANES), F32)],
        scratch_shapes=[pltpu.VMEM((N_EXPERTS, LANES), F32), pltpu.VMEM((N_EXPERTS, LANES), F32),
                        pltpu.VMEM((tm, tm), BF16)],
        compiler_params=pltpu.CompilerParams(dimension_semantics=("arbitrary",),
                                             vmem_limit_bytes=VMEM_LIMIT_BYTES),
        name="rank",
    )(ids_t, counts)


SC_CORES = 2
SC_SUBCORES = 16
SC_WORKERS = SC_CORES * SC_SUBCORES
SC_CHUNK = 128


def _sc_mesh():
    return plsc.VectorSubcoreMesh(core_axis_name="c", subcore_axis_name="s",
                                  num_cores=SC_CORES, num_subcores=SC_SUBCORES)


def _sc_dispatch(packed, pos_t, n_rows):
    n_chunks, n_tok, width = packed.shape
    per_worker = n_tok // SC_WORKERS
    assert n_tok % (SC_WORKERS * SC_CHUNK) == 0

    @functools.partial(
        pl.kernel, mesh=_sc_mesh(),
        out_type=jax.ShapeDtypeStruct((n_chunks, n_rows, width), packed.dtype),
        scratch_types=[pltpu.VMEM((SC_CHUNK,), jnp.int32), pltpu.VMEM((SC_CHUNK,), jnp.int32),
                       pltpu.VMEM((n_chunks, SC_CHUNK, width), packed.dtype)]
                      + [pltpu.SemaphoreType.DMA] * (3 * n_chunks),
        name="sc_dispatch")
    def run(packed_hbm, pos_hbm, out_hbm, idx0, idx1, bufs, *sems):
        load_sems, sems0, sems1 = sems[:n_chunks], sems[n_chunks:2 * n_chunks], sems[2 * n_chunks:]
        worker = lax.axis_index("s") * SC_CORES + lax.axis_index("c")

        @pl.loop(0, per_worker // SC_CHUNK)
        def _(j):
            base = worker * per_worker + j * SC_CHUNK
            loads = [pltpu.async_copy(packed_hbm.at[c, pl.ds(base, SC_CHUNK)], bufs.at[c], load_sems[c])
                     for c in range(n_chunks)]
            pltpu.sync_copy(pos_hbm.at[0, pl.ds(base, SC_CHUNK)], idx0)
            pltpu.sync_copy(pos_hbm.at[1, pl.ds(base, SC_CHUNK)], idx1)
            scatters = []
            for c in range(n_chunks):
                loads[c].wait()
                scatters.append(pltpu.async_copy(bufs.at[c], out_hbm.at[c].at[idx0], sems0[c]))
                scatters.append(pltpu.async_copy(bufs.at[c], out_hbm.at[c].at[idx1], sems1[c]))
            for s in scatters:
                s.wait()

    return run(packed, pos_t)


def _sc_combine(sorted_rows, pos_t):
    n_chunks, _, width = sorted_rows.shape
    n_tok = pos_t.shape[1]
    per_worker = n_tok // SC_WORKERS
    assert n_tok % (SC_WORKERS * SC_CHUNK) == 0

    @functools.partial(
        pl.kernel, mesh=_sc_mesh(),
        out_type=jax.ShapeDtypeStruct((2, n_chunks, n_tok, width), sorted_rows.dtype),
        scratch_types=[pltpu.VMEM((SC_CHUNK,), jnp.int32),
                       pltpu.VMEM((n_chunks, SC_CHUNK, width), sorted_rows.dtype)]
                      + [pltpu.SemaphoreType.DMA] * (2 * n_chunks),
        name="sc_combine")
    def run(rows_hbm, pos_hbm, out_hbm, idx, bufs, *sems):
        gather_sems, write_sems = sems[:n_chunks], sems[n_chunks:]
        worker = lax.axis_index("s") * SC_CORES + lax.axis_index("c")

        @pl.loop(0, per_worker // SC_CHUNK)
        def _(j):
            base = worker * per_worker + j * SC_CHUNK
            for k in range(2):
                pltpu.sync_copy(pos_hbm.at[k, pl.ds(base, SC_CHUNK)], idx)
                gathers = [pltpu.async_copy(rows_hbm.at[c].at[idx], bufs.at[c], gather_sems[c])
                           for c in range(n_chunks)]
                writes = []
                for c in range(n_chunks):
                    gathers[c].wait()
                    writes.append(pltpu.async_copy(bufs.at[c], out_hbm.at[k, c, pl.ds(base, SC_CHUNK)],
                                                   write_sems[c]))
                for w in writes:
                    w.wait()

    return run(sorted_rows, pos_t)


EXPERT_RING = 3


def _expert_kernel(tile_expert_ref, n_used_ref, first_ref, wslot_ref,
                   xs_hbm, wg_hbm, wu_hbm, wd_hbm, ys_ref,
                   x_buf, wg_buf, wu_buf, wd_buf, x_sems, w_sems, *, tile_rows):
    s = pl.program_id(0)
    n_used = n_used_ref[0]

    def x_copy(t):
        slot = t % EXPERT_RING
        return pltpu.make_async_copy(xs_hbm.at[:, pl.ds(pl.multiple_of(t * tile_rows, tile_rows), tile_rows), :],
                                     x_buf.at[slot], x_sems.at[slot])

    def w_copies(t):
        e, slot = tile_expert_ref[t], wslot_ref[t]
        return [pltpu.make_async_copy(hbm.at[e], buf.at[slot], w_sems.at[a, slot])
                for a, (hbm, buf) in enumerate(((wg_hbm, wg_buf), (wu_hbm, wu_buf), (wd_hbm, wd_buf)))]

    def start_tile(t):
        x_copy(t).start()

        @pl.when(first_ref[t] == 1)
        def _():
            for copy in w_copies(t):
                copy.start()

    @pl.when(s == 0)
    def _():
        for t in range(EXPERT_RING - 1):
            @pl.when(t < n_used)
            def _():
                start_tile(t)

    @pl.when(s + (EXPERT_RING - 1) < n_used)
    def _():
        start_tile(s + (EXPERT_RING - 1))

    @pl.when(s < n_used)
    def _():
        x_copy(s).wait()

        @pl.when(first_ref[s] == 1)
        def _():
            for copy in w_copies(s):
                copy.wait()

        x_slot = s % EXPERT_RING
        w_slot = wslot_ref[s]
        wg = wg_buf[w_slot]
        wu = wu_buf[w_slot]
        wd = wd_buf[w_slot]
        sub = tile_rows // 2

        def rows_chain(r0):
            rs = pl.ds(r0, sub)
            x = _unpack_bf16_pairs(jnp.concatenate([x_buf[x_slot, c, rs, :] for c in range(PACK_CHUNKS)], axis=1))
            yield
            gate = jnp.dot(x, wg, preferred_element_type=F32)
            up = jnp.dot(x, wu, preferred_element_type=F32)
            yield
            hidden = (jax.nn.silu(gate) * up).astype(BF16)
            yield
            y = jnp.dot(hidden, wd, preferred_element_type=F32)
            yield
            y = _pack_bf16_pairs(y.astype(BF16).astype(F32))
            for c in range(PACK_CHUNKS):
                ys_ref[c, rs, :] = y[:, c * LANES:(c + 1) * LANES]
            yield

        chains = [rows_chain(r0) for r0 in range(0, tile_rows, sub)]
        n_stages = 5
        for step in range(n_stages + len(chains) - 1):
            for lag, chain in enumerate(chains):
                if 0 <= step - lag < n_stages:
                    next(chain)


def _expert_call(xs, tile_expert, n_used, w_gate, w_up, w_down, tile_rows):
    _, n_rows, _ = xs.shape
    D = w_gate.shape[1]
    assert w_gate.dtype == BF16 and w_up.dtype == BF16 and w_down.dtype == BF16
    first = jnp.concatenate([jnp.ones((1,), jnp.int32),
                             (tile_expert[1:] != tile_expert[:-1]).astype(jnp.int32)])
    wslot = ((jnp.cumsum(first) - 1) % EXPERT_RING).astype(jnp.int32)
    row_block = pl.BlockSpec((PACK_CHUNKS, tile_rows, LANES),
                             lambda i, te, nu, fi, ws: (0, jnp.minimum(i, nu[0] - 1), 0))
    hbm = pl.BlockSpec(memory_space=pl.ANY)
    return pl.pallas_call(
        functools.partial(_expert_kernel, tile_rows=tile_rows),
        grid_spec=pltpu.PrefetchScalarGridSpec(
            num_scalar_prefetch=4,
            grid=(n_rows // tile_rows,),
            in_specs=[hbm, hbm, hbm, hbm],
            out_specs=row_block,
            scratch_shapes=[pltpu.VMEM((EXPERT_RING, PACK_CHUNKS, tile_rows, LANES), xs.dtype),
                            pltpu.VMEM((EXPERT_RING, D, D_EXPERT), BF16),
                            pltpu.VMEM((EXPERT_RING, D, D_EXPERT), BF16),
                            pltpu.VMEM((EXPERT_RING, D_EXPERT, D), BF16),
                            pltpu.SemaphoreType.DMA((EXPERT_RING,)),
                            pltpu.SemaphoreType.DMA((3, EXPERT_RING))]),
        out_shape=jax.ShapeDtypeStruct(xs.shape, xs.dtype),
        compiler_params=pltpu.CompilerParams(dimension_semantics=("arbitrary",),
                                             vmem_limit_bytes=VMEM_LIMIT_BYTES),
        name="experts",
    )(tile_expert, n_used, first, wslot, xs, w_gate, w_up, w_down)


def _final_kernel(h_ref, y_ref, wts_ref, g2_ref, b2_ref, *rest, alpha):
    out_ref = rest[-1]
    tm = h_ref.shape[0]
    on_diagonal = (lax.broadcasted_iota(jnp.int32, (tm, tm), 0)
                   == lax.broadcasted_iota(jnp.int32, (tm, tm), 1))
    z = alpha * h_ref[...]
    for k in range(2):
        w_col = jnp.sum(jnp.where(on_diagonal, jnp.broadcast_to(wts_ref[k:k + 1, :], (tm, tm)), 0.0),
                        axis=1, keepdims=True)
        yk = _unpack_bf16_pairs(jnp.concatenate([y_ref[k, c] for c in range(PACK_CHUNKS)], axis=1))
        z = z + w_col * yk.astype(F32)
    out_ref[...] = _layer_norm(z, g2_ref[...], b2_ref[...])


def _final_call(h, y, wts, gamma, beta, alpha, row0, n_total, earlier_out, tm=1024):
    N, D = h.shape
    assert N % tm == 0 and row0 % tm == 0
    operands = [h, y, wts, gamma.reshape(1, D), beta.reshape(1, D)]
    in_specs = [pl.BlockSpec((tm, D), lambda i: (i, 0)),
                pl.BlockSpec((2, PACK_CHUNKS, tm, LANES), lambda i: (0, 0, i, 0)),
                pl.BlockSpec((SUBLANES, tm), lambda i: (0, i)),
                pl.BlockSpec((1, D), lambda i: (0, 0)),
                pl.BlockSpec((1, D), lambda i: (0, 0))]
    aliases = {}
    if earlier_out is not None:
        aliases = {len(operands): 0}
        operands.append(earlier_out)
        in_specs.append(pl.BlockSpec(memory_space=pl.ANY))
    return pl.pallas_call(
        functools.partial(_final_kernel, alpha=alpha),
        grid=(N // tm,),
        in_specs=in_specs,
        out_specs=pl.BlockSpec((tm, D), lambda i: (i + row0 // tm, 0)),
        out_shape=jax.ShapeDtypeStruct((n_total, D), F32),
        input_output_aliases=aliases,
        compiler_params=pltpu.CompilerParams(dimension_semantics=("parallel",),
                                             vmem_limit_bytes=VMEM_LIMIT_BYTES),
        name="final",
    )(*operands)


def _moe(packed, ids_t, counts, w_gate, w_up, w_down, tile_rows=1024):
    N = ids_t.shape[1]
    pos_t, ends = _rank_call(ids_t, counts, tile_rows)
    n_tiles = 2 * N // tile_rows + N_EXPERTS
    seg_end = ends[0, :N_EXPERTS].astype(jnp.int32)
    tile_start = jnp.arange(n_tiles, dtype=jnp.int32) * tile_rows
    tile_expert = jnp.minimum(jnp.sum(seg_end[None, :] <= tile_start[:, None], axis=1),
                              N_EXPERTS - 1).astype(jnp.int32)
    n_used = (seg_end[N_EXPERTS - 1:] // tile_rows).astype(jnp.int32)
    xs = _sc_dispatch(packed, pos_t, n_tiles * tile_rows)
    ys = _expert_call(xs, tile_expert, n_used, w_gate, w_up, w_down, tile_rows)
    return _sc_combine(ys, pos_t)


@jax.jit
def kernel(x, w_in, b_in, rel_bias_table, w_pool, pool_scale, w_proj_attn, w_proj_pool, w_out, ln1_gamma, ln1_beta, w_router_group, b_router_group, w_router_expert, b_router_expert, w_expert_gate, w_expert_up, w_expert_down, ln2_gamma, ln2_beta):
    B, S, D = x.shape
    depth = w_in.shape[0]
    alpha = (2.0 * depth) ** 0.25
    n_parts = 2 if B % 2 == 0 else 1
    nb = B // n_parts
    for layer in range(depth):
        out = None
        experts32 = (w_expert_gate[layer], w_expert_up[layer], w_expert_down[layer])
        n_experts = experts32[0].shape[0]
        cast_in_attn = n_experts % nb == 0 and n_experts // nb <= N_GROUPS + 1
        experts16 = None if cast_in_attn else tuple(w.astype(BF16) for w in experts32)
        for part in range(n_parts):
            b0 = part * nb
            qkv, u, ga, gb = _proj_call(x, w_in[layer], b_in[layer], b0, nb)
            a, converted = _attn_call(qkv, rel_bias_table, experts32 if experts16 is None else None)
            if converted is not None:
                experts16 = converted
            h, packed, ids, wts, counts = _post_call(
                a, u, ga, gb, x, w_proj_attn[layer], w_pool[layer], pool_scale[layer],
                w_proj_pool[layer], w_out[layer], ln1_gamma[layer], ln1_beta[layer],
                w_router_group[layer], b_router_group[layer],
                w_router_expert[layer], b_router_expert[layer], alpha, b0)
            y = _moe(packed, ids, counts, *experts16)
            out = _final_call(h.reshape(nb * S, D), y, wts, ln2_gamma[layer],
                              ln2_beta[layer], alpha, b0 * S, B * S, out)
        x = out.reshape(B, S, D)
    return x
```

```python
import functools
import math

import jax
import jax.numpy as jnp
import numpy as np
from jax import lax
from jax.experimental import pallas as pl
from jax.experimental.pallas import tpu as pltpu
from jax.experimental.pallas import tpu_sc as plsc

F32 = jnp.float32
BF16 = jnp.bfloat16

HEAD_DIM = 64
ATTN_CONFIGS = ((128, 1), (512, 4), (2048, 16))
N_GROUPS = len(ATTN_CONFIGS)
HEADS_PER_GROUP = 4
GROUP_WIDTH = HEADS_PER_GROUP * HEAD_DIM
ATTN_WIDTH = N_GROUPS * GROUP_WIDTH
BLOCK = 128
N_REL_BUCKETS = 32
REL_MAX_DISTANCE = 2048
NEG_INF = -1e30

POOL_SIZES = (2, 4, 8, 16)
POOL_GROUP_DIM = 128
POOL_WIDTH = POOL_GROUP_DIM * len(POOL_SIZES)
POOL_HALO = 16

N_EXPERT_GROUPS = 4
EXPERTS_PER_GROUP = 8
N_EXPERTS = N_EXPERT_GROUPS * EXPERTS_PER_GROUP
D_EXPERT = 256
LN_EPS = 1e-5

VMEM_LIMIT_BYTES = 56 * 1024 * 1024
LANES = 128
HALVES = GROUP_WIDTH // LANES


def _layer_norm(z, gamma, beta):
    mu = jnp.mean(z, axis=-1, keepdims=True)
    zc = z - mu
    var = jnp.mean(zc * zc, axis=-1, keepdims=True)
    return zc * lax.rsqrt(var + LN_EPS) * gamma + beta


def _proj_kernel(x_ref, w_ref, b_ref, *refs, tm, d_model):
    qkv_refs = refs[:9]
    u_ref, ga_ref, gb_ref, xb_ref, acc_ref = refs[9:]
    xb_ref[...] = x_ref[...].astype(BF16)

    def chunk(c0, width):
        acc = jnp.dot(xb_ref[...], w_ref[:, c0:c0 + width], preferred_element_type=F32)
        return acc + b_ref[:, c0:c0 + width]

    pool_off = 3 * ATTN_WIDTH
    ga_off = pool_off + POOL_WIDTH

    def gate_job(gate_ref, off, c):
        def run():
            gate_ref[:, c * 256:(c + 1) * 256] = jax.nn.sigmoid(chunk(off + c * 256, 256)).astype(BF16)
        return run

    def pool_job(c):
        def run():
            u_ref[:, c * 256:(c + 1) * 256] = chunk(pool_off + c * 256, 256)
        return run

    def qkv_job(which, g, staged):
        def run():
            dil = ATTN_CONFIGS[g][1]
            out = qkv_refs[which * 3 + g]
            acc = chunk(which * ATTN_WIDTH + g * GROUP_WIDTH, GROUP_WIDTH)
            if which == 0:
                acc = acc * HEAD_DIM ** -0.5
            if dil == 1:
                out[0] = acc.astype(BF16)
            else:
                planes = [staged * HALVES + half for half in range(HALVES)]
                for half, plane in enumerate(planes):
                    acc_ref[plane] = acc[:, half * LANES:(half + 1) * LANES]
                for r in range(dil):
                    for half, plane in enumerate(planes):
                        out[r, :, half * LANES:(half + 1) * LANES] = (
                            acc_ref[plane, pl.ds(r, tm // dil, stride=dil), :].astype(BF16))
        return run

    gates = [gate_job(ref, off, c) for ref, off in ((ga_ref, ga_off), (gb_ref, ga_off + d_model))
             for c in range(d_model // 256)]
    dilated = [(which, g) for g in range(N_GROUPS) if ATTN_CONFIGS[g][1] > 1 for which in range(3)]
    strided = [qkv_job(which, g, k) for k, (which, g) in enumerate(dilated)]
    plain = [qkv_job(which, g, 0) for g in range(N_GROUPS) if ATTN_CONFIGS[g][1] == 1 for which in range(3)]
    pools = [pool_job(c) for c in range(POOL_WIDTH // 256)]
    order = []
    while gates or strided:
        if gates:
            order.append(gates.pop(0))
        if strided:
            order.append(strided.pop(0))
    for job in order + plain + pools:
        job()


def _proj_call(x, w_in, b_in, b0, B, tm=1024):
    _, S, D = x.shape
    in_width = w_in.shape[1]
    assert in_width == 3 * ATTN_WIDTH + POOL_WIDTH + 2 * D
    assert S % tm == 0
    grid = (B, S // tm)
    qkv_shapes, qkv_specs = [], []
    for _ in range(3):
        for (_, dil) in ATTN_CONFIGS:
            assert tm % (dil * 16) == 0
            qkv_shapes.append(jax.ShapeDtypeStruct((B, dil, S // dil, GROUP_WIDTH), BF16))
            qkv_specs.append(pl.BlockSpec((None, dil, tm // dil, GROUP_WIDTH), lambda b, i: (b, 0, i, 0)))
    row_spec = lambda width: pl.BlockSpec((None, tm, width), lambda b, i: (b, i, 0))
    out_shape = qkv_shapes + [jax.ShapeDtypeStruct((B, S, POOL_WIDTH), F32),
                              jax.ShapeDtypeStruct((B, S, D), BF16),
                              jax.ShapeDtypeStruct((B, S, D), BF16)]
    out_specs = qkv_specs + [row_spec(POOL_WIDTH), row_spec(D), row_spec(D)]
    outs = pl.pallas_call(
        functools.partial(_proj_kernel, tm=tm, d_model=D),
        grid=grid,
        in_specs=[pl.BlockSpec((None, tm, D), lambda b, i: (b + b0, i, 0)),
                  pl.BlockSpec((D, in_width), lambda b, i: (0, 0), pipeline_mode=pl.Buffered(1)),
                  pl.BlockSpec((1, in_width), lambda b, i: (0, 0))],
        out_specs=out_specs,
        out_shape=out_shape,
        scratch_shapes=[pltpu.VMEM((tm, D), BF16),
                        pltpu.VMEM((3 * sum(dil > 1 for _, dil in ATTN_CONFIGS) * HALVES, tm, LANES), F32)],
        compiler_params=pltpu.CompilerParams(dimension_semantics=("parallel", "parallel"),
                                             vmem_limit_bytes=VMEM_LIMIT_BYTES),
        name="proj",
    )(x, w_in.astype(BF16), b_in.reshape(1, in_width))
    qkv = [o.reshape(B, S, GROUP_WIDTH) for o in outs[:9]]
    return qkv, outs[9], outs[10], outs[11]


def _t5_causal_bucket(dist):
    max_exact = N_REL_BUCKETS // 2
    is_small = dist < max_exact
    d = jnp.maximum(dist, 1).astype(F32)
    large = max_exact + (jnp.log(d / max_exact) / math.log(REL_MAX_DISTANCE / max_exact)
                         * (N_REL_BUCKETS - max_exact)).astype(jnp.int32)
    large = jnp.minimum(large, N_REL_BUCKETS - 1)
    return jnp.where(is_small, dist, large)


def _attn_bias(rel_bias_table):
    full, first = [], []
    for g, (window, dil) in enumerate(ATTN_CONFIGS):
        span = window // dil
        table = rel_bias_table[:, g * HEADS_PER_GROUP:(g + 1) * HEADS_PER_GROUP].astype(F32)
        lq = np.arange(BLOCK)
        for lk, dst in ((np.arange(-BLOCK, BLOCK), full), (lq, first)):
            step = jnp.asarray(lq[:, None] - lk[None, :], jnp.int32)
            in_window = (step >= 0) & (step <= span)
            bucket = _t5_causal_bucket(jnp.clip(step, 0, span) * dil)
            bias = jnp.einsum('qkb,bh->hqk', jax.nn.one_hot(bucket, N_REL_BUCKETS, dtype=F32), table,
                              precision=lax.Precision.HIGHEST)
            bias = jnp.where(in_window[None], bias, NEG_INF)
            dst.append(bias.reshape(HEADS_PER_GROUP * BLOCK, lk.shape[0]))
    return jnp.stack(full), jnp.stack(first)


def _run_skewed(chains, n_stages):
    for step in range(n_stages + len(chains) - 1):
        for lag, chain in enumerate(chains):
            if 0 <= step - lag < n_stages:
                next(chain)


ATTN_STAGES = 4


def _attn_kernel(*refs, seq, cast_per_step):
    qkv = refs[:9]
    if cast_per_step:
        (bias_ref, bias_first_ref, w32_g, w32_u, w32_d, out_ref, w16_g, w16_u, w16_d,
         o_scr, l_scr, in_g, in_u, in_d, cast_g, cast_u, cast_d, in_sems, out_sems) = refs[9:]
        w32 = (w32_g, w32_u, w32_d)
        w16 = (w16_g, w16_u, w16_d)
        stage_in = (in_g, in_u, in_d)
        stage_out = (cast_g, cast_u, cast_d)
        first_expert = pl.program_id(0) * cast_per_step

        def load(k):
            return [pltpu.make_async_copy(w32[a].at[first_expert + k], stage_in[a], in_sems.at[a])
                    for a in range(3)]

        def store(k):
            return [pltpu.make_async_copy(stage_out[a], w16[a].at[first_expert + k], out_sems.at[a])
                    for a in range(3)]

        def cast_boundary(k):
            if k > cast_per_step:
                return
            if k > 0:
                for copy in load(k - 1):
                    copy.wait()
                if k > 1:
                    for copy in store(k - 2):
                        copy.wait()
                for a in range(3):
                    stage_out[a][...] = stage_in[a][...].astype(BF16)
                for copy in store(k - 1):
                    copy.start()
            if k < cast_per_step:
                for copy in load(k):
                    copy.start()
            else:
                for copy in store(k - 1):
                    copy.wait()
    else:
        bias_ref, bias_first_ref, out_ref, o_scr, l_scr = refs[9:]

        def cast_boundary(k):
            del k

    rows = HEADS_PER_GROUP * BLOCK
    row_head = lax.broadcasted_iota(jnp.int32, (rows, GROUP_WIDTH), 0) // BLOCK
    lane_head_r = lax.broadcasted_iota(jnp.int32, (rows, GROUP_WIDTH), 1) // HEAD_DIM
    head_mask = row_head == lane_head_r
    heads_per_half = LANES // HEAD_DIM
    lane_head = lax.broadcasted_iota(jnp.int32, (BLOCK, LANES), 1) // HEAD_DIM

    def heads_to_lanes(per_head):
        out = per_head[0]
        for h in range(1, heads_per_half):
            out = jnp.where(lane_head == h, per_head[h], out)
        return out

    for g, (_, dil) in enumerate(ATTN_CONFIGS):
        cast_boundary(g)
        q_ref, k_ref, v_ref = qkv[3 * g:3 * g + 3]
        sub_len = seq // dil
        n_blocks = sub_len // BLOCK

        def block(r, n, first, g=g, dil=dil, q_ref=q_ref, k_ref=k_ref, v_ref=v_ref, sub_len=sub_len):
            base = pl.multiple_of(r * sub_len + n * BLOCK, BLOCK)
            qb = q_ref[pl.ds(base, BLOCK), :]
            if first:
                kk = k_ref[pl.ds(base, BLOCK), :]
                vv = v_ref[pl.ds(base, BLOCK), :]
                bias = bias_first_ref[g]
            else:
                kbase = pl.multiple_of(base - BLOCK, BLOCK)
                kk = k_ref[pl.ds(kbase, 2 * BLOCK), :]
                vv = v_ref[pl.ds(kbase, 2 * BLOCK), :]
                bias = bias_ref[g]
            qs = jnp.where(head_mask, jnp.concatenate([qb] * HEADS_PER_GROUP, axis=0), jnp.zeros((), BF16))
            logits = lax.dot_general(qs, kk, (((1,), (1,)), ((), ())), preferred_element_type=F32) + bias
            yield
            m = jnp.max(logits, axis=1, keepdims=True)
            p = jnp.exp(logits - m)
            s = jnp.sum(p, axis=1, keepdims=True)
            p = p.astype(BF16)
            yield
            pv = jnp.dot(p, vv, preferred_element_type=F32)
            yield
            inv_s = 1.0 / s
            lse = m + jnp.log(s)
            start = n * (BLOCK * dil) + r
            if dil == 1:
                dst = pl.ds(pl.multiple_of(start, BLOCK), BLOCK)
            else:
                dst = pl.ds(start, BLOCK, stride=dil)
            for half in range(HALVES):
                cols = slice(half * LANES, (half + 1) * LANES)
                o_heads, lse_heads = [], []
                for h in range(half * heads_per_half, (half + 1) * heads_per_half):
                    head_rows = slice(h * BLOCK, (h + 1) * BLOCK)
                    o_heads.append(pv[head_rows, cols] * inv_s[head_rows])
                    lse_heads.append(jnp.broadcast_to(lse[head_rows], (BLOCK, LANES)))
                o_scr[g * HALVES + half, dst, :] = heads_to_lanes(o_heads)
                l_scr[g * HALVES + half, dst, :] = heads_to_lanes(lse_heads)
            yield

        if n_blocks > 1:
            group = 3 if (n_blocks - 1) % 3 == 0 else 1

            _run_skewed([block(r, 0, True) for r in range(dil)], ATTN_STAGES)

            def per_subsequence(r, carry, block=block, n_blocks=n_blocks, group=group):
                def per_group(i, c):
                    _run_skewed([block(r, 1 + i * group + k, False) for k in range(group)], ATTN_STAGES)
                    return c
                lax.fori_loop(0, (n_blocks - 1) // group, per_group, 0)
                return carry
            lax.fori_loop(0, dil, per_subsequence, 0)
        else:
            group = next(c for c in (8, 4, 2, 1) if dil % c == 0)

            def per_group(i, carry, block=block, group=group):
                _run_skewed([block(i * group + k, 0, True) for k in range(group)], ATTN_STAGES)
                return carry
            lax.fori_loop(0, dil // group, per_group, 0)

    cast_boundary(N_GROUPS)
    chunk = 256

    def merge(i, carry):
        sl = pl.ds(pl.multiple_of(i * chunk, chunk), chunk)
        for half in range(HALVES):
            ls = [l_scr[g * HALVES + half, sl, :] for g in range(N_GROUPS)]
            m = functools.reduce(jnp.maximum, ls)
            es = [jnp.exp(l - m) for l in ls]
            den = functools.reduce(lambda a, b: a + b, es)
            num = functools.reduce(lambda a, b: a + b,
                                   [e * o_scr[g * HALVES + half, sl, :] for g, e in enumerate(es)])
            out_ref[sl, half * LANES:(half + 1) * LANES] = (num / den).astype(BF16)
        return carry

    lax.fori_loop(0, seq // chunk, merge, 0)
    cast_boundary(N_GROUPS + 1)


def _attn_call(qkv, rel_bias_table, expert_weights=None):
    B, S, _ = qkv[0].shape
    for (_, dil) in ATTN_CONFIGS:
        assert S % (dil * BLOCK) == 0
    bias, bias_first = _attn_bias(rel_bias_table)
    seq_spec = pl.BlockSpec((None, S, GROUP_WIDTH), lambda b: (b, 0, 0))
    ordered = []
    for g in range(N_GROUPS):
        ordered += [qkv[g], qkv[3 + g], qkv[6 + g]]
    operands = ordered + [bias, bias_first]
    in_specs = [seq_spec] * 9 + [pl.BlockSpec(bias.shape, lambda b: (0, 0, 0)),
                                 pl.BlockSpec(bias_first.shape, lambda b: (0, 0, 0))]
    out_specs = [seq_spec]
    out_shape = [jax.ShapeDtypeStruct((B, S, GROUP_WIDTH), BF16)]
    scratch_shapes = [pltpu.VMEM((N_GROUPS * HALVES, S, LANES), F32),
                      pltpu.VMEM((N_GROUPS * HALVES, S, LANES), F32)]
    cast_per_step = 0
    if expert_weights is not None:
        n_experts = expert_weights[0].shape[0]
        assert n_experts % B == 0 and n_experts // B <= N_GROUPS + 1
        cast_per_step = n_experts // B
        operands += list(expert_weights)
        in_specs += [pl.BlockSpec(memory_space=pl.ANY)] * 3
        out_specs += [pl.BlockSpec(memory_space=pl.ANY)] * 3
        out_shape += [jax.ShapeDtypeStruct(w.shape, BF16) for w in expert_weights]
        scratch_shapes += ([pltpu.VMEM(w.shape[1:], F32) for w in expert_weights]
                           + [pltpu.VMEM(w.shape[1:], BF16) for w in expert_weights]
                           + [pltpu.SemaphoreType.DMA((3,)), pltpu.SemaphoreType.DMA((3,))])
    outs = pl.pallas_call(
        functools.partial(_attn_kernel, seq=S, cast_per_step=cast_per_step),
        grid=(B,),
        in_specs=in_specs,
        out_specs=out_specs,
        out_shape=out_shape,
        scratch_shapes=scratch_shapes,
        compiler_params=pltpu.CompilerParams(dimension_semantics=("arbitrary",),
                                             vmem_limit_bytes=VMEM_LIMIT_BYTES),
        name="attn",
    )(*operands)
    return outs[0], (tuple(outs[1:]) if expert_weights is not None else None)


PACK_CHUNKS = 4
SUBLANES = 8
ROUTER_LANES = LANES // 2


def _pack_bf16_pairs(rounded):
    w = rounded.shape[1] // 2
    bits = lax.bitcast_convert_type(rounded, jnp.int32)
    return bits[:, :w] | lax.shift_right_logical(bits[:, w:], jnp.full((), 16, jnp.int32))


def _unpack_bf16_pairs(words):
    hi = lax.bitcast_convert_type(words & jnp.int32(-65536), F32).astype(BF16)
    lo = lax.bitcast_convert_type(lax.shift_left(words, jnp.full((), 16, jnp.int32)), F32).astype(BF16)
    return jnp.concatenate([hi, lo], axis=1)


def _fold_pool_kernel(wp_ref, scale_ref, pb_ref, out_ref):
    a = wp_ref[...] * scale_ref[...]
    b = pb_ref[...]
    a_hi = a.astype(BF16)
    a_lo = (a - a_hi.astype(F32)).astype(BF16)
    b_hi = b.astype(BF16)
    b_lo = (b - b_hi.astype(F32)).astype(BF16)
    out_ref[...] = (jnp.dot(a_hi, b_hi, preferred_element_type=F32)
                    + jnp.dot(a_lo, b_hi, preferred_element_type=F32)
                    + jnp.dot(a_hi, b_lo, preferred_element_type=F32)).astype(BF16)


def _fold_pool_call(w_pool, pool_scale, w_proj_pool):
    n_groups, gd, _ = w_pool.shape
    D = w_proj_pool.shape[1]
    return pl.pallas_call(
        _fold_pool_kernel,
        grid=(n_groups,),
        in_specs=[pl.BlockSpec((None, gd, gd), lambda g: (g, 0, 0)),
                  pl.BlockSpec((None, 1, gd), lambda g: (g, 0, 0)),
                  pl.BlockSpec((gd, D), lambda g: (g, 0))],
        out_specs=pl.BlockSpec((gd, D), lambda g: (g, 0)),
        out_shape=jax.ShapeDtypeStruct((n_groups * gd, D), BF16),
        name="fold_pool",
    )(w_pool, pool_scale.reshape(n_groups, 1, gd), w_proj_pool)


def _post_kernel(a_ref, u_ref, halo_ref, ga_ref, gb_ref, x_ref,
                 pa_ref, pb_ref, wout_ref, g1_ref, b1_ref,
                 wr_cat_ref, br_ref,
                 h_ref, hb_ref, ids_ref, wts_ref, cnt_ref, pool_scr, tmp_scr, *, tm, sub, alpha):
    i = pl.program_id(1)
    halo = halo_ref[...]
    pool_scr[0:POOL_HALO, :] = jnp.zeros_like(halo)
    pool_scr[POOL_HALO:2 * POOL_HALO, :] = jnp.where(i > 0, halo, jnp.zeros_like(halo))
    pool_scr[2 * POOL_HALO:, :] = u_ref[...]
    tmp_scr[:, 0:POOL_HALO, :] = jnp.zeros((tmp_scr.shape[0], POOL_HALO, POOL_GROUP_DIM), F32)
    head_pos = i * tm + lax.broadcasted_iota(jnp.int32, (POOL_HALO, POOL_GROUP_DIM), 0)

    def rows_chain(r0):
        rs = pl.ds(r0, sub)
        group_cols = [slice(gi * POOL_GROUP_DIM, (gi + 1) * POOL_GROUP_DIM) for gi in range(len(POOL_SIZES))]
        diffs = []
        ext = sub + 2 * POOL_HALO
        for gi, (cols, w) in enumerate(zip(group_cols, POOL_SIZES)):
            ug = u_ref[rs, cols]
            levels = w.bit_length() - 1
            for level in range(levels):
                shift = 1 << level
                dst = tmp_scr.at[(r0 // sub * len(POOL_SIZES) + gi) * 2 + level % 2]
                if level == 0:
                    cur = pool_scr[pl.ds(r0 + POOL_HALO, ext - POOL_HALO), cols]
                    back = pool_scr[pl.ds(r0 + POOL_HALO - shift, ext - POOL_HALO), cols]
                else:
                    src = tmp_scr.at[(r0 // sub * len(POOL_SIZES) + gi) * 2 + (level - 1) % 2]
                    cur = src[pl.ds(POOL_HALO, ext - POOL_HALO), :]
                    back = src[pl.ds(POOL_HALO - shift, ext - POOL_HALO), :]
                if level == levels - 1:
                    acc = (cur + back)[POOL_HALO:]
                else:
                    dst[pl.ds(POOL_HALO, ext - POOL_HALO), :] = cur + back
            inv_count = jnp.full((sub, POOL_GROUP_DIM), 1.0 / w, F32)
            if r0 == 0:
                inv_count = jnp.concatenate([1.0 / jnp.minimum(head_pos + 1, w).astype(F32),
                                             inv_count[POOL_HALO:]], axis=0)
            diffs.append((acc * inv_count - ug).astype(BF16))
        yield
        y_pool = jnp.dot(jnp.concatenate(diffs, axis=1), pb_ref[...], preferred_element_type=F32)
        y_attn = jnp.dot(a_ref[rs, :], pa_ref[...], preferred_element_type=F32)
        yield
        mixed = ga_ref[rs, :] * y_attn.astype(BF16) + gb_ref[rs, :] * y_pool.astype(BF16)
        yield
        y = jnp.dot(mixed, wout_ref[...], preferred_element_type=F32)
        yield
        h = _layer_norm(alpha * x_ref[rs, :] + y, g1_ref[...], b1_ref[...])
        h_ref[rs, :] = h
        h_hi = h.astype(BF16)
        h_rounded = h_hi.astype(F32)
        packed = _pack_bf16_pairs(h_rounded)
        for c in range(PACK_CHUNKS):
            hb_ref[c, rs, :] = packed[:, c * LANES:(c + 1) * LANES]

        h_lo = (h - h_rounded).astype(BF16)
        yield
        nt = (((1,), (1,)), ((), ()))
        both = lax.dot_general(wr_cat_ref[...], h_hi, nt, preferred_element_type=F32)
        lo_hi = lax.dot_general(wr_cat_ref[0:ROUTER_LANES, :], h_lo, nt, preferred_element_type=F32)
        logits = both[0:ROUTER_LANES] + both[ROUTER_LANES:] + lo_hi + br_ref[...]
        yield
        row = lax.broadcasted_iota(jnp.int32, logits.shape, 0)
        big = jnp.int32(2 ** 30)
        is_group = row < N_EXPERT_GROUPS
        gl = jnp.where(is_group, logits, -jnp.inf)
        gmax = jnp.max(gl, axis=0, keepdims=True)
        g_idx = jnp.min(jnp.where(gl == gmax, row, big), axis=0, keepdims=True)
        g_prob = 1.0 / jnp.sum(jnp.exp(gl - gmax), axis=0, keepdims=True)
        expert = row - N_EXPERT_GROUPS
        in_group = (expert >= g_idx * EXPERTS_PER_GROUP) & (expert < (g_idx + 1) * EXPERTS_PER_GROUP)
        el = jnp.where(in_group, logits, -jnp.inf)
        v1 = jnp.max(el, axis=0, keepdims=True)
        i1 = jnp.min(jnp.where(el == v1, expert, big), axis=0, keepdims=True)
        el2 = jnp.where(expert == i1, -jnp.inf, el)
        v2 = jnp.max(el2, axis=0, keepdims=True)
        i2 = jnp.min(jnp.where(el2 == v2, expert, big), axis=0, keepdims=True)
        e2 = jnp.exp(v2 - v1)
        den = 1.0 + e2
        first_row = lax.broadcasted_iota(jnp.int32, (SUBLANES, sub), 0) == 0
        wts_ref[:, rs] = jnp.where(first_row, 1.0 / den * g_prob, e2 / den * g_prob)
        ids_ref[:, rs] = jnp.where(first_row, i1, i2).astype(F32)
        chosen = jnp.where((expert == i1) | (expert == i2), 1.0, 0.0)
        per_expert = jnp.broadcast_to(jnp.sum(chosen, axis=1, keepdims=True), (ROUTER_LANES, LANES))
        to_lane = (lax.broadcasted_iota(jnp.int32, (ROUTER_LANES, LANES), 0) - N_EXPERT_GROUPS
                   == lax.broadcasted_iota(jnp.int32, (ROUTER_LANES, LANES), 1))
        counts.append(jnp.sum(jnp.where(to_lane, per_expert, 0.0), axis=0, keepdims=True))
        yield

    counts = []
    _run_skewed([rows_chain(r0) for r0 in range(0, tm, sub)], 7)
    cnt_ref[...] = jnp.broadcast_to(functools.reduce(lambda a, b: a + b, counts), cnt_ref.shape)


def _post_call(a, u, ga, gb, x, w_proj_attn, w_pool, pool_scale, w_proj_pool, w_out, gamma, beta,
               w_router_group, b_router_group, w_router_expert, b_router_expert, alpha, b0, tm=1024, n_sub=2):
    B = a.shape[0]
    _, S, D = x.shape
    assert S % tm == 0 and tm % POOL_HALO == 0
    n_logits = N_EXPERT_GROUPS + N_EXPERTS
    assert n_logits <= ROUTER_LANES
    wr = jnp.concatenate([w_router_group, w_router_expert], axis=1).T
    wr = jnp.pad(wr, ((0, ROUTER_LANES - n_logits), (0, 0)))
    wr_hi = wr.astype(BF16)
    wr_lo = (wr - wr_hi.astype(F32)).astype(BF16)
    wr_cat = jnp.concatenate([wr_hi, wr_lo], axis=0)
    br = jnp.pad(jnp.concatenate([b_router_group, b_router_expert]),
                 (0, ROUTER_LANES - n_logits)).reshape(ROUTER_LANES, 1)
    n_tiles = B * (S // tm)
    row_spec = lambda width: pl.BlockSpec((None, tm, width), lambda b, i: (b, i, 0))
    full = lambda arr: pl.BlockSpec(arr.shape, lambda b, i: (0,) * arr.ndim)
    halo_blocks = tm // POOL_HALO
    halo_spec = pl.BlockSpec((None, POOL_HALO, POOL_WIDTH),
                             lambda b, i: (b, jnp.maximum(i * halo_blocks - 1, 0), 0))
    weights = [w_proj_attn.astype(BF16), _fold_pool_call(w_pool, pool_scale, w_proj_pool),
               w_out.astype(BF16), gamma.reshape(1, D), beta.reshape(1, D), wr_cat, br]
    return pl.pallas_call(
        functools.partial(_post_kernel, tm=tm, sub=tm // n_sub, alpha=alpha),
        grid=(B, S // tm),
        in_specs=[row_spec(GROUP_WIDTH), row_spec(POOL_WIDTH), halo_spec, row_spec(D), row_spec(D),
                  pl.BlockSpec((None, tm, D), lambda b, i: (b + b0, i, 0))] + [full(w) for w in weights],
        out_specs=[row_spec(D),
                   pl.BlockSpec((PACK_CHUNKS, tm, LANES), lambda b, i: (0, b * (S // tm) + i, 0)),
                   pl.BlockSpec((SUBLANES, tm), lambda b, i: (0, b * (S // tm) + i)),
                   pl.BlockSpec((SUBLANES, tm), lambda b, i: (0, b * (S // tm) + i)),
                   pl.BlockSpec((SUBLANES, LANES), lambda b, i: (b * (S // tm) + i, 0))],
        out_shape=[jax.ShapeDtypeStruct((B, S, D), F32),
                   jax.ShapeDtypeStruct((PACK_CHUNKS, B * S, LANES), jnp.int32),
                   jax.ShapeDtypeStruct((SUBLANES, B * S), F32), jax.ShapeDtypeStruct((SUBLANES, B * S), F32),
                   jax.ShapeDtypeStruct((n_tiles * SUBLANES, LANES), F32)],
        scratch_shapes=[pltpu.VMEM((tm + 2 * POOL_HALO, POOL_WIDTH), F32),
                        pltpu.VMEM((n_sub * len(POOL_SIZES) * 2, tm // n_sub + 2 * POOL_HALO, POOL_GROUP_DIM), F32)],
        compiler_params=pltpu.CompilerParams(dimension_semantics=("parallel", "parallel"),
                                             vmem_limit_bytes=VMEM_LIMIT_BYTES),
        name="post",
    )(a, u, u, ga, gb, x, *weights)


def _rank_kernel(ids_ref, cnt_ref, pos_ref, ends_ref, run_ref, start_ref, earlier_ref, *, tm, tile_rows):
    i = pl.program_id(0)

    @pl.when(i == 0)
    def _():
        total = jnp.sum(cnt_ref[...], axis=0, keepdims=True) / SUBLANES
        padded = jnp.broadcast_to(jnp.ceil(total / tile_rows) * tile_rows, ends_ref.shape)
        lane1 = lax.broadcasted_iota(jnp.int32, padded.shape, 1)
        incl = padded
        shift = 1
        while shift < LANES:
            incl = incl + jnp.where(lane1 >= shift, pltpu.roll(incl, shift, axis=1), 0.0)
            shift *= 2
        ends_ref[...] = incl
        start_row = jnp.broadcast_to((incl - padded)[0:1, :], start_ref.shape)
        on_diagonal = (lax.broadcasted_iota(jnp.int32, start_ref.shape, 0)
                       == lax.broadcasted_iota(jnp.int32, start_ref.shape, 1))
        start_col = jnp.sum(jnp.where(on_diagonal, start_row, 0.0), axis=1, keepdims=True)
        start_ref[...] = jnp.broadcast_to(start_col, start_ref.shape)
        run_ref[...] = jnp.zeros_like(run_ref)
        row = lax.broadcasted_iota(jnp.int32, (tm, tm), 0)
        col = lax.broadcasted_iota(jnp.int32, (tm, tm), 1)
        earlier_ref[...] = jnp.where(row < col, 1.0, 0.0).astype(BF16)

    ids = ids_ref[...]
    expert = lax.broadcasted_iota(jnp.int32, (N_EXPERTS, tm), 0).astype(F32)
    oh0 = expert == ids[0:1, :]
    oh1 = expert == ids[1:2, :]
    onehot = jnp.where(oh0 | oh1, 1.0, 0.0)
    before = jnp.dot(onehot.astype(BF16), earlier_ref[...], preferred_element_type=F32)
    slot = start_ref[:, 0:1] + run_ref[:, 0:1] + before
    p0 = jnp.sum(jnp.where(oh0, slot, 0.0), axis=0, keepdims=True)
    p1 = jnp.sum(jnp.where(oh1, slot, 0.0), axis=0, keepdims=True)
    first_row = lax.broadcasted_iota(jnp.int32, pos_ref.shape, 0) == 0
    pos_ref[...] = jnp.where(first_row, p0, p1).astype(jnp.int32)
    run_ref[...] += jnp.broadcast_to(jnp.sum(onehot, axis=1, keepdims=True), run_ref.shape)


def _rank_call(ids_t, counts, tile_rows, tm=1024):
    N = ids_t.shape[1]
    assert N % tm == 0 and SUBLANES * (2 * N + N_EXPERTS * tile_rows) < 2 ** 24
    assert N_EXPERTS <= LANES
    return pl.pallas_call(
        functools.partial(_rank_kernel, tm=tm, tile_rows=tile_rows),
        grid=(N // tm,),
        in_specs=[pl.BlockSpec((SUBLANES, tm), lambda i: (0, i)),
                  pl.BlockSpec(counts.shape, lambda i: (0, 0))],
        out_specs=[pl.BlockSpec((SUBLANES, tm), lambda i: (0, i)),
                   pl.BlockSpec((SUBLANES, LANES), lambda i: (0, 0))],
        out_shape=[jax.ShapeDtypeStruct((SUBLANES, N), jnp.int32),
                   jax.ShapeDtypeStruct((SUBLANES, LANES), F32)],
        scratch_shapes=[pltpu.VMEM((N_EXPERTS, LANES), F32), pltpu.VMEM((N_EXPERTS, LANES), F32),
                        pltpu.VMEM((tm, tm), BF16)],
        compiler_params=pltpu.CompilerParams(dimension_semantics=("arbitrary",),
                                             vmem_limit_bytes=VMEM_LIMIT_BYTES),
        name="rank",
    )(ids_t, counts)


SC_CORES = 2
SC_SUBCORES = 16
SC_WORKERS = SC_CORES * SC_SUBCORES
SC_CHUNK = 128


def _sc_mesh():
    return plsc.VectorSubcoreMesh(core_axis_name="c", subcore_axis_name="s",
                                  num_cores=SC_CORES, num_subcores=SC_SUBCORES)


def _sc_dispatch(packed, pos_t, n_rows):
    n_chunks, n_tok, width = packed.shape
    per_worker = n_tok // SC_WORKERS
    assert n_tok % (SC_WORKERS * SC_CHUNK) == 0

    @functools.partial(
        pl.kernel, mesh=_sc_mesh(),
        out_type=jax.ShapeDtypeStruct((n_chunks, n_rows, width), packed.dtype),
        scratch_types=[pltpu.VMEM((SC_CHUNK,), jnp.int32), pltpu.VMEM((SC_CHUNK,), jnp.int32),
                       pltpu.VMEM((n_chunks, SC_CHUNK, width), packed.dtype)]
                      + [pltpu.SemaphoreType.DMA] * (3 * n_chunks),
        name="sc_dispatch")
    def run(packed_hbm, pos_hbm, out_hbm, idx0, idx1, bufs, *sems):
        load_sems, sems0, sems1 = sems[:n_chunks], sems[n_chunks:2 * n_chunks], sems[2 * n_chunks:]
        worker = lax.axis_index("s") * SC_CORES + lax.axis_index("c")

        @pl.loop(0, per_worker // SC_CHUNK)
        def _(j):
            base = worker * per_worker + j * SC_CHUNK
            loads = [pltpu.async_copy(packed_hbm.at[c, pl.ds(base, SC_CHUNK)], bufs.at[c], load_sems[c])
                     for c in range(n_chunks)]
            pltpu.sync_copy(pos_hbm.at[0, pl.ds(base, SC_CHUNK)], idx0)
            pltpu.sync_copy(pos_hbm.at[1, pl.ds(base, SC_CHUNK)], idx1)
            scatters = []
            for c in range(n_chunks):
                loads[c].wait()
                scatters.append(pltpu.async_copy(bufs.at[c], out_hbm.at[c].at[idx0], sems0[c]))
                scatters.append(pltpu.async_copy(bufs.at[c], out_hbm.at[c].at[idx1], sems1[c]))
            for s in scatters:
                s.wait()

    return run(packed, pos_t)


def _sc_combine(sorted_rows, pos_t):
    n_chunks, _, width = sorted_rows.shape
    n_tok = pos_t.shape[1]
    per_worker = n_tok // SC_WORKERS
    assert n_tok % (SC_WORKERS * SC_CHUNK) == 0

    @functools.partial(
        pl.kernel, mesh=_sc_mesh(),
        out_type=jax.ShapeDtypeStruct((2, n_chunks, n_tok, width), sorted_rows.dtype),
        scratch_types=[pltpu.VMEM((SC_CHUNK,), jnp.int32),
                       pltpu.VMEM((n_chunks, SC_CHUNK, width), sorted_rows.dtype)]
                      + [pltpu.SemaphoreType.DMA] * (2 * n_chunks),
        name="sc_combine")
    def run(rows_hbm, pos_hbm, out_hbm, idx, bufs, *sems):
        gather_sems, write_sems = sems[:n_chunks], sems[n_chunks:]
        worker = lax.axis_index("s") * SC_CORES + lax.axis_index("c")

        @pl.loop(0, per_worker // SC_CHUNK)
        def _(j):
            base = worker * per_worker + j * SC_CHUNK
            for k in range(2):
                pltpu.sync_copy(pos_hbm.at[k, pl.ds(base, SC_CHUNK)], idx)
                gathers = [pltpu.async_copy(rows_hbm.at[c].at[idx], bufs.at[c], gather_sems[c])
                           for c in range(n_chunks)]
                writes = []
                for c in range(n_chunks):
                    gathers[c].wait()
                    writes.append(pltpu.async_copy(bufs.at[c], out_hbm.at[k, c, pl.ds(base, SC_CHUNK)],
                                                   write_sems[c]))
                for w in writes:
                    w.wait()

    return run(sorted_rows, pos_t)


EXPERT_RING = 3


def _expert_kernel(tile_expert_ref, n_used_ref, first_ref, wslot_ref,
                   xs_hbm, wg_hbm, wu_hbm, wd_hbm, ys_ref,
                   x_buf, wg_buf, wu_buf, wd_buf, x_sems, w_sems, *, tile_rows):
    s = pl.program_id(0)
    n_used = n_used_ref[0]

    def x_copy(t):
        slot = t % EXPERT_RING
        return pltpu.make_async_copy(xs_hbm.at[:, pl.ds(pl.multiple_of(t * tile_rows, tile_rows), tile_rows), :],
                                     x_buf.at[slot], x_sems.at[slot])

    def w_copies(t):
        e, slot = tile_expert_ref[t], wslot_ref[t]
        return [pltpu.make_async_copy(hbm.at[e], buf.at[slot], w_sems.at[a, slot])
                for a, (hbm, buf) in enumerate(((wg_hbm, wg_buf), (wu_hbm, wu_buf), (wd_hbm, wd_buf)))]

    def start_tile(t):
        x_copy(t).start()

        @pl.when(first_ref[t] == 1)
        def _():
            for copy in w_copies(t):
                copy.start()

    @pl.when(s == 0)
    def _():
        for t in range(EXPERT_RING - 1):
            @pl.when(t < n_used)
            def _():
                start_tile(t)

    @pl.when(s + (EXPERT_RING - 1) < n_used)
    def _():
        start_tile(s + (EXPERT_RING - 1))

    @pl.when(s < n_used)
    def _():
        x_copy(s).wait()

        @pl.when(first_ref[s] == 1)
        def _():
            for copy in w_copies(s):
                copy.wait()

        x_slot = s % EXPERT_RING
        w_slot = wslot_ref[s]
        wg = wg_buf[w_slot]
        wu = wu_buf[w_slot]
        wd = wd_buf[w_slot]
        sub = tile_rows // 2

        def rows_chain(r0):
            rs = pl.ds(r0, sub)
            x = _unpack_bf16_pairs(jnp.concatenate([x_buf[x_slot, c, rs, :] for c in range(PACK_CHUNKS)], axis=1))
            yield
            gate = jnp.dot(x, wg, preferred_element_type=F32)
            up = jnp.dot(x, wu, preferred_element_type=F32)
            yield
            hidden = (jax.nn.silu(gate) * up).astype(BF16)
            yield
            y = jnp.dot(hidden, wd, preferred_element_type=F32)
            yield
            y = _pack_bf16_pairs(y.astype(BF16).astype(F32))
            for c in range(PACK_CHUNKS):
                ys_ref[c, rs, :] = y[:, c * LANES:(c + 1) * LANES]
            yield

        chains = [rows_chain(r0) for r0 in range(0, tile_rows, sub)]
        n_stages = 5
        for step in range(n_stages + len(chains) - 1):
            for lag, chain in enumerate(chains):
                if 0 <= step - lag < n_stages:
                    next(chain)


def _expert_call(xs, tile_expert, n_used, w_gate, w_up, w_down, tile_rows):
    _, n_rows, _ = xs.shape
    D = w_gate.shape[1]
    assert w_gate.dtype == BF16 and w_up.dtype == BF16 and w_down.dtype == BF16
    first = jnp.concatenate([jnp.ones((1,), jnp.int32),
                             (tile_expert[1:] != tile_expert[:-1]).astype(jnp.int32)])
    wslot = ((jnp.cumsum(first) - 1) % EXPERT_RING).astype(jnp.int32)
    row_block = pl.BlockSpec((PACK_CHUNKS, tile_rows, LANES),
                             lambda i, te, nu, fi, ws: (0, jnp.minimum(i, nu[0] - 1), 0))
    hbm = pl.BlockSpec(memory_space=pl.ANY)
    return pl.pallas_call(
        functools.partial(_expert_kernel, tile_rows=tile_rows),
        grid_spec=pltpu.PrefetchScalarGridSpec(
            num_scalar_prefetch=4,
            grid=(n_rows // tile_rows,),
            in_specs=[hbm, hbm, hbm, hbm],
            out_specs=row_block,
            scratch_shapes=[pltpu.VMEM((EXPERT_RING, PACK_CHUNKS, tile_rows, LANES), xs.dtype),
                            pltpu.VMEM((EXPERT_RING, D, D_EXPERT), BF16),
                            pltpu.VMEM((EXPERT_RING, D, D_EXPERT), BF16),
                            pltpu.VMEM((EXPERT_RING, D_EXPERT, D), BF16),
                            pltpu.SemaphoreType.DMA((EXPERT_RING,)),
                            pltpu.SemaphoreType.DMA((3, EXPERT_RING))]),
        out_shape=jax.ShapeDtypeStruct(xs.shape, xs.dtype),
        compiler_params=pltpu.CompilerParams(dimension_semantics=("arbitrary",),
                                             vmem_limit_bytes=VMEM_LIMIT_BYTES),
        name="experts",
    )(tile_expert, n_used, first, wslot, xs, w_gate, w_up, w_down)


def _final_kernel(h_ref, y_ref, wts_ref, g2_ref, b2_ref, *rest, alpha):
    out_ref = rest[-1]
    tm = h_ref.shape[0]
    on_diagonal = (lax.broadcasted_iota(jnp.int32, (tm, tm), 0)
                   == lax.broadcasted_iota(jnp.int32, (tm, tm), 1))
    z = alpha * h_ref[...]
    for k in range(2):
        w_col = jnp.sum(jnp.where(on_diagonal, jnp.broadcast_to(wts_ref[k:k + 1, :], (tm, tm)), 0.0),
                        axis=1, keepdims=True)
        yk = _unpack_bf16_pairs(jnp.concatenate([y_ref[k, c] for c in range(PACK_CHUNKS)], axis=1))
        z = z + w_col * yk.astype(F32)
    out_ref[...] = _layer_norm(z, g2_ref[...], b2_ref[...])


def _final_call(h, y, wts, gamma, beta, alpha, row0, n_total, earlier_out, tm=1024):
    N, D = h.shape
    assert N % tm == 0 and row0 % tm == 0
    operands = [h, y, wts, gamma.reshape(1, D), beta.reshape(1, D)]
    in_specs = [pl.BlockSpec((tm, D), lambda i: (i, 0)),
                pl.BlockSpec((2, PACK_CHUNKS, tm, LANES), lambda i: (0, 0, i, 0)),
                pl.BlockSpec((SUBLANES, tm), lambda i: (0, i)),
                pl.BlockSpec((1, D), lambda i: (0, 0)),
                pl.BlockSpec((1, D), lambda i: (0, 0))]
    aliases = {}
    if earlier_out is not None:
        aliases = {len(operands): 0}
        operands.append(earlier_out)
        in_specs.append(pl.BlockSpec(memory_space=pl.ANY))
    return pl.pallas_call(
        functools.partial(_final_kernel, alpha=alpha),
        grid=(N // tm,),
        in_specs=in_specs,
        out_specs=pl.BlockSpec((tm, D), lambda i: (i + row0 // tm, 0)),
        out_shape=jax.ShapeDtypeStruct((n_total, D), F32),
        input_output_aliases=aliases,
        compiler_params=pltpu.CompilerParams(dimension_semantics=("parallel",),
                                             vmem_limit_bytes=VMEM_LIMIT_BYTES),
        name="final",
    )(*operands)


def _moe(packed, ids_t, counts, w_gate, w_up, w_down, tile_rows=512):
    N = ids_t.shape[1]
    pos_t, ends = _rank_call(ids_t, counts, tile_rows)
    n_tiles = 2 * N // tile_rows + N_EXPERTS
    seg_end = ends[0, :N_EXPERTS].astype(jnp.int32)
    tile_start = jnp.arange(n_tiles, dtype=jnp.int32) * tile_rows
    tile_expert = jnp.minimum(jnp.sum(seg_end[None, :] <= tile_start[:, None], axis=1),
                              N_EXPERTS - 1).astype(jnp.int32)
    n_used = (seg_end[N_EXPERTS - 1:] // tile_rows).astype(jnp.int32)
    xs = _sc_dispatch(packed, pos_t, n_tiles * tile_rows)
    ys = _expert_call(xs, tile_expert, n_used, w_gate, w_up, w_down, tile_rows)
    return _sc_combine(ys, pos_t)


@jax.jit
def kernel(x, w_in, b_in, rel_bias_table, w_pool, pool_scale, w_proj_attn, w_proj_pool, w_out, ln1_gamma, ln1_beta, w_router_group, b_router_group, w_router_expert, b_router_expert, w_expert_gate, w_expert_up, w_expert_down, ln2_gamma, ln2_beta):
    B, S, D = x.shape
    depth = w_in.shape[0]
    alpha = (2.0 * depth) ** 0.25
    n_parts = 2 if B % 2 == 0 else 1
    nb = B // n_parts
    for layer in range(depth):
        out = None
        experts32 = (w_expert_gate[layer], w_expert_up[layer], w_expert_down[layer])
        n_experts = experts32[0].shape[0]
        cast_in_attn = n_experts % nb == 0 and n_experts // nb <= N_GROUPS + 1
        experts16 = None if cast_in_attn else tuple(w.astype(BF16) for w in experts32)
        for part in range(n_parts):
            b0 = part * nb
            qkv, u, ga, gb = _proj_call(x, w_in[layer], b_in[layer], b0, nb)
            a, converted = _attn_call(qkv, rel_bias_table, experts32 if experts16 is None else None)
            if converted is not None:
                experts16 = converted
            h, packed, ids, wts, counts = _post_call(
                a, u, ga, gb, x, w_proj_attn[layer], w_pool[layer], pool_scale[layer],
                w_proj_pool[layer], w_out[layer], ln1_gamma[layer], ln1_beta[layer],
                w_router_group[layer], b_router_group[layer],
                w_router_expert[layer], b_router_expert[layer], alpha, b0)
            y = _moe(packed, ids, counts, *experts16)
            out = _final_call(h.reshape(nb * S, D), y, wts, ln2_gamma[layer],
                              ln2_beta[layer], alpha, b0 * S, B * S, out)
        x = out.reshape(B, S, D)
    return x
```

```python
import functools
import math

import jax
import jax.numpy as jnp
import numpy as np
from jax import lax
from jax.experimental import pallas as pl
from jax.experimental.pallas import tpu as pltpu
from jax.experimental.pallas import tpu_sc as plsc

F32 = jnp.float32
BF16 = jnp.bfloat16

HEAD_DIM = 64
ATTN_CONFIGS = ((128, 1), (512, 4), (2048, 16))
N_GROUPS = len(ATTN_CONFIGS)
HEADS_PER_GROUP = 4
GROUP_WIDTH = HEADS_PER_GROUP * HEAD_DIM
ATTN_WIDTH = N_GROUPS * GROUP_WIDTH
BLOCK = 128
N_REL_BUCKETS = 32
REL_MAX_DISTANCE = 2048
NEG_INF = -1e30

POOL_SIZES = (2, 4, 8, 16)
POOL_GROUP_DIM = 128
POOL_WIDTH = POOL_GROUP_DIM * len(POOL_SIZES)
POOL_HALO = 16

N_EXPERT_GROUPS = 4
EXPERTS_PER_GROUP = 8
N_EXPERTS = N_EXPERT_GROUPS * EXPERTS_PER_GROUP
D_EXPERT = 256
LN_EPS = 1e-5

VMEM_LIMIT_BYTES = 56 * 1024 * 1024
LANES = 128
HALVES = GROUP_WIDTH // LANES


def _layer_norm(z, gamma, beta):
    mu = jnp.mean(z, axis=-1, keepdims=True)
    zc = z - mu
    var = jnp.mean(zc * zc, axis=-1, keepdims=True)
    return zc * lax.rsqrt(var + LN_EPS) * gamma + beta


def _proj_kernel(x_ref, w_ref, b_ref, *refs, tm, d_model):
    qkv_refs = refs[:9]
    u_ref, ga_ref, gb_ref, xb_ref, acc_ref = refs[9:]
    xb_ref[...] = x_ref[...].astype(BF16)

    def chunk(c0, width):
        acc = jnp.dot(xb_ref[...], w_ref[:, c0:c0 + width], preferred_element_type=F32)
        return acc + b_ref[:, c0:c0 + width]

    pool_off = 3 * ATTN_WIDTH
    ga_off = pool_off + POOL_WIDTH

    def gate_job(gate_ref, off, c):
        def run():
            gate_ref[:, c * 256:(c + 1) * 256] = jax.nn.sigmoid(chunk(off + c * 256, 256)).astype(BF16)
        return run

    def pool_job(c):
        def run():
            u_ref[:, c * 256:(c + 1) * 256] = chunk(pool_off + c * 256, 256)
        return run

    def qkv_job(which, g, staged):
        def run():
            dil = ATTN_CONFIGS[g][1]
            out = qkv_refs[which * 3 + g]
            acc = chunk(which * ATTN_WIDTH + g * GROUP_WIDTH, GROUP_WIDTH)
            if which == 0:
                acc = acc * HEAD_DIM ** -0.5
            if dil == 1:
                out[0] = acc.astype(BF16)
            else:
                planes = [staged * HALVES + half for half in range(HALVES)]
                for half, plane in enumerate(planes):
                    acc_ref[plane] = acc[:, half * LANES:(half + 1) * LANES]
                for r in range(dil):
                    for half, plane in enumerate(planes):
                        out[r, :, half * LANES:(half + 1) * LANES] = (
                            acc_ref[plane, pl.ds(r, tm // dil, stride=dil), :].astype(BF16))
        return run

    gates = [gate_job(ref, off, c) for ref, off in ((ga_ref, ga_off), (gb_ref, ga_off + d_model))
             for c in range(d_model // 256)]
    dilated = [(which, g) for g in range(N_GROUPS) if ATTN_CONFIGS[g][1] > 1 for which in range(3)]
    strided = [qkv_job(which, g, k) for k, (which, g) in enumerate(dilated)]
    plain = [qkv_job(which, g, 0) for g in range(N_GROUPS) if ATTN_CONFIGS[g][1] == 1 for which in range(3)]
    pools = [pool_job(c) for c in range(POOL_WIDTH // 256)]
    order = []
    while gates or strided:
        if gates:
            order.append(gates.pop(0))
        if strided:
            order.append(strided.pop(0))
    for job in order + plain + pools:
        job()


def _proj_call(x, w_in, b_in, b0, B, tm=1024):
    _, S, D = x.shape
    in_width = w_in.shape[1]
    assert in_width == 3 * ATTN_WIDTH + POOL_WIDTH + 2 * D
    assert S % tm == 0
    grid = (B, S // tm)
    qkv_shapes, qkv_specs = [], []
    for _ in range(3):
        for (_, dil) in ATTN_CONFIGS:
            assert tm % (dil * 16) == 0
            qkv_shapes.append(jax.ShapeDtypeStruct((B, dil, S // dil, GROUP_WIDTH), BF16))
            qkv_specs.append(pl.BlockSpec((None, dil, tm // dil, GROUP_WIDTH), lambda b, i: (b, 0, i, 0)))
    row_spec = lambda width: pl.BlockSpec((None, tm, width), lambda b, i: (b, i, 0))
    out_shape = qkv_shapes + [jax.ShapeDtypeStruct((B, S, POOL_WIDTH), F32),
                              jax.ShapeDtypeStruct((B, S, D), BF16),
                              jax.ShapeDtypeStruct((B, S, D), BF16)]
    out_specs = qkv_specs + [row_spec(POOL_WIDTH), row_spec(D), row_spec(D)]
    outs = pl.pallas_call(
        functools.partial(_proj_kernel, tm=tm, d_model=D),
        grid=grid,
        in_specs=[pl.BlockSpec((None, tm, D), lambda b, i: (b + b0, i, 0)),
                  pl.BlockSpec((D, in_width), lambda b, i: (0, 0), pipeline_mode=pl.Buffered(1)),
                  pl.BlockSpec((1, in_width), lambda b, i: (0, 0))],
        out_specs=out_specs,
        out_shape=out_shape,
        scratch_shapes=[pltpu.VMEM((tm, D), BF16),
                        pltpu.VMEM((3 * sum(dil > 1 for _, dil in ATTN_CONFIGS) * HALVES, tm, LANES), F32)],
        compiler_params=pltpu.CompilerParams(dimension_semantics=("parallel", "parallel"),
                                             vmem_limit_bytes=VMEM_LIMIT_BYTES),
        name="proj",
    )(x, w_in.astype(BF16), b_in.reshape(1, in_width))
    qkv = [o.reshape(B, S, GROUP_WIDTH) for o in outs[:9]]
    return qkv, outs[9], outs[10], outs[11]


def _t5_causal_bucket(dist):
    max_exact = N_REL_BUCKETS // 2
    is_small = dist < max_exact
    d = jnp.maximum(dist, 1).astype(F32)
    large = max_exact + (jnp.log(d / max_exact) / math.log(REL_MAX_DISTANCE / max_exact)
                         * (N_REL_BUCKETS - max_exact)).astype(jnp.int32)
    large = jnp.minimum(large, N_REL_BUCKETS - 1)
    return jnp.where(is_small, dist, large)


def _attn_bias(rel_bias_table):
    full, first = [], []
    for g, (window, dil) in enumerate(ATTN_CONFIGS):
        span = window // dil
        table = rel_bias_table[:, g * HEADS_PER_GROUP:(g + 1) * HEADS_PER_GROUP].astype(F32)
        lq = np.arange(BLOCK)
        for lk, dst in ((np.arange(-BLOCK, BLOCK), full), (lq, first)):
            step = jnp.asarray(lq[:, None] - lk[None, :], jnp.int32)
            in_window = (step >= 0) & (step <= span)
            bucket = _t5_causal_bucket(jnp.clip(step, 0, span) * dil)
            bias = jnp.einsum('qkb,bh->hqk', jax.nn.one_hot(bucket, N_REL_BUCKETS, dtype=F32), table,
                              precision=lax.Precision.HIGHEST)
            bias = jnp.where(in_window[None], bias, NEG_INF)
            dst.append(bias.reshape(HEADS_PER_GROUP * BLOCK, lk.shape[0]))
    return jnp.stack(full), jnp.stack(first)


def _run_skewed(chains, n_stages):
    for step in range(n_stages + len(chains) - 1):
        for lag, chain in enumerate(chains):
            if 0 <= step - lag < n_stages:
                next(chain)


ATTN_STAGES = 4


def _attn_kernel(*refs, seq, cast_per_step):
    qkv = refs[:9]
    if cast_per_step:
        (bias_ref, bias_first_ref, w32_g, w32_u, w32_d, out_ref, w16_g, w16_u, w16_d,
         o_scr, l_scr, in_g, in_u, in_d, cast_g, cast_u, cast_d, in_sems, out_sems) = refs[9:]
        w32 = (w32_g, w32_u, w32_d)
        w16 = (w16_g, w16_u, w16_d)
        stage_in = (in_g, in_u, in_d)
        stage_out = (cast_g, cast_u, cast_d)
        first_expert = pl.program_id(0) * cast_per_step

        def load(k):
            return [pltpu.make_async_copy(w32[a].at[first_expert + k], stage_in[a], in_sems.at[a])
                    for a in range(3)]

        def store(k):
            return [pltpu.make_async_copy(stage_out[a], w16[a].at[first_expert + k], out_sems.at[a])
                    for a in range(3)]

        def cast_boundary(k):
            if k > cast_per_step:
                return
            if k > 0:
                for copy in load(k - 1):
                    copy.wait()
                if k > 1:
                    for copy in store(k - 2):
                        copy.wait()
                for a in range(3):
                    stage_out[a][...] = stage_in[a][...].astype(BF16)
                for copy in store(k - 1):
                    copy.start()
            if k < cast_per_step:
                for copy in load(k):
                    copy.start()
            else:
                for copy in store(k - 1):
                    copy.wait()
    else:
        bias_ref, bias_first_ref, out_ref, o_scr, l_scr = refs[9:]

        def cast_boundary(k):
            del k

    rows = HEADS_PER_GROUP * BLOCK
    row_head = lax.broadcasted_iota(jnp.int32, (rows, GROUP_WIDTH), 0) // BLOCK
    lane_head_r = lax.broadcasted_iota(jnp.int32, (rows, GROUP_WIDTH), 1) // HEAD_DIM
    head_mask = row_head == lane_head_r
    heads_per_half = LANES // HEAD_DIM
    lane_head = lax.broadcasted_iota(jnp.int32, (BLOCK, LANES), 1) // HEAD_DIM

    def heads_to_lanes(per_head):
        out = per_head[0]
        for h in range(1, heads_per_half):
            out = jnp.where(lane_head == h, per_head[h], out)
        return out

    for g, (_, dil) in enumerate(ATTN_CONFIGS):
        cast_boundary(g)
        q_ref, k_ref, v_ref = qkv[3 * g:3 * g + 3]
        sub_len = seq // dil
        n_blocks = sub_len // BLOCK

        def block(r, n, first, g=g, dil=dil, q_ref=q_ref, k_ref=k_ref, v_ref=v_ref, sub_len=sub_len):
            base = pl.multiple_of(r * sub_len + n * BLOCK, BLOCK)
            qb = q_ref[pl.ds(base, BLOCK), :]
            if first:
                kk = k_ref[pl.ds(base, BLOCK), :]
                vv = v_ref[pl.ds(base, BLOCK), :]
                bias = bias_first_ref[g]
            else:
                kbase = pl.multiple_of(base - BLOCK, BLOCK)
                kk = k_ref[pl.ds(kbase, 2 * BLOCK), :]
                vv = v_ref[pl.ds(kbase, 2 * BLOCK), :]
                bias = bias_ref[g]
            qs = jnp.where(head_mask, jnp.concatenate([qb] * HEADS_PER_GROUP, axis=0), jnp.zeros((), BF16))
            logits = lax.dot_general(qs, kk, (((1,), (1,)), ((), ())), preferred_element_type=F32) + bias
            yield
            m = jnp.max(logits, axis=1, keepdims=True)
            p = jnp.exp(logits - m)
            s = jnp.sum(p, axis=1, keepdims=True)
            p = p.astype(BF16)
            yield
            pv = jnp.dot(p, vv, preferred_element_type=F32)
            yield
            inv_s = 1.0 / s
            lse = m + jnp.log(s)
            start = n * (BLOCK * dil) + r
            if dil == 1:
                dst = pl.ds(pl.multiple_of(start, BLOCK), BLOCK)
            else:
                dst = pl.ds(start, BLOCK, stride=dil)
            for half in range(HALVES):
                cols = slice(half * LANES, (half + 1) * LANES)
                o_heads, lse_heads = [], []
                for h in range(half * heads_per_half, (half + 1) * heads_per_half):
                    head_rows = slice(h * BLOCK, (h + 1) * BLOCK)
                    o_heads.append(pv[head_rows, cols] * inv_s[head_rows])
                    lse_heads.append(jnp.broadcast_to(lse[head_rows], (BLOCK, LANES)))
                o_scr[g * HALVES + half, dst, :] = heads_to_lanes(o_heads)
                l_scr[g * HALVES + half, dst, :] = heads_to_lanes(lse_heads)
            yield

        if n_blocks > 1:
            group = 3 if (n_blocks - 1) % 3 == 0 else 1

            _run_skewed([block(r, 0, True) for r in range(dil)], ATTN_STAGES)

            def per_subsequence(r, carry, block=block, n_blocks=n_blocks, group=group):
                def per_group(i, c):
                    _run_skewed([block(r, 1 + i * group + k, False) for k in range(group)], ATTN_STAGES)
                    return c
                lax.fori_loop(0, (n_blocks - 1) // group, per_group, 0)
                return carry
            lax.fori_loop(0, dil, per_subsequence, 0)
        else:
            group = next(c for c in (8, 4, 2, 1) if dil % c == 0)

            def per_group(i, carry, block=block, group=group):
                _run_skewed([block(i * group + k, 0, True) for k in range(group)], ATTN_STAGES)
                return carry
            lax.fori_loop(0, dil // group, per_group, 0)

    cast_boundary(N_GROUPS)
    chunk = 256

    def merge(i, carry):
        sl = pl.ds(pl.multiple_of(i * chunk, chunk), chunk)
        for half in range(HALVES):
            ls = [l_scr[g * HALVES + half, sl, :] for g in range(N_GROUPS)]
            m = functools.reduce(jnp.maximum, ls)
            es = [jnp.exp(l - m) for l in ls]
            den = functools.reduce(lambda a, b: a + b, es)
            num = functools.reduce(lambda a, b: a + b,
                                   [e * o_scr[g * HALVES + half, sl, :] for g, e in enumerate(es)])
            out_ref[sl, half * LANES:(half + 1) * LANES] = (num / den).astype(BF16)
        return carry

    lax.fori_loop(0, seq // chunk, merge, 0)
    cast_boundary(N_GROUPS + 1)


def _attn_call(qkv, rel_bias_table, expert_weights=None):
    B, S, _ = qkv[0].shape
    for (_, dil) in ATTN_CONFIGS:
        assert S % (dil * BLOCK) == 0
    bias, bias_first = _attn_bias(rel_bias_table)
    seq_spec = pl.BlockSpec((None, S, GROUP_WIDTH), lambda b: (b, 0, 0))
    ordered = []
    for g in range(N_GROUPS):
        ordered += [qkv[g], qkv[3 + g], qkv[6 + g]]
    operands = ordered + [bias, bias_first]
    in_specs = [seq_spec] * 9 + [pl.BlockSpec(bias.shape, lambda b: (0, 0, 0)),
                                 pl.BlockSpec(bias_first.shape, lambda b: (0, 0, 0))]
    out_specs = [seq_spec]
    out_shape = [jax.ShapeDtypeStruct((B, S, GROUP_WIDTH), BF16)]
    scratch_shapes = [pltpu.VMEM((N_GROUPS * HALVES, S, LANES), F32),
                      pltpu.VMEM((N_GROUPS * HALVES, S, LANES), F32)]
    cast_per_step = 0
    if expert_weights is not None:
        n_experts = expert_weights[0].shape[0]
        assert n_experts % B == 0 and n_experts // B <= N_GROUPS + 1
        cast_per_step = n_experts // B
        operands += list(expert_weights)
        in_specs += [pl.BlockSpec(memory_space=pl.ANY)] * 3
        out_specs += [pl.BlockSpec(memory_space=pl.ANY)] * 3
        out_shape += [jax.ShapeDtypeStruct(w.shape, BF16) for w in expert_weights]
        scratch_shapes += ([pltpu.VMEM(w.shape[1:], F32) for w in expert_weights]
                           + [pltpu.VMEM(w.shape[1:], BF16) for w in expert_weights]
                           + [pltpu.SemaphoreType.DMA((3,)), pltpu.SemaphoreType.DMA((3,))])
    outs = pl.pallas_call(
        functools.partial(_attn_kernel, seq=S, cast_per_step=cast_per_step),
        grid=(B,),
        in_specs=in_specs,
        out_specs=out_specs,
        out_shape=out_shape,
        scratch_shapes=scratch_shapes,
        compiler_params=pltpu.CompilerParams(dimension_semantics=("arbitrary",),
                                             vmem_limit_bytes=VMEM_LIMIT_BYTES),
        name="attn",
    )(*operands)
    return outs[0], (tuple(outs[1:]) if expert_weights is not None else None)


PACK_CHUNKS = 4
SUBLANES = 8
ROUTER_LANES = LANES // 2


def _pack_bf16_pairs(rounded):
    w = rounded.shape[1] // 2
    bits = lax.bitcast_convert_type(rounded, jnp.int32)
    return bits[:, :w] | lax.shift_right_logical(bits[:, w:], jnp.full((), 16, jnp.int32))


def _unpack_bf16_pairs(words):
    hi = lax.bitcast_convert_type(words & jnp.int32(-65536), F32).astype(BF16)
    lo = lax.bitcast_convert_type(lax.shift_left(words, jnp.full((), 16, jnp.int32)), F32).astype(BF16)
    return jnp.concatenate([hi, lo], axis=1)


def _fold_pool_kernel(wp_ref, scale_ref, pb_ref, out_ref):
    a = wp_ref[...] * scale_ref[...]
    b = pb_ref[...]
    a_hi = a.astype(BF16)
    a_lo = (a - a_hi.astype(F32)).astype(BF16)
    b_hi = b.astype(BF16)
    b_lo = (b - b_hi.astype(F32)).astype(BF16)
    out_ref[...] = (jnp.dot(a_hi, b_hi, preferred_element_type=F32)
                    + jnp.dot(a_lo, b_hi, preferred_element_type=F32)
                    + jnp.dot(a_hi, b_lo, preferred_element_type=F32)).astype(BF16)


def _fold_pool_call(w_pool, pool_scale, w_proj_pool):
    n_groups, gd, _ = w_pool.shape
    D = w_proj_pool.shape[1]
    return pl.pallas_call(
        _fold_pool_kernel,
        grid=(n_groups,),
        in_specs=[pl.BlockSpec((None, gd, gd), lambda g: (g, 0, 0)),
                  pl.BlockSpec((None, 1, gd), lambda g: (g, 0, 0)),
                  pl.BlockSpec((gd, D), lambda g: (g, 0))],
        out_specs=pl.BlockSpec((gd, D), lambda g: (g, 0)),
        out_shape=jax.ShapeDtypeStruct((n_groups * gd, D), BF16),
        name="fold_pool",
    )(w_pool, pool_scale.reshape(n_groups, 1, gd), w_proj_pool)


def _post_kernel(a_ref, u_ref, halo_ref, ga_ref, gb_ref, x_ref,
                 pa_ref, pb_ref, wout_ref, g1_ref, b1_ref,
                 wr_cat_ref, br_ref,
                 h_ref, hb_ref, ids_ref, wts_ref, cnt_ref, pool_scr, tmp_scr, *, tm, sub, alpha):
    i = pl.program_id(1)
    halo = halo_ref[...]
    pool_scr[0:POOL_HALO, :] = jnp.zeros_like(halo)
    pool_scr[POOL_HALO:2 * POOL_HALO, :] = jnp.where(i > 0, halo, jnp.zeros_like(halo))
    pool_scr[2 * POOL_HALO:, :] = u_ref[...]
    tmp_scr[:, 0:POOL_HALO, :] = jnp.zeros((tmp_scr.shape[0], POOL_HALO, POOL_GROUP_DIM), F32)
    head_pos = i * tm + lax.broadcasted_iota(jnp.int32, (POOL_HALO, POOL_GROUP_DIM), 0)

    def rows_chain(r0):
        rs = pl.ds(r0, sub)
        group_cols = [slice(gi * POOL_GROUP_DIM, (gi + 1) * POOL_GROUP_DIM) for gi in range(len(POOL_SIZES))]
        diffs = []
        ext = sub + 2 * POOL_HALO
        for gi, (cols, w) in enumerate(zip(group_cols, POOL_SIZES)):
            ug = u_ref[rs, cols]
            levels = w.bit_length() - 1
            for level in range(levels):
                shift = 1 << level
                dst = tmp_scr.at[(r0 // sub * len(POOL_SIZES) + gi) * 2 + level % 2]
                if level == 0:
                    cur = pool_scr[pl.ds(r0 + POOL_HALO, ext - POOL_HALO), cols]
                    back = pool_scr[pl.ds(r0 + POOL_HALO - shift, ext - POOL_HALO), cols]
                else:
                    src = tmp_scr.at[(r0 // sub * len(POOL_SIZES) + gi) * 2 + (level - 1) % 2]
                    cur = src[pl.ds(POOL_HALO, ext - POOL_HALO), :]
                    back = src[pl.ds(POOL_HALO - shift, ext - POOL_HALO), :]
                if level == levels - 1:
                    acc = (cur + back)[POOL_HALO:]
                else:
                    dst[pl.ds(POOL_HALO, ext - POOL_HALO), :] = cur + back
            inv_count = jnp.full((sub, POOL_GROUP_DIM), 1.0 / w, F32)
            if r0 == 0:
                inv_count = jnp.concatenate([1.0 / jnp.minimum(head_pos + 1, w).astype(F32),
                                             inv_count[POOL_HALO:]], axis=0)
            diffs.append((acc * inv_count - ug).astype(BF16))
        yield
        y_pool = jnp.dot(jnp.concatenate(diffs, axis=1), pb_ref[...], preferred_element_type=F32)
        y_attn = jnp.dot(a_ref[rs, :], pa_ref[...], preferred_element_type=F32)
        yield
        mixed = ga_ref[rs, :] * y_attn.astype(BF16) + gb_ref[rs, :] * y_pool.astype(BF16)
        yield
        y = jnp.dot(mixed, wout_ref[...], preferred_element_type=F32)
        yield
        h = _layer_norm(alpha * x_ref[rs, :] + y, g1_ref[...], b1_ref[...])
        h_ref[rs, :] = h
        h_hi = h.astype(BF16)
        h_rounded = h_hi.astype(F32)
        packed = _pack_bf16_pairs(h_rounded)
        for c in range(PACK_CHUNKS):
            hb_ref[c, rs, :] = packed[:, c * LANES:(c + 1) * LANES]

        h_lo = (h - h_rounded).astype(BF16)
        yield
        nt = (((1,), (1,)), ((), ()))
        both = lax.dot_general(wr_cat_ref[...], h_hi, nt, preferred_element_type=F32)
        lo_hi = lax.dot_general(wr_cat_ref[0:ROUTER_LANES, :], h_lo, nt, preferred_element_type=F32)
        logits = both[0:ROUTER_LANES] + both[ROUTER_LANES:] + lo_hi + br_ref[...]
        yield
        row = lax.broadcasted_iota(jnp.int32, logits.shape, 0)
        big = jnp.int32(2 ** 30)
        is_group = row < N_EXPERT_GROUPS
        gl = jnp.where(is_group, logits, -jnp.inf)
        gmax = jnp.max(gl, axis=0, keepdims=True)
        g_idx = jnp.min(jnp.where(gl == gmax, row, big), axis=0, keepdims=True)
        g_prob = 1.0 / jnp.sum(jnp.exp(gl - gmax), axis=0, keepdims=True)
        expert = row - N_EXPERT_GROUPS
        in_group = (expert >= g_idx * EXPERTS_PER_GROUP) & (expert < (g_idx + 1) * EXPERTS_PER_GROUP)
        el = jnp.where(in_group, logits, -jnp.inf)
        v1 = jnp.max(el, axis=0, keepdims=True)
        i1 = jnp.min(jnp.where(el == v1, expert, big), axis=0, keepdims=True)
        el2 = jnp.where(expert == i1, -jnp.inf, el)
        v2 = jnp.max(el2, axis=0, keepdims=True)
        i2 = jnp.min(jnp.where(el2 == v2, expert, big), axis=0, keepdims=True)
        e2 = jnp.exp(v2 - v1)
        den = 1.0 + e2
        first_row = lax.broadcasted_iota(jnp.int32, (SUBLANES, sub), 0) == 0
        wts_ref[:, rs] = jnp.where(first_row, 1.0 / den * g_prob, e2 / den * g_prob)
        ids_ref[:, rs] = jnp.where(first_row, i1, i2).astype(F32)
        chosen = jnp.where((expert == i1) | (expert == i2), 1.0, 0.0)
        per_expert = jnp.broadcast_to(jnp.sum(chosen, axis=1, keepdims=True), (ROUTER_LANES, LANES))
        to_lane = (lax.broadcasted_iota(jnp.int32, (ROUTER_LANES, LANES), 0) - N_EXPERT_GROUPS
                   == lax.broadcasted_iota(jnp.int32, (ROUTER_LANES, LANES), 1))
        counts.append(jnp.sum(jnp.where(to_lane, per_expert, 0.0), axis=0, keepdims=True))
        yield

    counts = []
    _run_skewed([rows_chain(r0) for r0 in range(0, tm, sub)], 7)
    cnt_ref[...] = jnp.broadcast_to(functools.reduce(lambda a, b: a + b, counts), cnt_ref.shape)


def _post_call(a, u, ga, gb, x, w_proj_attn, w_pool, pool_scale, w_proj_pool, w_out, gamma, beta,
               w_router_group, b_router_group, w_router_expert, b_router_expert, alpha, b0, tm=1024, n_sub=2):
    B = a.shape[0]
    _, S, D = x.shape
    assert S % tm == 0 and tm % POOL_HALO == 0
    n_logits = N_EXPERT_GROUPS + N_EXPERTS
    assert n_logits <= ROUTER_LANES
    wr = jnp.concatenate([w_router_group, w_router_expert], axis=1).T
    wr = jnp.pad(wr, ((0, ROUTER_LANES - n_logits), (0, 0)))
    wr_hi = wr.astype(BF16)
    wr_lo = (wr - wr_hi.astype(F32)).astype(BF16)
    wr_cat = jnp.concatenate([wr_hi, wr_lo], axis=0)
    br = jnp.pad(jnp.concatenate([b_router_group, b_router_expert]),
                 (0, ROUTER_LANES - n_logits)).reshape(ROUTER_LANES, 1)
    n_tiles = B * (S // tm)
    row_spec = lambda width: pl.BlockSpec((None, tm, width), lambda b, i: (b, i, 0))
    full = lambda arr: pl.BlockSpec(arr.shape, lambda b, i: (0,) * arr.ndim)
    halo_blocks = tm // POOL_HALO
    halo_spec = pl.BlockSpec((None, POOL_HALO, POOL_WIDTH),
                             lambda b, i: (b, jnp.maximum(i * halo_blocks - 1, 0), 0))
    weights = [w_proj_attn.astype(BF16), _fold_pool_call(w_pool, pool_scale, w_proj_pool),
               w_out.astype(BF16), gamma.reshape(1, D), beta.reshape(1, D), wr_cat, br]
    return pl.pallas_call(
        functools.partial(_post_kernel, tm=tm, sub=tm // n_sub, alpha=alpha),
        grid=(B, S // tm),
        in_specs=[row_spec(GROUP_WIDTH), row_spec(POOL_WIDTH), halo_spec, row_spec(D), row_spec(D),
                  pl.BlockSpec((None, tm, D), lambda b, i: (b + b0, i, 0))] + [full(w) for w in weights],
        out_specs=[row_spec(D),
                   pl.BlockSpec((PACK_CHUNKS, tm, LANES), lambda b, i: (0, b * (S // tm) + i, 0)),
                   pl.BlockSpec((SUBLANES, tm), lambda b, i: (0, b * (S // tm) + i)),
                   pl.BlockSpec((SUBLANES, tm), lambda b, i: (0, b * (S // tm) + i)),
                   pl.BlockSpec((SUBLANES, LANES), lambda b, i: (b * (S // tm) + i, 0))],
        out_shape=[jax.ShapeDtypeStruct((B, S, D), F32),
                   jax.ShapeDtypeStruct((PACK_CHUNKS, B * S, LANES), jnp.int32),
                   jax.ShapeDtypeStruct((SUBLANES, B * S), F32), jax.ShapeDtypeStruct((SUBLANES, B * S), F32),
                   jax.ShapeDtypeStruct((n_tiles * SUBLANES, LANES), F32)],
        scratch_shapes=[pltpu.VMEM((tm + 2 * POOL_HALO, POOL_WIDTH), F32),
                        pltpu.VMEM((n_sub * len(POOL_SIZES) * 2, tm // n_sub + 2 * POOL_HALO, POOL_GROUP_DIM), F32)],
        compiler_params=pltpu.CompilerParams(dimension_semantics=("parallel", "parallel"),
                                             vmem_limit_bytes=VMEM_LIMIT_BYTES),
        name="post",
    )(a, u, u, ga, gb, x, *weights)


def _rank_kernel(ids_ref, cnt_ref, pos_ref, ends_ref, run_ref, start_ref, earlier_ref, *, tm, tile_rows):
    i = pl.program_id(0)

    @pl.when(i == 0)
    def _():
        total = jnp.sum(cnt_ref[...], axis=0, keepdims=True) / SUBLANES
        padded = jnp.broadcast_to(jnp.ceil(total / tile_rows) * tile_rows, ends_ref.shape)
        lane1 = lax.broadcasted_iota(jnp.int32, padded.shape, 1)
        incl = padded
        shift = 1
        while shift < LANES:
            incl = incl + jnp.where(lane1 >= shift, pltpu.roll(incl, shift, axis=1), 0.0)
            shift *= 2
        ends_ref[...] = incl
        start_row = jnp.broadcast_to((incl - padded)[0:1, :], start_ref.shape)
        on_diagonal = (lax.broadcasted_iota(jnp.int32, start_ref.shape, 0)
                       == lax.broadcasted_iota(jnp.int32, start_ref.shape, 1))
        start_col = jnp.sum(jnp.where(on_diagonal, start_row, 0.0), axis=1, keepdims=True)
        start_ref[...] = jnp.broadcast_to(start_col, start_ref.shape)
        run_ref[...] = jnp.zeros_like(run_ref)
        row = lax.broadcasted_iota(jnp.int32, (tm, tm), 0)
        col = lax.broadcasted_iota(jnp.int32, (tm, tm), 1)
        earlier_ref[...] = jnp.where(row < col, 1.0, 0.0).astype(BF16)

    ids = ids_ref[...]
    expert = lax.broadcasted_iota(jnp.int32, (N_EXPERTS, tm), 0).astype(F32)
    oh0 = expert == ids[0:1, :]
    oh1 = expert == ids[1:2, :]
    onehot = jnp.where(oh0 | oh1, 1.0, 0.0)
    before = jnp.dot(onehot.astype(BF16), earlier_ref[...], preferred_element_type=F32)
    slot = start_ref[:, 0:1] + run_ref[:, 0:1] + before
    p0 = jnp.sum(jnp.where(oh0, slot, 0.0), axis=0, keepdims=True)
    p1 = jnp.sum(jnp.where(oh1, slot, 0.0), axis=0, keepdims=True)
    first_row = lax.broadcasted_iota(jnp.int32, pos_ref.shape, 0) == 0
    pos_ref[...] = jnp.where(first_row, p0, p1).astype(jnp.int32)
    run_ref[...] += jnp.broadcast_to(jnp.sum(onehot, axis=1, keepdims=True), run_ref.shape)


def _rank_call(ids_t, counts, tile_rows, tm=1024):
    N = ids_t.shape[1]
    assert N % tm == 0 and SUBLANES * (2 * N + N_EXPERTS * tile_rows) < 2 ** 24
    assert N_EXPERTS <= LANES
    return pl.pallas_call(
        functools.partial(_rank_kernel, tm=tm, tile_rows=tile_rows),
        grid=(N // tm,),
        in_specs=[pl.BlockSpec((SUBLANES, tm), lambda i: (0, i)),
                  pl.BlockSpec(counts.shape, lambda i: (0, 0))],
        out_specs=[pl.BlockSpec((SUBLANES, tm), lambda i: (0, i)),
                   pl.BlockSpec((SUBLANES, LANES), lambda i: (0, 0))],
        out_shape=[jax.ShapeDtypeStruct((SUBLANES, N), jnp.int32),
                   jax.ShapeDtypeStruct((SUBLANES, LANES), F32)],
        scratch_shapes=[pltpu.VMEM((N_EXPERTS, LANES), F32), pltpu.VMEM((N_EXPERTS, LANES), F32),
                        pltpu.VMEM((tm, tm), BF16)],
        compiler_params=pltpu.CompilerParams(dimension_semantics=("arbitrary",),
                                             vmem_limit_bytes=VMEM_LIMIT_BYTES),
        name="rank",
    )(ids_t, counts)


SC_CORES = 2
SC_SUBCORES = 16
SC_WORKERS = SC_CORES * SC_SUBCORES
SC_CHUNK = 128


def _sc_mesh():
    return plsc.VectorSubcoreMesh(core_axis_name="c", subcore_axis_name="s",
                                  num_cores=SC_CORES, num_subcores=SC_SUBCORES)


def _sc_dispatch(packed, pos_t, n_rows):
    n_chunks, n_tok, width = packed.shape
    per_worker = n_tok // SC_WORKERS
    assert n_tok % (SC_WORKERS * SC_CHUNK) == 0

    @functools.partial(
        pl.kernel, mesh=_sc_mesh(),
        out_type=jax.ShapeDtypeStruct((n_chunks, n_rows, width), packed.dtype),
        scratch_types=[pltpu.VMEM((SC_CHUNK,), jnp.int32), pltpu.VMEM((SC_CHUNK,), jnp.int32),
                       pltpu.VMEM((n_chunks, SC_CHUNK, width), packed.dtype)]
                      + [pltpu.SemaphoreType.DMA] * (3 * n_chunks),
        name="sc_dispatch")
    def run(packed_hbm, pos_hbm, out_hbm, idx0, idx1, bufs, *sems):
        load_sems, sems0, sems1 = sems[:n_chunks], sems[n_chunks:2 * n_chunks], sems[2 * n_chunks:]
        worker = lax.axis_index("s") * SC_CORES + lax.axis_index("c")

        @pl.loop(0, per_worker // SC_CHUNK)
        def _(j):
            base = worker * per_worker + j * SC_CHUNK
            loads = [pltpu.async_copy(packed_hbm.at[c, pl.ds(base, SC_CHUNK)], bufs.at[c], load_sems[c])
                     for c in range(n_chunks)]
            pltpu.sync_copy(pos_hbm.at[0, pl.ds(base, SC_CHUNK)], idx0)
            pltpu.sync_copy(pos_hbm.at[1, pl.ds(base, SC_CHUNK)], idx1)
            scatters = []
            for c in range(n_chunks):
                loads[c].wait()
                scatters.append(pltpu.async_copy(bufs.at[c], out_hbm.at[c].at[idx0], sems0[c]))
                scatters.append(pltpu.async_copy(bufs.at[c], out_hbm.at[c].at[idx1], sems1[c]))
            for s in scatters:
                s.wait()

    return run(packed, pos_t)


def _sc_combine(sorted_rows, pos_t):
    n_chunks, _, width = sorted_rows.shape
    n_tok = pos_t.shape[1]
    per_worker = n_tok // SC_WORKERS
    assert n_tok % (SC_WORKERS * SC_CHUNK) == 0

    @functools.partial(
        pl.kernel, mesh=_sc_mesh(),
        out_type=jax.ShapeDtypeStruct((2, n_chunks, n_tok, width), sorted_rows.dtype),
        scratch_types=[pltpu.VMEM((SC_CHUNK,), jnp.int32),
                       pltpu.VMEM((n_chunks, SC_CHUNK, width), sorted_rows.dtype)]
                      + [pltpu.SemaphoreType.DMA] * (2 * n_chunks),
        name="sc_combine")
    def run(rows_hbm, pos_hbm, out_hbm, idx, bufs, *sems):
        gather_sems, write_sems = sems[:n_chunks], sems[n_chunks:]
        worker = lax.axis_index("s") * SC_CORES + lax.axis_index("c")

        @pl.loop(0, per_worker // SC_CHUNK)
        def _(j):
            base = worker * per_worker + j * SC_CHUNK
            for k in range(2):
                pltpu.sync_copy(pos_hbm.at[k, pl.ds(base, SC_CHUNK)], idx)
                gathers = [pltpu.async_copy(rows_hbm.at[c].at[idx], bufs.at[c], gather_sems[c])
                           for c in range(n_chunks)]
                writes = []
                for c in range(n_chunks):
                    gathers[c].wait()
                    writes.append(pltpu.async_copy(bufs.at[c], out_hbm.at[k, c, pl.ds(base, SC_CHUNK)],
                                                   write_sems[c]))
                for w in writes:
                    w.wait()

    return run(sorted_rows, pos_t)


EXPERT_TILES_PER_STEP = 2
EXPERT_RING = 3 * EXPERT_TILES_PER_STEP


def _expert_kernel(tile_expert_ref, n_used_ref, first_ref, wslot_ref,
                   xs_hbm, wg_hbm, wu_hbm, wd_hbm, ys_ref,
                   x_buf, wg_buf, wu_buf, wd_buf, x_sems, w_sems, *, tile_rows):
    s = pl.program_id(0)
    n_used = n_used_ref[0]
    per_step = EXPERT_TILES_PER_STEP
    ahead = EXPERT_RING - per_step

    def x_copy(t):
        slot = t % EXPERT_RING
        return pltpu.make_async_copy(xs_hbm.at[:, pl.ds(pl.multiple_of(t * tile_rows, tile_rows), tile_rows), :],
                                     x_buf.at[slot], x_sems.at[slot])

    def w_copies(t):
        e, slot = tile_expert_ref[t], wslot_ref[t]
        return [pltpu.make_async_copy(hbm.at[e], buf.at[slot], w_sems.at[a, slot])
                for a, (hbm, buf) in enumerate(((wg_hbm, wg_buf), (wu_hbm, wu_buf), (wd_hbm, wd_buf)))]

    def start_tile(t):
        x_copy(t).start()

        @pl.when(first_ref[t] == 1)
        def _():
            for copy in w_copies(t):
                copy.start()

    def start_if_used(t):
        @pl.when(t < n_used)
        def _():
            start_tile(t)

    @pl.when(s == 0)
    def _():
        for t in range(ahead):
            start_if_used(t)

    for j in range(per_step):
        start_if_used(s * per_step + ahead + j)

    def compute_tile(t, row_off):
        x_copy(t).wait()

        @pl.when(first_ref[t] == 1)
        def _():
            for copy in w_copies(t):
                copy.wait()

        x_slot = t % EXPERT_RING
        w_slot = wslot_ref[t]
        wg = wg_buf[w_slot]
        wu = wu_buf[w_slot]
        wd = wd_buf[w_slot]
        sub = tile_rows // 2

        def rows_chain(r0):
            x = _unpack_bf16_pairs(jnp.concatenate([x_buf[x_slot, c, pl.ds(r0, sub), :]
                                                    for c in range(PACK_CHUNKS)], axis=1))
            yield
            gate = jnp.dot(x, wg, preferred_element_type=F32)
            up = jnp.dot(x, wu, preferred_element_type=F32)
            yield
            hidden = (jax.nn.silu(gate) * up).astype(BF16)
            yield
            y = jnp.dot(hidden, wd, preferred_element_type=F32)
            yield
            y = _pack_bf16_pairs(y.astype(BF16).astype(F32))
            for c in range(PACK_CHUNKS):
                ys_ref[c, pl.ds(row_off + r0, sub), :] = y[:, c * LANES:(c + 1) * LANES]
            yield

        _run_skewed([rows_chain(r0) for r0 in range(0, tile_rows, sub)], 5)

    for j in range(per_step):
        def compute_if_used(t=s * per_step + j, row_off=j * tile_rows):
            @pl.when(t < n_used)
            def _():
                compute_tile(t, row_off)
        compute_if_used()


def _expert_call(xs, tile_expert, n_used, w_gate, w_up, w_down, tile_rows):
    _, n_rows, _ = xs.shape
    D = w_gate.shape[1]
    assert w_gate.dtype == BF16 and w_up.dtype == BF16 and w_down.dtype == BF16
    first = jnp.concatenate([jnp.ones((1,), jnp.int32),
                             (tile_expert[1:] != tile_expert[:-1]).astype(jnp.int32)])
    wslot = ((jnp.cumsum(first) - 1) % EXPERT_RING).astype(jnp.int32)
    per_step = EXPERT_TILES_PER_STEP
    assert (n_rows // tile_rows) % per_step == 0
    row_block = pl.BlockSpec((PACK_CHUNKS, per_step * tile_rows, LANES),
                             lambda i, te, nu, fi, ws: (0, jnp.minimum(i, (nu[0] + per_step - 1) // per_step - 1), 0))
    hbm = pl.BlockSpec(memory_space=pl.ANY)
    return pl.pallas_call(
        functools.partial(_expert_kernel, tile_rows=tile_rows),
        grid_spec=pltpu.PrefetchScalarGridSpec(
            num_scalar_prefetch=4,
            grid=(n_rows // tile_rows // per_step,),
            in_specs=[hbm, hbm, hbm, hbm],
            out_specs=row_block,
            scratch_shapes=[pltpu.VMEM((EXPERT_RING, PACK_CHUNKS, tile_rows, LANES), xs.dtype),
                            pltpu.VMEM((EXPERT_RING, D, D_EXPERT), BF16),
                            pltpu.VMEM((EXPERT_RING, D, D_EXPERT), BF16),
                            pltpu.VMEM((EXPERT_RING, D_EXPERT, D), BF16),
                            pltpu.SemaphoreType.DMA((EXPERT_RING,)),
                            pltpu.SemaphoreType.DMA((3, EXPERT_RING))]),
        out_shape=jax.ShapeDtypeStruct(xs.shape, xs.dtype),
        compiler_params=pltpu.CompilerParams(dimension_semantics=("arbitrary",),
                                             vmem_limit_bytes=VMEM_LIMIT_BYTES),
        name="experts",
    )(tile_expert, n_used, first, wslot, xs, w_gate, w_up, w_down)


def _final_kernel(h_ref, y_ref, wts_ref, g2_ref, b2_ref, *rest, alpha):
    out_ref = rest[-1]
    tm = h_ref.shape[0]
    on_diagonal = (lax.broadcasted_iota(jnp.int32, (tm, tm), 0)
                   == lax.broadcasted_iota(jnp.int32, (tm, tm), 1))
    z = alpha * h_ref[...]
    for k in range(2):
        w_col = jnp.sum(jnp.where(on_diagonal, jnp.broadcast_to(wts_ref[k:k + 1, :], (tm, tm)), 0.0),
                        axis=1, keepdims=True)
        yk = _unpack_bf16_pairs(jnp.concatenate([y_ref[k, c] for c in range(PACK_CHUNKS)], axis=1))
        z = z + w_col * yk.astype(F32)
    out_ref[...] = _layer_norm(z, g2_ref[...], b2_ref[...])


def _final_call(h, y, wts, gamma, beta, alpha, row0, n_total, earlier_out, tm=1024):
    N, D = h.shape
    assert N % tm == 0 and row0 % tm == 0
    operands = [h, y, wts, gamma.reshape(1, D), beta.reshape(1, D)]
    in_specs = [pl.BlockSpec((tm, D), lambda i: (i, 0)),
                pl.BlockSpec((2, PACK_CHUNKS, tm, LANES), lambda i: (0, 0, i, 0)),
                pl.BlockSpec((SUBLANES, tm), lambda i: (0, i)),
                pl.BlockSpec((1, D), lambda i: (0, 0)),
                pl.BlockSpec((1, D), lambda i: (0, 0))]
    aliases = {}
    if earlier_out is not None:
        aliases = {len(operands): 0}
        operands.append(earlier_out)
        in_specs.append(pl.BlockSpec(memory_space=pl.ANY))
    return pl.pallas_call(
        functools.partial(_final_kernel, alpha=alpha),
        grid=(N // tm,),
        in_specs=in_specs,
        out_specs=pl.BlockSpec((tm, D), lambda i: (i + row0 // tm, 0)),
        out_shape=jax.ShapeDtypeStruct((n_total, D), F32),
        input_output_aliases=aliases,
        compiler_params=pltpu.CompilerParams(dimension_semantics=("parallel",),
                                             vmem_limit_bytes=VMEM_LIMIT_BYTES),
        name="final",
    )(*operands)


def _moe(packed, ids_t, counts, w_gate, w_up, w_down, tile_rows=512):
    N = ids_t.shape[1]
    pos_t, ends = _rank_call(ids_t, counts, tile_rows)
    n_tiles = 2 * N // tile_rows + N_EXPERTS
    seg_end = ends[0, :N_EXPERTS].astype(jnp.int32)
    tile_start = jnp.arange(n_tiles, dtype=jnp.int32) * tile_rows
    tile_expert = jnp.minimum(jnp.sum(seg_end[None, :] <= tile_start[:, None], axis=1),
                              N_EXPERTS - 1).astype(jnp.int32)
    n_used = (seg_end[N_EXPERTS - 1:] // tile_rows).astype(jnp.int32)
    xs = _sc_dispatch(packed, pos_t, n_tiles * tile_rows)
    ys = _expert_call(xs, tile_expert, n_used, w_gate, w_up, w_down, tile_rows)
    return _sc_combine(ys, pos_t)


@jax.jit
def kernel(x, w_in, b_in, rel_bias_table, w_pool, pool_scale, w_proj_attn, w_proj_pool, w_out, ln1_gamma, ln1_beta, w_router_group, b_router_group, w_router_expert, b_router_expert, w_expert_gate, w_expert_up, w_expert_down, ln2_gamma, ln2_beta):
    B, S, D = x.shape
    depth = w_in.shape[0]
    alpha = (2.0 * depth) ** 0.25
    n_parts = 2 if B % 2 == 0 else 1
    nb = B // n_parts
    for layer in range(depth):
        out = None
        experts32 = (w_expert_gate[layer], w_expert_up[layer], w_expert_down[layer])
        n_experts = experts32[0].shape[0]
        cast_in_attn = n_experts % nb == 0 and n_experts // nb <= N_GROUPS + 1
        experts16 = None if cast_in_attn else tuple(w.astype(BF16) for w in experts32)
        for part in range(n_parts):
            b0 = part * nb
            qkv, u, ga, gb = _proj_call(x, w_in[layer], b_in[layer], b0, nb)
            a, converted = _attn_call(qkv, rel_bias_table, experts32 if experts16 is None else None)
            if converted is not None:
                experts16 = converted
            h, packed, ids, wts, counts = _post_call(
                a, u, ga, gb, x, w_proj_attn[layer], w_pool[layer], pool_scale[layer],
                w_proj_pool[layer], w_out[layer], ln1_gamma[layer], ln1_beta[layer],
                w_router_group[layer], b_router_group[layer],
                w_router_expert[layer], b_router_expert[layer], alpha, b0)
            y = _moe(packed, ids, counts, *experts16)
            out = _final_call(h.reshape(nb * S, D), y, wts, ln2_gamma[layer],
                              ln2_beta[layer], alpha, b0 * S, B * S, out)
        x = out.reshape(B, S, D)
    return x
```
